```python
import math
import jax, jax.numpy as jnp
from jax import lax
import numpy as np

D_MODEL = 1024
BATCH = 8
SEQ = 2048
DEPTH = 1
DEC_BATCH = 2
DEC_SEQ = 16384
PAST_LEN = 128

D_MIX = D_MODEL
HY_CH = D_MIX // 2
DN_HEADS = 4
DN_DK = 128
DN_DV = 128
DN_QK = DN_HEADS * DN_DK
DN_W = DN_HEADS * DN_DV
HY_IN = 3 * HY_CH
DN_CONV = 2 * DN_QK + DN_W
DN_IN = DN_CONV + DN_W + 4 * DN_HEADS
IN_COLS = HY_IN + DN_IN
SHORT_CONV = 3
HY_EMB = 33
HY_BANDS = (HY_EMB - 1) // 2
HY_FILTER_W = 64
HY_DECAY_TARGET = 1e-2
HY_FAST_PCT = 0.3
HY_SLOW_PCT = 1.5
CHUNK = 64
N_EXPERTS = 32
TOP_K = 4
D_FF = D_MODEL
SWIGLU_ALPHA = 1.702
SWIGLU_LIMIT = 7.0
MOE_BLOCK = 512
EPS = 1e-6

kernel_name = 'hymba_hyena_gdn_moe_encoder'


def rmsnorm(x, w):
    xf = x.astype(jnp.float32)
    xf = xf * lax.rsqrt(jnp.mean(xf * xf, axis=-1, keepdims=True) + EPS)
    return xf.astype(x.dtype) * w


def l2norm(x):
    return x * lax.rsqrt(jnp.sum(x * x, axis=-1, keepdims=True) + EPS)


def centred_dwconv(x, w):
    width = w.shape[0]
    pad = width // 2
    L = x.shape[1]
    xp = jnp.pad(x, ((0, 0), (pad, pad), (0, 0)))
    out = xp[:, 0:L] * w[0]
    for i in range(1, width):
        out = out + xp[:, i:i + L] * w[i]
    return out


def hyena_filter(L, w1, b1, w2, b2, w3, freq):
    f32 = jnp.float32
    pos = jnp.arange(L, dtype=f32)
    t = jnp.linspace(0.0, 1.0, L, dtype=f32)[:, None]
    bands = jnp.linspace(1e-4, HY_BANDS - 1, HY_BANDS, dtype=f32)
    ang = (2.0 * math.pi / L) * pos[:, None] * bands[None, :]
    z = jnp.concatenate([t, jnp.cos(ang), -jnp.sin(ang)], axis=-1)
    fr = freq.astype(f32)
    h = jnp.sin(fr * (z @ w1.astype(f32) + b1.astype(f32)))
    h = jnp.sin(fr * (h @ w2.astype(f32) + b2.astype(f32)))
    h = h @ w3.astype(f32)
    deltas = jnp.abs(jnp.linspace(math.log(HY_DECAY_TARGET) / HY_SLOW_PCT,
                                  math.log(HY_DECAY_TARGET) / HY_FAST_PCT, HY_CH, dtype=f32))
    window = jnp.exp(-t * deltas[None, :])
    h_f = h[:, :HY_CH] * window
    h_b = h[:, HY_CH:] * window
    l1 = jnp.sum(jnp.abs(h_f), axis=0) + jnp.sum(jnp.abs(h_b[1:]), axis=0)
    k2 = jnp.concatenate([h_f, jnp.zeros((1, HY_CH), f32), h_b[:0:-1]], axis=0)
    return k2 / l1


def hyena_mixer(p, conv_w, conv_b, w1, b1, w2, b2, w3, freq, bias):
    B, L, _ = p.shape
    u = centred_dwconv(p, conv_w) + conv_b
    x0, x1, v = jnp.split(u, 3, axis=-1)
    k2 = hyena_filter(L, w1, b1, w2, b2, w3, freq)
    s = (x1 * v).astype(jnp.float32)
    S = jnp.fft.rfft(s, n=2 * L, axis=1)
    K = jnp.fft.rfft(k2, axis=0)
    conv = jnp.fft.irfft(S * K[None], n=2 * L, axis=1)[:, :L]
    y = x0.astype(jnp.float32) * (conv + s * bias.astype(jnp.float32))
    return y.astype(p.dtype)


def chunk_gated_delta(q, k, v, g, beta):
    f32 = jnp.float32
    q, k, v, g, beta = (a.astype(f32) for a in (q, k, v, g, beta))
    B, L, H, dk = q.shape
    dv = v.shape[-1]
    n = L // CHUNK

    def c4(a):
        return a.reshape(B, n, CHUNK, H, a.shape[-1]).transpose(1, 0, 3, 2, 4)

    def c3(a):
        return a.reshape(B, n, CHUNK, H).transpose(1, 0, 3, 2)

    qc = c4(q * (dk ** -0.5))
    kc = c4(k)
    vc = c4(v)
    bc = c3(beta)
    gc = jnp.cumsum(c3(g), axis=-1)
    idx = jnp.arange(CHUNK)
    incl = idx[:, None] >= idx[None, :]
    strict = idx[:, None] > idx[None, :]
    decay = jnp.exp(jnp.where(incl, gc[..., :, None] - gc[..., None, :], -jnp.inf))
    kb = kc * bc[..., None]
    a = jnp.einsum('nbhcd,nbhsd->nbhcs', kb, kc) * decay
    a = jnp.where(strict, a, 0.0) + jnp.eye(CHUNK, dtype=f32)
    rhs = jnp.concatenate([vc * bc[..., None], kb * jnp.exp(gc)[..., None]], axis=-1)
    sol = lax.linalg.triangular_solve(a, rhs, left_side=True, lower=True, unit_diagonal=True)
    uc, wc = sol[..., :dv], sol[..., dv:]
    qk = jnp.einsum('nbhcd,nbhsd->nbhcs', qc, kc) * decay

    def step(S, inp):
        q_i, k_i, u_i, w_i, g_i, qk_i = inp
        v_new = u_i - jnp.einsum('bhcd,bhde->bhce', w_i, S)
        o = (jnp.einsum('bhcd,bhde->bhce', q_i * jnp.exp(g_i)[..., None], S)
             + jnp.einsum('bhcs,bhse->bhce', qk_i, v_new))
        g_last = g_i[..., -1]
        S = (S * jnp.exp(g_last)[..., None, None]
             + jnp.einsum('bhcd,bhce->bhde', k_i * jnp.exp(g_last[..., None] - g_i)[..., None], v_new))
        return S, o

    S0 = jnp.zeros((B, H, dk, dv), f32)
    _, o = lax.scan(step, S0, (qc, kc, uc, wc, gc, qk))
    return o.transpose(1, 0, 3, 2, 4).reshape(B, L, H, dv)


def deltanet_mixer(p, conv_w, a_log, dt_bias, norm_w):
    B, L, _ = p.shape
    f32 = jnp.float32
    qkv = jax.nn.silu(centred_dwconv(p[..., :DN_CONV], conv_w))
    q = l2norm(qkv[..., :DN_QK].astype(f32).reshape(B, L, DN_HEADS, DN_DK))
    k = l2norm(qkv[..., DN_QK:2 * DN_QK].astype(f32).reshape(B, L, DN_HEADS, DN_DK))
    v = qkv[..., 2 * DN_QK:].reshape(B, L, DN_HEADS, DN_DV)
    z = p[..., DN_CONV:DN_CONV + DN_W].reshape(B, L, DN_HEADS, DN_DV)
    off = DN_CONV + DN_W
    a = p[..., off:off + 2 * DN_HEADS].astype(f32).reshape(B, L, 2, DN_HEADS)
    b = p[..., off + 2 * DN_HEADS:off + 4 * DN_HEADS].astype(f32).reshape(B, L, 2, DN_HEADS)
    g = -jnp.exp(a_log.astype(f32)) * jax.nn.softplus(a + dt_bias.astype(f32))
    beta = jax.nn.sigmoid(b)
    o_f = chunk_gated_delta(q, k, v, g[:, :, 0], beta[:, :, 0])
    rev = lambda t: jnp.flip(t, axis=1)
    o_b = rev(chunk_gated_delta(rev(q), rev(k), rev(v), rev(g[:, :, 1]), rev(beta[:, :, 1])))
    o = o_f + o_b
    o = o * lax.rsqrt(jnp.mean(o * o, axis=-1, keepdims=True) + EPS)
    o = o * norm_w.astype(f32) * jax.nn.silu(z.astype(f32))
    return o.reshape(B, L, DN_W).astype(p.dtype)


def clamped_swiglu(h):
    x_glu = jnp.minimum(h[..., ::2], SWIGLU_LIMIT)
    x_lin = jnp.clip(h[..., 1::2], -SWIGLU_LIMIT, SWIGLU_LIMIT)
    return x_glu * jax.nn.sigmoid(SWIGLU_ALPHA * x_glu) * (x_lin + 1.0)


def moe(x, w_router, b_router, w_gate_up, b_gate_up, w_down, b_down):
    T, D = x.shape
    TK = T * TOP_K
    logits = (x @ w_router + b_router).astype(jnp.float32)
    top_vals, top_idx = lax.top_k(logits, TOP_K)
    gates = jax.nn.softmax(top_vals, axis=-1).astype(x.dtype)
    flat_e = top_idx.reshape(-1)
    flat_tok = jnp.arange(TK, dtype=jnp.int32) // TOP_K
    flat_gate = gates.reshape(-1)
    order = jnp.argsort(flat_e)
    sorted_e = flat_e[order]
    counts = jnp.zeros((N_EXPERTS,), jnp.int32).at[flat_e].add(1)
    padded = (counts + MOE_BLOCK - 1) // MOE_BLOCK * MOE_BLOCK
    pad_end = jnp.cumsum(padded)
    pad_start = pad_end - padded
    grp_start = jnp.cumsum(counts) - counts
    rank = jnp.arange(TK, dtype=jnp.int32) - grp_start[sorted_e]
    dest = pad_start[sorted_e] + rank
    n_blocks = (TK + MOE_BLOCK - 1) // MOE_BLOCK + N_EXPERTS
    n_rows = n_blocks * MOE_BLOCK
    row_tok = jnp.full((n_rows,), T, jnp.int32).at[dest].set(flat_tok[order])
    row_gate = jnp.zeros((n_rows,), x.dtype).at[dest].set(flat_gate[order])
    block_start = jnp.arange(n_blocks, dtype=jnp.int32) * MOE_BLOCK
    block_e = jnp.minimum(jnp.searchsorted(pad_end, block_start, side='right'), N_EXPERTS - 1)
    x_pad = jnp.concatenate([x, jnp.zeros((1, D), x.dtype)], axis=0)
    xb = x_pad[row_tok].reshape(n_blocks, MOE_BLOCK, D)

    def expert_block(args):
        xe, e = args
        h = xe @ w_gate_up[e] + b_gate_up[e]
        return clamped_swiglu(h) @ w_down[e] + b_down[e]

    yb = lax.map(expert_block, (xb, block_e)).reshape(n_rows, D)
    y = jax.ops.segment_sum(yb * row_gate[:, None], row_tok, num_segments=T + 1)
    return y[:T]


def encoder_layer(x, norm_mix_w, w_in, hy_conv_w, hy_conv_b, hy_pos_w1, hy_pos_b1, hy_pos_w2,
                  hy_pos_b2, hy_pos_w3, hy_sin_freq, hy_bias, dn_conv_w, dn_a_log, dn_dt_bias,
                  dn_norm_w, w_out, norm_ffn_w, w_router, b_router, w_gate_up, b_gate_up,
                  w_down, b_down):
    B, L, D = x.shape
    h = rmsnorm(x, norm_mix_w)
    proj = h @ w_in
    y_hy = hyena_mixer(proj[..., :HY_IN], hy_conv_w, hy_conv_b, hy_pos_w1, hy_pos_b1,
                       hy_pos_w2, hy_pos_b2, hy_pos_w3, hy_sin_freq, hy_bias)
    y_dn = deltanet_mixer(proj[..., HY_IN:], dn_conv_w, dn_a_log, dn_dt_bias, dn_norm_w)
    x = x + jnp.concatenate([y_hy, y_dn], axis=-1) @ w_out
    h = rmsnorm(x, norm_ffn_w)
    x = x + moe(h.reshape(B * L, D), w_router, b_router, w_gate_up, b_gate_up,
                w_down, b_down).reshape(B, L, D)
    return x


def trunk(x, layers, norm_final_w):
    for l in range(DEPTH):
        x = encoder_layer(x, *[p[l] for p in layers])
    return rmsnorm(x, norm_final_w)


def setup_inputs(seed: int = 0) -> dict:
    key = jax.random.key(seed)
    ks = jax.random.split(key, 26)
    nrm = lambda k, s, sc: jax.random.normal(k, s, jnp.float32) * sc
    dt = jnp.exp(jax.random.uniform(ks[15], (DEPTH, 2, DN_HEADS), jnp.float32,
                                    math.log(1e-3), math.log(1e-1)))
    return {
        'x_prompt': nrm(ks[0], (BATCH, SEQ, D_MODEL), 1.0),
        'x_sample': nrm(ks[1], (DEC_BATCH, DEC_SEQ, D_MODEL), 1.0),
        'norm_mix_w': 1.0 + nrm(ks[2], (DEPTH, D_MODEL), 0.02),
        'w_in': nrm(ks[3], (DEPTH, D_MODEL, IN_COLS), D_MODEL ** -0.5),
        'hy_conv_w': nrm(ks[4], (DEPTH, SHORT_CONV, HY_IN), SHORT_CONV ** -0.5),
        'hy_conv_b': nrm(ks[5], (DEPTH, HY_IN), 0.02),
        'hy_pos_w1': nrm(ks[6], (DEPTH, HY_EMB, HY_FILTER_W), HY_EMB ** -0.5),
        'hy_pos_b1': nrm(ks[7], (DEPTH, HY_FILTER_W), 0.02),
        'hy_pos_w2': nrm(ks[8], (DEPTH, HY_FILTER_W, HY_FILTER_W), HY_FILTER_W ** -0.5),
        'hy_pos_b2': nrm(ks[9], (DEPTH, HY_FILTER_W), 0.02),
        'hy_pos_w3': nrm(ks[10], (DEPTH, HY_FILTER_W, 2 * HY_CH), HY_FILTER_W ** -0.5),
        'hy_sin_freq': 1.0 + nrm(ks[11], (DEPTH, HY_FILTER_W), 0.1),
        'hy_bias': nrm(ks[12], (DEPTH, HY_CH), 1.0),
        'dn_conv_w': nrm(ks[13], (DEPTH, SHORT_CONV, DN_CONV), SHORT_CONV ** -0.5),
        'dn_a_log': jnp.log(jax.random.uniform(ks[14], (DEPTH, 2, DN_HEADS), jnp.float32, 1.0, 16.0)),
        'dn_dt_bias': dt + jnp.log(-jnp.expm1(-dt)),
        'dn_norm_w': 1.0 + nrm(ks[16], (DEPTH, DN_DV), 0.02),
        'w_out': nrm(ks[17], (DEPTH, D_MIX, D_MODEL), D_MIX ** -0.5),
        'norm_ffn_w': 1.0 + nrm(ks[18], (DEPTH, D_MODEL), 0.02),
        'w_router': nrm(ks[19], (DEPTH, D_MODEL, N_EXPERTS), D_MODEL ** -0.5),
        'b_router': nrm(ks[20], (DEPTH, N_EXPERTS), 0.01),
        'w_gate_up': nrm(ks[21], (DEPTH, N_EXPERTS, D_MODEL, 2 * D_FF), D_MODEL ** -0.5),
        'b_gate_up': nrm(ks[22], (DEPTH, N_EXPERTS, 2 * D_FF), 0.01),
        'w_down': nrm(ks[23], (DEPTH, N_EXPERTS, D_FF, D_MODEL), D_FF ** -0.5),
        'b_down': nrm(ks[24], (DEPTH, N_EXPERTS, D_MODEL), 0.01),
        'norm_final_w': 1.0 + nrm(ks[25], (D_MODEL,), 0.02),
    }


def reference(x_prompt, x_sample, norm_mix_w, w_in, hy_conv_w, hy_conv_b, hy_pos_w1, hy_pos_b1,
              hy_pos_w2, hy_pos_b2, hy_pos_w3, hy_sin_freq, hy_bias, dn_conv_w, dn_a_log,
              dn_dt_bias, dn_norm_w, w_out, norm_ffn_w, w_router, b_router, w_gate_up,
              b_gate_up, w_down, b_down, norm_final_w):
    layers = (norm_mix_w, w_in, hy_conv_w, hy_conv_b, hy_pos_w1, hy_pos_b1, hy_pos_w2,
              hy_pos_b2, hy_pos_w3, hy_sin_freq, hy_bias, dn_conv_w, dn_a_log, dn_dt_bias,
              dn_norm_w, w_out, norm_ffn_w, w_router, b_router, w_gate_up, b_gate_up,
              w_down, b_down)
    y_prompt = trunk(x_prompt, layers, norm_final_w)
    y_sample = trunk(x_sample, layers, norm_final_w)
    return (y_prompt, y_sample)
```

```python
import functools
import math

import jax
import jax.numpy as jnp
import numpy as np
from jax import lax
from jax.experimental import pallas as pl
from jax.experimental.pallas import tpu as pltpu

D_MODEL = 1024
HY_CH = 512
DN_HEADS = 4
DN_DK = 128
DN_DV = 128
DN_QK = DN_HEADS * DN_DK
DN_W = DN_HEADS * DN_DV
HY_IN = 3 * HY_CH
DN_CONV = 2 * DN_QK + DN_W
DN_MAIN = DN_CONV + DN_W
N_GATE = 4 * DN_HEADS
HY_EMB = 33
HY_BANDS = (HY_EMB - 1) // 2
HY_DECAY_TARGET = 1e-2
HY_FAST_PCT = 0.3
HY_SLOW_PCT = 1.5
CHUNK = 64
N_EXPERTS = 32
TOP_K = 4
D_FF = D_MODEL
SWIGLU_ALPHA = 1.702
SWIGLU_LIMIT = 7.0
MOE_BLOCK = 512
EPS = 1e-6

LANES = 128
ROW_TILE = 512
VMEM_LIMIT = 56 * 1024 * 1024

F32 = jnp.float32
BF16 = jnp.bfloat16


def _rms(x, g):
    return x * lax.rsqrt(jnp.mean(x * x, axis=-1, keepdims=True) + EPS) * g


def _in_proj_kernel(x_ref, g_ref, whyr, wdnr, wgr, ohy, odn, og):
    h = _rms(x_ref[...], g_ref[...]).astype(BF16)
    ohy[...] = jnp.dot(h, whyr[...], preferred_element_type=F32)
    odn[...] = jnp.dot(h, wdnr[...], preferred_element_type=F32)
    og[...] = jnp.dot(h, wgr[...], preferred_element_type=F32)


def _in_proj(x2d, g, w_in):
    T = x2d.shape[0]
    w = w_in.astype(BF16)
    w_hy = w[:, :HY_IN]
    w_dn = w[:, HY_IN:HY_IN + DN_MAIN]
    w_g = jnp.pad(w[:, HY_IN + DN_MAIN:], ((0, 0), (0, LANES - N_GATE)))
    const = lambda i: (0, 0)
    row = lambda i: (i, 0)
    return pl.pallas_call(
        _in_proj_kernel,
        grid=(T // ROW_TILE,),
        in_specs=[pl.BlockSpec((ROW_TILE, D_MODEL), row),
                  pl.BlockSpec((1, D_MODEL), const),
                  pl.BlockSpec((D_MODEL, HY_IN), const),
                  pl.BlockSpec((D_MODEL, DN_MAIN), const),
                  pl.BlockSpec((D_MODEL, LANES), const)],
        out_specs=[pl.BlockSpec((ROW_TILE, HY_IN), row),
                   pl.BlockSpec((ROW_TILE, DN_MAIN), row),
                   pl.BlockSpec((ROW_TILE, LANES), row)],
        out_shape=[jax.ShapeDtypeStruct((T, HY_IN), F32),
                   jax.ShapeDtypeStruct((T, DN_MAIN), F32),
                   jax.ShapeDtypeStruct((T, LANES), F32)],
        compiler_params=pltpu.CompilerParams(dimension_semantics=("parallel",),
                                             vmem_limit_bytes=VMEM_LIMIT),
        name="in_proj",
    )(x2d, g.reshape(1, D_MODEL), w_hy, w_dn, w_g)


def _out_proj_kernel(x_ref, yh_ref, yd_ref, wh_ref, wd_ref, g_ref, wrh_ref, wrl_ref, br_ref,
                     x1_ref, h_ref, lg_ref):
    x1 = (x_ref[...]
          + jnp.dot(yh_ref[...].astype(BF16), wh_ref[...], preferred_element_type=F32)
          + jnp.dot(yd_ref[...].astype(BF16), wd_ref[...], preferred_element_type=F32))
    x1_ref[...] = x1
    h = _rms(x1, g_ref[...])
    h_hi = h.astype(BF16)
    h_lo = (h - h_hi.astype(F32)).astype(BF16)
    h_ref[...] = h_hi
    lg_ref[...] = (jnp.dot(h_hi, wrh_ref[...], preferred_element_type=F32)
                   + jnp.dot(h_lo, wrh_ref[...], preferred_element_type=F32)
                   + jnp.dot(h_hi, wrl_ref[...], preferred_element_type=F32)
                   + br_ref[...])


def _out_proj(x2d, y_hy, y_dn, w_out, g, w_router, b_router):
    T = x2d.shape[0]
    w = w_out.astype(BF16)
    wr = jnp.pad(w_router, ((0, 0), (0, LANES - N_EXPERTS)))
    wr_hi = wr.astype(BF16)
    wr_lo = (wr - wr_hi.astype(F32)).astype(BF16)
    br = jnp.pad(b_router, (0, LANES - N_EXPERTS)).reshape(1, LANES)
    const = lambda i: (0, 0)
    row = lambda i: (i, 0)
    return pl.pallas_call(
        _out_proj_kernel,
        grid=(T // ROW_TILE,),
        in_specs=[pl.BlockSpec((ROW_TILE, D_MODEL), row),
                  pl.BlockSpec((ROW_TILE, HY_CH), row),
                  pl.BlockSpec((ROW_TILE, DN_W), row),
                  pl.BlockSpec((HY_CH, D_MODEL), const),
                  pl.BlockSpec((DN_W, D_MODEL), const),
                  pl.BlockSpec((1, D_MODEL), const),
                  pl.BlockSpec((D_MODEL, LANES), const),
                  pl.BlockSpec((D_MODEL, LANES), const),
                  pl.BlockSpec((1, LANES), const)],
        out_specs=[pl.BlockSpec((ROW_TILE, D_MODEL), row),
                   pl.BlockSpec((ROW_TILE, D_MODEL), row),
                   pl.BlockSpec((ROW_TILE, LANES), row)],
        out_shape=[jax.ShapeDtypeStruct((T, D_MODEL), F32),
                   jax.ShapeDtypeStruct((T, D_MODEL), BF16),
                   jax.ShapeDtypeStruct((T, LANES), F32)],
        compiler_params=pltpu.CompilerParams(dimension_semantics=("parallel",),
                                             vmem_limit_bytes=VMEM_LIMIT),
        name="out_proj_router",
    )(x2d, y_hy, y_dn, w[:HY_CH], w[HY_CH:], g.reshape(1, D_MODEL), wr_hi, wr_lo, br)


def _expert_kernel(be_ref, nb_ref, xb_ref, gate_ref, wg_ref, wl_ref, bg_ref, bl_ref, wd_ref, bd_ref,
                   y_ref):
    i = pl.program_id(0)

    @pl.when(i < nb_ref[0])
    def _():
        xb = xb_ref[...]
        hg = jnp.dot(xb, wg_ref[0], preferred_element_type=F32) + bg_ref[0]
        hl = jnp.dot(xb, wl_ref[0], preferred_element_type=F32) + bl_ref[0]
        x_glu = jnp.minimum(hg, SWIGLU_LIMIT)
        x_lin = jnp.clip(hl, -SWIGLU_LIMIT, SWIGLU_LIMIT)
        act = x_glu * jax.nn.sigmoid(SWIGLU_ALPHA * x_glu) * (x_lin + 1.0)
        y = jnp.dot(act.astype(BF16), wd_ref[0], preferred_element_type=F32) + bd_ref[0]
        y_ref[...] = y * gate_ref[...]

    @pl.when(i >= nb_ref[0])
    def _():
        y_ref[...] = jnp.zeros_like(y_ref)


def _expert_mlp(xb, row_gate, block_e, n_used, w_gate_up, b_gate_up, w_down, b_down):
    n_rows = xb.shape[0]
    n_blocks = n_rows // MOE_BLOCK
    wg = w_gate_up[:, :, 0::2].astype(BF16)
    wl = w_gate_up[:, :, 1::2].astype(BF16)
    bg = b_gate_up[:, 0::2].reshape(N_EXPERTS, 1, D_FF)
    bl = b_gate_up[:, 1::2].reshape(N_EXPERTS, 1, D_FF)
    wd = w_down.astype(BF16)
    bd = b_down.reshape(N_EXPERTS, 1, D_MODEL)
    rowm = lambda i, be, nb: (i, 0)
    exp3 = lambda i, be, nb: (be[i], 0, 0)
    grid_spec = pltpu.PrefetchScalarGridSpec(
        num_scalar_prefetch=2,
        grid=(n_blocks,),
        in_specs=[pl.BlockSpec((MOE_BLOCK, D_MODEL), rowm),
                  pl.BlockSpec((MOE_BLOCK, 1), rowm),
                  pl.BlockSpec((1, D_MODEL, D_FF), exp3),
                  pl.BlockSpec((1, D_MODEL, D_FF), exp3),
                  pl.BlockSpec((1, 1, D_FF), exp3),
                  pl.BlockSpec((1, 1, D_FF), exp3),
                  pl.BlockSpec((1, D_FF, D_MODEL), exp3),
                  pl.BlockSpec((1, 1, D_MODEL), exp3)],
        out_specs=pl.BlockSpec((MOE_BLOCK, D_MODEL), rowm),
    )
    return pl.pallas_call(
        _expert_kernel,
        grid_spec=grid_spec,
        out_shape=jax.ShapeDtypeStruct((n_rows, D_MODEL), F32),
        compiler_params=pltpu.CompilerParams(dimension_semantics=("arbitrary",),
                                             vmem_limit_bytes=VMEM_LIMIT),
        name="expert_mlp",
    )(block_e, n_used, xb, row_gate.reshape(n_rows, 1), wg, wl, bg, bl, wd, bd)


def _moe(h_bf, logits, w_gate_up, b_gate_up, w_down, b_down):
    T = h_bf.shape[0]
    TK = T * TOP_K
    top_vals, top_idx = lax.top_k(logits, TOP_K)
    gates = jax.nn.softmax(top_vals, axis=-1)
    sel = jnp.sum(jax.nn.one_hot(top_idx, N_EXPERTS, dtype=jnp.int32), axis=1)
    before = jnp.cumsum(sel, axis=0) - sel
    counts = jnp.sum(sel, axis=0)
    padded = (counts + MOE_BLOCK - 1) // MOE_BLOCK * MOE_BLOCK
    pad_end = jnp.cumsum(padded)
    pad_start = pad_end - padded
    rank = jnp.take_along_axis(before, top_idx, axis=1)
    dest = (pad_start[top_idx] + rank).astype(jnp.int32)
    n_blocks = (TK + MOE_BLOCK - 1) // MOE_BLOCK + N_EXPERTS
    n_rows = n_blocks * MOE_BLOCK
    tok = jnp.broadcast_to(jnp.arange(T, dtype=jnp.int32)[:, None], (T, TOP_K))
    row_tok = jnp.zeros((n_rows,), jnp.int32).at[dest.reshape(-1)].set(tok.reshape(-1))
    row_gate = jnp.zeros((n_rows,), F32).at[dest.reshape(-1)].set(gates.reshape(-1))
    block_start = jnp.arange(n_blocks, dtype=jnp.int32) * MOE_BLOCK
    block_e = jnp.minimum(jnp.searchsorted(pad_end, block_start, side='right'),
                          N_EXPERTS - 1).astype(jnp.int32)
    n_used = (pad_end[-1] // MOE_BLOCK).astype(jnp.int32).reshape(1)
    xb = h_bf[row_tok]
    yb = _expert_mlp(xb, row_gate, block_e, n_used, w_gate_up, b_gate_up, w_down, b_down)
    return jnp.sum(yb[dest], axis=1)


def _final_kernel(x_ref, m_ref, g_ref, o_ref):
    o_ref[...] = _rms(x_ref[...] + m_ref[...], g_ref[...])


def _final(x1, m, g):
    T = x1.shape[0]
    row = lambda i: (i, 0)
    return pl.pallas_call(
        _final_kernel,
        grid=(T // ROW_TILE,),
        in_specs=[pl.BlockSpec((ROW_TILE, D_MODEL), row),
                  pl.BlockSpec((ROW_TILE, D_MODEL), row),
                  pl.BlockSpec((1, D_MODEL), lambda i: (0, 0))],
        out_specs=pl.BlockSpec((ROW_TILE, D_MODEL), row),
        out_shape=jax.ShapeDtypeStruct((T, D_MODEL), F32),
        compiler_params=pltpu.CompilerParams(dimension_semantics=("parallel",)),
        name="final_norm",
    )(x1, m, g.reshape(1, D_MODEL))


def _dwconv(x, w):
    L = x.shape[1]
    xp = jnp.pad(x, ((0, 0), (1, 1), (0, 0)))
    return xp[:, 0:L] * w[0] + xp[:, 1:L + 1] * w[1] + xp[:, 2:L + 2] * w[2]


def _hyena_filter(L, w1, b1, w2, b2, w3, freq):
    pos = jnp.arange(L, dtype=F32)
    t = jnp.linspace(0.0, 1.0, L, dtype=F32)[:, None]
    bands = jnp.linspace(1e-4, HY_BANDS - 1, HY_BANDS, dtype=F32)
    ang = (2.0 * math.pi / L) * pos[:, None] * bands[None, :]
    z = jnp.concatenate([t, jnp.cos(ang), -jnp.sin(ang)], axis=-1)
    h = jnp.sin(freq * (z @ w1 + b1))
    h = jnp.sin(freq * (h @ w2 + b2))
    h = h @ w3
    deltas = jnp.abs(jnp.linspace(math.log(HY_DECAY_TARGET) / HY_SLOW_PCT,
                                  math.log(HY_DECAY_TARGET) / HY_FAST_PCT, HY_CH, dtype=F32))
    window = jnp.exp(-t * deltas[None, :])
    h_f = h[:, :HY_CH] * window
    h_b = h[:, HY_CH:] * window
    l1 = jnp.sum(jnp.abs(h_f), axis=0) + jnp.sum(jnp.abs(h_b[1:]), axis=0)
    k2 = jnp.concatenate([h_f, jnp.zeros((1, HY_CH), F32), h_b[:0:-1]], axis=0)
    return k2 / l1


def _hyena(p, conv_w, conv_b, w1, b1, w2, b2, w3, freq, bias):
    B, L, _ = p.shape
    u = _dwconv(p, conv_w) + conv_b
    x0, x1, v = jnp.split(u, 3, axis=-1)
    k2 = _hyena_filter(L, w1, b1, w2, b2, w3, freq)
    s = x1 * v
    S = jnp.fft.rfft(s, n=2 * L, axis=1)
    K = jnp.fft.rfft(k2, axis=0)
    conv = jnp.fft.irfft(S * K[None], n=2 * L, axis=1)[:, :L]
    return x0 * (conv + s * bias)


def _chunk_gated_delta(q, k, v, g, beta):
    B, L, H, dk = q.shape
    dv = v.shape[-1]
    n = L // CHUNK
    c4 = lambda a: a.reshape(B, n, CHUNK, H, a.shape[-1]).transpose(1, 0, 3, 2, 4)
    c3 = lambda a: a.reshape(B, n, CHUNK, H).transpose(1, 0, 3, 2)
    qc = c4(q * (dk ** -0.5))
    kc = c4(k)
    vc = c4(v)
    bc = c3(beta)
    gc = jnp.cumsum(c3(g), axis=-1)
    idx = jnp.arange(CHUNK)
    incl = idx[:, None] >= idx[None, :]
    strict = idx[:, None] > idx[None, :]
    decay = jnp.exp(jnp.where(incl, gc[..., :, None] - gc[..., None, :], -jnp.inf))
    kb = kc * bc[..., None]
    a = jnp.einsum('nbhcd,nbhsd->nbhcs', kb, kc) * decay
    a = jnp.where(strict, a, 0.0) + jnp.eye(CHUNK, dtype=F32)
    rhs = jnp.concatenate([vc * bc[..., None], kb * jnp.exp(gc)[..., None]], axis=-1)
    sol = lax.linalg.triangular_solve(a, rhs, left_side=True, lower=True, unit_diagonal=True)
    uc, wc = sol[..., :dv], sol[..., dv:]
    qk = jnp.einsum('nbhcd,nbhsd->nbhcs', qc, kc) * decay

    def step(S, inp):
        q_i, k_i, u_i, w_i, g_i, qk_i = inp
        v_new = u_i - jnp.einsum('bhcd,bhde->bhce', w_i, S)
        o = (jnp.einsum('bhcd,bhde->bhce', q_i * jnp.exp(g_i)[..., None], S)
             + jnp.einsum('bhcs,bhse->bhce', qk_i, v_new))
        g_last = g_i[..., -1]
        S = (S * jnp.exp(g_last)[..., None, None]
             + jnp.einsum('bhcd,bhce->bhde', k_i * jnp.exp(g_last[..., None] - g_i)[..., None], v_new))
        return S, o

    S0 = jnp.zeros((B, H, dk, dv), F32)
    _, o = lax.scan(step, S0, (qc, kc, uc, wc, gc, qk))
    return o.transpose(1, 0, 3, 2, 4).reshape(B, L, H, dv)


def _l2norm(x):
    return x * lax.rsqrt(jnp.sum(x * x, axis=-1, keepdims=True) + EPS)


def _deltanet(p, gate_cols, conv_w, a_log, dt_bias, norm_w):
    B, L, _ = p.shape
    qkv = jax.nn.silu(_dwconv(p[..., :DN_CONV], conv_w))
    q = _l2norm(qkv[..., :DN_QK].reshape(B, L, DN_HEADS, DN_DK))
    k = _l2norm(qkv[..., DN_QK:2 * DN_QK].reshape(B, L, DN_HEADS, DN_DK))
    v = qkv[..., 2 * DN_QK:].reshape(B, L, DN_HEADS, DN_DV)
    z = p[..., DN_CONV:].reshape(B, L, DN_HEADS, DN_DV)
    a = gate_cols[..., :2 * DN_HEADS].reshape(B, L, 2, DN_HEADS)
    b = gate_cols[..., 2 * DN_HEADS:4 * DN_HEADS].reshape(B, L, 2, DN_HEADS)
    g = -jnp.exp(a_log) * jax.nn.softplus(a + dt_bias)
    beta = jax.nn.sigmoid(b)
    o_f = _chunk_gated_delta(q, k, v, g[:, :, 0], beta[:, :, 0])
    rev = lambda t: jnp.flip(t, axis=1)
    o_b = rev(_chunk_gated_delta(rev(q), rev(k), rev(v), rev(g[:, :, 1]), rev(beta[:, :, 1])))
    o = o_f + o_b
    o = o * lax.rsqrt(jnp.mean(o * o, axis=-1, keepdims=True) + EPS)
    o = o * norm_w * jax.nn.silu(z)
    return o.reshape(B, L, DN_W)


def _trunk(x, norm_mix_w, w_in, hy_conv_w, hy_conv_b, hy_pos_w1, hy_pos_b1, hy_pos_w2, hy_pos_b2,
           hy_pos_w3, hy_sin_freq, hy_bias, dn_conv_w, dn_a_log, dn_dt_bias, dn_norm_w, w_out,
           norm_ffn_w, w_router, b_router, w_gate_up, b_gate_up, w_down, b_down, norm_final_w):
    B, L, D = x.shape
    T = B * L
    x2d = x.reshape(T, D)
    p_hy, p_dn, p_g = _in_proj(x2d, norm_mix_w, w_in)
    y_hy = _hyena(p_hy.reshape(B, L, HY_IN), hy_conv_w, hy_conv_b, hy_pos_w1, hy_pos_b1, hy_pos_w2,
                  hy_pos_b2, hy_pos_w3, hy_sin_freq, hy_bias)
    y_dn = _deltanet(p_dn.reshape(B, L, DN_MAIN), p_g.reshape(B, L, LANES), dn_conv_w, dn_a_log,
                     dn_dt_bias, dn_norm_w)
    x1, h_bf, logits = _out_proj(x2d, y_hy.reshape(T, HY_CH), y_dn.reshape(T, DN_W), w_out,
                                 norm_ffn_w, w_router, b_router)
    m = _moe(h_bf, logits[:, :N_EXPERTS], w_gate_up, b_gate_up, w_down, b_down)
    return _final(x1, m, norm_final_w).reshape(B, L, D)


def kernel(x_prompt, x_sample, norm_mix_w, w_in, hy_conv_w, hy_conv_b, hy_pos_w1, hy_pos_b1, hy_pos_w2, hy_pos_b2, hy_pos_w3, hy_sin_freq, hy_bias, dn_conv_w, dn_a_log, dn_dt_bias, dn_norm_w, w_out, norm_ffn_w, w_router, b_router, w_gate_up, b_gate_up, w_down, b_down, norm_final_w):
    layer = (norm_mix_w[0], w_in[0], hy_conv_w[0], hy_conv_b[0], hy_pos_w1[0], hy_pos_b1[0],
             hy_pos_w2[0], hy_pos_b2[0], hy_pos_w3[0], hy_sin_freq[0], hy_bias[0], dn_conv_w[0],
             dn_a_log[0], dn_dt_bias[0], dn_norm_w[0], w_out[0], norm_ffn_w[0], w_router[0],
             b_router[0], w_gate_up[0], b_gate_up[0], w_down[0], b_down[0], norm_final_w)
    return (_trunk(x_prompt, *layer), _trunk(x_sample, *layer))
```

```python
import functools
import math

import jax
import jax.numpy as jnp
import numpy as np
from jax import lax
from jax.experimental import pallas as pl
from jax.experimental.pallas import tpu as pltpu

D_MODEL = 1024
HY_CH = 512
DN_HEADS = 4
DN_DK = 128
DN_DV = 128
DN_QK = DN_HEADS * DN_DK
DN_W = DN_HEADS * DN_DV
HY_IN = 3 * HY_CH
DN_CONV = 2 * DN_QK + DN_W
DN_MAIN = DN_CONV + DN_W
N_GATE = 4 * DN_HEADS
HY_EMB = 33
HY_BANDS = (HY_EMB - 1) // 2
HY_DECAY_TARGET = 1e-2
HY_FAST_PCT = 0.3
HY_SLOW_PCT = 1.5
CHUNK = 64
N_EXPERTS = 32
TOP_K = 4
D_FF = D_MODEL
SWIGLU_ALPHA = 1.702
SWIGLU_LIMIT = 7.0
MOE_BLOCK = 512
EPS = 1e-6

LANES = 128
SUB = 8
ROW_TILE = 512
FFT_CB = 256
DN_STEP_CHUNKS = 4
DEINT_COLS = 512
VMEM_LIMIT = 56 * 1024 * 1024

F32 = jnp.float32
BF16 = jnp.bfloat16

_NT = (((1,), (1,)), ((), ()))
_TN = (((0,), (0,)), ((), ()))


def _rms(x, g):
    return x * lax.rsqrt(jnp.mean(x * x, axis=-1, keepdims=True) + EPS) * g


def _in_proj_kernel(x_ref, g_ref, whyr, wdnr, wgr, ohy, odn, og):
    h = _rms(x_ref[...], g_ref[...]).astype(BF16)
    ohy[...] = jnp.dot(h, whyr[...], preferred_element_type=F32)
    odn[...] = jnp.dot(h, wdnr[...], preferred_element_type=F32)
    og[...] = jnp.dot(h, wgr[...], preferred_element_type=F32)


def _in_proj(x2d, g, w_hy, w_dn, w_g):
    T = x2d.shape[0]
    const = lambda i: (0, 0)
    row = lambda i: (i, 0)
    return pl.pallas_call(
        _in_proj_kernel,
        grid=(T // ROW_TILE,),
        in_specs=[pl.BlockSpec((ROW_TILE, D_MODEL), row),
                  pl.BlockSpec((1, D_MODEL), const),
                  pl.BlockSpec((D_MODEL, HY_IN), const),
                  pl.BlockSpec((D_MODEL, DN_MAIN), const),
                  pl.BlockSpec((D_MODEL, LANES), const)],
        out_specs=[pl.BlockSpec((ROW_TILE, HY_IN), row),
                   pl.BlockSpec((ROW_TILE, DN_MAIN), row),
                   pl.BlockSpec((ROW_TILE, LANES), row)],
        out_shape=[jax.ShapeDtypeStruct((T, HY_IN), F32),
                   jax.ShapeDtypeStruct((T, DN_MAIN), F32),
                   jax.ShapeDtypeStruct((T, LANES), F32)],
        compiler_params=pltpu.CompilerParams(dimension_semantics=("parallel",),
                                             vmem_limit_bytes=VMEM_LIMIT),
        name="in_proj",
    )(x2d, g.reshape(1, D_MODEL), w_hy, w_dn, w_g)


def _out_proj_kernel(x_ref, yh_ref, yd_ref, wh_ref, wd_ref, g_ref, wrh_ref, wrl_ref, br_ref,
                     x1_ref, h_ref, lg_ref):
    x1 = (x_ref[...]
          + jnp.dot(yh_ref[...].astype(BF16), wh_ref[...], preferred_element_type=F32)
          + jnp.dot(yd_ref[...].astype(BF16), wd_ref[...], preferred_element_type=F32))
    x1_ref[...] = x1
    h = _rms(x1, g_ref[...])
    h_hi = h.astype(BF16)
    h_lo = (h - h_hi.astype(F32)).astype(BF16)
    h_ref[...] = h_hi
    lg_ref[...] = (jnp.dot(h_hi, wrh_ref[...], preferred_element_type=F32)
                   + jnp.dot(h_lo, wrh_ref[...], preferred_element_type=F32)
                   + jnp.dot(h_hi, wrl_ref[...], preferred_element_type=F32)
                   + br_ref[...])


def _out_proj(x2d, y_hy, y_dn, w_oh, w_od, g, wr_hi, wr_lo, br):
    T = x2d.shape[0]
    const = lambda i: (0, 0)
    row = lambda i: (i, 0)
    return pl.pallas_call(
        _out_proj_kernel,
        grid=(T // ROW_TILE,),
        in_specs=[pl.BlockSpec((ROW_TILE, D_MODEL), row),
                  pl.BlockSpec((ROW_TILE, HY_CH), row),
                  pl.BlockSpec((ROW_TILE, DN_W), row),
                  pl.BlockSpec((HY_CH, D_MODEL), const),
                  pl.BlockSpec((DN_W, D_MODEL), const),
                  pl.BlockSpec((1, D_MODEL), const),
                  pl.BlockSpec((D_MODEL, LANES), const),
                  pl.BlockSpec((D_MODEL, LANES), const),
                  pl.BlockSpec((1, LANES), const)],
        out_specs=[pl.BlockSpec((ROW_TILE, D_MODEL), row),
                   pl.BlockSpec((ROW_TILE, D_MODEL), row),
                   pl.BlockSpec((ROW_TILE, LANES), row)],
        out_shape=[jax.ShapeDtypeStruct((T, D_MODEL), F32),
                   jax.ShapeDtypeStruct((T, D_MODEL), BF16),
                   jax.ShapeDtypeStruct((T, LANES), F32)],
        compiler_params=pltpu.CompilerParams(dimension_semantics=("parallel",),
                                             vmem_limit_bytes=VMEM_LIMIT),
        name="out_proj_router",
    )(x2d, y_hy, y_dn, w_oh, w_od, g.reshape(1, D_MODEL), wr_hi, wr_lo, br)


def _deint_kernel(w_ref, p_ref, og_ref, ol_ref):
    half = DEINT_COLS // 2
    sel = jnp.dot(w_ref[0].astype(BF16), p_ref[...], preferred_element_type=F32)
    og_ref[0] = sel[:, :half].astype(BF16)
    ol_ref[0] = sel[:, half:].astype(BF16)


def _deinterleave(w_gate_up):
    half = DEINT_COLS // 2
    r = np.arange(DEINT_COLS)[:, None]
    c = np.arange(DEINT_COLS)[None, :]
    perm = jnp.asarray(np.where(c < half, r == 2 * c, r == 2 * (c - half) + 1), BF16)
    out = jax.ShapeDtypeStruct((N_EXPERTS, D_MODEL, D_FF), BF16)
    return pl.pallas_call(
        _deint_kernel,
        grid=(N_EXPERTS, 2 * D_FF // DEINT_COLS),
        in_specs=[pl.BlockSpec((1, D_MODEL, DEINT_COLS), lambda e, j: (e, 0, j)),
                  pl.BlockSpec((DEINT_COLS, DEINT_COLS), lambda e, j: (0, 0))],
        out_specs=[pl.BlockSpec((1, D_MODEL, half), lambda e, j: (e, 0, j)),
                   pl.BlockSpec((1, D_MODEL, half), lambda e, j: (e, 0, j))],
        out_shape=[out, out],
        compiler_params=pltpu.CompilerParams(dimension_semantics=("parallel", "parallel")),
        name="deinterleave_gate_up",
    )(w_gate_up, perm)


def _expert_kernel(be_ref, nb_ref, xb_ref, gate_ref, wg_ref, wl_ref, bg_ref, bl_ref, wd_ref, bd_ref,
                   y_ref):
    i = pl.program_id(0)

    @pl.when(i < nb_ref[0])
    def _():
        xb = xb_ref[...]
        hg = jnp.dot(xb, wg_ref[0], preferred_element_type=F32) + bg_ref[0]
        hl = jnp.dot(xb, wl_ref[0], preferred_element_type=F32) + bl_ref[0]
        x_glu = jnp.minimum(hg, SWIGLU_LIMIT)
        x_lin = jnp.clip(hl, -SWIGLU_LIMIT, SWIGLU_LIMIT)
        act = x_glu * jax.nn.sigmoid(SWIGLU_ALPHA * x_glu) * (x_lin + 1.0)
        y = jnp.dot(act.astype(BF16), wd_ref[0], preferred_element_type=F32) + bd_ref[0]
        y_ref[...] = y * gate_ref[...]

    @pl.when(i >= nb_ref[0])
    def _():
        y_ref[...] = jnp.zeros_like(y_ref)


def _expert_mlp(xb, row_gate, block_e, n_used, wg, wl, bg, bl, wd, bd):
    n_rows = xb.shape[0]
    n_blocks = n_rows // MOE_BLOCK
    rowm = lambda i, be, nb: (i, 0)
    exp3 = lambda i, be, nb: (be[i], 0, 0)
    grid_spec = pltpu.PrefetchScalarGridSpec(
        num_scalar_prefetch=2,
        grid=(n_blocks,),
        in_specs=[pl.BlockSpec((MOE_BLOCK, D_MODEL), rowm),
                  pl.BlockSpec((MOE_BLOCK, 1), rowm),
                  pl.BlockSpec((1, D_MODEL, D_FF), exp3),
                  pl.BlockSpec((1, D_MODEL, D_FF), exp3),
                  pl.BlockSpec((1, 1, D_FF), exp3),
                  pl.BlockSpec((1, 1, D_FF), exp3),
                  pl.BlockSpec((1, D_FF, D_MODEL), exp3),
                  pl.BlockSpec((1, 1, D_MODEL), exp3)],
        out_specs=pl.BlockSpec((MOE_BLOCK, D_MODEL), rowm),
    )
    return pl.pallas_call(
        _expert_kernel,
        grid_spec=grid_spec,
        out_shape=jax.ShapeDtypeStruct((n_rows, D_MODEL), F32),
        compiler_params=pltpu.CompilerParams(dimension_semantics=("arbitrary",),
                                             vmem_limit_bytes=VMEM_LIMIT),
        name="expert_mlp",
    )(block_e, n_used, xb, row_gate.reshape(n_rows, 1), wg, wl, bg, bl, wd, bd)


def _moe(h_bf, logits, wg, wl, bg, bl, wd, bd):
    T = h_bf.shape[0]
    TK = T * TOP_K
    top_vals, top_idx = lax.top_k(logits, TOP_K)
    gates = jax.nn.softmax(top_vals, axis=-1)
    sel = jnp.sum(jax.nn.one_hot(top_idx, N_EXPERTS, dtype=jnp.int32), axis=1)
    before = jnp.cumsum(sel, axis=0) - sel
    counts = jnp.sum(sel, axis=0)
    padded = (counts + MOE_BLOCK - 1) // MOE_BLOCK * MOE_BLOCK
    pad_end = jnp.cumsum(padded)
    pad_start = pad_end - padded
    rank = jnp.take_along_axis(before, top_idx, axis=1)
    dest = (pad_start[top_idx] + rank).astype(jnp.int32)
    n_blocks = (TK + MOE_BLOCK - 1) // MOE_BLOCK + N_EXPERTS
    n_rows = n_blocks * MOE_BLOCK
    tok = jnp.broadcast_to(jnp.arange(T, dtype=jnp.int32)[:, None], (T, TOP_K))
    row_tok = jnp.zeros((n_rows,), jnp.int32).at[dest.reshape(-1)].set(tok.reshape(-1))
    row_gate = jnp.zeros((n_rows,), F32).at[dest.reshape(-1)].set(gates.reshape(-1))
    block_start = jnp.arange(n_blocks, dtype=jnp.int32) * MOE_BLOCK
    block_e = jnp.minimum(jnp.searchsorted(pad_end, block_start, side='right'),
                          N_EXPERTS - 1).astype(jnp.int32)
    n_used = (pad_end[-1] // MOE_BLOCK).astype(jnp.int32).reshape(1)
    xb = h_bf[row_tok]
    yb = _expert_mlp(xb, row_gate, block_e, n_used, wg, wl, bg, bl, wd, bd)
    return jnp.sum(yb[dest], axis=1)


def _final_kernel(x_ref, m_ref, g_ref, o_ref):
    o_ref[...] = _rms(x_ref[...] + m_ref[...], g_ref[...])


def _final(x1, m, g):
    T = x1.shape[0]
    row = lambda i: (i, 0)
    return pl.pallas_call(
        _final_kernel,
        grid=(T // ROW_TILE,),
        in_specs=[pl.BlockSpec((ROW_TILE, D_MODEL), row),
                  pl.BlockSpec((ROW_TILE, D_MODEL), row),
                  pl.BlockSpec((1, D_MODEL), lambda i: (0, 0))],
        out_specs=pl.BlockSpec((ROW_TILE, D_MODEL), row),
        out_shape=jax.ShapeDtypeStruct((T, D_MODEL), F32),
        compiler_params=pltpu.CompilerParams(dimension_semantics=("parallel",)),
        name="final_norm",
    )(x1, m, g.reshape(1, D_MODEL))


def _fft_tables(nc, nf, rows_in, real_in):
    n = nc * nf
    kc = np.arange(nc, dtype=np.float64)
    a1 = 2.0 * np.pi * np.outer(kc, np.arange(rows_in)) / nc
    c1, s1 = np.cos(a1), np.sin(a1)
    l1 = np.concatenate([c1, -s1], 0) if real_in else np.block([[c1, s1], [-s1, c1]])
    a2 = 2.0 * np.pi * np.outer(np.arange(nf), np.arange(nf)) / nf
    c2, s2 = np.cos(a2), np.sin(a2)
    m2 = np.block([[c2, s2], [-s2, c2]])
    m2i = np.block([[c2, -s2], [s2, c2]])
    a3 = 2.0 * np.pi * np.outer(np.arange(nc // 2), kc) / nc
    c3, s3 = np.cos(a3), np.sin(a3)
    l3 = np.block([[c3, -s3], [s3, c3]]) / n
    return (jnp.asarray(l1, BF16), jnp.asarray(m2, BF16), jnp.asarray(m2i, BF16), jnp.asarray(l3, BF16))


def _twiddle(nc, nf):
    n = nc * nf
    ph = (jnp.arange(nf, dtype=jnp.int32)[:, None] * jnp.arange(nc, dtype=jnp.int32)[None, :]) % n
    ang = ph.astype(F32) * (2.0 * math.pi / n)
    tw = jnp.stack([jnp.cos(ang), jnp.sin(ang)], axis=1)
    return jnp.broadcast_to(tw[..., None], (nf, 2, nc, LANES))


def _lane_tile(t, width):
    return t if width == LANES else jnp.concatenate([t] * (width // LANES), axis=1)


def _fft1_kernel(z_ref, l1_ref, tw_ref, o_ref, *, nc, real_in):
    cb = o_ref.shape[-1]
    for j in range(SUB):
        if real_in:
            rhs = z_ref[:, j, :].astype(BF16)
        else:
            rhs = jnp.concatenate([z_ref[0, :, j, :], z_ref[1, :, j, :]], axis=0).astype(BF16)
        a = jnp.dot(l1_ref[...], rhs, preferred_element_type=F32)
        ar, ai = a[:nc], a[nc:]
        twr = _lane_tile(tw_ref[j, 0], cb)
        twi = _lane_tile(tw_ref[j, 1], cb)
        o_ref[j, 0] = ar * twr + ai * twi
        o_ref[j, 1] = ai * twr - ar * twi


def _fft1(z, l1, tw, nc, nf, real_in):
    C = z.shape[-1]
    cb = FFT_CB
    if real_in:
        P = 1
        z_spec = pl.BlockSpec((nc, SUB, cb), lambda p, j, c: (0, j, c))
    else:
        P = z.shape[0]
        z_spec = pl.BlockSpec((None, 2, nc // 2, SUB, cb), lambda p, j, c: (p, 0, 0, j, c))
    return pl.pallas_call(
        functools.partial(_fft1_kernel, nc=nc, real_in=real_in),
        grid=(P, nf // SUB, C // cb),
        in_specs=[z_spec,
                  pl.BlockSpec(l1.shape, lambda p, j, c: (0, 0)),
                  pl.BlockSpec((SUB, 2, nc, LANES), lambda p, j, c: (j, 0, 0, 0))],
        out_specs=pl.BlockSpec((None, SUB, 2, nc, cb), lambda p, j, c: (p, j, 0, 0, c)),
        out_shape=jax.ShapeDtypeStruct((P, nf, 2, nc, C), F32),
        compiler_params=pltpu.CompilerParams(dimension_semantics=("parallel", "parallel", "parallel"),
                                             vmem_limit_bytes=VMEM_LIMIT),
        name="hyena_fft_stage1",
    )(z, l1, tw)


def _fft2_filter_kernel(a_ref, m2_ref, k_ref, *, nf):
    for j in range(SUB):
        rhs = jnp.concatenate([a_ref[:, 0, j, :], a_ref[:, 1, j, :]], axis=0).astype(BF16)
        x = jnp.dot(m2_ref[...], rhs, preferred_element_type=F32)
        k_ref[j, 0] = x[:nf]
        k_ref[j, 1] = x[nf:]


def _fft2_filter(a, m2, nc, nf):
    C = a.shape[-1]
    cb = FFT_CB
    return pl.pallas_call(
        functools.partial(_fft2_filter_kernel, nf=nf),
        grid=(nc // SUB, C // cb),
        in_specs=[pl.BlockSpec((None, nf, 2, SUB, cb), lambda k, c: (0, 0, 0, k, c)),
                  pl.BlockSpec(m2.shape, lambda k, c: (0, 0))],
        out_specs=pl.BlockSpec((SUB, 2, nf, cb), lambda k, c: (k, 0, 0, c)),
        out_shape=jax.ShapeDtypeStruct((nc, 2, nf, C), F32),
        compiler_params=pltpu.CompilerParams(dimension_semantics=("parallel", "parallel"),
                                             vmem_limit_bytes=VMEM_LIMIT),
        name="hyena_filter_spectrum",
    )(a, m2)


def _fft2_kernel(a_ref, k_ref, m2_ref, m2i_ref, o_ref, *, nf):
    for j in range(SUB):
        rhs = jnp.concatenate([a_ref[:, 0, j, :], a_ref[:, 1, j, :]], axis=0).astype(BF16)
        x = jnp.dot(m2_ref[...], rhs, preferred_element_type=F32)
        xr, xi = x[:nf], x[nf:]
        kr, ki = k_ref[j, 0], k_ref[j, 1]
        y = jnp.concatenate([xr * kr - xi * ki, xr * ki + xi * kr], axis=0).astype(BF16)
        b = jnp.dot(m2i_ref[...], y, preferred_element_type=F32)
        o_ref[:, 0, j, :] = b[:nf]
        o_ref[:, 1, j, :] = b[nf:]


def _fft2(a, kspec, m2, m2i, nc, nf):
    P, C = a.shape[0], a.shape[-1]
    cb = FFT_CB
    blk = pl.BlockSpec((None, nf, 2, SUB, cb), lambda p, k, c: (p, 0, 0, k, c))
    return pl.pallas_call(
        functools.partial(_fft2_kernel, nf=nf),
        grid=(P, nc // SUB, C // cb),
        in_specs=[blk,
                  pl.BlockSpec((SUB, 2, nf, cb), lambda p, k, c: (k, 0, 0, c)),
                  pl.BlockSpec(m2.shape, lambda p, k, c: (0, 0)),
                  pl.BlockSpec(m2i.shape, lambda p, k, c: (0, 0))],
        out_specs=blk,
        out_shape=jax.ShapeDtypeStruct(a.shape, F32),
        compiler_params=pltpu.CompilerParams(dimension_semantics=("parallel", "parallel", "parallel"),
                                             vmem_limit_bytes=VMEM_LIMIT),
        name="hyena_fft_stage2",
    )(a, kspec, m2, m2i)


def _fft3_kernel(b_ref, tw_ref, l3_ref, s_ref, x0_ref, bias_ref, o_ref, *, nc):
    cb = o_ref.shape[-1]
    half = nc // 2
    for j in range(SUB):
        br, bi = b_ref[j, 0], b_ref[j, 1]
        twr = _lane_tile(tw_ref[j, 0], cb)
        twi = _lane_tile(tw_ref[j, 1], cb)
        rhs = jnp.concatenate([br * twr - bi * twi, br * twi + bi * twr], axis=0).astype(BF16)
        y = jnp.dot(l3_ref[...], rhs, preferred_element_type=F32)
        for q in range(2):
            s = s_ref[q, :, j, :]
            o_ref[q, :, j, :] = x0_ref[q, :, j, :] * (y[q * half:(q + 1) * half] + s * bias_ref[...])


def _fft3(b, tw, l3, s5, x05, bias, nc, nf):
    P, C = b.shape[0], b.shape[-1]
    cb = FFT_CB
    seq = pl.BlockSpec((None, 2, nc // 2, SUB, cb), lambda p, j, c: (p, 0, 0, j, c))
    return pl.pallas_call(
        functools.partial(_fft3_kernel, nc=nc),
        grid=(P, nf // SUB, C // cb),
        in_specs=[pl.BlockSpec((None, SUB, 2, nc, cb), lambda p, j, c: (p, j, 0, 0, c)),
                  pl.BlockSpec((SUB, 2, nc, LANES), lambda p, j, c: (j, 0, 0, 0)),
                  pl.BlockSpec(l3.shape, lambda p, j, c: (0, 0)),
                  seq, seq,
                  pl.BlockSpec((1, cb), lambda p, j, c: (0, c))],
        out_specs=seq,
        out_shape=jax.ShapeDtypeStruct(s5.shape, F32),
        compiler_params=pltpu.CompilerParams(dimension_semantics=("parallel", "parallel", "parallel"),
                                             vmem_limit_bytes=VMEM_LIMIT),
        name="hyena_fft_stage3",
    )(b, tw, l3, s5, x05, bias)


def _fft_split(n):
    nf = 1 << (int(math.log2(n)) // 2)
    return n // nf, nf


def _hyena_conv(s, x0, k2, bias):
    B, L, C = s.shape
    nc, nf = _fft_split(2 * L)
    tw = _twiddle(nc, nf)
    l1f, m2, m2i, l3 = _fft_tables(nc, nf, nc, True)
    l1 = _fft_tables(nc, nf, nc // 2, False)[0]
    kspec = _fft2_filter(_fft1(k2.reshape(nc, nf, C), l1f, tw, nc, nf, True), m2, nc, nf)
    s5 = s.reshape(B // 2, 2, nc // 2, nf, C)
    x05 = x0.reshape(B // 2, 2, nc // 2, nf, C)
    a = _fft1(s5, l1, tw, nc, nf, False)
    b = _fft2(a, kspec, m2, m2i, nc, nf)
    return _fft3(b, tw, l3, s5, x05, bias.reshape(1, C), nc, nf).reshape(B, L, C)


def _dwconv(x, w):
    L = x.shape[1]
    xp = jnp.pad(x, ((0, 0), (1, 1), (0, 0)))
    return xp[:, 0:L] * w[0] + xp[:, 1:L + 1] * w[1] + xp[:, 2:L + 2] * w[2]


def _hyena_filter(L, w1, b1, w2, b2, w3, freq):
    pos = jnp.arange(L, dtype=F32)
    t = jnp.linspace(0.0, 1.0, L, dtype=F32)[:, None]
    bands = jnp.linspace(1e-4, HY_BANDS - 1, HY_BANDS, dtype=F32)
    ang = (2.0 * math.pi / L) * pos[:, None] * bands[None, :]
    z = jnp.concatenate([t, jnp.cos(ang), -jnp.sin(ang)], axis=-1)
    h = jnp.sin(freq * (z @ w1 + b1))
    h = jnp.sin(freq * (h @ w2 + b2))
    h = h @ w3
    deltas = jnp.abs(jnp.linspace(math.log(HY_DECAY_TARGET) / HY_SLOW_PCT,
                                  math.log(HY_DECAY_TARGET) / HY_FAST_PCT, HY_CH, dtype=F32))
    window = jnp.exp(-t * deltas[None, :])
    h_f = h[:, :HY_CH] * window
    h_b = h[:, HY_CH:] * window
    l1 = jnp.sum(jnp.abs(h_f), axis=0) + jnp.sum(jnp.abs(h_b[1:]), axis=0)
    k2 = jnp.concatenate([h_f, jnp.zeros((1, HY_CH), F32), h_b[:0:-1]], axis=0)
    return k2 / l1


def _hyena(p, conv_w, conv_b, w1, b1, w2, b2, w3, freq, bias):
    B, L, _ = p.shape
    u = _dwconv(p, conv_w) + conv_b
    x0, x1, v = jnp.split(u, 3, axis=-1)
    k2 = _hyena_filter(L, w1, b1, w2, b2, w3, freq)
    return _hyena_conv(x1 * v, x0, k2, bias)


def _split3(x):
    x1 = x.astype(BF16)
    r = x - x1.astype(F32)
    x2 = r.astype(BF16)
    x3 = (r - x2.astype(F32)).astype(BF16)
    return x1, x2, x3


def _delta_kernel(q_ref, k_ref, v_ref, g_ref, o_ref, s_ref, *, n_sub):
    d = pl.program_id(0)

    @pl.when(pl.program_id(2) == 0)
    def _():
        s_ref[...] = jnp.zeros_like(s_ref)

    row = lax.broadcasted_iota(jnp.int32, (CHUNK, CHUNK), 0)
    col = lax.broadcasted_iota(jnp.int32, (CHUNK, CHUNK), 1)
    diff = (row - col) * (1 - 2 * d)
    incl = diff >= 0
    strict = diff > 0
    tri = jnp.where(incl, 1.0, 0.0).astype(BF16)
    eye_f = jnp.where(diff == 0, 1.0, 0.0).astype(F32)

    def chunk_body(ci, carry):
        c_eff = ci + d * (n_sub - 1 - 2 * ci)
        r0 = pl.multiple_of(c_eff * CHUNK, CHUNK)
        gates = g_ref[pl.ds(r0, CHUNK), :]
        g1, g2, g3 = _split3(gates)
        gc_all = (jnp.dot(tri, g1, preferred_element_type=F32)
                  + jnp.dot(tri, g2, preferred_element_type=F32)
                  + jnp.dot(tri, g3, preferred_element_type=F32))
        gsum_all = jnp.sum(gates, axis=0, keepdims=True)
        for h in range(DN_HEADS):
            lo = h * DN_DK
            kh = k_ref[pl.ds(r0, CHUNK), lo:lo + DN_DK]
            qh = q_ref[pl.ds(r0, CHUNK), lo:lo + DN_DK]
            vh = v_ref[pl.ds(r0, CHUNK), lo:lo + DN_DK]
            beta = gates[:, DN_HEADS + h:DN_HEADS + h + 1]
            gc = gc_all[:, h:h + 1]
            g_last = gsum_all[:, h:h + 1]
            gc_b = jnp.broadcast_to(gc, (CHUNK, CHUNK))
            gc_row = jnp.sum(gc_b * eye_f, axis=0, keepdims=True)
            decay = jnp.where(incl, jnp.exp(jnp.minimum(gc_b - gc_row, 0.0)), 0.0)
            kb = kh * beta
            kh16 = kh.astype(BF16)
            a = lax.dot_general(kb.astype(BF16), kh16, _NT, preferred_element_type=F32) * decay
            a = jnp.where(strict, a, 0.0)
            qk = lax.dot_general(qh.astype(BF16), kh16, _NT, preferred_element_type=F32) * decay
            t = eye_f - a
            pw = a
            for _ in range(5):
                pw16 = pw.astype(BF16)
                pw = jnp.dot(pw16, pw16, preferred_element_type=F32)
                t = t + jnp.dot(t.astype(BF16), pw.astype(BF16), preferred_element_type=F32)
            t16 = t.astype(BF16)
            u = jnp.dot(t16, (vh * beta).astype(BF16), preferred_element_type=F32)
            w = jnp.dot(t16, (kb * jnp.exp(gc)).astype(BF16), preferred_element_type=F32)
            s_h = s_ref[h]
            s16 = s_h.astype(BF16)
            v_new = u - jnp.dot(w.astype(BF16), s16, preferred_element_type=F32)
            vn16 = v_new.astype(BF16)
            o = (jnp.dot((qh * jnp.exp(gc)).astype(BF16), s16, preferred_element_type=F32)
                 + jnp.dot(qk.astype(BF16), vn16, preferred_element_type=F32))
            kd = kh * jnp.exp(g_last - gc)
            s_ref[h] = (s_h * jnp.exp(g_last)
                        + lax.dot_general(kd.astype(BF16), vn16, _TN, preferred_element_type=F32))
            o_ref[pl.ds(r0, CHUNK), lo:lo + DN_DK] = o
        return carry

    lax.fori_loop(0, n_sub, chunk_body, 0)


def _delta_scan(q, k, v, gates):
    B, L, _ = q.shape
    n_sub = DN_STEP_CHUNKS
    rows = n_sub * CHUNK
    nblk = L // rows
    seq = pl.BlockSpec((None, rows, DN_W), lambda d, b, j: (b, j + d * (nblk - 1 - 2 * j), 0))
    dseq = lambda width: pl.BlockSpec((None, None, rows, width),
                                      lambda d, b, j: (d, b, j + d * (nblk - 1 - 2 * j), 0))
    return pl.pallas_call(
        functools.partial(_delta_kernel, n_sub=n_sub),
        grid=(2, B, nblk),
        in_specs=[seq, seq, seq, dseq(LANES)],
        out_specs=dseq(DN_W),
        out_shape=jax.ShapeDtypeStruct((2, B, L, DN_W), F32),
        scratch_shapes=[pltpu.VMEM((DN_HEADS, DN_DK, DN_DV), F32)],
        compiler_params=pltpu.CompilerParams(dimension_semantics=("parallel", "parallel", "arbitrary"),
                                             vmem_limit_bytes=VMEM_LIMIT),
        name="delta_scan",
    )(q, k, v, gates)


def _l2norm(x):
    return x * lax.rsqrt(jnp.sum(x * x, axis=-1, keepdims=True) + EPS)


def _deltanet(p, gate_cols, conv_w, a_log, dt_bias, norm_w):
    B, L, _ = p.shape
    qkv = jax.nn.silu(_dwconv(p[..., :DN_CONV], conv_w))
    q = _l2norm(qkv[..., :DN_QK].reshape(B, L, DN_HEADS, DN_DK)) * (DN_DK ** -0.5)
    k = _l2norm(qkv[..., DN_QK:2 * DN_QK].reshape(B, L, DN_HEADS, DN_DK))
    v = qkv[..., 2 * DN_QK:]
    z = p[..., DN_CONV:].reshape(B, L, DN_HEADS, DN_DV)
    a = gate_cols[..., :2 * DN_HEADS].reshape(B, L, 2, DN_HEADS)
    b = gate_cols[..., 2 * DN_HEADS:4 * DN_HEADS].reshape(B, L, 2, DN_HEADS)
    g = -jnp.exp(a_log) * jax.nn.softplus(a + dt_bias)
    beta = jax.nn.sigmoid(b)
    gates = jnp.moveaxis(jnp.concatenate([g, beta], axis=-1), 2, 0)
    gates = jnp.pad(gates, ((0, 0), (0, 0), (0, 0), (0, LANES - 2 * DN_HEADS)))
    o2 = _delta_scan(q.reshape(B, L, DN_QK), k.reshape(B, L, DN_QK), v, gates)
    o = (o2[0] + o2[1]).reshape(B, L, DN_HEADS, DN_DV)
    o = o * lax.rsqrt(jnp.mean(o * o, axis=-1, keepdims=True) + EPS)
    o = o * norm_w * jax.nn.silu(z)
    return o.reshape(B, L, DN_W)


def _trunk(x, w, hy_conv_w, hy_conv_b, hy_pos_w1, hy_pos_b1, hy_pos_w2, hy_pos_b2, hy_pos_w3,
           hy_sin_freq, hy_bias, dn_conv_w, dn_a_log, dn_dt_bias, dn_norm_w):
    B, L, D = x.shape
    T = B * L
    x2d = x.reshape(T, D)
    p_hy, p_dn, p_g = _in_proj(x2d, w["norm_mix"], w["in_hy"], w["in_dn"], w["in_gate"])
    y_hy = _hyena(p_hy.reshape(B, L, HY_IN), hy_conv_w, hy_conv_b, hy_pos_w1, hy_pos_b1, hy_pos_w2,
                  hy_pos_b2, hy_pos_w3, hy_sin_freq, hy_bias)
    y_dn = _deltanet(p_dn.reshape(B, L, DN_MAIN), p_g.reshape(B, L, LANES), dn_conv_w, dn_a_log,
                     dn_dt_bias, dn_norm_w)
    x1, h_bf, logits = _out_proj(x2d, y_hy.reshape(T, HY_CH), y_dn.reshape(T, DN_W), w["out_hy"],
                                 w["out_dn"], w["norm_ffn"], w["router_hi"], w["router_lo"],
                                 w["router_b"])
    m = _moe(h_bf, logits[:, :N_EXPERTS], w["wg"], w["wl"], w["bg"], w["bl"], w["wd"], w["bd"])
    return _final(x1, m, w["norm_final"]).reshape(B, L, D)


def kernel(x_prompt, x_sample, norm_mix_w, w_in, hy_conv_w, hy_conv_b, hy_pos_w1, hy_pos_b1, hy_pos_w2, hy_pos_b2, hy_pos_w3, hy_sin_freq, hy_bias, dn_conv_w, dn_a_log, dn_dt_bias, dn_norm_w, w_out, norm_ffn_w, w_router, b_router, w_gate_up, b_gate_up, w_down, b_down, norm_final_w):
    w_in16 = w_in[0].astype(BF16)
    w_out16 = w_out[0].astype(BF16)
    wr = jnp.pad(w_router[0], ((0, 0), (0, LANES - N_EXPERTS)))
    wr_hi = wr.astype(BF16)
    wg, wl = _deinterleave(w_gate_up[0])
    w = {
        "norm_mix": norm_mix_w[0],
        "in_hy": w_in16[:, :HY_IN],
        "in_dn": w_in16[:, HY_IN:HY_IN + DN_MAIN],
        "in_gate": jnp.pad(w_in16[:, HY_IN + DN_MAIN:], ((0, 0), (0, LANES - N_GATE))),
        "out_hy": w_out16[:HY_CH],
        "out_dn": w_out16[HY_CH:],
        "norm_ffn": norm_ffn_w[0],
        "router_hi": wr_hi,
        "router_lo": (wr - wr_hi.astype(F32)).astype(BF16),
        "router_b": jnp.pad(b_router[0], (0, LANES - N_EXPERTS)).reshape(1, LANES),
        "wg": wg,
        "wl": wl,
        "bg": b_gate_up[0][:, 0::2].reshape(N_EXPERTS, 1, D_FF),
        "bl": b_gate_up[0][:, 1::2].reshape(N_EXPERTS, 1, D_FF),
        "wd": w_down[0].astype(BF16),
        "bd": b_down[0].reshape(N_EXPERTS, 1, D_MODEL),
        "norm_final": norm_final_w,
    }
    mix = (hy_conv_w[0], hy_conv_b[0], hy_pos_w1[0], hy_pos_b1[0], hy_pos_w2[0], hy_pos_b2[0],
           hy_pos_w3[0], hy_sin_freq[0], hy_bias[0], dn_conv_w[0], dn_a_log[0], dn_dt_bias[0],
           dn_norm_w[0])
    return (_trunk(x_prompt, w, *mix), _trunk(x_sample, w, *mix))
```

```python
import functools
import math

import jax
import jax.numpy as jnp
import numpy as np
from jax import lax
from jax.experimental import pallas as pl
from jax.experimental.pallas import tpu as pltpu

D_MODEL = 1024
HY_CH = 512
DN_HEADS = 4
DN_DK = 128
DN_DV = 128
DN_QK = DN_HEADS * DN_DK
DN_W = DN_HEADS * DN_DV
HY_IN = 3 * HY_CH
DN_CONV = 2 * DN_QK + DN_W
DN_MAIN = DN_CONV + DN_W
N_GATE = 4 * DN_HEADS
HY_EMB = 33
HY_BANDS = (HY_EMB - 1) // 2
HY_DECAY_TARGET = 1e-2
HY_FAST_PCT = 0.3
HY_SLOW_PCT = 1.5
CHUNK = 64
N_EXPERTS = 32
TOP_K = 4
D_FF = D_MODEL
SWIGLU_ALPHA = 1.702
SWIGLU_LIMIT = 7.0
MOE_BLOCK = 512
EPS = 1e-6

LANES = 128
SUB = 8
ROW_TILE = 512
FFT_CB = 256
DN_STEP_CHUNKS = 4
DEINT_COLS = 512
DISPATCH_TILE = 512
COMBINE_TILE = 256
VMEM_LIMIT = 56 * 1024 * 1024

F32 = jnp.float32
BF16 = jnp.bfloat16

_NT = (((1,), (1,)), ((), ()))


def _rms(x, g):
    return x * lax.rsqrt(jnp.mean(x * x, axis=-1, keepdims=True) + EPS) * g


def _in_proj_kernel(x_ref, g_ref, whyr, wdnr, wgr, ohy, odn, og):
    h = _rms(x_ref[...], g_ref[...]).astype(BF16)
    ohy[...] = jnp.dot(h, whyr[...], preferred_element_type=F32)
    odn[...] = jnp.dot(h, wdnr[...], preferred_element_type=F32)
    og[...] = jnp.dot(h, wgr[...], preferred_element_type=F32)


def _in_proj(x2d, g, w_hy, w_dn, w_g):
    T = x2d.shape[0]
    const = lambda i: (0, 0)
    row = lambda i: (i, 0)
    return pl.pallas_call(
        _in_proj_kernel,
        grid=(T // ROW_TILE,),
        in_specs=[pl.BlockSpec((ROW_TILE, D_MODEL), row),
                  pl.BlockSpec((1, D_MODEL), const),
                  pl.BlockSpec((D_MODEL, HY_IN), const),
                  pl.BlockSpec((D_MODEL, DN_MAIN), const),
                  pl.BlockSpec((D_MODEL, LANES), const)],
        out_specs=[pl.BlockSpec((ROW_TILE, HY_IN), row),
                   pl.BlockSpec((ROW_TILE, DN_MAIN), row),
                   pl.BlockSpec((ROW_TILE, LANES), row)],
        out_shape=[jax.ShapeDtypeStruct((T, HY_IN), F32),
                   jax.ShapeDtypeStruct((T, DN_MAIN), F32),
                   jax.ShapeDtypeStruct((T, LANES), F32)],
        compiler_params=pltpu.CompilerParams(dimension_semantics=("parallel",),
                                             vmem_limit_bytes=VMEM_LIMIT),
        name="in_proj",
    )(x2d, g.reshape(1, D_MODEL), w_hy, w_dn, w_g)


def _out_proj_kernel(x_ref, yh_ref, of_ref, ob_ref, z_ref, nw_ref, wh_ref, wd_ref, g_ref, wrh_ref,
                     wrl_ref, br_ref, x1_ref, h_ref, lg_ref):
    o = of_ref[...] + ob_ref[...]
    heads = []
    for hd in range(DN_HEADS):
        oh = o[:, hd * DN_DV:(hd + 1) * DN_DV]
        heads.append(oh * lax.rsqrt(jnp.mean(oh * oh, axis=-1, keepdims=True) + EPS))
    z = z_ref[...]
    y_dn = jnp.concatenate(heads, axis=1) * nw_ref[...] * (z * jax.nn.sigmoid(z))
    x1 = (x_ref[...]
          + jnp.dot(yh_ref[...].astype(BF16), wh_ref[...], preferred_element_type=F32)
          + jnp.dot(y_dn.astype(BF16), wd_ref[...], preferred_element_type=F32))
    x1_ref[...] = x1
    h = _rms(x1, g_ref[...])
    h_hi = h.astype(BF16)
    h_lo = (h - h_hi.astype(F32)).astype(BF16)
    h_ref[...] = h
    lg_ref[...] = (jnp.dot(h_hi, wrh_ref[...], preferred_element_type=F32)
                   + jnp.dot(h_lo, wrh_ref[...], preferred_element_type=F32)
                   + jnp.dot(h_hi, wrl_ref[...], preferred_element_type=F32)
                   + br_ref[...])


def _out_proj(x2d, y_hy, o_f, o_b, p_dn, dn_norm_w, w_oh, w_od, g, wr_hi, wr_lo, br):
    T = x2d.shape[0]
    const = lambda i: (0, 0)
    row = lambda i: (i, 0)
    return pl.pallas_call(
        _out_proj_kernel,
        grid=(T // ROW_TILE,),
        in_specs=[pl.BlockSpec((ROW_TILE, D_MODEL), row),
                  pl.BlockSpec((ROW_TILE, HY_CH), row),
                  pl.BlockSpec((ROW_TILE, DN_W), row),
                  pl.BlockSpec((ROW_TILE, DN_W), row),
                  pl.BlockSpec((ROW_TILE, DN_W), lambda i: (i, DN_CONV // DN_W)),
                  pl.BlockSpec((1, DN_W), const),
                  pl.BlockSpec((HY_CH, D_MODEL), const),
                  pl.BlockSpec((DN_W, D_MODEL), const),
                  pl.BlockSpec((1, D_MODEL), const),
                  pl.BlockSpec((D_MODEL, LANES), const),
                  pl.BlockSpec((D_MODEL, LANES), const),
                  pl.BlockSpec((1, LANES), const)],
        out_specs=[pl.BlockSpec((ROW_TILE, D_MODEL), row),
                   pl.BlockSpec((ROW_TILE, D_MODEL), row),
                   pl.BlockSpec((ROW_TILE, LANES), row)],
        out_shape=[jax.ShapeDtypeStruct((T, D_MODEL), F32),
                   jax.ShapeDtypeStruct((T, D_MODEL), F32),
                   jax.ShapeDtypeStruct((T, LANES), F32)],
        compiler_params=pltpu.CompilerParams(dimension_semantics=("parallel",),
                                             vmem_limit_bytes=VMEM_LIMIT),
        name="out_proj_router",
    )(x2d, y_hy, o_f, o_b, p_dn, jnp.tile(dn_norm_w, DN_HEADS).reshape(1, DN_W), w_oh, w_od,
      g.reshape(1, D_MODEL), wr_hi, wr_lo, br)


def _deint_kernel(w_ref, p_ref, og_ref, ol_ref):
    half = DEINT_COLS // 2
    sel = jnp.dot(w_ref[0].astype(BF16), p_ref[...], preferred_element_type=F32)
    og_ref[0] = sel[:, :half].astype(BF16)
    ol_ref[0] = sel[:, half:].astype(BF16)


def _deinterleave(w_gate_up):
    half = DEINT_COLS // 2
    r = np.arange(DEINT_COLS)[:, None]
    c = np.arange(DEINT_COLS)[None, :]
    perm = jnp.asarray(np.where(c < half, r == 2 * c, r == 2 * (c - half) + 1), BF16)
    out = jax.ShapeDtypeStruct((N_EXPERTS, D_MODEL, D_FF), BF16)
    return pl.pallas_call(
        _deint_kernel,
        grid=(N_EXPERTS, 2 * D_FF // DEINT_COLS),
        in_specs=[pl.BlockSpec((1, D_MODEL, DEINT_COLS), lambda e, j: (e, 0, j)),
                  pl.BlockSpec((DEINT_COLS, DEINT_COLS), lambda e, j: (0, 0))],
        out_specs=[pl.BlockSpec((1, D_MODEL, half), lambda e, j: (e, 0, j)),
                   pl.BlockSpec((1, D_MODEL, half), lambda e, j: (e, 0, j))],
        out_shape=[out, out],
        compiler_params=pltpu.CompilerParams(dimension_semantics=("parallel", "parallel")),
        name="deinterleave_gate_up",
    )(w_gate_up, perm)


def _dispatch_kernel(dest_ref, h_ref, xz_ref, xb_ref, sem):
    del xz_ref
    n = h_ref.shape[0]

    def issue(r, c):
        for k in range(TOP_K):
            d = dest_ref[0, r * TOP_K + k]
            pltpu.make_async_copy(h_ref.at[pl.ds(r, 1), :], xb_ref.at[pl.ds(d, 1), :], sem).start()
        return c

    lax.fori_loop(0, n, issue, 0, unroll=8)

    def drain(r, c):
        for k in range(TOP_K):
            pltpu.make_async_copy(h_ref.at[pl.ds(0, 1), :], xb_ref.at[pl.ds(0, 1), :], sem).wait()
        return c

    lax.fori_loop(0, n, drain, 0, unroll=8)


def _dispatch(h, dest, n_rows):
    T = h.shape[0]
    tm = DISPATCH_TILE
    return pl.pallas_call(
        _dispatch_kernel,
        grid=(T // tm,),
        in_specs=[pl.BlockSpec((None, 1, tm * TOP_K), lambda i: (i, 0, 0), memory_space=pltpu.SMEM),
                  pl.BlockSpec((tm, D_MODEL), lambda i: (i, 0)),
                  pl.BlockSpec(memory_space=pl.ANY)],
        out_specs=pl.BlockSpec(memory_space=pl.ANY),
        out_shape=jax.ShapeDtypeStruct((n_rows, D_MODEL), F32),
        scratch_shapes=[pltpu.SemaphoreType.DMA],
        input_output_aliases={2: 0},
        compiler_params=pltpu.CompilerParams(dimension_semantics=("arbitrary",)),
        name="moe_dispatch",
    )(dest.reshape(T // tm, 1, tm * TOP_K), h, jnp.zeros((n_rows, D_MODEL), F32))


def _expert_kernel(be_ref, nb_ref, xb_ref, wg_ref, wl_ref, bg_ref, bl_ref, wd_ref, bd_ref, y_ref):
    i = pl.program_id(0)

    @pl.when(i < nb_ref[0])
    def _():
        xb = xb_ref[...].astype(BF16)
        hg = jnp.dot(xb, wg_ref[0], preferred_element_type=F32) + bg_ref[0]
        hl = jnp.dot(xb, wl_ref[0], preferred_element_type=F32) + bl_ref[0]
        x_glu = jnp.minimum(hg, SWIGLU_LIMIT)
        x_lin = jnp.clip(hl, -SWIGLU_LIMIT, SWIGLU_LIMIT)
        act = x_glu * jax.nn.sigmoid(SWIGLU_ALPHA * x_glu) * (x_lin + 1.0)
        y = jnp.dot(act.astype(BF16), wd_ref[0], preferred_element_type=F32) + bd_ref[0]
        y_ref[...] = y

    @pl.when(i >= nb_ref[0])
    def _():
        y_ref[...] = jnp.zeros_like(y_ref)


def _expert_mlp(xb, block_e, n_used, wg, wl, bg, bl, wd, bd):
    n_rows = xb.shape[0]
    n_blocks = n_rows // MOE_BLOCK
    rowm = lambda i, be, nb: (i, 0)
    exp3 = lambda i, be, nb: (be[i], 0, 0)
    grid_spec = pltpu.PrefetchScalarGridSpec(
        num_scalar_prefetch=2,
        grid=(n_blocks,),
        in_specs=[pl.BlockSpec((MOE_BLOCK, D_MODEL), rowm),
                  pl.BlockSpec((1, D_MODEL, D_FF), exp3),
                  pl.BlockSpec((1, D_MODEL, D_FF), exp3),
                  pl.BlockSpec((1, 1, D_FF), exp3),
                  pl.BlockSpec((1, 1, D_FF), exp3),
                  pl.BlockSpec((1, D_FF, D_MODEL), exp3),
                  pl.BlockSpec((1, 1, D_MODEL), exp3)],
        out_specs=pl.BlockSpec((MOE_BLOCK, D_MODEL), rowm),
    )
    return pl.pallas_call(
        _expert_kernel,
        grid_spec=grid_spec,
        out_shape=jax.ShapeDtypeStruct((n_rows, D_MODEL), F32),
        compiler_params=pltpu.CompilerParams(dimension_semantics=("arbitrary",),
                                             vmem_limit_bytes=VMEM_LIMIT),
        name="expert_mlp",
    )(block_e, n_used, xb, wg, wl, bg, bl, wd, bd)


def _moe(h, logits, wg, wl, bg, bl, wd, bd):
    T = h.shape[0]
    TK = T * TOP_K
    top_vals, top_idx = lax.top_k(logits, TOP_K)
    gates = jax.nn.softmax(top_vals, axis=-1)
    sel = jnp.sum(jax.nn.one_hot(top_idx, N_EXPERTS, dtype=jnp.int32), axis=1)
    before = jnp.cumsum(sel, axis=0) - sel
    counts = jnp.sum(sel, axis=0)
    padded = (counts + MOE_BLOCK - 1) // MOE_BLOCK * MOE_BLOCK
    pad_end = jnp.cumsum(padded)
    pad_start = pad_end - padded
    rank = jnp.take_along_axis(before, top_idx, axis=1)
    dest = (pad_start[top_idx] + rank).astype(jnp.int32)
    n_blocks = (TK + MOE_BLOCK - 1) // MOE_BLOCK + N_EXPERTS
    n_rows = n_blocks * MOE_BLOCK
    block_start = jnp.arange(n_blocks, dtype=jnp.int32) * MOE_BLOCK
    block_e = jnp.minimum(jnp.sum((block_start[:, None] >= pad_end[None, :]).astype(jnp.int32), axis=1),
                          N_EXPERTS - 1)
    n_used = (pad_end[-1] // MOE_BLOCK).astype(jnp.int32).reshape(1)
    xb = _dispatch(h, dest, n_rows)
    yb = _expert_mlp(xb, block_e, n_used, wg, wl, bg, bl, wd, bd)
    return yb, dest, gates


def _combine_kernel(dest_ref, gate_ref, x1_ref, g_ref, yb_ref, o_ref, buf, sem):
    n = x1_ref.shape[0]

    def issue(r, c):
        for k in range(TOP_K):
            d = dest_ref[0, r * TOP_K + k]
            pltpu.make_async_copy(yb_ref.at[pl.ds(d, 1), :], buf.at[k, pl.ds(r, 1), :], sem).start()
        return c

    lax.fori_loop(0, n, issue, 0, unroll=8)

    def drain(r, c):
        for k in range(TOP_K):
            pltpu.make_async_copy(yb_ref.at[pl.ds(0, 1), :], buf.at[0, pl.ds(0, 1), :], sem).wait()
        return c

    lax.fori_loop(0, n, drain, 0, unroll=8)
    x = x1_ref[...]
    for k in range(TOP_K):
        x = x + gate_ref[:, k:k + 1] * buf[k]
    o_ref[...] = _rms(x, g_ref[...])


def _combine_final(yb, dest, gates, x1, g):
    T = x1.shape[0]
    tm = COMBINE_TILE
    row = lambda i: (i, 0)
    return pl.pallas_call(
        _combine_kernel,
        grid=(T // tm,),
        in_specs=[pl.BlockSpec((None, 1, tm * TOP_K), lambda i: (i, 0, 0), memory_space=pltpu.SMEM),
                  pl.BlockSpec((tm, TOP_K), row),
                  pl.BlockSpec((tm, D_MODEL), row),
                  pl.BlockSpec((1, D_MODEL), lambda i: (0, 0)),
                  pl.BlockSpec(memory_space=pl.ANY)],
        out_specs=pl.BlockSpec((tm, D_MODEL), row),
        out_shape=jax.ShapeDtypeStruct((T, D_MODEL), F32),
        scratch_shapes=[pltpu.VMEM((TOP_K, tm, D_MODEL), F32), pltpu.SemaphoreType.DMA],
        compiler_params=pltpu.CompilerParams(dimension_semantics=("arbitrary",)),
        name="moe_combine_final",
    )(dest.reshape(T // tm, 1, tm * TOP_K), gates, x1, g.reshape(1, D_MODEL), yb)


def _fft_tables(nc, nf, rows_in, real_in):
    n = nc * nf
    kc = np.arange(nc, dtype=np.float64)
    a1 = 2.0 * np.pi * np.outer(kc, np.arange(rows_in)) / nc
    c1, s1 = np.cos(a1), np.sin(a1)
    l1 = np.concatenate([c1, -s1], 0) if real_in else np.block([[c1, s1], [-s1, c1]])
    a2 = 2.0 * np.pi * np.outer(np.arange(nf), np.arange(nf)) / nf
    c2, s2 = np.cos(a2), np.sin(a2)
    m2 = np.block([[c2, s2], [-s2, c2]])
    m2i = np.block([[c2, -s2], [s2, c2]])
    a3 = 2.0 * np.pi * np.outer(np.arange(nc // 2), kc) / nc
    c3, s3 = np.cos(a3), np.sin(a3)
    l3 = np.block([[c3, -s3], [s3, c3]]) / n
    return tuple(jnp.asarray(m, F32).astype(BF16) for m in (l1, m2, m2i, l3))


def _twiddle(nc, nf):
    n = nc * nf
    ph = (jnp.arange(nf, dtype=jnp.int32)[:, None] * jnp.arange(nc, dtype=jnp.int32)[None, :]) % n
    ang = ph.astype(F32) * (2.0 * math.pi / n)
    tw = jnp.stack([jnp.cos(ang), jnp.sin(ang)], axis=1)
    return jnp.broadcast_to(tw[..., None], (nf, 2, nc, LANES))


def _lane_tile(t, width):
    return t if width == LANES else jnp.concatenate([t] * (width // LANES), axis=1)


def _fft1_kernel(z_ref, l1_ref, tw_ref, o_ref, *, nc, real_in):
    cb = o_ref.shape[-1]
    for j in range(SUB):
        if real_in:
            rhs = z_ref[:, j, :].astype(BF16)
        else:
            rhs = jnp.concatenate([z_ref[0, :, j, :], z_ref[1, :, j, :]], axis=0).astype(BF16)
        a = jnp.dot(l1_ref[...], rhs, preferred_element_type=F32)
        ar, ai = a[:nc], a[nc:]
        twr = _lane_tile(tw_ref[j, 0], cb)
        twi = _lane_tile(tw_ref[j, 1], cb)
        o_ref[j, 0] = ar * twr + ai * twi
        o_ref[j, 1] = ai * twr - ar * twi


def _fft1(z, l1, tw, nc, nf, real_in):
    C = z.shape[-1]
    cb = FFT_CB
    if real_in:
        P = 1
        z_spec = pl.BlockSpec((nc, SUB, cb), lambda p, j, c: (0, j, c))
    else:
        P = z.shape[0]
        z_spec = pl.BlockSpec((None, 2, nc // 2, SUB, cb), lambda p, j, c: (p, 0, 0, j, c))
    return pl.pallas_call(
        functools.partial(_fft1_kernel, nc=nc, real_in=real_in),
        grid=(P, nf // SUB, C // cb),
        in_specs=[z_spec,
                  pl.BlockSpec(l1.shape, lambda p, j, c: (0, 0)),
                  pl.BlockSpec((SUB, 2, nc, LANES), lambda p, j, c: (j, 0, 0, 0))],
        out_specs=pl.BlockSpec((None, SUB, 2, nc, cb), lambda p, j, c: (p, j, 0, 0, c)),
        out_shape=jax.ShapeDtypeStruct((P, nf, 2, nc, C), F32),
        compiler_params=pltpu.CompilerParams(dimension_semantics=("parallel", "parallel", "parallel"),
                                             vmem_limit_bytes=VMEM_LIMIT),
        name="hyena_fft_stage1",
    )(z, l1, tw)


def _fft2_filter_kernel(a_ref, m2_ref, k_ref, *, nf):
    for j in range(SUB):
        rhs = jnp.concatenate([a_ref[:, 0, j, :], a_ref[:, 1, j, :]], axis=0).astype(BF16)
        x = jnp.dot(m2_ref[...], rhs, preferred_element_type=F32)
        k_ref[j, 0] = x[:nf]
        k_ref[j, 1] = x[nf:]


def _fft2_filter(a, m2, nc, nf):
    C = a.shape[-1]
    cb = FFT_CB
    return pl.pallas_call(
        functools.partial(_fft2_filter_kernel, nf=nf),
        grid=(nc // SUB, C // cb),
        in_specs=[pl.BlockSpec((None, nf, 2, SUB, cb), lambda k, c: (0, 0, 0, k, c)),
                  pl.BlockSpec(m2.shape, lambda k, c: (0, 0))],
        out_specs=pl.BlockSpec((SUB, 2, nf, cb), lambda k, c: (k, 0, 0, c)),
        out_shape=jax.ShapeDtypeStruct((nc, 2, nf, C), F32),
        compiler_params=pltpu.CompilerParams(dimension_semantics=("parallel", "parallel"),
                                             vmem_limit_bytes=VMEM_LIMIT),
        name="hyena_filter_spectrum",
    )(a, m2)


def _fft2_kernel(a_ref, k_ref, m2_ref, m2i_ref, o_ref, *, nf):
    for j in range(SUB):
        rhs = jnp.concatenate([a_ref[:, 0, j, :], a_ref[:, 1, j, :]], axis=0).astype(BF16)
        x = jnp.dot(m2_ref[...], rhs, preferred_element_type=F32)
        xr, xi = x[:nf], x[nf:]
        kr, ki = k_ref[j, 0], k_ref[j, 1]
        y = jnp.concatenate([xr * kr - xi * ki, xr * ki + xi * kr], axis=0).astype(BF16)
        b = jnp.dot(m2i_ref[...], y, preferred_element_type=F32)
        o_ref[:, 0, j, :] = b[:nf]
        o_ref[:, 1, j, :] = b[nf:]


def _fft2(a, kspec, m2, m2i, nc, nf):
    P, C = a.shape[0], a.shape[-1]
    cb = FFT_CB
    blk = pl.BlockSpec((None, nf, 2, SUB, cb), lambda p, k, c: (p, 0, 0, k, c))
    return pl.pallas_call(
        functools.partial(_fft2_kernel, nf=nf),
        grid=(P, nc // SUB, C // cb),
        in_specs=[blk,
                  pl.BlockSpec((SUB, 2, nf, cb), lambda p, k, c: (k, 0, 0, c)),
                  pl.BlockSpec(m2.shape, lambda p, k, c: (0, 0)),
                  pl.BlockSpec(m2i.shape, lambda p, k, c: (0, 0))],
        out_specs=blk,
        out_shape=jax.ShapeDtypeStruct(a.shape, F32),
        compiler_params=pltpu.CompilerParams(dimension_semantics=("parallel", "parallel", "parallel"),
                                             vmem_limit_bytes=VMEM_LIMIT),
        name="hyena_fft_stage2",
    )(a, kspec, m2, m2i)


def _fft3_kernel(b_ref, tw_ref, l3_ref, s_ref, x0_ref, bias_ref, o_ref, *, nc):
    cb = o_ref.shape[-1]
    half = nc // 2
    for j in range(SUB):
        br, bi = b_ref[j, 0], b_ref[j, 1]
        twr = _lane_tile(tw_ref[j, 0], cb)
        twi = _lane_tile(tw_ref[j, 1], cb)
        rhs = jnp.concatenate([br * twr - bi * twi, br * twi + bi * twr], axis=0).astype(BF16)
        y = jnp.dot(l3_ref[...], rhs, preferred_element_type=F32)
        for q in range(2):
            s = s_ref[q, :, j, :]
            o_ref[q, :, j, :] = x0_ref[q, :, j, :] * (y[q * half:(q + 1) * half] + s * bias_ref[...])


def _fft3(b, tw, l3, s5, x05, bias, nc, nf):
    P, C = b.shape[0], b.shape[-1]
    cb = FFT_CB
    seq = pl.BlockSpec((None, 2, nc // 2, SUB, cb), lambda p, j, c: (p, 0, 0, j, c))
    return pl.pallas_call(
        functools.partial(_fft3_kernel, nc=nc),
        grid=(P, nf // SUB, C // cb),
        in_specs=[pl.BlockSpec((None, SUB, 2, nc, cb), lambda p, j, c: (p, j, 0, 0, c)),
                  pl.BlockSpec((SUB, 2, nc, LANES), lambda p, j, c: (j, 0, 0, 0)),
                  pl.BlockSpec(l3.shape, lambda p, j, c: (0, 0)),
                  seq, seq,
                  pl.BlockSpec((1, cb), lambda p, j, c: (0, c))],
        out_specs=seq,
        out_shape=jax.ShapeDtypeStruct(s5.shape, F32),
        compiler_params=pltpu.CompilerParams(dimension_semantics=("parallel", "parallel", "parallel"),
                                             vmem_limit_bytes=VMEM_LIMIT),
        name="hyena_fft_stage3",
    )(b, tw, l3, s5, x05, bias)


def _fft_split(n):
    nf = 1 << (int(math.log2(n)) // 2)
    return n // nf, nf


def _hyena_conv(s, x0, k2, bias):
    B, L, C = s.shape
    nc, nf = _fft_split(2 * L)
    tw = _twiddle(nc, nf)
    l1f, m2, m2i, l3 = _fft_tables(nc, nf, nc, True)
    l1 = _fft_tables(nc, nf, nc // 2, False)[0]
    kspec = _fft2_filter(_fft1(k2.reshape(nc, nf, C), l1f, tw, nc, nf, True), m2, nc, nf)
    s5 = s.reshape(B // 2, 2, nc // 2, nf, C)
    x05 = x0.reshape(B // 2, 2, nc // 2, nf, C)
    a = _fft1(s5, l1, tw, nc, nf, False)
    b = _fft2(a, kspec, m2, m2i, nc, nf)
    return _fft3(b, tw, l3, s5, x05, bias.reshape(1, C), nc, nf).reshape(B, L, C)


def _dwconv(x, w):
    L = x.shape[1]
    xp = jnp.pad(x, ((0, 0), (1, 1), (0, 0)))
    return xp[:, 0:L] * w[0] + xp[:, 1:L + 1] * w[1] + xp[:, 2:L + 2] * w[2]


def _hyena_filter(L, w1, b1, w2, b2, w3, freq):
    pos = jnp.arange(L, dtype=F32)
    t = jnp.linspace(0.0, 1.0, L, dtype=F32)[:, None]
    bands = jnp.linspace(1e-4, HY_BANDS - 1, HY_BANDS, dtype=F32)
    ang = (2.0 * math.pi / L) * pos[:, None] * bands[None, :]
    z = jnp.concatenate([t, jnp.cos(ang), -jnp.sin(ang)], axis=-1)
    h = jnp.sin(freq * (z @ w1 + b1))
    h = jnp.sin(freq * (h @ w2 + b2))
    h = h @ w3
    deltas = jnp.abs(jnp.linspace(math.log(HY_DECAY_TARGET) / HY_SLOW_PCT,
                                  math.log(HY_DECAY_TARGET) / HY_FAST_PCT, HY_CH, dtype=F32))
    window = jnp.exp(-t * deltas[None, :])
    h_f = h[:, :HY_CH] * window
    h_b = h[:, HY_CH:] * window
    l1 = jnp.sum(jnp.abs(h_f), axis=0) + jnp.sum(jnp.abs(h_b[1:]), axis=0)
    k2 = jnp.concatenate([h_f, jnp.zeros((1, HY_CH), F32), h_b[:0:-1]], axis=0)
    return k2 / l1


def _hyena(p, conv_w, conv_b, w1, b1, w2, b2, w3, freq, bias):
    B, L, _ = p.shape
    u = _dwconv(p, conv_w) + conv_b
    x0, x1, v = jnp.split(u, 3, axis=-1)
    k2 = _hyena_filter(L, w1, b1, w2, b2, w3, freq)
    return _hyena_conv(x1 * v, x0, k2, bias)


def _split3(x):
    x1 = x.astype(BF16)
    r = x - x1.astype(F32)
    x2 = r.astype(BF16)
    x3 = (r - x2.astype(F32)).astype(BF16)
    return x1, x2, x3


def _mm(a, b):
    return jnp.dot(a.astype(BF16), b.astype(BF16), preferred_element_type=F32)


def _delta_kernel(qf_ref, kf_ref, vf_ref, gf_ref, qb_ref, kb_ref, vb_ref, gb_ref, of_ref, ob_ref, s_ref,
                  *, n_sub):
    @pl.when(pl.program_id(1) == 0)
    def _():
        s_ref[...] = jnp.zeros_like(s_ref)

    row = lax.broadcasted_iota(jnp.int32, (CHUNK, CHUNK), 0)
    col = lax.broadcasted_iota(jnp.int32, (CHUNK, CHUNK), 1)
    eye_f = jnp.where(row == col, 1.0, 0.0).astype(F32)
    dirs = ((qf_ref, kf_ref, vf_ref, gf_ref, of_ref, row >= col, row > col),
            (qb_ref, kb_ref, vb_ref, gb_ref, ob_ref, row <= col, row < col))

    ch = []
    for d, (q_ref, k_ref, v_ref, g_ref, _, incl, strict) in enumerate(dirs):
        tri = jnp.where(incl, 1.0, 0.0).astype(BF16)
        for c in range(n_sub):
            rows = slice(c * CHUNK, (c + 1) * CHUNK)
            gates = g_ref[rows, :]
            g1, g2, g3 = _split3(gates)
            gc_all = (jnp.dot(tri, g1, preferred_element_type=F32)
                      + jnp.dot(tri, g2, preferred_element_type=F32)
                      + jnp.dot(tri, g3, preferred_element_type=F32))
            gsum_all = jnp.sum(gates, axis=0, keepdims=True)
            for h in range(DN_HEADS):
                lanes = slice(h * DN_DK, (h + 1) * DN_DK)
                kh = k_ref[rows, lanes]
                beta = gates[:, DN_HEADS + h:DN_HEADS + h + 1]
                gc = gc_all[:, h:h + 1]
                g_last = gsum_all[:, h:h + 1]
                gc_b = jnp.broadcast_to(gc, (CHUNK, CHUNK))
                gc_row = jnp.sum(gc_b * eye_f, axis=0, keepdims=True)
                decay = jnp.where(incl, jnp.exp(jnp.minimum(gc_b - gc_row, 0.0)), 0.0)
                e_gc = jnp.exp(gc)
                kb = kh * beta
                ch.append(dict(d=d, c=c, h=h, lanes=lanes, rows=rows, strict=strict, decay=decay,
                               kh16=kh.astype(BF16), kb16=kb.astype(BF16),
                               q16=q_ref[rows, lanes].astype(BF16),
                               rhs=jnp.concatenate([v_ref[rows, lanes] * beta, kb * e_gc],
                                                   axis=1).astype(BF16),
                               qe=q_ref[rows, lanes] * e_gc,
                               kdT=(kh * jnp.exp(g_last - gc)).T.astype(BF16),
                               e_last=jnp.exp(g_last)))
    for x in ch:
        a = lax.dot_general(x["kb16"], x["kh16"], _NT, preferred_element_type=F32) * x["decay"]
        x["a"] = jnp.where(x["strict"], a, 0.0)
        x["qk"] = (lax.dot_general(x["q16"], x["kh16"], _NT, preferred_element_type=F32)
                   * x["decay"]).astype(BF16)
    for x in ch:
        x["t"] = eye_f - x["a"]
        x["p"] = _mm(x["a"], x["a"])
    for level in range(5):
        for x in ch:
            p16 = x["p"].astype(BF16)
            x["t"] = x["t"] + jnp.dot(x["t"].astype(BF16), p16, preferred_element_type=F32)
            if level < 4:
                x["p"] = jnp.dot(p16, p16, preferred_element_type=F32)
    for x in ch:
        uw = jnp.dot(x["t"].astype(BF16), x["rhs"], preferred_element_type=F32)
        x["u"] = uw[:, :DN_DV]
        x["wq"] = jnp.concatenate([uw[:, DN_DV:], x["qe"]], axis=0).astype(BF16)

    for step in range(n_sub):
        cur = [x for x in ch if x["c"] == (step if x["d"] == 0 else n_sub - 1 - step)]
        for x in cur:
            x["s"] = s_ref[x["d"], x["h"]]
            x["ws"] = jnp.dot(x["wq"], x["s"].astype(BF16), preferred_element_type=F32)
        for x in cur:
            x["vn"] = (x["u"] - x["ws"][:CHUNK]).astype(BF16)
        for x in cur:
            o = x["ws"][CHUNK:] + jnp.dot(x["qk"], x["vn"], preferred_element_type=F32)
            dirs[x["d"]][4][x["rows"], x["lanes"]] = o
            s_ref[x["d"], x["h"]] = (x["s"] * x["e_last"]
                                     + jnp.dot(x["kdT"], x["vn"], preferred_element_type=F32))


def _delta_scan(q, k, v, gates):
    B, L, _ = q.shape
    n_sub = DN_STEP_CHUNKS
    rows = n_sub * CHUNK
    nblk = L // rows
    fwd = pl.BlockSpec((None, rows, DN_W), lambda b, j: (b, j, 0))
    bwd = pl.BlockSpec((None, rows, DN_W), lambda b, j: (b, nblk - 1 - j, 0))
    gfwd = pl.BlockSpec((None, None, rows, LANES), lambda b, j: (0, b, j, 0))
    gbwd = pl.BlockSpec((None, None, rows, LANES), lambda b, j: (1, b, nblk - 1 - j, 0))
    out = jax.ShapeDtypeStruct((B, L, DN_W), F32)
    return pl.pallas_call(
        functools.partial(_delta_kernel, n_sub=n_sub),
        grid=(B, nblk),
        in_specs=[fwd, fwd, fwd, gfwd, bwd, bwd, bwd, gbwd],
        out_specs=[fwd, bwd],
        out_shape=[out, out],
        scratch_shapes=[pltpu.VMEM((2, DN_HEADS, DN_DK, DN_DV), F32)],
        compiler_params=pltpu.CompilerParams(dimension_semantics=("parallel", "arbitrary"),
                                             vmem_limit_bytes=VMEM_LIMIT),
        name="delta_scan",
    )(q, k, v, gates, q, k, v, gates)


def _l2norm(x):
    return x * lax.rsqrt(jnp.sum(x * x, axis=-1, keepdims=True) + EPS)


def _deltanet(p, gate_cols, conv_w, a_log, dt_bias):
    B, L, _ = p.shape
    qkv = jax.nn.silu(_dwconv(p[..., :DN_CONV], conv_w))
    q = _l2norm(qkv[..., :DN_QK].reshape(B, L, DN_HEADS, DN_DK)) * (DN_DK ** -0.5)
    k = _l2norm(qkv[..., DN_QK:2 * DN_QK].reshape(B, L, DN_HEADS, DN_DK))
    v = qkv[..., 2 * DN_QK:]
    a = gate_cols[..., :2 * DN_HEADS].reshape(B, L, 2, DN_HEADS)
    b = gate_cols[..., 2 * DN_HEADS:4 * DN_HEADS].reshape(B, L, 2, DN_HEADS)
    g = -jnp.exp(a_log) * jax.nn.softplus(a + dt_bias)
    beta = jax.nn.sigmoid(b)
    gates = jnp.moveaxis(jnp.concatenate([g, beta], axis=-1), 2, 0)
    gates = jnp.pad(gates, ((0, 0), (0, 0), (0, 0), (0, LANES - 2 * DN_HEADS)))
    return _delta_scan(q.reshape(B, L, DN_QK), k.reshape(B, L, DN_QK), v, gates)


def _trunk(x, w, hy_conv_w, hy_conv_b, hy_pos_w1, hy_pos_b1, hy_pos_w2, hy_pos_b2, hy_pos_w3,
           hy_sin_freq, hy_bias, dn_conv_w, dn_a_log, dn_dt_bias, dn_norm_w):
    B, L, D = x.shape
    T = B * L
    x2d = x.reshape(T, D)
    p_hy, p_dn, p_g = _in_proj(x2d, w["norm_mix"], w["in_hy"], w["in_dn"], w["in_gate"])
    y_hy = _hyena(p_hy.reshape(B, L, HY_IN), hy_conv_w, hy_conv_b, hy_pos_w1, hy_pos_b1, hy_pos_w2,
                  hy_pos_b2, hy_pos_w3, hy_sin_freq, hy_bias)
    o_f, o_b = _deltanet(p_dn.reshape(B, L, DN_MAIN), p_g.reshape(B, L, LANES), dn_conv_w, dn_a_log,
                         dn_dt_bias)
    x1, h_ffn, logits = _out_proj(x2d, y_hy.reshape(T, HY_CH), o_f.reshape(T, DN_W), o_b.reshape(T, DN_W),
                                 p_dn, dn_norm_w, w["out_hy"], w["out_dn"], w["norm_ffn"],
                                 w["router_hi"], w["router_lo"], w["router_b"])
    yb, dest, gates = _moe(h_ffn, logits[:, :N_EXPERTS], w["wg"], w["wl"], w["bg"], w["bl"], w["wd"],
                           w["bd"])
    return _combine_final(yb, dest, gates, x1, w["norm_final"]).reshape(B, L, D)


def kernel(x_prompt, x_sample, norm_mix_w, w_in, hy_conv_w, hy_conv_b, hy_pos_w1, hy_pos_b1, hy_pos_w2, hy_pos_b2, hy_pos_w3, hy_sin_freq, hy_bias, dn_conv_w, dn_a_log, dn_dt_bias, dn_norm_w, w_out, norm_ffn_w, w_router, b_router, w_gate_up, b_gate_up, w_down, b_down, norm_final_w):
    w_in16 = w_in[0].astype(BF16)
    w_out16 = w_out[0].astype(BF16)
    wr = jnp.pad(w_router[0], ((0, 0), (0, LANES - N_EXPERTS)))
    wr_hi = wr.astype(BF16)
    wg, wl = _deinterleave(w_gate_up[0])
    w = {
        "norm_mix": norm_mix_w[0],
        "in_hy": w_in16[:, :HY_IN],
        "in_dn": w_in16[:, HY_IN:HY_IN + DN_MAIN],
        "in_gate": jnp.pad(w_in16[:, HY_IN + DN_MAIN:], ((0, 0), (0, LANES - N_GATE))),
        "out_hy": w_out16[:HY_CH],
        "out_dn": w_out16[HY_CH:],
        "norm_ffn": norm_ffn_w[0],
        "router_hi": wr_hi,
        "router_lo": (wr - wr_hi.astype(F32)).astype(BF16),
        "router_b": jnp.pad(b_router[0], (0, LANES - N_EXPERTS)).reshape(1, LANES),
        "wg": wg,
        "wl": wl,
        "bg": b_gate_up[0][:, 0::2].reshape(N_EXPERTS, 1, D_FF),
        "bl": b_gate_up[0][:, 1::2].reshape(N_EXPERTS, 1, D_FF),
        "wd": w_down[0].astype(BF16),
        "bd": b_down[0].reshape(N_EXPERTS, 1, D_MODEL),
        "norm_final": norm_final_w,
    }
    mix = (hy_conv_w[0], hy_conv_b[0], hy_pos_w1[0], hy_pos_b1[0], hy_pos_w2[0], hy_pos_b2[0],
           hy_pos_w3[0], hy_sin_freq[0], hy_bias[0], dn_conv_w[0], dn_a_log[0], dn_dt_bias[0],
           dn_norm_w[0])
    return (_trunk(x_prompt, w, *mix), _trunk(x_sample, w, *mix))
```

```python
import functools
import math

import jax
import jax.numpy as jnp
import numpy as np
from jax import lax
from jax.experimental import pallas as pl
from jax.experimental.pallas import tpu as pltpu

D_MODEL = 1024
HY_CH = 512
DN_HEADS = 4
DN_DK = 128
DN_DV = 128
DN_QK = DN_HEADS * DN_DK
DN_W = DN_HEADS * DN_DV
HY_IN = 3 * HY_CH
DN_CONV = 2 * DN_QK + DN_W
N_GATE = 4 * DN_HEADS
SHORT_CONV = 3
CONV_COLS = HY_IN + DN_CONV
REST_COLS = DN_W + 128
HY_EMB = 33
HY_BANDS = (HY_EMB - 1) // 2
HY_DECAY_TARGET = 1e-2
HY_FAST_PCT = 0.3
HY_SLOW_PCT = 1.5
CHUNK = 64
N_EXPERTS = 32
TOP_K = 4
D_FF = D_MODEL
SWIGLU_ALPHA = 1.702
SWIGLU_LIMIT = 7.0
MOE_BLOCK = 512
EPS = 1e-6

LANES = 128
SUB = 8
ROW_TILE = 512
FFT_CB = 256
DN_STEP_CHUNKS = 4
DEINT_COLS = 512
DISPATCH_TILE = 512
COMBINE_TILE = 256
VMEM_LIMIT = 56 * 1024 * 1024

F32 = jnp.float32
BF16 = jnp.bfloat16

_NT = (((1,), (1,)), ((), ()))


def _rms(x, g):
    return x * lax.rsqrt(jnp.mean(x * x, axis=-1, keepdims=True) + EPS) * g


def _silu(x):
    return x * jax.nn.sigmoid(x)


def _head_l2norm(x):
    parts = []
    for hd in range(DN_HEADS):
        xh = x[:, hd * DN_DK:(hd + 1) * DN_DK]
        parts.append(xh * lax.rsqrt(jnp.sum(xh * xh, axis=-1, keepdims=True) + EPS))
    return jnp.concatenate(parts, axis=1)


def _in_proj_kernel(xp_ref, x_ref, xn_ref, g_ref, wc_ref, wr_ref, cw_ref, cb_ref, gt_ref,
                    x0_ref, s_ref, q_ref, k_ref, v_ref, z_ref, gate_ref, p_scr, *, tiles_per_seq):
    i = pl.program_id(0)
    first = (i % tiles_per_seq) == 0
    last = (i % tiles_per_seq) == tiles_per_seq - 1
    g = g_ref[...]
    h = _rms(x_ref[...], g).astype(BF16)
    hp = _rms(xp_ref[...], g).astype(BF16)
    hn = _rms(xn_ref[...], g).astype(BF16)
    wc = wc_ref[...]
    p_scr[pl.ds(SUB, ROW_TILE), :] = jnp.dot(h, wc, preferred_element_type=F32)
    p_scr[pl.ds(0, SUB), :] = jnp.where(first, 0.0, jnp.dot(hp, wc, preferred_element_type=F32))
    p_scr[pl.ds(SUB + ROW_TILE, SUB), :] = jnp.where(last, 0.0,
                                                     jnp.dot(hn, wc, preferred_element_type=F32))
    rest = jnp.dot(h, wr_ref[...], preferred_element_type=F32)
    z_ref[...] = rest[:, :DN_W]
    a = rest[:, DN_W:] + gt_ref[1:2]
    softplus = jnp.maximum(a, 0.0) + jnp.log(1.0 + jnp.exp(-jnp.abs(a)))
    gate_ref[...] = jnp.where(gt_ref[2:3] > 0.5, -gt_ref[0:1] * softplus, jax.nn.sigmoid(rest[:, DN_W:]))

    def conv(c0):
        cols = slice(c0, c0 + HY_CH)
        return (p_scr[pl.ds(SUB - 1, ROW_TILE), cols] * cw_ref[0:1, cols]
                + p_scr[pl.ds(SUB, ROW_TILE), cols] * cw_ref[1:2, cols]
                + p_scr[pl.ds(SUB + 1, ROW_TILE), cols] * cw_ref[2:3, cols])

    x0_ref[...] = conv(0) + cb_ref[:, 0:HY_CH]
    s_ref[...] = (conv(HY_CH) + cb_ref[:, HY_CH:2 * HY_CH]) * (conv(2 * HY_CH) + cb_ref[:, 2 * HY_CH:])
    q_ref[...] = _head_l2norm(_silu(conv(HY_IN))) * (DN_DK ** -0.5)
    k_ref[...] = _head_l2norm(_silu(conv(HY_IN + DN_QK)))
    v_ref[...] = _silu(conv(HY_IN + 2 * DN_QK))


def _in_proj(x2d, seq_len, g, w_conv, w_rest, conv_w, conv_b, gate_tab):
    T = x2d.shape[0]
    per = ROW_TILE // SUB
    last_sub = T // SUB - 1
    const = lambda i: (0, 0)
    row = lambda i: (i, 0)
    o512 = pl.BlockSpec((ROW_TILE, HY_CH), row)
    s512 = jax.ShapeDtypeStruct((T, HY_CH), F32)
    return pl.pallas_call(
        functools.partial(_in_proj_kernel, tiles_per_seq=seq_len // ROW_TILE),
        grid=(T // ROW_TILE,),
        in_specs=[pl.BlockSpec((SUB, D_MODEL), lambda i: (jnp.maximum(i * per - 1, 0), 0)),
                  pl.BlockSpec((ROW_TILE, D_MODEL), row),
                  pl.BlockSpec((SUB, D_MODEL), lambda i: (jnp.minimum((i + 1) * per, last_sub), 0)),
                  pl.BlockSpec((1, D_MODEL), const),
                  pl.BlockSpec((D_MODEL, CONV_COLS), const),
                  pl.BlockSpec((D_MODEL, REST_COLS), const),
                  pl.BlockSpec((SHORT_CONV, CONV_COLS), const),
                  pl.BlockSpec((1, HY_IN), const),
                  pl.BlockSpec((3, LANES), const)],
        out_specs=[o512, o512, o512, o512, o512, o512, pl.BlockSpec((ROW_TILE, LANES), row)],
        out_shape=[s512, s512, s512, s512, s512, s512, jax.ShapeDtypeStruct((T, LANES), F32)],
        scratch_shapes=[pltpu.VMEM((ROW_TILE + 2 * SUB, CONV_COLS), F32)],
        compiler_params=pltpu.CompilerParams(dimension_semantics=("parallel",),
                                             vmem_limit_bytes=VMEM_LIMIT),
        name="in_proj",
    )(x2d, x2d, x2d, g.reshape(1, D_MODEL), w_conv, w_rest, conv_w, conv_b, gate_tab)


def _in_proj_params(w_in, hy_conv_w, hy_conv_b, dn_conv_w, dn_a_log, dn_dt_bias):
    H = DN_HEADS
    w16 = w_in.astype(BF16)
    gc = w16[:, CONV_COLS + DN_W:]
    gc = jnp.concatenate([gc[:, 0:H], gc[:, 2 * H:3 * H], gc[:, H:2 * H], gc[:, 3 * H:]], axis=1)
    w_rest = jnp.concatenate([w16[:, CONV_COLS:CONV_COLS + DN_W],
                              jnp.pad(gc, ((0, 0), (0, LANES - N_GATE)))], axis=1)
    zero, one, pad = jnp.zeros((H,), F32), jnp.ones((H,), F32), jnp.zeros((LANES - N_GATE,), F32)
    gate_tab = jnp.stack([jnp.concatenate([jnp.exp(dn_a_log[0]), zero, jnp.exp(dn_a_log[1]), zero, pad]),
                          jnp.concatenate([dn_dt_bias[0], zero, dn_dt_bias[1], zero, pad]),
                          jnp.concatenate([one, zero, one, zero, pad])])
    return (w16[:, :CONV_COLS], w_rest, jnp.concatenate([hy_conv_w, dn_conv_w], axis=1),
            hy_conv_b.reshape(1, HY_IN), gate_tab)


def _out_proj_kernel(x_ref, yh_ref, of_ref, ob_ref, z_ref, nw_ref, wh_ref, wd_ref, g_ref, wrh_ref,
                     wrl_ref, br_ref, x1_ref, h_ref, lg_ref):
    o = of_ref[...] + ob_ref[...]
    heads = []
    for hd in range(DN_HEADS):
        oh = o[:, hd * DN_DV:(hd + 1) * DN_DV]
        heads.append(oh * lax.rsqrt(jnp.mean(oh * oh, axis=-1, keepdims=True) + EPS))
    z = z_ref[...]
    y_dn = jnp.concatenate(heads, axis=1) * nw_ref[...] * (z * jax.nn.sigmoid(z))
    x1 = (x_ref[...]
          + jnp.dot(yh_ref[...].astype(BF16), wh_ref[...], preferred_element_type=F32)
          + jnp.dot(y_dn.astype(BF16), wd_ref[...], preferred_element_type=F32))
    x1_ref[...] = x1
    h = _rms(x1, g_ref[...])
    h_hi = h.astype(BF16)
    h_lo = (h - h_hi.astype(F32)).astype(BF16)
    h_ref[...] = h
    lg_ref[...] = (jnp.dot(h_hi, wrh_ref[...], preferred_element_type=F32)
                   + jnp.dot(h_lo, wrh_ref[...], preferred_element_type=F32)
                   + jnp.dot(h_hi, wrl_ref[...], preferred_element_type=F32)
                   + br_ref[...])


def _out_proj(x2d, y_hy, o_f, o_b, z, dn_norm_w, w_oh, w_od, g, wr_hi, wr_lo, br):
    T = x2d.shape[0]
    const = lambda i: (0, 0)
    row = lambda i: (i, 0)
    return pl.pallas_call(
        _out_proj_kernel,
        grid=(T // ROW_TILE,),
        in_specs=[pl.BlockSpec((ROW_TILE, D_MODEL), row),
                  pl.BlockSpec((ROW_TILE, HY_CH), row),
                  pl.BlockSpec((ROW_TILE, DN_W), row),
                  pl.BlockSpec((ROW_TILE, DN_W), row),
                  pl.BlockSpec((ROW_TILE, DN_W), row),
                  pl.BlockSpec((1, DN_W), const),
                  pl.BlockSpec((HY_CH, D_MODEL), const),
                  pl.BlockSpec((DN_W, D_MODEL), const),
                  pl.BlockSpec((1, D_MODEL), const),
                  pl.BlockSpec((D_MODEL, LANES), const),
                  pl.BlockSpec((D_MODEL, LANES), const),
                  pl.BlockSpec((1, LANES), const)],
        out_specs=[pl.BlockSpec((ROW_TILE, D_MODEL), row),
                   pl.BlockSpec((ROW_TILE, D_MODEL), row),
                   pl.BlockSpec((ROW_TILE, LANES), row)],
        out_shape=[jax.ShapeDtypeStruct((T, D_MODEL), F32),
                   jax.ShapeDtypeStruct((T, D_MODEL), F32),
                   jax.ShapeDtypeStruct((T, LANES), F32)],
        compiler_params=pltpu.CompilerParams(dimension_semantics=("parallel",),
                                             vmem_limit_bytes=VMEM_LIMIT),
        name="out_proj_router",
    )(x2d, y_hy, o_f, o_b, z, jnp.tile(dn_norm_w, DN_HEADS).reshape(1, DN_W), w_oh, w_od,
      g.reshape(1, D_MODEL), wr_hi, wr_lo, br)


def _deint_kernel(w_ref, p_ref, og_ref, ol_ref):
    half = DEINT_COLS // 2
    sel = jnp.dot(w_ref[0].astype(BF16), p_ref[...], preferred_element_type=F32)
    og_ref[0] = sel[:, :half].astype(BF16)
    ol_ref[0] = sel[:, half:].astype(BF16)


def _deinterleave(w_gate_up):
    half = DEINT_COLS // 2
    r = np.arange(DEINT_COLS)[:, None]
    c = np.arange(DEINT_COLS)[None, :]
    perm = jnp.asarray(np.where(c < half, r == 2 * c, r == 2 * (c - half) + 1), BF16)
    out = jax.ShapeDtypeStruct((N_EXPERTS, D_MODEL, D_FF), BF16)
    return pl.pallas_call(
        _deint_kernel,
        grid=(N_EXPERTS, 2 * D_FF // DEINT_COLS),
        in_specs=[pl.BlockSpec((1, D_MODEL, DEINT_COLS), lambda e, j: (e, 0, j)),
                  pl.BlockSpec((DEINT_COLS, DEINT_COLS), lambda e, j: (0, 0))],
        out_specs=[pl.BlockSpec((1, D_MODEL, half), lambda e, j: (e, 0, j)),
                   pl.BlockSpec((1, D_MODEL, half), lambda e, j: (e, 0, j))],
        out_shape=[out, out],
        compiler_params=pltpu.CompilerParams(dimension_semantics=("parallel", "parallel")),
        name="deinterleave_gate_up",
    )(w_gate_up, perm)


def _dispatch_kernel(tail_ref, dest_ref, h_ref, xb_ref, zero_scr, sem):
    n = h_ref.shape[0]

    @pl.when(pl.program_id(0) == 0)
    def _():
        zero_scr[...] = jnp.zeros_like(zero_scr)
        for e in range(N_EXPERTS):
            t0 = pl.multiple_of(tail_ref[0, e], SUB)
            pltpu.make_async_copy(zero_scr, xb_ref.at[pl.ds(t0, MOE_BLOCK), :], sem).start()
        for e in range(N_EXPERTS):
            pltpu.make_async_copy(zero_scr, xb_ref.at[pl.ds(0, MOE_BLOCK), :], sem).wait()

    def issue(r, c):
        for k in range(TOP_K):
            d = dest_ref[0, r * TOP_K + k]
            pltpu.make_async_copy(h_ref.at[pl.ds(r, 1), :], xb_ref.at[pl.ds(d, 1), :], sem).start()
        return c

    lax.fori_loop(0, n, issue, 0, unroll=8)

    def drain(r, c):
        for k in range(TOP_K):
            pltpu.make_async_copy(h_ref.at[pl.ds(0, 1), :], xb_ref.at[pl.ds(0, 1), :], sem).wait()
        return c

    lax.fori_loop(0, n, drain, 0, unroll=8)


def _dispatch(h, dest, tail_start, n_rows):
    T = h.shape[0]
    tm = DISPATCH_TILE
    return pl.pallas_call(
        _dispatch_kernel,
        grid=(T // tm,),
        in_specs=[pl.BlockSpec((1, N_EXPERTS), lambda i: (0, 0), memory_space=pltpu.SMEM),
                  pl.BlockSpec((None, 1, tm * TOP_K), lambda i: (i, 0, 0), memory_space=pltpu.SMEM),
                  pl.BlockSpec((tm, D_MODEL), lambda i: (i, 0))],
        out_specs=pl.BlockSpec(memory_space=pl.ANY),
        out_shape=jax.ShapeDtypeStruct((n_rows, D_MODEL), F32),
        scratch_shapes=[pltpu.VMEM((MOE_BLOCK, D_MODEL), F32), pltpu.SemaphoreType.DMA],
        compiler_params=pltpu.CompilerParams(dimension_semantics=("arbitrary",)),
        name="moe_dispatch",
    )(tail_start.reshape(1, N_EXPERTS), dest.reshape(T // tm, 1, tm * TOP_K), h)


def _expert_kernel(be_ref, nb_ref, xb_ref, wg_ref, wl_ref, bg_ref, bl_ref, wd_ref, bd_ref, y_ref):
    i = pl.program_id(0)

    @pl.when(i < nb_ref[0])
    def _():
        xb = xb_ref[...].astype(BF16)
        hg = jnp.dot(xb, wg_ref[0], preferred_element_type=F32) + bg_ref[0]
        hl = jnp.dot(xb, wl_ref[0], preferred_element_type=F32) + bl_ref[0]
        x_glu = jnp.minimum(hg, SWIGLU_LIMIT)
        x_lin = jnp.clip(hl, -SWIGLU_LIMIT, SWIGLU_LIMIT)
        act = x_glu * jax.nn.sigmoid(SWIGLU_ALPHA * x_glu) * (x_lin + 1.0)
        y = jnp.dot(act.astype(BF16), wd_ref[0], preferred_element_type=F32) + bd_ref[0]
        y_ref[...] = y

    @pl.when(i >= nb_ref[0])
    def _():
        y_ref[...] = jnp.zeros_like(y_ref)


def _expert_mlp(xb, block_e, n_used, wg, wl, bg, bl, wd, bd):
    n_rows = xb.shape[0]
    n_blocks = n_rows // MOE_BLOCK
    rowm = lambda i, be, nb: (i, 0)
    exp3 = lambda i, be, nb: (be[i], 0, 0)
    grid_spec = pltpu.PrefetchScalarGridSpec(
        num_scalar_prefetch=2,
        grid=(n_blocks,),
        in_specs=[pl.BlockSpec((MOE_BLOCK, D_MODEL), rowm),
                  pl.BlockSpec((1, D_MODEL, D_FF), exp3),
                  pl.BlockSpec((1, D_MODEL, D_FF), exp3),
                  pl.BlockSpec((1, 1, D_FF), exp3),
                  pl.BlockSpec((1, 1, D_FF), exp3),
                  pl.BlockSpec((1, D_FF, D_MODEL), exp3),
                  pl.BlockSpec((1, 1, D_MODEL), exp3)],
        out_specs=pl.BlockSpec((MOE_BLOCK, D_MODEL), rowm),
    )
    return pl.pallas_call(
        _expert_kernel,
        grid_spec=grid_spec,
        out_shape=jax.ShapeDtypeStruct((n_rows, D_MODEL), F32),
        compiler_params=pltpu.CompilerParams(dimension_semantics=("arbitrary",),
                                             vmem_limit_bytes=VMEM_LIMIT),
        name="expert_mlp",
    )(block_e, n_used, xb, wg, wl, bg, bl, wd, bd)


def _moe(h, logits, wg, wl, bg, bl, wd, bd):
    T = h.shape[0]
    TK = T * TOP_K
    top_vals, top_idx = lax.top_k(logits, TOP_K)
    gates = jax.nn.softmax(top_vals, axis=-1)
    sel = jnp.sum(jax.nn.one_hot(top_idx, N_EXPERTS, dtype=jnp.int32), axis=1)
    before = jnp.cumsum(sel, axis=0) - sel
    counts = jnp.sum(sel, axis=0)
    padded = (counts + MOE_BLOCK - 1) // MOE_BLOCK * MOE_BLOCK
    pad_end = jnp.cumsum(padded)
    pad_start = pad_end - padded
    rank = jnp.take_along_axis(before, top_idx, axis=1)
    dest = (pad_start[top_idx] + rank).astype(jnp.int32)
    n_blocks = (TK + MOE_BLOCK - 1) // MOE_BLOCK + N_EXPERTS
    n_rows = n_blocks * MOE_BLOCK
    block_start = jnp.arange(n_blocks, dtype=jnp.int32) * MOE_BLOCK
    block_e = jnp.minimum(jnp.sum((block_start[:, None] >= pad_end[None, :]).astype(jnp.int32), axis=1),
                          N_EXPERTS - 1)
    n_used = (pad_end[-1] // MOE_BLOCK).astype(jnp.int32).reshape(1)
    tail_start = jnp.minimum((pad_start + counts) // SUB * SUB, n_rows - MOE_BLOCK).astype(jnp.int32)
    xb = _dispatch(h, dest, tail_start, n_rows)
    yb = _expert_mlp(xb, block_e, n_used, wg, wl, bg, bl, wd, bd)
    return yb, dest, gates


def _combine_kernel(dest_ref, gate_ref, x1_ref, g_ref, yb_ref, o_ref, buf, sem):
    n = x1_ref.shape[0]

    def issue(r, c):
        for k in range(TOP_K):
            d = dest_ref[0, r * TOP_K + k]
            pltpu.make_async_copy(yb_ref.at[pl.ds(d, 1), :], buf.at[k, pl.ds(r, 1), :], sem).start()
        return c

    lax.fori_loop(0, n, issue, 0, unroll=8)

    def drain(r, c):
        for k in range(TOP_K):
            pltpu.make_async_copy(yb_ref.at[pl.ds(0, 1), :], buf.at[0, pl.ds(0, 1), :], sem).wait()
        return c

    lax.fori_loop(0, n, drain, 0, unroll=8)
    x = x1_ref[...]
    for k in range(TOP_K):
        x = x + gate_ref[:, k:k + 1] * buf[k]
    o_ref[...] = _rms(x, g_ref[...])


def _combine_final(yb, dest, gates, x1, g):
    T = x1.shape[0]
    tm = COMBINE_TILE
    row = lambda i: (i, 0)
    return pl.pallas_call(
        _combine_kernel,
        grid=(T // tm,),
        in_specs=[pl.BlockSpec((None, 1, tm * TOP_K), lambda i: (i, 0, 0), memory_space=pltpu.SMEM),
                  pl.BlockSpec((tm, TOP_K), row),
                  pl.BlockSpec((tm, D_MODEL), row),
                  pl.BlockSpec((1, D_MODEL), lambda i: (0, 0)),
                  pl.BlockSpec(memory_space=pl.ANY)],
        out_specs=pl.BlockSpec((tm, D_MODEL), row),
        out_shape=jax.ShapeDtypeStruct((T, D_MODEL), F32),
        scratch_shapes=[pltpu.VMEM((TOP_K, tm, D_MODEL), F32), pltpu.SemaphoreType.DMA],
        compiler_params=pltpu.CompilerParams(dimension_semantics=("arbitrary",)),
        name="moe_combine_final",
    )(dest.reshape(T // tm, 1, tm * TOP_K), gates, x1, g.reshape(1, D_MODEL), yb)


def _fft_tables(nc, nf, rows_in, real_in):
    n = nc * nf
    kc = np.arange(nc, dtype=np.float64)
    a1 = 2.0 * np.pi * np.outer(kc, np.arange(rows_in)) / nc
    c1, s1 = np.cos(a1), np.sin(a1)
    l1 = np.concatenate([c1, -s1], 0) if real_in else np.block([[c1, s1], [-s1, c1]])
    a2 = 2.0 * np.pi * np.outer(np.arange(nf), np.arange(nf)) / nf
    c2, s2 = np.cos(a2), np.sin(a2)
    m2 = np.block([[c2, s2], [-s2, c2]])
    m2i = np.block([[c2, -s2], [s2, c2]])
    a3 = 2.0 * np.pi * np.outer(np.arange(nc // 2), kc) / nc
    c3, s3 = np.cos(a3), np.sin(a3)
    l3 = np.block([[c3, -s3], [s3, c3]]) / n
    return tuple(jnp.asarray(m, F32).astype(BF16) for m in (l1, m2, m2i, l3))


def _twiddle(nc, nf):
    n = nc * nf
    ph = (jnp.arange(nf, dtype=jnp.int32)[:, None] * jnp.arange(nc, dtype=jnp.int32)[None, :]) % n
    ang = ph.astype(F32) * (2.0 * math.pi / n)
    tw = jnp.stack([jnp.cos(ang), jnp.sin(ang)], axis=1)
    return jnp.broadcast_to(tw[..., None], (nf, 2, nc, LANES))


def _lane_tile(t, width):
    return t if width == LANES else jnp.concatenate([t] * (width // LANES), axis=1)


def _fft1_kernel(z_ref, l1_ref, tw_ref, o_ref, *, nc, real_in):
    cb = o_ref.shape[-1]
    for j in range(SUB):
        if real_in:
            rhs = z_ref[:, j, :].astype(BF16)
        else:
            rhs = jnp.concatenate([z_ref[0, :, j, :], z_ref[1, :, j, :]], axis=0).astype(BF16)
        a = jnp.dot(l1_ref[...], rhs, preferred_element_type=F32)
        ar, ai = a[:nc], a[nc:]
        twr = _lane_tile(tw_ref[j, 0], cb)
        twi = _lane_tile(tw_ref[j, 1], cb)
        o_ref[j, 0] = ar * twr + ai * twi
        o_ref[j, 1] = ai * twr - ar * twi


def _fft1(z, l1, tw, nc, nf, real_in):
    C = z.shape[-1]
    cb = FFT_CB
    if real_in:
        P = 1
        z_spec = pl.BlockSpec((nc, SUB, cb), lambda p, j, c: (0, j, c))
    else:
        P = z.shape[0]
        z_spec = pl.BlockSpec((None, 2, nc // 2, SUB, cb), lambda p, j, c: (p, 0, 0, j, c))
    return pl.pallas_call(
        functools.partial(_fft1_kernel, nc=nc, real_in=real_in),
        grid=(P, nf // SUB, C // cb),
        in_specs=[z_spec,
                  pl.BlockSpec(l1.shape, lambda p, j, c: (0, 0)),
                  pl.BlockSpec((SUB, 2, nc, LANES), lambda p, j, c: (j, 0, 0, 0))],
        out_specs=pl.BlockSpec((None, SUB, 2, nc, cb), lambda p, j, c: (p, j, 0, 0, c)),
        out_shape=jax.ShapeDtypeStruct((P, nf, 2, nc, C), F32),
        compiler_params=pltpu.CompilerParams(dimension_semantics=("parallel", "parallel", "parallel"),
                                             vmem_limit_bytes=VMEM_LIMIT),
        name="hyena_fft_stage1",
    )(z, l1, tw)


def _fft2_filter_kernel(a_ref, m2_ref, k_ref, *, nf):
    for j in range(SUB):
        rhs = jnp.concatenate([a_ref[:, 0, j, :], a_ref[:, 1, j, :]], axis=0).astype(BF16)
        x = jnp.dot(m2_ref[...], rhs, preferred_element_type=F32)
        k_ref[j, 0] = x[:nf]
        k_ref[j, 1] = x[nf:]


def _fft2_filter(a, m2, nc, nf):
    C = a.shape[-1]
    cb = FFT_CB
    return pl.pallas_call(
        functools.partial(_fft2_filter_kernel, nf=nf),
        grid=(nc // SUB, C // cb),
        in_specs=[pl.BlockSpec((None, nf, 2, SUB, cb), lambda k, c: (0, 0, 0, k, c)),
                  pl.BlockSpec(m2.shape, lambda k, c: (0, 0))],
        out_specs=pl.BlockSpec((SUB, 2, nf, cb), lambda k, c: (k, 0, 0, c)),
        out_shape=jax.ShapeDtypeStruct((nc, 2, nf, C), F32),
        compiler_params=pltpu.CompilerParams(dimension_semantics=("parallel", "parallel"),
                                             vmem_limit_bytes=VMEM_LIMIT),
        name="hyena_filter_spectrum",
    )(a, m2)


def _fft2_kernel(a_ref, k_ref, m2_ref, m2i_ref, o_ref, *, nf):
    for j in range(SUB):
        rhs = jnp.concatenate([a_ref[:, 0, j, :], a_ref[:, 1, j, :]], axis=0).astype(BF16)
        x = jnp.dot(m2_ref[...], rhs, preferred_element_type=F32)
        xr, xi = x[:nf], x[nf:]
        kr, ki = k_ref[j, 0], k_ref[j, 1]
        y = jnp.concatenate([xr * kr - xi * ki, xr * ki + xi * kr], axis=0).astype(BF16)
        b = jnp.dot(m2i_ref[...], y, preferred_element_type=F32)
        o_ref[:, 0, j, :] = b[:nf]
        o_ref[:, 1, j, :] = b[nf:]


def _fft2(a, kspec, m2, m2i, nc, nf):
    P, C = a.shape[0], a.shape[-1]
    cb = FFT_CB
    blk = pl.BlockSpec((None, nf, 2, SUB, cb), lambda p, k, c: (p, 0, 0, k, c))
    return pl.pallas_call(
        functools.partial(_fft2_kernel, nf=nf),
        grid=(P, nc // SUB, C // cb),
        in_specs=[blk,
                  pl.BlockSpec((SUB, 2, nf, cb), lambda p, k, c: (k, 0, 0, c)),
                  pl.BlockSpec(m2.shape, lambda p, k, c: (0, 0)),
                  pl.BlockSpec(m2i.shape, lambda p, k, c: (0, 0))],
        out_specs=blk,
        out_shape=jax.ShapeDtypeStruct(a.shape, F32),
        compiler_params=pltpu.CompilerParams(dimension_semantics=("parallel", "parallel", "parallel"),
                                             vmem_limit_bytes=VMEM_LIMIT),
        name="hyena_fft_stage2",
    )(a, kspec, m2, m2i)


def _fft3_kernel(b_ref, tw_ref, l3_ref, s_ref, x0_ref, bias_ref, o_ref, *, nc):
    cb = o_ref.shape[-1]
    half = nc // 2
    for j in range(SUB):
        br, bi = b_ref[j, 0], b_ref[j, 1]
        twr = _lane_tile(tw_ref[j, 0], cb)
        twi = _lane_tile(tw_ref[j, 1], cb)
        rhs = jnp.concatenate([br * twr - bi * twi, br * twi + bi * twr], axis=0).astype(BF16)
        y = jnp.dot(l3_ref[...], rhs, preferred_element_type=F32)
        for q in range(2):
            s = s_ref[q, :, j, :]
            o_ref[q, :, j, :] = x0_ref[q, :, j, :] * (y[q * half:(q + 1) * half] + s * bias_ref[...])


def _fft3(b, tw, l3, s5, x05, bias, nc, nf):
    P, C = b.shape[0], b.shape[-1]
    cb = FFT_CB
    seq = pl.BlockSpec((None, 2, nc // 2, SUB, cb), lambda p, j, c: (p, 0, 0, j, c))
    return pl.pallas_call(
        functools.partial(_fft3_kernel, nc=nc),
        grid=(P, nf // SUB, C // cb),
        in_specs=[pl.BlockSpec((None, SUB, 2, nc, cb), lambda p, j, c: (p, j, 0, 0, c)),
                  pl.BlockSpec((SUB, 2, nc, LANES), lambda p, j, c: (j, 0, 0, 0)),
                  pl.BlockSpec(l3.shape, lambda p, j, c: (0, 0)),
                  seq, seq,
                  pl.BlockSpec((1, cb), lambda p, j, c: (0, c))],
        out_specs=seq,
        out_shape=jax.ShapeDtypeStruct(s5.shape, F32),
        compiler_params=pltpu.CompilerParams(dimension_semantics=("parallel", "parallel", "parallel"),
                                             vmem_limit_bytes=VMEM_LIMIT),
        name="hyena_fft_stage3",
    )(b, tw, l3, s5, x05, bias)


def _fft_split(n):
    nf = 1 << (int(math.log2(n)) // 2)
    return n // nf, nf


def _hyena_conv(s, x0, k2, bias):
    B, L, C = s.shape
    nc, nf = _fft_split(2 * L)
    tw = _twiddle(nc, nf)
    l1f, m2, m2i, l3 = _fft_tables(nc, nf, nc, True)
    l1 = _fft_tables(nc, nf, nc // 2, False)[0]
    kspec = _fft2_filter(_fft1(k2.reshape(nc, nf, C), l1f, tw, nc, nf, True), m2, nc, nf)
    s5 = s.reshape(B // 2, 2, nc // 2, nf, C)
    x05 = x0.reshape(B // 2, 2, nc // 2, nf, C)
    a = _fft1(s5, l1, tw, nc, nf, False)
    b = _fft2(a, kspec, m2, m2i, nc, nf)
    return _fft3(b, tw, l3, s5, x05, bias.reshape(1, C), nc, nf).reshape(B, L, C)


def _hyena_filter(L, w1, b1, w2, b2, w3, freq):
    pos = jnp.arange(L, dtype=F32)
    t = jnp.linspace(0.0, 1.0, L, dtype=F32)[:, None]
    bands = jnp.linspace(1e-4, HY_BANDS - 1, HY_BANDS, dtype=F32)
    ang = (2.0 * math.pi / L) * pos[:, None] * bands[None, :]
    z = jnp.concatenate([t, jnp.cos(ang), -jnp.sin(ang)], axis=-1)
    h = jnp.sin(freq * (z @ w1 + b1))
    h = jnp.sin(freq * (h @ w2 + b2))
    h = h @ w3
    deltas = jnp.abs(jnp.linspace(math.log(HY_DECAY_TARGET) / HY_SLOW_PCT,
                                  math.log(HY_DECAY_TARGET) / HY_FAST_PCT, HY_CH, dtype=F32))
    window = jnp.exp(-t * deltas[None, :])
    h_f = h[:, :HY_CH] * window
    h_b = h[:, HY_CH:] * window
    l1 = jnp.sum(jnp.abs(h_f), axis=0) + jnp.sum(jnp.abs(h_b[1:]), axis=0)
    k2 = jnp.concatenate([h_f, jnp.zeros((1, HY_CH), F32), h_b[:0:-1]], axis=0)
    return k2 / l1


def _split3(x):
    x1 = x.astype(BF16)
    r = x - x1.astype(F32)
    x2 = r.astype(BF16)
    x3 = (r - x2.astype(F32)).astype(BF16)
    return x1, x2, x3


def _mm(a, b):
    return jnp.dot(a.astype(BF16), b.astype(BF16), preferred_element_type=F32)


def _delta_kernel(qf_ref, kf_ref, vf_ref, gf_ref, qb_ref, kb_ref, vb_ref, gb_ref, of_ref, ob_ref, s_ref,
                  *, n_sub):
    @pl.when(pl.program_id(1) == 0)
    def _():
        s_ref[...] = jnp.zeros_like(s_ref)

    row = lax.broadcasted_iota(jnp.int32, (CHUNK, CHUNK), 0)
    col = lax.broadcasted_iota(jnp.int32, (CHUNK, CHUNK), 1)
    eye_f = jnp.where(row == col, 1.0, 0.0).astype(F32)
    dirs = ((qf_ref, kf_ref, vf_ref, gf_ref, of_ref, row >= col, row > col),
            (qb_ref, kb_ref, vb_ref, gb_ref, ob_ref, row <= col, row < col))

    ch = []
    for d, (q_ref, k_ref, v_ref, g_ref, _, incl, strict) in enumerate(dirs):
        tri = jnp.where(incl, 1.0, 0.0).astype(BF16)
        for c in range(n_sub):
            rows = slice(c * CHUNK, (c + 1) * CHUNK)
            gates = g_ref[rows, :]
            gl = 2 * DN_HEADS * d
            g1, g2, g3 = _split3(gates)
            gc_all = (jnp.dot(tri, g1, preferred_element_type=F32)
                      + jnp.dot(tri, g2, preferred_element_type=F32)
                      + jnp.dot(tri, g3, preferred_element_type=F32))
            gsum_all = jnp.sum(gates, axis=0, keepdims=True)
            for h in range(DN_HEADS):
                lanes = slice(h * DN_DK, (h + 1) * DN_DK)
                kh = k_ref[rows, lanes]
                beta = gates[:, gl + DN_HEADS + h:gl + DN_HEADS + h + 1]
                gc = gc_all[:, gl + h:gl + h + 1]
                g_last = gsum_all[:, gl + h:gl + h + 1]
                gc_b = jnp.broadcast_to(gc, (CHUNK, CHUNK))
                gc_row = jnp.sum(gc_b * eye_f, axis=0, keepdims=True)
                decay = jnp.where(incl, jnp.exp(jnp.minimum(gc_b - gc_row, 0.0)), 0.0)
                e_gc = jnp.exp(gc)
                kb = kh * beta
                ch.append(dict(d=d, c=c, h=h, lanes=lanes, rows=rows, strict=strict, decay=decay,
                               kh16=kh.astype(BF16), kb16=kb.astype(BF16),
                               q16=q_ref[rows, lanes].astype(BF16),
                               rhs=jnp.concatenate([v_ref[rows, lanes] * beta, kb * e_gc],
                                                   axis=1).astype(BF16),
                               qe=q_ref[rows, lanes] * e_gc,
                               kdT=(kh * jnp.exp(g_last - gc)).T.astype(BF16),
                               e_last=jnp.exp(g_last)))
    for x in ch:
        a = lax.dot_general(x["kb16"], x["kh16"], _NT, preferred_element_type=F32) * x["decay"]
        x["a"] = jnp.where(x["strict"], a, 0.0)
        x["qk"] = (lax.dot_general(x["q16"], x["kh16"], _NT, preferred_element_type=F32)
                   * x["decay"]).astype(BF16)
    for x in ch:
        x["t"] = eye_f - x["a"]
        x["p"] = _mm(x["a"], x["a"])
    for level in range(5):
        for x in ch:
            p16 = x["p"].astype(BF16)
            x["t"] = x["t"] + jnp.dot(x["t"].astype(BF16), p16, preferred_element_type=F32)
            if level < 4:
                x["p"] = jnp.dot(p16, p16, preferred_element_type=F32)
    for x in ch:
        uw = jnp.dot(x["t"].astype(BF16), x["rhs"], preferred_element_type=F32)
        x["u"] = uw[:, :DN_DV]
        x["wq"] = jnp.concatenate([uw[:, DN_DV:], x["qe"]], axis=0).astype(BF16)

    for step in range(n_sub):
        cur = [x for x in ch if x["c"] == (step if x["d"] == 0 else n_sub - 1 - step)]
        for x in cur:
            x["s"] = s_ref[x["d"], x["h"]]
            x["ws"] = jnp.dot(x["wq"], x["s"].astype(BF16), preferred_element_type=F32)
        for x in cur:
            x["vn"] = (x["u"] - x["ws"][:CHUNK]).astype(BF16)
        for x in cur:
            o = x["ws"][CHUNK:] + jnp.dot(x["qk"], x["vn"], preferred_element_type=F32)
            dirs[x["d"]][4][x["rows"], x["lanes"]] = o
            s_ref[x["d"], x["h"]] = (x["s"] * x["e_last"]
                                     + jnp.dot(x["kdT"], x["vn"], preferred_element_type=F32))


def _delta_scan(q, k, v, gates):
    B, L, _ = q.shape
    n_sub = DN_STEP_CHUNKS
    rows = n_sub * CHUNK
    nblk = L // rows
    fwd = pl.BlockSpec((None, rows, DN_W), lambda b, j: (b, j, 0))
    bwd = pl.BlockSpec((None, rows, DN_W), lambda b, j: (b, nblk - 1 - j, 0))
    gfwd = pl.BlockSpec((None, rows, LANES), lambda b, j: (b, j, 0))
    gbwd = pl.BlockSpec((None, rows, LANES), lambda b, j: (b, nblk - 1 - j, 0))
    out = jax.ShapeDtypeStruct((B, L, DN_W), F32)
    return pl.pallas_call(
        functools.partial(_delta_kernel, n_sub=n_sub),
        grid=(B, nblk),
        in_specs=[fwd, fwd, fwd, gfwd, bwd, bwd, bwd, gbwd],
        out_specs=[fwd, bwd],
        out_shape=[out, out],
        scratch_shapes=[pltpu.VMEM((2, DN_HEADS, DN_DK, DN_DV), F32)],
        compiler_params=pltpu.CompilerParams(dimension_semantics=("parallel", "arbitrary"),
                                             vmem_limit_bytes=VMEM_LIMIT),
        name="delta_scan",
    )(q, k, v, gates, q, k, v, gates)


def _trunk(x, w, hy_pos_w1, hy_pos_b1, hy_pos_w2, hy_pos_b2, hy_pos_w3, hy_sin_freq, hy_bias, dn_norm_w):
    B, L, D = x.shape
    T = B * L
    x2d = x.reshape(T, D)
    x0, s, q, k, v, z, dn_gates = _in_proj(x2d, L, w["norm_mix"], *w["in_proj"])
    seq = lambda a: a.reshape(B, L, a.shape[-1])
    k2 = _hyena_filter(L, hy_pos_w1, hy_pos_b1, hy_pos_w2, hy_pos_b2, hy_pos_w3, hy_sin_freq)
    y_hy = _hyena_conv(seq(s), seq(x0), k2, hy_bias)
    o_f, o_b = _delta_scan(seq(q), seq(k), seq(v), seq(dn_gates))
    x1, h_ffn, logits = _out_proj(x2d, y_hy.reshape(T, HY_CH), o_f.reshape(T, DN_W), o_b.reshape(T, DN_W),
                                 z, dn_norm_w, w["out_hy"], w["out_dn"], w["norm_ffn"],
                                 w["router_hi"], w["router_lo"], w["router_b"])
    yb, dest, gates = _moe(h_ffn, logits[:, :N_EXPERTS], w["wg"], w["wl"], w["bg"], w["bl"], w["wd"],
                           w["bd"])
    return _combine_final(yb, dest, gates, x1, w["norm_final"]).reshape(B, L, D)


def kernel(x_prompt, x_sample, norm_mix_w, w_in, hy_conv_w, hy_conv_b, hy_pos_w1, hy_pos_b1, hy_pos_w2, hy_pos_b2, hy_pos_w3, hy_sin_freq, hy_bias, dn_conv_w, dn_a_log, dn_dt_bias, dn_norm_w, w_out, norm_ffn_w, w_router, b_router, w_gate_up, b_gate_up, w_down, b_down, norm_final_w):
    w_out16 = w_out[0].astype(BF16)
    wr = jnp.pad(w_router[0], ((0, 0), (0, LANES - N_EXPERTS)))
    wr_hi = wr.astype(BF16)
    wg, wl = _deinterleave(w_gate_up[0])
    w = {
        "norm_mix": norm_mix_w[0],
        "in_proj": _in_proj_params(w_in[0], hy_conv_w[0], hy_conv_b[0], dn_conv_w[0], dn_a_log[0],
                                   dn_dt_bias[0]),
        "out_hy": w_out16[:HY_CH],
        "out_dn": w_out16[HY_CH:],
        "norm_ffn": norm_ffn_w[0],
        "router_hi": wr_hi,
        "router_lo": (wr - wr_hi.astype(F32)).astype(BF16),
        "router_b": jnp.pad(b_router[0], (0, LANES - N_EXPERTS)).reshape(1, LANES),
        "wg": wg,
        "wl": wl,
        "bg": b_gate_up[0][:, 0::2].reshape(N_EXPERTS, 1, D_FF),
        "bl": b_gate_up[0][:, 1::2].reshape(N_EXPERTS, 1, D_FF),
        "wd": w_down[0].astype(BF16),
        "bd": b_down[0].reshape(N_EXPERTS, 1, D_MODEL),
        "norm_final": norm_final_w,
    }
    mix = (hy_pos_w1[0], hy_pos_b1[0], hy_pos_w2[0], hy_pos_b2[0], hy_pos_w3[0], hy_sin_freq[0],
           hy_bias[0], dn_norm_w[0])
    return (_trunk(x_prompt, w, *mix), _trunk(x_sample, w, *mix))
```

```python
import functools
import math

import jax
import jax.numpy as jnp
import numpy as np
from jax import lax
from jax.experimental import pallas as pl
from jax.experimental.pallas import tpu as pltpu

D_MODEL = 1024
HY_CH = 512
DN_HEADS = 4
DN_DK = 128
DN_DV = 128
DN_QK = DN_HEADS * DN_DK
DN_W = DN_HEADS * DN_DV
HY_IN = 3 * HY_CH
DN_CONV = 2 * DN_QK + DN_W
N_GATE = 4 * DN_HEADS
SHORT_CONV = 3
CONV_COLS = HY_IN + DN_CONV
REST_COLS = DN_W + 128
HY_EMB = 33
HY_BANDS = (HY_EMB - 1) // 2
HY_DECAY_TARGET = 1e-2
HY_FAST_PCT = 0.3
HY_SLOW_PCT = 1.5
CHUNK = 64
N_EXPERTS = 32
TOP_K = 4
D_FF = D_MODEL
SWIGLU_ALPHA = 1.702
SWIGLU_LIMIT = 7.0
MOE_BLOCK = 512
EPS = 1e-6

LANES = 128
SUB = 8
ROW_TILE = 512
FFT_CB = 256
DN_STEP_CHUNKS = 4
DEINT_COLS = 512
DISPATCH_TILE = 512
COMBINE_TILE = 512
VMEM_LIMIT = 56 * 1024 * 1024

F32 = jnp.float32
BF16 = jnp.bfloat16

_NT = (((1,), (1,)), ((), ()))


def _rms(x, g):
    return x * lax.rsqrt(jnp.mean(x * x, axis=-1, keepdims=True) + EPS) * g


def _silu(x):
    return x * jax.nn.sigmoid(x)


def _head_l2norm(x):
    parts = []
    for hd in range(DN_HEADS):
        xh = x[:, hd * DN_DK:(hd + 1) * DN_DK]
        parts.append(xh * lax.rsqrt(jnp.sum(xh * xh, axis=-1, keepdims=True) + EPS))
    return jnp.concatenate(parts, axis=1)


def _in_proj_kernel(xp_ref, x_ref, xn_ref, g_ref, wc_ref, wr_ref, cw_ref, cb_ref, gt_ref,
                    x0_ref, s_ref, q_ref, k_ref, v_ref, z_ref, gate_ref, p_scr, *, tiles_per_seq):
    i = pl.program_id(0)
    first = (i % tiles_per_seq) == 0
    last = (i % tiles_per_seq) == tiles_per_seq - 1
    g = g_ref[...]
    h = _rms(x_ref[...], g).astype(BF16)
    hp = _rms(xp_ref[...], g).astype(BF16)
    hn = _rms(xn_ref[...], g).astype(BF16)
    def project(c0):
        cols = slice(c0, c0 + HY_CH)
        w = wc_ref[:, cols]
        p_scr[pl.ds(SUB, ROW_TILE), cols] = jnp.dot(h, w, preferred_element_type=F32)
        p_scr[pl.ds(0, SUB), cols] = jnp.where(first, 0.0, jnp.dot(hp, w, preferred_element_type=F32))
        p_scr[pl.ds(SUB + ROW_TILE, SUB), cols] = jnp.where(last, 0.0,
                                                            jnp.dot(hn, w, preferred_element_type=F32))

    def conv(c0):
        cols = slice(c0, c0 + HY_CH)
        return (p_scr[pl.ds(SUB - 1, ROW_TILE), cols] * cw_ref[0:1, cols]
                + p_scr[pl.ds(SUB, ROW_TILE), cols] * cw_ref[1:2, cols]
                + p_scr[pl.ds(SUB + 1, ROW_TILE), cols] * cw_ref[2:3, cols])

    project(0)
    project(HY_CH)
    x0_ref[...] = conv(0) + cb_ref[:, 0:HY_CH]
    project(2 * HY_CH)
    project(HY_IN)
    s_ref[...] = (conv(HY_CH) + cb_ref[:, HY_CH:2 * HY_CH]) * (conv(2 * HY_CH) + cb_ref[:, 2 * HY_CH:])
    project(HY_IN + DN_QK)
    q_ref[...] = _head_l2norm(_silu(conv(HY_IN))) * (DN_DK ** -0.5)
    project(HY_IN + 2 * DN_QK)
    k_ref[...] = _head_l2norm(_silu(conv(HY_IN + DN_QK)))
    rest = jnp.dot(h, wr_ref[...], preferred_element_type=F32)
    v_ref[...] = _silu(conv(HY_IN + 2 * DN_QK))
    z_ref[...] = rest[:, :DN_W]
    a = rest[:, DN_W:] + gt_ref[1:2]
    softplus = jnp.maximum(a, 0.0) + jnp.log(1.0 + jnp.exp(-jnp.abs(a)))
    gate_ref[...] = jnp.where(gt_ref[2:3] > 0.5, -gt_ref[0:1] * softplus, jax.nn.sigmoid(rest[:, DN_W:]))


def _in_proj(x2d, seq_len, g, w_conv, w_rest, conv_w, conv_b, gate_tab):
    T = x2d.shape[0]
    per = ROW_TILE // SUB
    last_sub = T // SUB - 1
    const = lambda i: (0, 0)
    row = lambda i: (i, 0)
    o512 = pl.BlockSpec((ROW_TILE, HY_CH), row)
    s512 = jax.ShapeDtypeStruct((T, HY_CH), F32)
    return pl.pallas_call(
        functools.partial(_in_proj_kernel, tiles_per_seq=seq_len // ROW_TILE),
        grid=(T // ROW_TILE,),
        in_specs=[pl.BlockSpec((SUB, D_MODEL), lambda i: (jnp.maximum(i * per - 1, 0), 0)),
                  pl.BlockSpec((ROW_TILE, D_MODEL), row),
                  pl.BlockSpec((SUB, D_MODEL), lambda i: (jnp.minimum((i + 1) * per, last_sub), 0)),
                  pl.BlockSpec((1, D_MODEL), const),
                  pl.BlockSpec((D_MODEL, CONV_COLS), const),
                  pl.BlockSpec((D_MODEL, REST_COLS), const),
                  pl.BlockSpec((SHORT_CONV, CONV_COLS), const),
                  pl.BlockSpec((1, HY_IN), const),
                  pl.BlockSpec((3, LANES), const)],
        out_specs=[o512, o512, o512, o512, o512, o512, pl.BlockSpec((ROW_TILE, LANES), row)],
        out_shape=[s512, s512, s512, s512, s512, s512, jax.ShapeDtypeStruct((T, LANES), F32)],
        scratch_shapes=[pltpu.VMEM((ROW_TILE + 2 * SUB, CONV_COLS), F32)],
        compiler_params=pltpu.CompilerParams(dimension_semantics=("parallel",),
                                             vmem_limit_bytes=VMEM_LIMIT),
        name="in_proj",
    )(x2d, x2d, x2d, g.reshape(1, D_MODEL), w_conv, w_rest, conv_w, conv_b, gate_tab)


def _in_proj_params(w_in, hy_conv_w, hy_conv_b, dn_conv_w, dn_a_log, dn_dt_bias):
    H = DN_HEADS
    w16 = w_in.astype(BF16)
    gc = w16[:, CONV_COLS + DN_W:]
    gc = jnp.concatenate([gc[:, 0:H], gc[:, 2 * H:3 * H], gc[:, H:2 * H], gc[:, 3 * H:]], axis=1)
    w_rest = jnp.concatenate([w16[:, CONV_COLS:CONV_COLS + DN_W],
                              jnp.pad(gc, ((0, 0), (0, LANES - N_GATE)))], axis=1)
    zero, one, pad = jnp.zeros((H,), F32), jnp.ones((H,), F32), jnp.zeros((LANES - N_GATE,), F32)
    gate_tab = jnp.stack([jnp.concatenate([jnp.exp(dn_a_log[0]), zero, jnp.exp(dn_a_log[1]), zero, pad]),
                          jnp.concatenate([dn_dt_bias[0], zero, dn_dt_bias[1], zero, pad]),
                          jnp.concatenate([one, zero, one, zero, pad])])
    return (w16[:, :CONV_COLS], w_rest, jnp.concatenate([hy_conv_w, dn_conv_w], axis=1),
            hy_conv_b.reshape(1, HY_IN), gate_tab)


def _out_proj_kernel(x_ref, yh_ref, of_ref, ob_ref, z_ref, nw_ref, wh_ref, wd_ref, g_ref, wrh_ref,
                     wrl_ref, br_ref, x1_ref, h_ref, lg_ref):
    o = of_ref[...] + ob_ref[...]
    heads = []
    for hd in range(DN_HEADS):
        oh = o[:, hd * DN_DV:(hd + 1) * DN_DV]
        heads.append(oh * lax.rsqrt(jnp.mean(oh * oh, axis=-1, keepdims=True) + EPS))
    z = z_ref[...]
    y_dn = jnp.concatenate(heads, axis=1) * nw_ref[...] * (z * jax.nn.sigmoid(z))
    x1 = (x_ref[...]
          + jnp.dot(yh_ref[...].astype(BF16), wh_ref[...], preferred_element_type=F32)
          + jnp.dot(y_dn.astype(BF16), wd_ref[...], preferred_element_type=F32))
    x1_ref[...] = x1
    h = _rms(x1, g_ref[...])
    h_hi = h.astype(BF16)
    h_lo = (h - h_hi.astype(F32)).astype(BF16)
    h_ref[...] = h
    lg_ref[...] = (jnp.dot(h_hi, wrh_ref[...], preferred_element_type=F32)
                   + jnp.dot(h_lo, wrh_ref[...], preferred_element_type=F32)
                   + jnp.dot(h_hi, wrl_ref[...], preferred_element_type=F32)
                   + br_ref[...])


def _out_proj(x2d, y_hy, o_f, o_b, z, dn_norm_w, w_oh, w_od, g, wr_hi, wr_lo, br):
    T = x2d.shape[0]
    const = lambda i: (0, 0)
    row = lambda i: (i, 0)
    return pl.pallas_call(
        _out_proj_kernel,
        grid=(T // ROW_TILE,),
        in_specs=[pl.BlockSpec((ROW_TILE, D_MODEL), row),
                  pl.BlockSpec((ROW_TILE, HY_CH), row),
                  pl.BlockSpec((ROW_TILE, DN_W), row),
                  pl.BlockSpec((ROW_TILE, DN_W), row),
                  pl.BlockSpec((ROW_TILE, DN_W), row),
                  pl.BlockSpec((1, DN_W), const),
                  pl.BlockSpec((HY_CH, D_MODEL), const),
                  pl.BlockSpec((DN_W, D_MODEL), const),
                  pl.BlockSpec((1, D_MODEL), const),
                  pl.BlockSpec((D_MODEL, LANES), const),
                  pl.BlockSpec((D_MODEL, LANES), const),
                  pl.BlockSpec((1, LANES), const)],
        out_specs=[pl.BlockSpec((ROW_TILE, D_MODEL), row),
                   pl.BlockSpec((ROW_TILE, D_MODEL), row),
                   pl.BlockSpec((ROW_TILE, LANES), row)],
        out_shape=[jax.ShapeDtypeStruct((T, D_MODEL), F32),
                   jax.ShapeDtypeStruct((T, D_MODEL), F32),
                   jax.ShapeDtypeStruct((T, LANES), F32)],
        compiler_params=pltpu.CompilerParams(dimension_semantics=("parallel",),
                                             vmem_limit_bytes=VMEM_LIMIT),
        name="out_proj_router",
    )(x2d, y_hy, o_f, o_b, z, jnp.tile(dn_norm_w, DN_HEADS).reshape(1, DN_W), w_oh, w_od,
      g.reshape(1, D_MODEL), wr_hi, wr_lo, br)


def _deint_kernel(w_ref, p_ref, og_ref, ol_ref):
    half = DEINT_COLS // 2
    sel = jnp.dot(w_ref[0].astype(BF16), p_ref[...], preferred_element_type=F32)
    og_ref[0] = sel[:, :half].astype(BF16)
    ol_ref[0] = sel[:, half:].astype(BF16)


def _deinterleave(w_gate_up):
    half = DEINT_COLS // 2
    r = np.arange(DEINT_COLS)[:, None]
    c = np.arange(DEINT_COLS)[None, :]
    perm = jnp.asarray(np.where(c < half, r == 2 * c, r == 2 * (c - half) + 1), BF16)
    out = jax.ShapeDtypeStruct((N_EXPERTS, D_MODEL, D_FF), BF16)
    return pl.pallas_call(
        _deint_kernel,
        grid=(N_EXPERTS, 2 * D_FF // DEINT_COLS),
        in_specs=[pl.BlockSpec((1, D_MODEL, DEINT_COLS), lambda e, j: (e, 0, j)),
                  pl.BlockSpec((DEINT_COLS, DEINT_COLS), lambda e, j: (0, 0))],
        out_specs=[pl.BlockSpec((1, D_MODEL, half), lambda e, j: (e, 0, j)),
                   pl.BlockSpec((1, D_MODEL, half), lambda e, j: (e, 0, j))],
        out_shape=[out, out],
        compiler_params=pltpu.CompilerParams(dimension_semantics=("parallel", "parallel")),
        name="deinterleave_gate_up",
    )(w_gate_up, perm)


def _dispatch_kernel(tail_ref, dest_ref, h_ref, xb_ref, zero_scr, sem):
    n = h_ref.shape[0]

    @pl.when(pl.program_id(0) == 0)
    def _():
        zero_scr[...] = jnp.zeros_like(zero_scr)
        for e in range(N_EXPERTS):
            t0 = pl.multiple_of(tail_ref[0, e], SUB)
            pltpu.make_async_copy(zero_scr, xb_ref.at[pl.ds(t0, MOE_BLOCK), :], sem).start()
        for e in range(N_EXPERTS):
            pltpu.make_async_copy(zero_scr, xb_ref.at[pl.ds(0, MOE_BLOCK), :], sem).wait()

    def issue(r, c):
        for k in range(TOP_K):
            d = dest_ref[0, r * TOP_K + k]
            pltpu.make_async_copy(h_ref.at[pl.ds(r, 1), :], xb_ref.at[pl.ds(d, 1), :], sem).start()
        return c

    lax.fori_loop(0, n, issue, 0, unroll=8)

    def drain(r, c):
        for k in range(TOP_K):
            pltpu.make_async_copy(h_ref.at[pl.ds(0, 1), :], xb_ref.at[pl.ds(0, 1), :], sem).wait()
        return c

    lax.fori_loop(0, n, drain, 0, unroll=8)


def _dispatch(h, dest, tail_start, n_rows):
    T = h.shape[0]
    tm = DISPATCH_TILE
    return pl.pallas_call(
        _dispatch_kernel,
        grid=(T // tm,),
        in_specs=[pl.BlockSpec((1, N_EXPERTS), lambda i: (0, 0), memory_space=pltpu.SMEM),
                  pl.BlockSpec((None, 1, tm * TOP_K), lambda i: (i, 0, 0), memory_space=pltpu.SMEM),
                  pl.BlockSpec((tm, D_MODEL), lambda i: (i, 0))],
        out_specs=pl.BlockSpec(memory_space=pl.ANY),
        out_shape=jax.ShapeDtypeStruct((n_rows, D_MODEL), F32),
        scratch_shapes=[pltpu.VMEM((MOE_BLOCK, D_MODEL), F32), pltpu.SemaphoreType.DMA],
        compiler_params=pltpu.CompilerParams(dimension_semantics=("arbitrary",)),
        name="moe_dispatch",
    )(tail_start.reshape(1, N_EXPERTS), dest.reshape(T // tm, 1, tm * TOP_K), h)


def _expert_kernel(be_ref, nb_ref, xb_ref, wg_ref, wl_ref, bg_ref, bl_ref, wd_ref, bd_ref, y_ref):
    i = pl.program_id(0)

    @pl.when(i < nb_ref[0])
    def _():
        xb = xb_ref[...].astype(BF16)
        hg = jnp.dot(xb, wg_ref[0], preferred_element_type=F32) + bg_ref[0]
        hl = jnp.dot(xb, wl_ref[0], preferred_element_type=F32) + bl_ref[0]
        x_glu = jnp.minimum(hg, SWIGLU_LIMIT)
        x_lin = jnp.clip(hl, -SWIGLU_LIMIT, SWIGLU_LIMIT)
        act = x_glu * jax.nn.sigmoid(SWIGLU_ALPHA * x_glu) * (x_lin + 1.0)
        y = jnp.dot(act.astype(BF16), wd_ref[0], preferred_element_type=F32) + bd_ref[0]
        y_ref[...] = y

    @pl.when(i >= nb_ref[0])
    def _():
        y_ref[...] = jnp.zeros_like(y_ref)


def _expert_mlp(xb, block_e, n_used, wg, wl, bg, bl, wd, bd):
    n_rows = xb.shape[0]
    n_blocks = n_rows // MOE_BLOCK
    rowm = lambda i, be, nb: (i, 0)
    exp3 = lambda i, be, nb: (be[i], 0, 0)
    grid_spec = pltpu.PrefetchScalarGridSpec(
        num_scalar_prefetch=2,
        grid=(n_blocks,),
        in_specs=[pl.BlockSpec((MOE_BLOCK, D_MODEL), rowm),
                  pl.BlockSpec((1, D_MODEL, D_FF), exp3),
                  pl.BlockSpec((1, D_MODEL, D_FF), exp3),
                  pl.BlockSpec((1, 1, D_FF), exp3),
                  pl.BlockSpec((1, 1, D_FF), exp3),
                  pl.BlockSpec((1, D_FF, D_MODEL), exp3),
                  pl.BlockSpec((1, 1, D_MODEL), exp3)],
        out_specs=pl.BlockSpec((MOE_BLOCK, D_MODEL), rowm),
    )
    return pl.pallas_call(
        _expert_kernel,
        grid_spec=grid_spec,
        out_shape=jax.ShapeDtypeStruct((n_rows, D_MODEL), F32),
        compiler_params=pltpu.CompilerParams(dimension_semantics=("arbitrary",),
                                             vmem_limit_bytes=VMEM_LIMIT),
        name="expert_mlp",
    )(block_e, n_used, xb, wg, wl, bg, bl, wd, bd)


def _moe(h, logits, wg, wl, bg, bl, wd, bd):
    T = h.shape[0]
    TK = T * TOP_K
    top_vals, top_idx = lax.top_k(logits, TOP_K)
    gates = jax.nn.softmax(top_vals, axis=-1)
    sel = jnp.sum(jax.nn.one_hot(top_idx, N_EXPERTS, dtype=jnp.int32), axis=1)
    before = jnp.cumsum(sel, axis=0) - sel
    counts = jnp.sum(sel, axis=0)
    padded = (counts + MOE_BLOCK - 1) // MOE_BLOCK * MOE_BLOCK
    pad_end = jnp.cumsum(padded)
    pad_start = pad_end - padded
    rank = jnp.take_along_axis(before, top_idx, axis=1)
    dest = (pad_start[top_idx] + rank).astype(jnp.int32)
    n_blocks = (TK + MOE_BLOCK - 1) // MOE_BLOCK + N_EXPERTS
    n_rows = n_blocks * MOE_BLOCK
    block_start = jnp.arange(n_blocks, dtype=jnp.int32) * MOE_BLOCK
    block_e = jnp.minimum(jnp.sum((block_start[:, None] >= pad_end[None, :]).astype(jnp.int32), axis=1),
                          N_EXPERTS - 1)
    n_used = (pad_end[-1] // MOE_BLOCK).astype(jnp.int32).reshape(1)
    tail_start = jnp.minimum((pad_start + counts) // SUB * SUB, n_rows - MOE_BLOCK).astype(jnp.int32)
    xb = _dispatch(h, dest, tail_start, n_rows)
    yb = _expert_mlp(xb, block_e, n_used, wg, wl, bg, bl, wd, bd)
    return yb, dest, gates


def _combine_kernel(dest_ref, gate_ref, x1_ref, g_ref, yb_ref, o_ref, buf, sem):
    n = x1_ref.shape[0]

    def issue(r, c):
        for k in range(TOP_K):
            d = dest_ref[0, r * TOP_K + k]
            pltpu.make_async_copy(yb_ref.at[pl.ds(d, 1), :], buf.at[k, pl.ds(r, 1), :], sem).start()
        return c

    lax.fori_loop(0, n, issue, 0, unroll=8)

    def drain(r, c):
        for k in range(TOP_K):
            pltpu.make_async_copy(yb_ref.at[pl.ds(0, 1), :], buf.at[0, pl.ds(0, 1), :], sem).wait()
        return c

    lax.fori_loop(0, n, drain, 0, unroll=8)
    x = x1_ref[...]
    for k in range(TOP_K):
        x = x + gate_ref[:, k:k + 1] * buf[k]
    o_ref[...] = _rms(x, g_ref[...])


def _combine_final(yb, dest, gates, x1, g):
    T = x1.shape[0]
    tm = COMBINE_TILE
    row = lambda i: (i, 0)
    return pl.pallas_call(
        _combine_kernel,
        grid=(T // tm,),
        in_specs=[pl.BlockSpec((None, 1, tm * TOP_K), lambda i: (i, 0, 0), memory_space=pltpu.SMEM),
                  pl.BlockSpec((tm, TOP_K), row),
                  pl.BlockSpec((tm, D_MODEL), row),
                  pl.BlockSpec((1, D_MODEL), lambda i: (0, 0)),
                  pl.BlockSpec(memory_space=pl.ANY)],
        out_specs=pl.BlockSpec((tm, D_MODEL), row),
        out_shape=jax.ShapeDtypeStruct((T, D_MODEL), F32),
        scratch_shapes=[pltpu.VMEM((TOP_K, tm, D_MODEL), F32), pltpu.SemaphoreType.DMA],
        compiler_params=pltpu.CompilerParams(dimension_semantics=("arbitrary",),
                                             vmem_limit_bytes=VMEM_LIMIT),
        name="moe_combine_final",
    )(dest.reshape(T // tm, 1, tm * TOP_K), gates, x1, g.reshape(1, D_MODEL), yb)


def _fft_tables(nc, nf):
    n = nc * nf
    kc = np.arange(nc, dtype=np.float64)
    a1 = 2.0 * np.pi * np.outer(kc, np.arange(nc // 2)) / nc
    c1, s1 = np.cos(a1), np.sin(a1)
    l1 = np.block([[c1, s1], [-s1, c1]])
    a2 = 2.0 * np.pi * np.outer(np.arange(nf), np.arange(nf)) / nf
    c2, s2 = np.cos(a2), np.sin(a2)
    m2 = np.block([[c2, s2], [-s2, c2]])
    m2i = np.block([[c2, -s2], [s2, c2]])
    a3 = 2.0 * np.pi * np.outer(np.arange(nc // 2), kc) / nc
    c3, s3 = np.cos(a3), np.sin(a3)
    l3 = np.block([[c3, -s3], [s3, c3]]) / n
    return tuple(jnp.asarray(m, F32).astype(BF16) for m in (l1, m2, m2i, l3))


def _twiddle(nc, nf):
    n = nc * nf
    ph = (jnp.arange(nf, dtype=jnp.int32)[:, None] * jnp.arange(nc, dtype=jnp.int32)[None, :]) % n
    ang = ph.astype(F32) * (2.0 * math.pi / n)
    tw = jnp.stack([jnp.cos(ang), jnp.sin(ang)], axis=1)
    return jnp.broadcast_to(tw[..., None], (nf, 2, nc, LANES))


def _lane_tile(t, width):
    return t if width == LANES else jnp.concatenate([t] * (width // LANES), axis=1)


def _fft1_kernel(z_ref, l1_ref, tw_ref, o_ref, *, nc):
    cb = o_ref.shape[-1]
    for j in range(SUB):
        rhs = jnp.concatenate([z_ref[0, :, j, :], z_ref[1, :, j, :]], axis=0).astype(BF16)
        a = jnp.dot(l1_ref[...], rhs, preferred_element_type=F32)
        ar, ai = a[:nc], a[nc:]
        twr = _lane_tile(tw_ref[j, 0], cb)
        twi = _lane_tile(tw_ref[j, 1], cb)
        o_ref[j, 0] = ar * twr + ai * twi
        o_ref[j, 1] = ai * twr - ar * twi


def _fft1(z, l1, tw, nc, nf):
    P, C = z.shape[0], z.shape[-1]
    cb = FFT_CB
    z_spec = pl.BlockSpec((None, 2, nc // 2, SUB, cb), lambda p, j, c: (p, 0, 0, j, c))
    return pl.pallas_call(
        functools.partial(_fft1_kernel, nc=nc),
        grid=(P, nf // SUB, C // cb),
        in_specs=[z_spec,
                  pl.BlockSpec(l1.shape, lambda p, j, c: (0, 0)),
                  pl.BlockSpec((SUB, 2, nc, LANES), lambda p, j, c: (j, 0, 0, 0))],
        out_specs=pl.BlockSpec((None, SUB, 2, nc, cb), lambda p, j, c: (p, j, 0, 0, c)),
        out_shape=jax.ShapeDtypeStruct((P, nf, 2, nc, C), F32),
        compiler_params=pltpu.CompilerParams(dimension_semantics=("parallel", "parallel", "parallel"),
                                             vmem_limit_bytes=VMEM_LIMIT),
        name="hyena_fft_stage1",
    )(z, l1, tw)


def _fft2_filter_kernel(a_ref, m2_ref, k_ref, *, nf):
    for j in range(SUB):
        rhs = jnp.concatenate([a_ref[:, 0, j, :], a_ref[:, 1, j, :]], axis=0).astype(BF16)
        x = jnp.dot(m2_ref[...], rhs, preferred_element_type=F32)
        k_ref[j, 0] = x[:nf]
        k_ref[j, 1] = x[nf:]


def _fft2_filter(a, m2, nc, nf):
    C = a.shape[-1]
    cb = FFT_CB
    return pl.pallas_call(
        functools.partial(_fft2_filter_kernel, nf=nf),
        grid=(nc // SUB, C // cb),
        in_specs=[pl.BlockSpec((None, nf, 2, SUB, cb), lambda k, c: (0, 0, 0, k, c)),
                  pl.BlockSpec(m2.shape, lambda k, c: (0, 0))],
        out_specs=pl.BlockSpec((SUB, 2, nf, cb), lambda k, c: (k, 0, 0, c)),
        out_shape=jax.ShapeDtypeStruct((nc, 2, nf, C), F32),
        compiler_params=pltpu.CompilerParams(dimension_semantics=("parallel", "parallel"),
                                             vmem_limit_bytes=VMEM_LIMIT),
        name="hyena_filter_spectrum",
    )(a, m2)


def _fft2_kernel(a_ref, k_ref, m2_ref, m2i_ref, o_ref, *, nf):
    for j in range(SUB):
        rhs = jnp.concatenate([a_ref[:, 0, j, :], a_ref[:, 1, j, :]], axis=0).astype(BF16)
        x = jnp.dot(m2_ref[...], rhs, preferred_element_type=F32)
        xr, xi = x[:nf], x[nf:]
        kr, ki = k_ref[j, 0], k_ref[j, 1]
        y = jnp.concatenate([xr * kr - xi * ki, xr * ki + xi * kr], axis=0).astype(BF16)
        b = jnp.dot(m2i_ref[...], y, preferred_element_type=F32)
        o_ref[:, 0, j, :] = b[:nf]
        o_ref[:, 1, j, :] = b[nf:]


def _fft2(a, kspec, m2, m2i, nc, nf):
    P, C = a.shape[0], a.shape[-1]
    cb = FFT_CB
    blk = pl.BlockSpec((None, nf, 2, SUB, cb), lambda p, k, c: (p, 0, 0, k, c))
    return pl.pallas_call(
        functools.partial(_fft2_kernel, nf=nf),
        grid=(P, nc // SUB, C // cb),
        in_specs=[blk,
                  pl.BlockSpec((SUB, 2, nf, cb), lambda p, k, c: (k, 0, 0, c)),
                  pl.BlockSpec(m2.shape, lambda p, k, c: (0, 0)),
                  pl.BlockSpec(m2i.shape, lambda p, k, c: (0, 0))],
        out_specs=blk,
        out_shape=jax.ShapeDtypeStruct(a.shape, F32),
        compiler_params=pltpu.CompilerParams(dimension_semantics=("parallel", "parallel", "parallel"),
                                             vmem_limit_bytes=VMEM_LIMIT),
        name="hyena_fft_stage2",
    )(a, kspec, m2, m2i)


def _fft3_kernel(b_ref, tw_ref, l3_ref, s_ref, x0_ref, bias_ref, o_ref, *, nc):
    cb = o_ref.shape[-1]
    half = nc // 2
    for j in range(SUB):
        br, bi = b_ref[j, 0], b_ref[j, 1]
        twr = _lane_tile(tw_ref[j, 0], cb)
        twi = _lane_tile(tw_ref[j, 1], cb)
        rhs = jnp.concatenate([br * twr - bi * twi, br * twi + bi * twr], axis=0).astype(BF16)
        y = jnp.dot(l3_ref[...], rhs, preferred_element_type=F32)
        for q in range(2):
            s = s_ref[q, :, j, :]
            o_ref[q, :, j, :] = x0_ref[q, :, j, :] * (y[q * half:(q + 1) * half] + s * bias_ref[...])


def _fft3(b, tw, l3, s5, x05, bias, nc, nf):
    P, C = b.shape[0], b.shape[-1]
    cb = FFT_CB
    seq = pl.BlockSpec((None, 2, nc // 2, SUB, cb), lambda p, j, c: (p, 0, 0, j, c))
    return pl.pallas_call(
        functools.partial(_fft3_kernel, nc=nc),
        grid=(P, nf // SUB, C // cb),
        in_specs=[pl.BlockSpec((None, SUB, 2, nc, cb), lambda p, j, c: (p, j, 0, 0, c)),
                  pl.BlockSpec((SUB, 2, nc, LANES), lambda p, j, c: (j, 0, 0, 0)),
                  pl.BlockSpec(l3.shape, lambda p, j, c: (0, 0)),
                  seq, seq,
                  pl.BlockSpec((1, cb), lambda p, j, c: (0, c))],
        out_specs=seq,
        out_shape=jax.ShapeDtypeStruct(s5.shape, F32),
        compiler_params=pltpu.CompilerParams(dimension_semantics=("parallel", "parallel", "parallel"),
                                             vmem_limit_bytes=VMEM_LIMIT),
        name="hyena_fft_stage3",
    )(b, tw, l3, s5, x05, bias)


def _fft_split(n):
    nf = 1 << (int(math.log2(n)) // 2)
    return n // nf, nf


def _filter_tables(nc):
    half = nc // 2
    kc = np.arange(nc, dtype=np.float64)[:, None]
    r = np.arange(half, dtype=np.float64)
    mats = []
    for rows_b, drop0 in ((half + (half - 1 - r), False), (nc - r, True)):
        a = 2.0 * np.pi * kc * np.concatenate([r, rows_b])[None, :] / nc
        m = np.concatenate([np.cos(a), -np.sin(a)], axis=0)
        if drop0:
            m[:, half] = 0.0
        mats.append(m)
    return tuple(jnp.asarray(m, F32).astype(BF16) for m in mats)


def _fft1_filter_kernel(hf_ref, hba_ref, hbb_ref, mb_ref, mb0_ref, tw_ref, o_ref, *, nc):
    cb = o_ref.shape[-1]
    m_first = jnp.where(pl.program_id(0) == 0, mb0_ref[...], mb_ref[...])
    for j in range(SUB):
        src = hbb_ref[:, 0, :] if j == 0 else hba_ref[:, SUB - j, :]
        rhs = jnp.concatenate([hf_ref[:, j, :], src], axis=0).astype(BF16)
        a = jnp.dot(m_first if j == 0 else mb_ref[...], rhs, preferred_element_type=F32)
        ar, ai = a[:nc], a[nc:]
        twr = _lane_tile(tw_ref[j, 0], cb)
        twi = _lane_tile(tw_ref[j, 1], cb)
        o_ref[j, 0] = ar * twr + ai * twi
        o_ref[j, 1] = ai * twr - ar * twi


def _fft1_filter(h_f, h_b, mb, mb0, tw, nc, nf):
    C = h_f.shape[-1]
    cb = FFT_CB
    nblk = nf // SUB
    hf3 = h_f.reshape(nc // 2, nf, C)
    hb3 = h_b.reshape(nc // 2, nf, C)
    blk = lambda f: pl.BlockSpec((nc // 2, SUB, cb), f)
    return pl.pallas_call(
        functools.partial(_fft1_filter_kernel, nc=nc),
        grid=(nblk, C // cb),
        in_specs=[blk(lambda j, c: (0, j, c)),
                  blk(lambda j, c: (0, nblk - 1 - j, c)),
                  blk(lambda j, c: (0, (nblk - j) % nblk, c)),
                  pl.BlockSpec(mb.shape, lambda j, c: (0, 0)),
                  pl.BlockSpec(mb0.shape, lambda j, c: (0, 0)),
                  pl.BlockSpec((SUB, 2, nc, LANES), lambda j, c: (j, 0, 0, 0))],
        out_specs=pl.BlockSpec((None, SUB, 2, nc, cb), lambda j, c: (0, j, 0, 0, c)),
        out_shape=jax.ShapeDtypeStruct((1, nf, 2, nc, C), F32),
        compiler_params=pltpu.CompilerParams(dimension_semantics=("parallel", "parallel"),
                                             vmem_limit_bytes=VMEM_LIMIT),
        name="hyena_filter_stage1",
    )(hf3, hb3, hb3, mb, mb0, tw)


def _hyena_conv(s, x0, h_f, h_b, bias):
    B, L, C = s.shape
    nc, nf = _fft_split(2 * L)
    tw = _twiddle(nc, nf)
    l1, m2, m2i, l3 = _fft_tables(nc, nf)
    mb, mb0 = _filter_tables(nc)
    kspec = _fft2_filter(_fft1_filter(h_f, h_b, mb, mb0, tw, nc, nf), m2, nc, nf)
    s5 = s.reshape(B // 2, 2, nc // 2, nf, C)
    x05 = x0.reshape(B // 2, 2, nc // 2, nf, C)
    a = _fft1(s5, l1, tw, nc, nf)
    b = _fft2(a, kspec, m2, m2i, nc, nf)
    return _fft3(b, tw, l3, s5, x05, bias.reshape(1, C), nc, nf).reshape(B, L, C)


def _hyena_filter(L, w1, b1, w2, b2, w3, freq):
    pos = jnp.arange(L, dtype=F32)
    t = jnp.linspace(0.0, 1.0, L, dtype=F32)[:, None]
    bands = jnp.linspace(1e-4, HY_BANDS - 1, HY_BANDS, dtype=F32)
    ang = (2.0 * math.pi / L) * pos[:, None] * bands[None, :]
    z = jnp.concatenate([t, jnp.cos(ang), -jnp.sin(ang)], axis=-1)
    h = jnp.sin(freq * (z @ w1 + b1))
    h = jnp.sin(freq * (h @ w2 + b2))
    h = h @ w3
    deltas = jnp.abs(jnp.linspace(math.log(HY_DECAY_TARGET) / HY_SLOW_PCT,
                                  math.log(HY_DECAY_TARGET) / HY_FAST_PCT, HY_CH, dtype=F32))
    window = jnp.exp(-t * deltas[None, :])
    h_f = h[:, :HY_CH] * window
    h_b = h[:, HY_CH:] * window
    l1 = jnp.sum(jnp.abs(h_f), axis=0) + jnp.sum(jnp.abs(h_b[1:]), axis=0)
    return h_f / l1, h_b / l1


def _split3(x):
    x1 = x.astype(BF16)
    r = x - x1.astype(F32)
    x2 = r.astype(BF16)
    x3 = (r - x2.astype(F32)).astype(BF16)
    return x1, x2, x3


def _mm(a, b):
    return jnp.dot(a.astype(BF16), b.astype(BF16), preferred_element_type=F32)


def _delta_kernel(qf_ref, kf_ref, vf_ref, gf_ref, qb_ref, kb_ref, vb_ref, gb_ref, of_ref, ob_ref, s_ref,
                  *, n_sub):
    @pl.when(pl.program_id(1) == 0)
    def _():
        s_ref[...] = jnp.zeros_like(s_ref)

    row = lax.broadcasted_iota(jnp.int32, (CHUNK, CHUNK), 0)
    col = lax.broadcasted_iota(jnp.int32, (CHUNK, CHUNK), 1)
    eye_f = jnp.where(row == col, 1.0, 0.0).astype(F32)
    dirs = ((qf_ref, kf_ref, vf_ref, gf_ref, of_ref, row >= col, row > col),
            (qb_ref, kb_ref, vb_ref, gb_ref, ob_ref, row <= col, row < col))

    ch = []
    for d, (q_ref, k_ref, v_ref, g_ref, _, incl, strict) in enumerate(dirs):
        tri = jnp.where(incl, 1.0, 0.0).astype(BF16)
        for c in range(n_sub):
            rows = slice(c * CHUNK, (c + 1) * CHUNK)
            gates = g_ref[rows, :]
            gl = 2 * DN_HEADS * d
            g1, g2, g3 = _split3(gates)
            gc_all = (jnp.dot(tri, g1, preferred_element_type=F32)
                      + jnp.dot(tri, g2, preferred_element_type=F32)
                      + jnp.dot(tri, g3, preferred_element_type=F32))
            gsum_all = jnp.sum(gates, axis=0, keepdims=True)
            for h in range(DN_HEADS):
                lanes = slice(h * DN_DK, (h + 1) * DN_DK)
                kh = k_ref[rows, lanes]
                beta = gates[:, gl + DN_HEADS + h:gl + DN_HEADS + h + 1]
                gc = gc_all[:, gl + h:gl + h + 1]
                g_last = gsum_all[:, gl + h:gl + h + 1]
                gc_b = jnp.broadcast_to(gc, (CHUNK, CHUNK))
                gc_row = jnp.sum(gc_b * eye_f, axis=0, keepdims=True)
                decay = jnp.where(incl, jnp.exp(jnp.minimum(gc_b - gc_row, 0.0)), 0.0)
                e_gc = jnp.exp(gc)
                kb = kh * beta
                ch.append(dict(d=d, c=c, h=h, lanes=lanes, rows=rows, strict=strict, decay=decay,
                               kh16=kh.astype(BF16), kb16=kb.astype(BF16),
                               q16=q_ref[rows, lanes].astype(BF16),
                               rhs=jnp.concatenate([v_ref[rows, lanes] * beta, kb * e_gc],
                                                   axis=1).astype(BF16),
                               qe=q_ref[rows, lanes] * e_gc,
                               kdT=(kh * jnp.exp(g_last - gc)).T.astype(BF16),
                               e_last=jnp.exp(g_last)))
    for x in ch:
        a = lax.dot_general(x["kb16"], x["kh16"], _NT, preferred_element_type=F32) * x["decay"]
        x["a"] = jnp.where(x["strict"], a, 0.0)
        x["qk"] = (lax.dot_general(x["q16"], x["kh16"], _NT, preferred_element_type=F32)
                   * x["decay"]).astype(BF16)
    for x in ch:
        x["t"] = eye_f - x["a"]
        x["p"] = _mm(x["a"], x["a"])
    for level in range(5):
        for x in ch:
            p16 = x["p"].astype(BF16)
            x["t"] = x["t"] + jnp.dot(x["t"].astype(BF16), p16, preferred_element_type=F32)
            if level < 4:
                x["p"] = jnp.dot(p16, p16, preferred_element_type=F32)
    for x in ch:
        uw = jnp.dot(x["t"].astype(BF16), x["rhs"], preferred_element_type=F32)
        x["u"] = uw[:, :DN_DV]
        x["wq"] = jnp.concatenate([uw[:, DN_DV:], x["qe"]], axis=0).astype(BF16)

    for step in range(n_sub):
        cur = [x for x in ch if x["c"] == (step if x["d"] == 0 else n_sub - 1 - step)]
        for x in cur:
            x["s"] = s_ref[x["d"], x["h"]]
            x["ws"] = jnp.dot(x["wq"], x["s"].astype(BF16), preferred_element_type=F32)
        for x in cur:
            x["vn"] = (x["u"] - x["ws"][:CHUNK]).astype(BF16)
        for x in cur:
            o = x["ws"][CHUNK:] + jnp.dot(x["qk"], x["vn"], preferred_element_type=F32)
            dirs[x["d"]][4][x["rows"], x["lanes"]] = o
            s_ref[x["d"], x["h"]] = (x["s"] * x["e_last"]
                                     + jnp.dot(x["kdT"], x["vn"], preferred_element_type=F32))


def _delta_scan(q, k, v, gates):
    B, L, _ = q.shape
    n_sub = DN_STEP_CHUNKS
    rows = n_sub * CHUNK
    nblk = L // rows
    fwd = pl.BlockSpec((None, rows, DN_W), lambda b, j: (b, j, 0))
    bwd = pl.BlockSpec((None, rows, DN_W), lambda b, j: (b, nblk - 1 - j, 0))
    gfwd = pl.BlockSpec((None, rows, LANES), lambda b, j: (b, j, 0))
    gbwd = pl.BlockSpec((None, rows, LANES), lambda b, j: (b, nblk - 1 - j, 0))
    out = jax.ShapeDtypeStruct((B, L, DN_W), F32)
    return pl.pallas_call(
        functools.partial(_delta_kernel, n_sub=n_sub),
        grid=(B, nblk),
        in_specs=[fwd, fwd, fwd, gfwd, bwd, bwd, bwd, gbwd],
        out_specs=[fwd, bwd],
        out_shape=[out, out],
        scratch_shapes=[pltpu.VMEM((2, DN_HEADS, DN_DK, DN_DV), F32)],
        compiler_params=pltpu.CompilerParams(dimension_semantics=("parallel", "arbitrary"),
                                             vmem_limit_bytes=VMEM_LIMIT),
        name="delta_scan",
    )(q, k, v, gates, q, k, v, gates)


def _trunk(x, w, hy_pos_w1, hy_pos_b1, hy_pos_w2, hy_pos_b2, hy_pos_w3, hy_sin_freq, hy_bias, dn_norm_w):
    B, L, D = x.shape
    T = B * L
    x2d = x.reshape(T, D)
    x0, s, q, k, v, z, dn_gates = _in_proj(x2d, L, w["norm_mix"], *w["in_proj"])
    seq = lambda a: a.reshape(B, L, a.shape[-1])
    h_f, h_b = _hyena_filter(L, hy_pos_w1, hy_pos_b1, hy_pos_w2, hy_pos_b2, hy_pos_w3, hy_sin_freq)
    y_hy = _hyena_conv(seq(s), seq(x0), h_f, h_b, hy_bias)
    o_f, o_b = _delta_scan(seq(q), seq(k), seq(v), seq(dn_gates))
    x1, h_ffn, logits = _out_proj(x2d, y_hy.reshape(T, HY_CH), o_f.reshape(T, DN_W), o_b.reshape(T, DN_W),
                                 z, dn_norm_w, w["out_hy"], w["out_dn"], w["norm_ffn"],
                                 w["router_hi"], w["router_lo"], w["router_b"])
    yb, dest, gates = _moe(h_ffn, logits[:, :N_EXPERTS], w["wg"], w["wl"], w["bg"], w["bl"], w["wd"],
                           w["bd"])
    return _combine_final(yb, dest, gates, x1, w["norm_final"]).reshape(B, L, D)


def kernel(x_prompt, x_sample, norm_mix_w, w_in, hy_conv_w, hy_conv_b, hy_pos_w1, hy_pos_b1, hy_pos_w2, hy_pos_b2, hy_pos_w3, hy_sin_freq, hy_bias, dn_conv_w, dn_a_log, dn_dt_bias, dn_norm_w, w_out, norm_ffn_w, w_router, b_router, w_gate_up, b_gate_up, w_down, b_down, norm_final_w):
    w_out16 = w_out[0].astype(BF16)
    wr = jnp.pad(w_router[0], ((0, 0), (0, LANES - N_EXPERTS)))
    wr_hi = wr.astype(BF16)
    wg, wl = _deinterleave(w_gate_up[0])
    w = {
        "norm_mix": norm_mix_w[0],
        "in_proj": _in_proj_params(w_in[0], hy_conv_w[0], hy_conv_b[0], dn_conv_w[0], dn_a_log[0],
                                   dn_dt_bias[0]),
        "out_hy": w_out16[:HY_CH],
        "out_dn": w_out16[HY_CH:],
        "norm_ffn": norm_ffn_w[0],
        "router_hi": wr_hi,
        "router_lo": (wr - wr_hi.astype(F32)).astype(BF16),
        "router_b": jnp.pad(b_router[0], (0, LANES - N_EXPERTS)).reshape(1, LANES),
        "wg": wg,
        "wl": wl,
        "bg": b_gate_up[0][:, 0::2].reshape(N_EXPERTS, 1, D_FF),
        "bl": b_gate_up[0][:, 1::2].reshape(N_EXPERTS, 1, D_FF),
        "wd": w_down[0].astype(BF16),
        "bd": b_down[0].reshape(N_EXPERTS, 1, D_MODEL),
        "norm_final": norm_final_w,
    }
    mix = (hy_pos_w1[0], hy_pos_b1[0], hy_pos_w2[0], hy_pos_b2[0], hy_pos_w3[0], hy_sin_freq[0],
           hy_bias[0], dn_norm_w[0])
    return (_trunk(x_prompt, w, *mix), _trunk(x_sample, w, *mix))
```

```python
import functools
import math

import jax
import jax.numpy as jnp
import numpy as np
from jax import lax
from jax.experimental import pallas as pl
from jax.experimental.pallas import tpu as pltpu

D_MODEL = 1024
HY_CH = 512
DN_HEADS = 4
DN_DK = 128
DN_DV = 128
DN_QK = DN_HEADS * DN_DK
DN_W = DN_HEADS * DN_DV
HY_IN = 3 * HY_CH
DN_CONV = 2 * DN_QK + DN_W
N_GATE = 4 * DN_HEADS
SHORT_CONV = 3
CONV_COLS = HY_IN + DN_CONV
REST_COLS = DN_W + 128
HY_EMB = 33
HY_BANDS = (HY_EMB - 1) // 2
HY_DECAY_TARGET = 1e-2
HY_FAST_PCT = 0.3
HY_SLOW_PCT = 1.5
CHUNK = 64
N_EXPERTS = 32
TOP_K = 4
D_FF = D_MODEL
SWIGLU_ALPHA = 1.702
SWIGLU_LIMIT = 7.0
MOE_BLOCK = 512
EPS = 1e-6

LANES = 128
SUB = 8
ROW_TILE = 512
FFT_CB = 256
DN_STEP_CHUNKS = 4
DEINT_COLS = 512
DISPATCH_TILE = 512
COMBINE_TILE = 512
VMEM_LIMIT = 56 * 1024 * 1024

F32 = jnp.float32
BF16 = jnp.bfloat16

_NT = (((1,), (1,)), ((), ()))


def _rms(x, g):
    return x * lax.rsqrt(jnp.mean(x * x, axis=-1, keepdims=True) + EPS) * g


def _store_row_tiles(ref, x):
    for j in range(D_MODEL // LANES):
        ref[:, j, :] = x[:, j * LANES:(j + 1) * LANES]


def _load_row_tiles(ref):
    return jnp.concatenate([ref[:, j, :] for j in range(D_MODEL // LANES)], axis=1)


def _silu(x):
    return x * jax.nn.sigmoid(x)


def _head_l2norm(x):
    parts = []
    for hd in range(DN_HEADS):
        xh = x[:, hd * DN_DK:(hd + 1) * DN_DK]
        parts.append(xh * lax.rsqrt(jnp.sum(xh * xh, axis=-1, keepdims=True) + EPS))
    return jnp.concatenate(parts, axis=1)


def _in_proj_kernel(xp_ref, x_ref, xn_ref, g_ref, wc_ref, wr_ref, cw_ref, cb_ref, gt_ref,
                    x0_ref, s_ref, q_ref, k_ref, v_ref, z_ref, gate_ref, p_scr, *, tiles_per_seq):
    i = pl.program_id(0)
    first = (i % tiles_per_seq) == 0
    last = (i % tiles_per_seq) == tiles_per_seq - 1
    g = g_ref[...]
    h = _rms(x_ref[...], g).astype(BF16)
    hp = _rms(xp_ref[...], g).astype(BF16)
    hn = _rms(xn_ref[...], g).astype(BF16)
    def project(c0):
        cols = slice(c0, c0 + HY_CH)
        w = wc_ref[:, cols]
        p_scr[pl.ds(SUB, ROW_TILE), cols] = jnp.dot(h, w, preferred_element_type=F32)
        p_scr[pl.ds(0, SUB), cols] = jnp.where(first, 0.0, jnp.dot(hp, w, preferred_element_type=F32))
        p_scr[pl.ds(SUB + ROW_TILE, SUB), cols] = jnp.where(last, 0.0,
                                                            jnp.dot(hn, w, preferred_element_type=F32))

    def conv(c0):
        cols = slice(c0, c0 + HY_CH)
        return (p_scr[pl.ds(SUB - 1, ROW_TILE), cols] * cw_ref[0:1, cols]
                + p_scr[pl.ds(SUB, ROW_TILE), cols] * cw_ref[1:2, cols]
                + p_scr[pl.ds(SUB + 1, ROW_TILE), cols] * cw_ref[2:3, cols])

    project(0)
    project(HY_CH)
    x0_ref[...] = conv(0) + cb_ref[:, 0:HY_CH]
    project(2 * HY_CH)
    project(HY_IN)
    s_ref[...] = (conv(HY_CH) + cb_ref[:, HY_CH:2 * HY_CH]) * (conv(2 * HY_CH) + cb_ref[:, 2 * HY_CH:])
    project(HY_IN + DN_QK)
    q_ref[...] = _head_l2norm(_silu(conv(HY_IN))) * (DN_DK ** -0.5)
    project(HY_IN + 2 * DN_QK)
    k_ref[...] = _head_l2norm(_silu(conv(HY_IN + DN_QK)))
    rest = jnp.dot(h, wr_ref[...], preferred_element_type=F32)
    v_ref[...] = _silu(conv(HY_IN + 2 * DN_QK))
    z_ref[...] = rest[:, :DN_W]
    a = rest[:, DN_W:] + gt_ref[1:2]
    softplus = jnp.maximum(a, 0.0) + jnp.log(1.0 + jnp.exp(-jnp.abs(a)))
    gate_ref[...] = jnp.where(gt_ref[2:3] > 0.5, -gt_ref[0:1] * softplus, jax.nn.sigmoid(rest[:, DN_W:]))


def _in_proj(x2d, seq_len, g, w_conv, w_rest, conv_w, conv_b, gate_tab):
    T = x2d.shape[0]
    per = ROW_TILE // SUB
    last_sub = T // SUB - 1
    const = lambda i: (0, 0)
    row = lambda i: (i, 0)
    o512 = pl.BlockSpec((ROW_TILE, HY_CH), row)
    s512 = jax.ShapeDtypeStruct((T, HY_CH), F32)
    return pl.pallas_call(
        functools.partial(_in_proj_kernel, tiles_per_seq=seq_len // ROW_TILE),
        grid=(T // ROW_TILE,),
        in_specs=[pl.BlockSpec((SUB, D_MODEL), lambda i: (jnp.maximum(i * per - 1, 0), 0)),
                  pl.BlockSpec((ROW_TILE, D_MODEL), row),
                  pl.BlockSpec((SUB, D_MODEL), lambda i: (jnp.minimum((i + 1) * per, last_sub), 0)),
                  pl.BlockSpec((1, D_MODEL), const),
                  pl.BlockSpec((D_MODEL, CONV_COLS), const),
                  pl.BlockSpec((D_MODEL, REST_COLS), const),
                  pl.BlockSpec((SHORT_CONV, CONV_COLS), const),
                  pl.BlockSpec((1, HY_IN), const),
                  pl.BlockSpec((3, LANES), const)],
        out_specs=[o512, o512, o512, o512, o512, o512, pl.BlockSpec((ROW_TILE, LANES), row)],
        out_shape=[s512, s512, s512, s512, s512, s512, jax.ShapeDtypeStruct((T, LANES), F32)],
        scratch_shapes=[pltpu.VMEM((ROW_TILE + 2 * SUB, CONV_COLS), F32)],
        compiler_params=pltpu.CompilerParams(dimension_semantics=("parallel",),
                                             vmem_limit_bytes=VMEM_LIMIT),
        name="in_proj",
    )(x2d, x2d, x2d, g.reshape(1, D_MODEL), w_conv, w_rest, conv_w, conv_b, gate_tab)


def _in_proj_params(w_in, hy_conv_w, hy_conv_b, dn_conv_w, dn_a_log, dn_dt_bias):
    H = DN_HEADS
    w16 = w_in.astype(BF16)
    gc = w16[:, CONV_COLS + DN_W:]
    gc = jnp.concatenate([gc[:, 0:H], gc[:, 2 * H:3 * H], gc[:, H:2 * H], gc[:, 3 * H:]], axis=1)
    w_rest = jnp.concatenate([w16[:, CONV_COLS:CONV_COLS + DN_W],
                              jnp.pad(gc, ((0, 0), (0, LANES - N_GATE)))], axis=1)
    zero, one, pad = jnp.zeros((H,), F32), jnp.ones((H,), F32), jnp.zeros((LANES - N_GATE,), F32)
    gate_tab = jnp.stack([jnp.concatenate([jnp.exp(dn_a_log[0]), zero, jnp.exp(dn_a_log[1]), zero, pad]),
                          jnp.concatenate([dn_dt_bias[0], zero, dn_dt_bias[1], zero, pad]),
                          jnp.concatenate([one, zero, one, zero, pad])])
    return (w16[:, :CONV_COLS], w_rest, jnp.concatenate([hy_conv_w, dn_conv_w], axis=1),
            hy_conv_b.reshape(1, HY_IN), gate_tab)


def _out_proj_kernel(x_ref, yh_ref, of_ref, ob_ref, z_ref, nw_ref, wh_ref, wd_ref, g_ref, wrh_ref,
                     wrl_ref, br_ref, x1_ref, h_ref, lg_ref):
    o = of_ref[...] + ob_ref[...]
    heads = []
    for hd in range(DN_HEADS):
        oh = o[:, hd * DN_DV:(hd + 1) * DN_DV]
        heads.append(oh * lax.rsqrt(jnp.mean(oh * oh, axis=-1, keepdims=True) + EPS))
    z = z_ref[...]
    y_dn = jnp.concatenate(heads, axis=1) * nw_ref[...] * (z * jax.nn.sigmoid(z))
    x1 = (x_ref[...]
          + jnp.dot(yh_ref[...].astype(BF16), wh_ref[...], preferred_element_type=F32)
          + jnp.dot(y_dn.astype(BF16), wd_ref[...], preferred_element_type=F32))
    x1_ref[...] = x1
    h = _rms(x1, g_ref[...])
    h_hi = h.astype(BF16)
    h_lo = (h - h_hi.astype(F32)).astype(BF16)
    _store_row_tiles(h_ref, h)
    lg_ref[...] = (jnp.dot(h_hi, wrh_ref[...], preferred_element_type=F32)
                   + jnp.dot(h_lo, wrh_ref[...], preferred_element_type=F32)
                   + jnp.dot(h_hi, wrl_ref[...], preferred_element_type=F32)
                   + br_ref[...])


def _out_proj(x2d, y_hy, o_f, o_b, z, dn_norm_w, w_oh, w_od, g, wr_hi, wr_lo, br):
    T = x2d.shape[0]
    const = lambda i: (0, 0)
    row = lambda i: (i, 0)
    return pl.pallas_call(
        _out_proj_kernel,
        grid=(T // ROW_TILE,),
        in_specs=[pl.BlockSpec((ROW_TILE, D_MODEL), row),
                  pl.BlockSpec((ROW_TILE, HY_CH), row),
                  pl.BlockSpec((ROW_TILE, DN_W), row),
                  pl.BlockSpec((ROW_TILE, DN_W), row),
                  pl.BlockSpec((ROW_TILE, DN_W), row),
                  pl.BlockSpec((1, DN_W), const),
                  pl.BlockSpec((HY_CH, D_MODEL), const),
                  pl.BlockSpec((DN_W, D_MODEL), const),
                  pl.BlockSpec((1, D_MODEL), const),
                  pl.BlockSpec((D_MODEL, LANES), const),
                  pl.BlockSpec((D_MODEL, LANES), const),
                  pl.BlockSpec((1, LANES), const)],
        out_specs=[pl.BlockSpec((ROW_TILE, D_MODEL), row),
                   pl.BlockSpec((ROW_TILE, SUB, LANES), lambda i: (i, 0, 0)),
                   pl.BlockSpec((ROW_TILE, LANES), row)],
        out_shape=[jax.ShapeDtypeStruct((T, D_MODEL), F32),
                   jax.ShapeDtypeStruct((T, SUB, LANES), F32),
                   jax.ShapeDtypeStruct((T, LANES), F32)],
        compiler_params=pltpu.CompilerParams(dimension_semantics=("parallel",),
                                             vmem_limit_bytes=VMEM_LIMIT),
        name="out_proj_router",
    )(x2d, y_hy, o_f, o_b, z, jnp.tile(dn_norm_w, DN_HEADS).reshape(1, DN_W), w_oh, w_od,
      g.reshape(1, D_MODEL), wr_hi, wr_lo, br)


def _deint_kernel(w_ref, p_ref, og_ref, ol_ref):
    half = DEINT_COLS // 2
    sel = jnp.dot(w_ref[0].astype(BF16), p_ref[...], preferred_element_type=F32)
    og_ref[0] = sel[:, :half].astype(BF16)
    ol_ref[0] = sel[:, half:].astype(BF16)


def _deinterleave(w_gate_up):
    half = DEINT_COLS // 2
    r = np.arange(DEINT_COLS)[:, None]
    c = np.arange(DEINT_COLS)[None, :]
    perm = jnp.asarray(np.where(c < half, r == 2 * c, r == 2 * (c - half) + 1), BF16)
    out = jax.ShapeDtypeStruct((N_EXPERTS, D_MODEL, D_FF), BF16)
    return pl.pallas_call(
        _deint_kernel,
        grid=(N_EXPERTS, 2 * D_FF // DEINT_COLS),
        in_specs=[pl.BlockSpec((1, D_MODEL, DEINT_COLS), lambda e, j: (e, 0, j)),
                  pl.BlockSpec((DEINT_COLS, DEINT_COLS), lambda e, j: (0, 0))],
        out_specs=[pl.BlockSpec((1, D_MODEL, half), lambda e, j: (e, 0, j)),
                   pl.BlockSpec((1, D_MODEL, half), lambda e, j: (e, 0, j))],
        out_shape=[out, out],
        compiler_params=pltpu.CompilerParams(dimension_semantics=("parallel", "parallel")),
        name="deinterleave_gate_up",
    )(w_gate_up, perm)


def _dispatch_kernel(tail_ref, dest_ref, h_ref, xb_ref, zero_scr, sem):
    n = h_ref.shape[0]

    @pl.when(pl.program_id(0) == 0)
    def _():
        zero_scr[...] = jnp.zeros_like(zero_scr)
        for e in range(N_EXPERTS):
            pltpu.make_async_copy(zero_scr, xb_ref.at[pl.ds(tail_ref[0, e], MOE_BLOCK)], sem).start()
        for e in range(N_EXPERTS):
            pltpu.make_async_copy(zero_scr, xb_ref.at[pl.ds(0, MOE_BLOCK)], sem).wait()

    def issue(r, c):
        for k in range(TOP_K):
            d = dest_ref[0, r * TOP_K + k]
            pltpu.make_async_copy(h_ref.at[r], xb_ref.at[d], sem).start()
        return c

    lax.fori_loop(0, n, issue, 0, unroll=8)

    def drain(r, c):
        for k in range(TOP_K):
            pltpu.make_async_copy(h_ref.at[0], xb_ref.at[0], sem).wait()
        return c

    lax.fori_loop(0, n, drain, 0, unroll=8)


def _dispatch(h, dest, tail_start, n_rows):
    T = h.shape[0]
    tm = DISPATCH_TILE
    return pl.pallas_call(
        _dispatch_kernel,
        grid=(T // tm,),
        in_specs=[pl.BlockSpec((1, N_EXPERTS), lambda i: (0, 0), memory_space=pltpu.SMEM),
                  pl.BlockSpec((None, 1, tm * TOP_K), lambda i: (i, 0, 0), memory_space=pltpu.SMEM),
                  pl.BlockSpec((tm, SUB, LANES), lambda i: (i, 0, 0))],
        out_specs=pl.BlockSpec(memory_space=pl.ANY),
        out_shape=jax.ShapeDtypeStruct((n_rows, SUB, LANES), F32),
        scratch_shapes=[pltpu.VMEM((MOE_BLOCK, SUB, LANES), F32), pltpu.SemaphoreType.DMA],
        compiler_params=pltpu.CompilerParams(dimension_semantics=("arbitrary",)),
        name="moe_dispatch",
    )(tail_start.reshape(1, N_EXPERTS), dest.reshape(T // tm, 1, tm * TOP_K), h)


def _expert_kernel(be_ref, nb_ref, xb_ref, wg_ref, wl_ref, bg_ref, bl_ref, wd_ref, bd_ref, y_ref):
    i = pl.program_id(0)

    @pl.when(i < nb_ref[0])
    def _():
        xb = _load_row_tiles(xb_ref).astype(BF16)
        hg = jnp.dot(xb, wg_ref[0], preferred_element_type=F32) + bg_ref[0]
        hl = jnp.dot(xb, wl_ref[0], preferred_element_type=F32) + bl_ref[0]
        x_glu = jnp.minimum(hg, SWIGLU_LIMIT)
        x_lin = jnp.clip(hl, -SWIGLU_LIMIT, SWIGLU_LIMIT)
        act = x_glu * jax.nn.sigmoid(SWIGLU_ALPHA * x_glu) * (x_lin + 1.0)
        y = jnp.dot(act.astype(BF16), wd_ref[0], preferred_element_type=F32) + bd_ref[0]
        _store_row_tiles(y_ref, y)

    @pl.when(i >= nb_ref[0])
    def _():
        y_ref[...] = jnp.zeros_like(y_ref)


def _expert_mlp(xb, block_e, n_used, wg, wl, bg, bl, wd, bd):
    n_rows = xb.shape[0]
    n_blocks = n_rows // MOE_BLOCK
    rowm = lambda i, be, nb: (i, 0, 0)
    exp3 = lambda i, be, nb: (be[i], 0, 0)
    grid_spec = pltpu.PrefetchScalarGridSpec(
        num_scalar_prefetch=2,
        grid=(n_blocks,),
        in_specs=[pl.BlockSpec((MOE_BLOCK, SUB, LANES), rowm),
                  pl.BlockSpec((1, D_MODEL, D_FF), exp3),
                  pl.BlockSpec((1, D_MODEL, D_FF), exp3),
                  pl.BlockSpec((1, 1, D_FF), exp3),
                  pl.BlockSpec((1, 1, D_FF), exp3),
                  pl.BlockSpec((1, D_FF, D_MODEL), exp3),
                  pl.BlockSpec((1, 1, D_MODEL), exp3)],
        out_specs=pl.BlockSpec((MOE_BLOCK, SUB, LANES), rowm),
    )
    return pl.pallas_call(
        _expert_kernel,
        grid_spec=grid_spec,
        out_shape=jax.ShapeDtypeStruct((n_rows, SUB, LANES), F32),
        compiler_params=pltpu.CompilerParams(dimension_semantics=("arbitrary",),
                                             vmem_limit_bytes=VMEM_LIMIT),
        name="expert_mlp",
    )(block_e, n_used, xb, wg, wl, bg, bl, wd, bd)


def _moe(h, logits, wg, wl, bg, bl, wd, bd):
    T = h.shape[0]
    TK = T * TOP_K
    top_vals, top_idx = lax.top_k(logits, TOP_K)
    gates = jax.nn.softmax(top_vals, axis=-1)
    sel = jnp.sum(jax.nn.one_hot(top_idx, N_EXPERTS, dtype=jnp.int32), axis=1)
    before = jnp.cumsum(sel, axis=0) - sel
    counts = jnp.sum(sel, axis=0)
    padded = (counts + MOE_BLOCK - 1) // MOE_BLOCK * MOE_BLOCK
    pad_end = jnp.cumsum(padded)
    pad_start = pad_end - padded
    rank = jnp.take_along_axis(before, top_idx, axis=1)
    dest = (pad_start[top_idx] + rank).astype(jnp.int32)
    n_blocks = (TK + MOE_BLOCK - 1) // MOE_BLOCK + N_EXPERTS
    n_rows = n_blocks * MOE_BLOCK
    block_start = jnp.arange(n_blocks, dtype=jnp.int32) * MOE_BLOCK
    block_e = jnp.minimum(jnp.sum((block_start[:, None] >= pad_end[None, :]).astype(jnp.int32), axis=1),
                          N_EXPERTS - 1)
    n_used = (pad_end[-1] // MOE_BLOCK).astype(jnp.int32).reshape(1)
    tail_start = jnp.minimum(pad_start + counts, n_rows - MOE_BLOCK).astype(jnp.int32)
    xb = _dispatch(h, dest, tail_start, n_rows)
    yb = _expert_mlp(xb, block_e, n_used, wg, wl, bg, bl, wd, bd)
    return yb, dest, gates


def _combine_kernel(dest_ref, gate_ref, x1_ref, g_ref, yb_ref, o_ref, buf, sem):
    n = x1_ref.shape[0]

    def issue(r, c):
        for k in range(TOP_K):
            d = dest_ref[0, r * TOP_K + k]
            pltpu.make_async_copy(yb_ref.at[d], buf.at[k, r], sem).start()
        return c

    lax.fori_loop(0, n, issue, 0, unroll=8)

    def drain(r, c):
        for k in range(TOP_K):
            pltpu.make_async_copy(yb_ref.at[0], buf.at[0, 0], sem).wait()
        return c

    lax.fori_loop(0, n, drain, 0, unroll=8)
    x = x1_ref[...]
    for k in range(TOP_K):
        x = x + gate_ref[:, k:k + 1] * _load_row_tiles(buf.at[k])
    o_ref[...] = _rms(x, g_ref[...])


def _combine_final(yb, dest, gates, x1, g):
    T = x1.shape[0]
    tm = COMBINE_TILE
    row = lambda i: (i, 0)
    return pl.pallas_call(
        _combine_kernel,
        grid=(T // tm,),
        in_specs=[pl.BlockSpec((None, 1, tm * TOP_K), lambda i: (i, 0, 0), memory_space=pltpu.SMEM),
                  pl.BlockSpec((tm, TOP_K), row),
                  pl.BlockSpec((tm, D_MODEL), row),
                  pl.BlockSpec((1, D_MODEL), lambda i: (0, 0)),
                  pl.BlockSpec(memory_space=pl.ANY)],
        out_specs=pl.BlockSpec((tm, D_MODEL), row),
        out_shape=jax.ShapeDtypeStruct((T, D_MODEL), F32),
        scratch_shapes=[pltpu.VMEM((TOP_K, tm, SUB, LANES), F32), pltpu.SemaphoreType.DMA],
        compiler_params=pltpu.CompilerParams(dimension_semantics=("arbitrary",),
                                             vmem_limit_bytes=VMEM_LIMIT),
        name="moe_combine_final",
    )(dest.reshape(T // tm, 1, tm * TOP_K), gates, x1, g.reshape(1, D_MODEL), yb)


def _fft_tables(nc, nf):
    n = nc * nf
    kc = np.arange(nc, dtype=np.float64)
    a1 = 2.0 * np.pi * np.outer(kc, np.arange(nc // 2)) / nc
    c1, s1 = np.cos(a1), np.sin(a1)
    l1 = np.block([[c1, s1], [-s1, c1]])
    a2 = 2.0 * np.pi * np.outer(np.arange(nf), np.arange(nf)) / nf
    c2, s2 = np.cos(a2), np.sin(a2)
    m2 = np.block([[c2, s2], [-s2, c2]])
    m2i = np.block([[c2, -s2], [s2, c2]])
    a3 = 2.0 * np.pi * np.outer(np.arange(nc // 2), kc) / nc
    c3, s3 = np.cos(a3), np.sin(a3)
    l3 = np.block([[c3, -s3], [s3, c3]]) / n
    return tuple(jnp.asarray(m, F32).astype(BF16) for m in (l1, m2, m2i, l3))


def _twiddle(nc, nf):
    n = nc * nf
    ph = (jnp.arange(nf, dtype=jnp.int32)[:, None] * jnp.arange(nc, dtype=jnp.int32)[None, :]) % n
    ang = ph.astype(F32) * (2.0 * math.pi / n)
    tw = jnp.stack([jnp.cos(ang), jnp.sin(ang)], axis=1)
    return jnp.broadcast_to(tw[..., None], (nf, 2, nc, LANES))


def _lane_tile(t, width):
    return t if width == LANES else jnp.concatenate([t] * (width // LANES), axis=1)


def _fft1_kernel(z_ref, l1_ref, tw_ref, o_ref, *, nc):
    cb = o_ref.shape[-1]
    for j in range(SUB):
        rhs = jnp.concatenate([z_ref[0, :, j, :], z_ref[1, :, j, :]], axis=0).astype(BF16)
        a = jnp.dot(l1_ref[...], rhs, preferred_element_type=F32)
        ar, ai = a[:nc], a[nc:]
        twr = _lane_tile(tw_ref[j, 0], cb)
        twi = _lane_tile(tw_ref[j, 1], cb)
        o_ref[j, 0] = ar * twr + ai * twi
        o_ref[j, 1] = ai * twr - ar * twi


def _fft1(z, l1, tw, nc, nf):
    P, C = z.shape[0], z.shape[-1]
    cb = FFT_CB
    z_spec = pl.BlockSpec((None, 2, nc // 2, SUB, cb), lambda p, j, c: (p, 0, 0, j, c))
    return pl.pallas_call(
        functools.partial(_fft1_kernel, nc=nc),
        grid=(P, nf // SUB, C // cb),
        in_specs=[z_spec,
                  pl.BlockSpec(l1.shape, lambda p, j, c: (0, 0)),
                  pl.BlockSpec((SUB, 2, nc, LANES), lambda p, j, c: (j, 0, 0, 0))],
        out_specs=pl.BlockSpec((None, SUB, 2, nc, cb), lambda p, j, c: (p, j, 0, 0, c)),
        out_shape=jax.ShapeDtypeStruct((P, nf, 2, nc, C), F32),
        compiler_params=pltpu.CompilerParams(dimension_semantics=("parallel", "parallel", "parallel"),
                                             vmem_limit_bytes=VMEM_LIMIT),
        name="hyena_fft_stage1",
    )(z, l1, tw)


def _fft2_filter_kernel(a_ref, m2_ref, k_ref, *, nf):
    for j in range(SUB):
        rhs = jnp.concatenate([a_ref[:, 0, j, :], a_ref[:, 1, j, :]], axis=0).astype(BF16)
        x = jnp.dot(m2_ref[...], rhs, preferred_element_type=F32)
        k_ref[j, 0] = x[:nf]
        k_ref[j, 1] = x[nf:]


def _fft2_filter(a, m2, nc, nf):
    C = a.shape[-1]
    cb = FFT_CB
    return pl.pallas_call(
        functools.partial(_fft2_filter_kernel, nf=nf),
        grid=(nc // SUB, C // cb),
        in_specs=[pl.BlockSpec((None, nf, 2, SUB, cb), lambda k, c: (0, 0, 0, k, c)),
                  pl.BlockSpec(m2.shape, lambda k, c: (0, 0))],
        out_specs=pl.BlockSpec((SUB, 2, nf, cb), lambda k, c: (k, 0, 0, c)),
        out_shape=jax.ShapeDtypeStruct((nc, 2, nf, C), F32),
        compiler_params=pltpu.CompilerParams(dimension_semantics=("parallel", "parallel"),
                                             vmem_limit_bytes=VMEM_LIMIT),
        name="hyena_filter_spectrum",
    )(a, m2)


def _fft2_kernel(a_ref, k_ref, m2_ref, m2i_ref, o_ref, *, nf):
    for j in range(SUB):
        rhs = jnp.concatenate([a_ref[:, 0, j, :], a_ref[:, 1, j, :]], axis=0).astype(BF16)
        x = jnp.dot(m2_ref[...], rhs, preferred_element_type=F32)
        xr, xi = x[:nf], x[nf:]
        kr, ki = k_ref[j, 0], k_ref[j, 1]
        y = jnp.concatenate([xr * kr - xi * ki, xr * ki + xi * kr], axis=0).astype(BF16)
        b = jnp.dot(m2i_ref[...], y, preferred_element_type=F32)
        o_ref[:, 0, j, :] = b[:nf]
        o_ref[:, 1, j, :] = b[nf:]


def _fft2(a, kspec, m2, m2i, nc, nf):
    P, C = a.shape[0], a.shape[-1]
    cb = FFT_CB
    blk = pl.BlockSpec((None, nf, 2, SUB, cb), lambda p, k, c: (p, 0, 0, k, c))
    return pl.pallas_call(
        functools.partial(_fft2_kernel, nf=nf),
        grid=(P, nc // SUB, C // cb),
        in_specs=[blk,
                  pl.BlockSpec((SUB, 2, nf, cb), lambda p, k, c: (k, 0, 0, c)),
                  pl.BlockSpec(m2.shape, lambda p, k, c: (0, 0)),
                  pl.BlockSpec(m2i.shape, lambda p, k, c: (0, 0))],
        out_specs=blk,
        out_shape=jax.ShapeDtypeStruct(a.shape, F32),
        compiler_params=pltpu.CompilerParams(dimension_semantics=("parallel", "parallel", "parallel"),
                                             vmem_limit_bytes=VMEM_LIMIT),
        name="hyena_fft_stage2",
    )(a, kspec, m2, m2i)


def _fft3_kernel(b_ref, tw_ref, l3_ref, s_ref, x0_ref, bias_ref, o_ref, *, nc):
    cb = o_ref.shape[-1]
    half = nc // 2
    for j in range(SUB):
        br, bi = b_ref[j, 0], b_ref[j, 1]
        twr = _lane_tile(tw_ref[j, 0], cb)
        twi = _lane_tile(tw_ref[j, 1], cb)
        rhs = jnp.concatenate([br * twr - bi * twi, br * twi + bi * twr], axis=0).astype(BF16)
        y = jnp.dot(l3_ref[...], rhs, preferred_element_type=F32)
        for q in range(2):
            s = s_ref[q, :, j, :]
            o_ref[q, :, j, :] = x0_ref[q, :, j, :] * (y[q * half:(q + 1) * half] + s * bias_ref[...])


def _fft3(b, tw, l3, s5, x05, bias, nc, nf):
    P, C = b.shape[0], b.shape[-1]
    cb = FFT_CB
    seq = pl.BlockSpec((None, 2, nc // 2, SUB, cb), lambda p, j, c: (p, 0, 0, j, c))
    return pl.pallas_call(
        functools.partial(_fft3_kernel, nc=nc),
        grid=(P, nf // SUB, C // cb),
        in_specs=[pl.BlockSpec((None, SUB, 2, nc, cb), lambda p, j, c: (p, j, 0, 0, c)),
                  pl.BlockSpec((SUB, 2, nc, LANES), lambda p, j, c: (j, 0, 0, 0)),
                  pl.BlockSpec(l3.shape, lambda p, j, c: (0, 0)),
                  seq, seq,
                  pl.BlockSpec((1, cb), lambda p, j, c: (0, c))],
        out_specs=seq,
        out_shape=jax.ShapeDtypeStruct(s5.shape, F32),
        compiler_params=pltpu.CompilerParams(dimension_semantics=("parallel", "parallel", "parallel"),
                                             vmem_limit_bytes=VMEM_LIMIT),
        name="hyena_fft_stage3",
    )(b, tw, l3, s5, x05, bias)


def _fft_split(n):
    nf = 1 << (int(math.log2(n)) // 2)
    return n // nf, nf


def _filter_tables(nc):
    half = nc // 2
    kc = np.arange(nc, dtype=np.float64)[:, None]
    r = np.arange(half, dtype=np.float64)
    mats = []
    for rows_b, drop0 in ((half + (half - 1 - r), False), (nc - r, True)):
        a = 2.0 * np.pi * kc * np.concatenate([r, rows_b])[None, :] / nc
        m = np.concatenate([np.cos(a), -np.sin(a)], axis=0)
        if drop0:
            m[:, half] = 0.0
        mats.append(m)
    return tuple(jnp.asarray(m, F32).astype(BF16) for m in mats)


def _fft1_filter_kernel(hf_ref, hba_ref, hbb_ref, mb_ref, mb0_ref, tw_ref, o_ref, *, nc):
    cb = o_ref.shape[-1]
    m_first = jnp.where(pl.program_id(0) == 0, mb0_ref[...], mb_ref[...])
    for j in range(SUB):
        src = hbb_ref[:, 0, :] if j == 0 else hba_ref[:, SUB - j, :]
        rhs = jnp.concatenate([hf_ref[:, j, :], src], axis=0).astype(BF16)
        a = jnp.dot(m_first if j == 0 else mb_ref[...], rhs, preferred_element_type=F32)
        ar, ai = a[:nc], a[nc:]
        twr = _lane_tile(tw_ref[j, 0], cb)
        twi = _lane_tile(tw_ref[j, 1], cb)
        o_ref[j, 0] = ar * twr + ai * twi
        o_ref[j, 1] = ai * twr - ar * twi


def _fft1_filter(h_f, h_b, mb, mb0, tw, nc, nf):
    C = h_f.shape[-1]
    cb = FFT_CB
    nblk = nf // SUB
    hf3 = h_f.reshape(nc // 2, nf, C)
    hb3 = h_b.reshape(nc // 2, nf, C)
    blk = lambda f: pl.BlockSpec((nc // 2, SUB, cb), f)
    return pl.pallas_call(
        functools.partial(_fft1_filter_kernel, nc=nc),
        grid=(nblk, C // cb),
        in_specs=[blk(lambda j, c: (0, j, c)),
                  blk(lambda j, c: (0, nblk - 1 - j, c)),
                  blk(lambda j, c: (0, (nblk - j) % nblk, c)),
                  pl.BlockSpec(mb.shape, lambda j, c: (0, 0)),
                  pl.BlockSpec(mb0.shape, lambda j, c: (0, 0)),
                  pl.BlockSpec((SUB, 2, nc, LANES), lambda j, c: (j, 0, 0, 0))],
        out_specs=pl.BlockSpec((None, SUB, 2, nc, cb), lambda j, c: (0, j, 0, 0, c)),
        out_shape=jax.ShapeDtypeStruct((1, nf, 2, nc, C), F32),
        compiler_params=pltpu.CompilerParams(dimension_semantics=("parallel", "parallel"),
                                             vmem_limit_bytes=VMEM_LIMIT),
        name="hyena_filter_stage1",
    )(hf3, hb3, hb3, mb, mb0, tw)


def _hyena_conv(s, x0, h_f, h_b, bias):
    B, L, C = s.shape
    nc, nf = _fft_split(2 * L)
    tw = _twiddle(nc, nf)
    l1, m2, m2i, l3 = _fft_tables(nc, nf)
    mb, mb0 = _filter_tables(nc)
    kspec = _fft2_filter(_fft1_filter(h_f, h_b, mb, mb0, tw, nc, nf), m2, nc, nf)
    s5 = s.reshape(B // 2, 2, nc // 2, nf, C)
    x05 = x0.reshape(B // 2, 2, nc // 2, nf, C)
    a = _fft1(s5, l1, tw, nc, nf)
    b = _fft2(a, kspec, m2, m2i, nc, nf)
    return _fft3(b, tw, l3, s5, x05, bias.reshape(1, C), nc, nf).reshape(B, L, C)


def _hyena_filter(L, w1, b1, w2, b2, w3, freq):
    pos = jnp.arange(L, dtype=F32)
    t = jnp.linspace(0.0, 1.0, L, dtype=F32)[:, None]
    bands = jnp.linspace(1e-4, HY_BANDS - 1, HY_BANDS, dtype=F32)
    ang = (2.0 * math.pi / L) * pos[:, None] * bands[None, :]
    z = jnp.concatenate([t, jnp.cos(ang), -jnp.sin(ang)], axis=-1)
    h = jnp.sin(freq * (z @ w1 + b1))
    h = jnp.sin(freq * (h @ w2 + b2))
    h = h @ w3
    deltas = jnp.abs(jnp.linspace(math.log(HY_DECAY_TARGET) / HY_SLOW_PCT,
                                  math.log(HY_DECAY_TARGET) / HY_FAST_PCT, HY_CH, dtype=F32))
    window = jnp.exp(-t * deltas[None, :])
    h_f = h[:, :HY_CH] * window
    h_b = h[:, HY_CH:] * window
    l1 = jnp.sum(jnp.abs(h_f), axis=0) + jnp.sum(jnp.abs(h_b[1:]), axis=0)
    return h_f / l1, h_b / l1


def _split3(x):
    x1 = x.astype(BF16)
    r = x - x1.astype(F32)
    x2 = r.astype(BF16)
    x3 = (r - x2.astype(F32)).astype(BF16)
    return x1, x2, x3


def _mm(a, b):
    return jnp.dot(a.astype(BF16), b.astype(BF16), preferred_element_type=F32)


def _delta_kernel(qf_ref, kf_ref, vf_ref, gf_ref, qb_ref, kb_ref, vb_ref, gb_ref, of_ref, ob_ref, s_ref,
                  *, n_sub):
    @pl.when(pl.program_id(1) == 0)
    def _():
        s_ref[...] = jnp.zeros_like(s_ref)

    row = lax.broadcasted_iota(jnp.int32, (CHUNK, CHUNK), 0)
    col = lax.broadcasted_iota(jnp.int32, (CHUNK, CHUNK), 1)
    eye_f = jnp.where(row == col, 1.0, 0.0).astype(F32)
    dirs = ((qf_ref, kf_ref, vf_ref, gf_ref, of_ref, row >= col, row > col),
            (qb_ref, kb_ref, vb_ref, gb_ref, ob_ref, row <= col, row < col))

    ch = []
    for d, (q_ref, k_ref, v_ref, g_ref, _, incl, strict) in enumerate(dirs):
        tri = jnp.where(incl, 1.0, 0.0).astype(BF16)
        for c in range(n_sub):
            rows = slice(c * CHUNK, (c + 1) * CHUNK)
            gates = g_ref[rows, :]
            gl = 2 * DN_HEADS * d
            g1, g2, g3 = _split3(gates)
            gc_all = (jnp.dot(tri, g1, preferred_element_type=F32)
                      + jnp.dot(tri, g2, preferred_element_type=F32)
                      + jnp.dot(tri, g3, preferred_element_type=F32))
            gsum_all = jnp.sum(gates, axis=0, keepdims=True)
            for h in range(DN_HEADS):
                lanes = slice(h * DN_DK, (h + 1) * DN_DK)
                kh = k_ref[rows, lanes]
                beta = gates[:, gl + DN_HEADS + h:gl + DN_HEADS + h + 1]
                gc = gc_all[:, gl + h:gl + h + 1]
                g_last = gsum_all[:, gl + h:gl + h + 1]
                gc_b = jnp.broadcast_to(gc, (CHUNK, CHUNK))
                gc_row = jnp.sum(gc_b * eye_f, axis=0, keepdims=True)
                decay = jnp.where(incl, jnp.exp(jnp.minimum(gc_b - gc_row, 0.0)), 0.0)
                e_gc = jnp.exp(gc)
                kb = kh * beta
                ch.append(dict(d=d, c=c, h=h, lanes=lanes, rows=rows, strict=strict, decay=decay,
                               kh16=kh.astype(BF16), kb16=kb.astype(BF16),
                               q16=q_ref[rows, lanes].astype(BF16),
                               rhs=jnp.concatenate([v_ref[rows, lanes] * beta, kb * e_gc],
                                                   axis=1).astype(BF16),
                               qe=q_ref[rows, lanes] * e_gc,
                               kdT=(kh * jnp.exp(g_last - gc)).T.astype(BF16),
                               e_last=jnp.exp(g_last)))
    for x in ch:
        a = lax.dot_general(x["kb16"], x["kh16"], _NT, preferred_element_type=F32) * x["decay"]
        x["a"] = jnp.where(x["strict"], a, 0.0)
        x["qk"] = (lax.dot_general(x["q16"], x["kh16"], _NT, preferred_element_type=F32)
                   * x["decay"]).astype(BF16)
    for x in ch:
        x["t"] = eye_f - x["a"]
        x["p"] = _mm(x["a"], x["a"])
    for level in range(5):
        for x in ch:
            p16 = x["p"].astype(BF16)
            x["t"] = x["t"] + jnp.dot(x["t"].astype(BF16), p16, preferred_element_type=F32)
            if level < 4:
                x["p"] = jnp.dot(p16, p16, preferred_element_type=F32)
    for x in ch:
        uw = jnp.dot(x["t"].astype(BF16), x["rhs"], preferred_element_type=F32)
        x["u"] = uw[:, :DN_DV]
        x["wq"] = jnp.concatenate([uw[:, DN_DV:], x["qe"]], axis=0).astype(BF16)

    for step in range(n_sub):
        cur = [x for x in ch if x["c"] == (step if x["d"] == 0 else n_sub - 1 - step)]
        for x in cur:
            x["s"] = s_ref[x["d"], x["h"]]
            x["ws"] = jnp.dot(x["wq"], x["s"].astype(BF16), preferred_element_type=F32)
        for x in cur:
            x["vn"] = (x["u"] - x["ws"][:CHUNK]).astype(BF16)
        for x in cur:
            o = x["ws"][CHUNK:] + jnp.dot(x["qk"], x["vn"], preferred_element_type=F32)
            dirs[x["d"]][4][x["rows"], x["lanes"]] = o
            s_ref[x["d"], x["h"]] = (x["s"] * x["e_last"]
                                     + jnp.dot(x["kdT"], x["vn"], preferred_element_type=F32))


def _delta_scan(q, k, v, gates):
    B, L, _ = q.shape
    n_sub = DN_STEP_CHUNKS
    rows = n_sub * CHUNK
    nblk = L // rows
    fwd = pl.BlockSpec((None, rows, DN_W), lambda b, j: (b, j, 0))
    bwd = pl.BlockSpec((None, rows, DN_W), lambda b, j: (b, nblk - 1 - j, 0))
    gfwd = pl.BlockSpec((None, rows, LANES), lambda b, j: (b, j, 0))
    gbwd = pl.BlockSpec((None, rows, LANES), lambda b, j: (b, nblk - 1 - j, 0))
    out = jax.ShapeDtypeStruct((B, L, DN_W), F32)
    return pl.pallas_call(
        functools.partial(_delta_kernel, n_sub=n_sub),
        grid=(B, nblk),
        in_specs=[fwd, fwd, fwd, gfwd, bwd, bwd, bwd, gbwd],
        out_specs=[fwd, bwd],
        out_shape=[out, out],
        scratch_shapes=[pltpu.VMEM((2, DN_HEADS, DN_DK, DN_DV), F32)],
        compiler_params=pltpu.CompilerParams(dimension_semantics=("parallel", "arbitrary"),
                                             vmem_limit_bytes=VMEM_LIMIT),
        name="delta_scan",
    )(q, k, v, gates, q, k, v, gates)


def _trunk(x, w, hy_pos_w1, hy_pos_b1, hy_pos_w2, hy_pos_b2, hy_pos_w3, hy_sin_freq, hy_bias, dn_norm_w):
    B, L, D = x.shape
    T = B * L
    x2d = x.reshape(T, D)
    x0, s, q, k, v, z, dn_gates = _in_proj(x2d, L, w["norm_mix"], *w["in_proj"])
    seq = lambda a: a.reshape(B, L, a.shape[-1])
    h_f, h_b = _hyena_filter(L, hy_pos_w1, hy_pos_b1, hy_pos_w2, hy_pos_b2, hy_pos_w3, hy_sin_freq)
    y_hy = _hyena_conv(seq(s), seq(x0), h_f, h_b, hy_bias)
    o_f, o_b = _delta_scan(seq(q), seq(k), seq(v), seq(dn_gates))
    x1, h_ffn, logits = _out_proj(x2d, y_hy.reshape(T, HY_CH), o_f.reshape(T, DN_W), o_b.reshape(T, DN_W),
                                 z, dn_norm_w, w["out_hy"], w["out_dn"], w["norm_ffn"],
                                 w["router_hi"], w["router_lo"], w["router_b"])
    yb, dest, gates = _moe(h_ffn, logits[:, :N_EXPERTS], w["wg"], w["wl"], w["bg"], w["bl"], w["wd"],
                           w["bd"])
    return _combine_final(yb, dest, gates, x1, w["norm_final"]).reshape(B, L, D)


def kernel(x_prompt, x_sample, norm_mix_w, w_in, hy_conv_w, hy_conv_b, hy_pos_w1, hy_pos_b1, hy_pos_w2, hy_pos_b2, hy_pos_w3, hy_sin_freq, hy_bias, dn_conv_w, dn_a_log, dn_dt_bias, dn_norm_w, w_out, norm_ffn_w, w_router, b_router, w_gate_up, b_gate_up, w_down, b_down, norm_final_w):
    w_out16 = w_out[0].astype(BF16)
    wr = jnp.pad(w_router[0], ((0, 0), (0, LANES - N_EXPERTS)))
    wr_hi = wr.astype(BF16)
    wg, wl = _deinterleave(w_gate_up[0])
    w = {
        "norm_mix": norm_mix_w[0],
        "in_proj": _in_proj_params(w_in[0], hy_conv_w[0], hy_conv_b[0], dn_conv_w[0], dn_a_log[0],
                                   dn_dt_bias[0]),
        "out_hy": w_out16[:HY_CH],
        "out_dn": w_out16[HY_CH:],
        "norm_ffn": norm_ffn_w[0],
        "router_hi": wr_hi,
        "router_lo": (wr - wr_hi.astype(F32)).astype(BF16),
        "router_b": jnp.pad(b_router[0], (0, LANES - N_EXPERTS)).reshape(1, LANES),
        "wg": wg,
        "wl": wl,
        "bg": b_gate_up[0][:, 0::2].reshape(N_EXPERTS, 1, D_FF),
        "bl": b_gate_up[0][:, 1::2].reshape(N_EXPERTS, 1, D_FF),
        "wd": w_down[0].astype(BF16),
        "bd": b_down[0].reshape(N_EXPERTS, 1, D_MODEL),
        "norm_final": norm_final_w,
    }
    mix = (hy_pos_w1[0], hy_pos_b1[0], hy_pos_w2[0], hy_pos_b2[0], hy_pos_w3[0], hy_sin_freq[0],
           hy_bias[0], dn_norm_w[0])
    return (_trunk(x_prompt, w, *mix), _trunk(x_sample, w, *mix))
```

```python
import functools
import math

import jax
import jax.numpy as jnp
import numpy as np
from jax import lax
from jax.experimental import pallas as pl
from jax.experimental.pallas import tpu as pltpu

D_MODEL = 1024
HY_CH = 512
DN_HEADS = 4
DN_DK = 128
DN_DV = 128
DN_QK = DN_HEADS * DN_DK
DN_W = DN_HEADS * DN_DV
HY_IN = 3 * HY_CH
DN_CONV = 2 * DN_QK + DN_W
N_GATE = 4 * DN_HEADS
SHORT_CONV = 3
CONV_COLS = HY_IN + DN_CONV
REST_COLS = DN_W + 128
HY_EMB = 33
HY_BANDS = (HY_EMB - 1) // 2
HY_DECAY_TARGET = 1e-2
HY_FAST_PCT = 0.3
HY_SLOW_PCT = 1.5
CHUNK = 64
N_EXPERTS = 32
TOP_K = 4
D_FF = D_MODEL
SWIGLU_ALPHA = 1.702
SWIGLU_LIMIT = 7.0
MOE_BLOCK = 512
EPS = 1e-6

LANES = 128
SUB = 8
ROW_TILE = 512
FFT_CB = 256
DN_STEP_CHUNKS = 4
DEINT_COLS = 512
DISPATCH_TILE = 512
COMBINE_TILE = 512
VMEM_LIMIT = 56 * 1024 * 1024

F32 = jnp.float32
BF16 = jnp.bfloat16

_NT = (((1,), (1,)), ((), ()))


def _rms(x, g):
    return x * lax.rsqrt(jnp.mean(x * x, axis=-1, keepdims=True) + EPS) * g


def _silu(x):
    return x * jax.nn.sigmoid(x)


def _head_l2norm(x):
    parts = []
    for hd in range(DN_HEADS):
        xh = x[:, hd * DN_DK:(hd + 1) * DN_DK]
        parts.append(xh * lax.rsqrt(jnp.sum(xh * xh, axis=-1, keepdims=True) + EPS))
    return jnp.concatenate(parts, axis=1)


def _in_proj_kernel(xp_ref, x_ref, xn_ref, g_ref, wc_ref, wr_ref, cw_ref, cb_ref, gt_ref,
                    x0_ref, s_ref, q_ref, k_ref, v_ref, z_ref, gate_ref, p_scr, *, tiles_per_seq):
    i = pl.program_id(0)
    first = (i % tiles_per_seq) == 0
    last = (i % tiles_per_seq) == tiles_per_seq - 1
    g = g_ref[...]
    h = _rms(x_ref[...], g).astype(BF16)
    hp = _rms(xp_ref[...], g).astype(BF16)
    hn = _rms(xn_ref[...], g).astype(BF16)
    def project(c0):
        cols = slice(c0, c0 + HY_CH)
        w = wc_ref[:, cols]
        p_scr[pl.ds(SUB, ROW_TILE), cols] = jnp.dot(h, w, preferred_element_type=F32)
        p_scr[pl.ds(0, SUB), cols] = jnp.where(first, 0.0, jnp.dot(hp, w, preferred_element_type=F32))
        p_scr[pl.ds(SUB + ROW_TILE, SUB), cols] = jnp.where(last, 0.0,
                                                            jnp.dot(hn, w, preferred_element_type=F32))

    def conv(c0):
        cols = slice(c0, c0 + HY_CH)
        return (p_scr[pl.ds(SUB - 1, ROW_TILE), cols] * cw_ref[0:1, cols]
                + p_scr[pl.ds(SUB, ROW_TILE), cols] * cw_ref[1:2, cols]
                + p_scr[pl.ds(SUB + 1, ROW_TILE), cols] * cw_ref[2:3, cols])

    project(0)
    project(HY_CH)
    x0_ref[...] = conv(0) + cb_ref[:, 0:HY_CH]
    project(2 * HY_CH)
    project(HY_IN)
    s_ref[...] = (conv(HY_CH) + cb_ref[:, HY_CH:2 * HY_CH]) * (conv(2 * HY_CH) + cb_ref[:, 2 * HY_CH:])
    project(HY_IN + DN_QK)
    q_ref[...] = _head_l2norm(_silu(conv(HY_IN))) * (DN_DK ** -0.5)
    project(HY_IN + 2 * DN_QK)
    k_ref[...] = _head_l2norm(_silu(conv(HY_IN + DN_QK)))
    rest = jnp.dot(h, wr_ref[...], preferred_element_type=F32)
    v_ref[...] = _silu(conv(HY_IN + 2 * DN_QK))
    z_ref[...] = rest[:, :DN_W]
    a = rest[:, DN_W:] + gt_ref[1:2]
    softplus = jnp.maximum(a, 0.0) + jnp.log(1.0 + jnp.exp(-jnp.abs(a)))
    gate_ref[...] = jnp.where(gt_ref[2:3] > 0.5, -gt_ref[0:1] * softplus, jax.nn.sigmoid(rest[:, DN_W:]))


def _in_proj(x2d, seq_len, g, w_conv, w_rest, conv_w, conv_b, gate_tab):
    T = x2d.shape[0]
    per = ROW_TILE // SUB
    last_sub = T // SUB - 1
    const = lambda i: (0, 0)
    row = lambda i: (i, 0)
    o512 = pl.BlockSpec((ROW_TILE, HY_CH), row)
    s512 = jax.ShapeDtypeStruct((T, HY_CH), F32)
    return pl.pallas_call(
        functools.partial(_in_proj_kernel, tiles_per_seq=seq_len // ROW_TILE),
        grid=(T // ROW_TILE,),
        in_specs=[pl.BlockSpec((SUB, D_MODEL), lambda i: (jnp.maximum(i * per - 1, 0), 0)),
                  pl.BlockSpec((ROW_TILE, D_MODEL), row),
                  pl.BlockSpec((SUB, D_MODEL), lambda i: (jnp.minimum((i + 1) * per, last_sub), 0)),
                  pl.BlockSpec((1, D_MODEL), const),
                  pl.BlockSpec((D_MODEL, CONV_COLS), const),
                  pl.BlockSpec((D_MODEL, REST_COLS), const),
                  pl.BlockSpec((SHORT_CONV, CONV_COLS), const),
                  pl.BlockSpec((1, HY_IN), const),
                  pl.BlockSpec((3, LANES), const)],
        out_specs=[o512, o512, o512, o512, o512, o512, pl.BlockSpec((ROW_TILE, LANES), row)],
        out_shape=[s512, s512, s512, s512, s512, s512, jax.ShapeDtypeStruct((T, LANES), F32)],
        scratch_shapes=[pltpu.VMEM((ROW_TILE + 2 * SUB, CONV_COLS), F32)],
        compiler_params=pltpu.CompilerParams(dimension_semantics=("parallel",),
                                             vmem_limit_bytes=VMEM_LIMIT),
        name="in_proj",
    )(x2d, x2d, x2d, g.reshape(1, D_MODEL), w_conv, w_rest, conv_w, conv_b, gate_tab)


def _in_proj_params(w_in, hy_conv_w, hy_conv_b, dn_conv_w, dn_a_log, dn_dt_bias):
    H = DN_HEADS
    w16 = w_in.astype(BF16)
    gc = w16[:, CONV_COLS + DN_W:]
    gc = jnp.concatenate([gc[:, 0:H], gc[:, 2 * H:3 * H], gc[:, H:2 * H], gc[:, 3 * H:]], axis=1)
    w_rest = jnp.concatenate([w16[:, CONV_COLS:CONV_COLS + DN_W],
                              jnp.pad(gc, ((0, 0), (0, LANES - N_GATE)))], axis=1)
    zero, one, pad = jnp.zeros((H,), F32), jnp.ones((H,), F32), jnp.zeros((LANES - N_GATE,), F32)
    gate_tab = jnp.stack([jnp.concatenate([jnp.exp(dn_a_log[0]), zero, jnp.exp(dn_a_log[1]), zero, pad]),
                          jnp.concatenate([dn_dt_bias[0], zero, dn_dt_bias[1], zero, pad]),
                          jnp.concatenate([one, zero, one, zero, pad])])
    return (w16[:, :CONV_COLS], w_rest, jnp.concatenate([hy_conv_w, dn_conv_w], axis=1),
            hy_conv_b.reshape(1, HY_IN), gate_tab)


def _out_proj_kernel(x_ref, yh_ref, of_ref, ob_ref, z_ref, nw_ref, wh_ref, wd_ref, g_ref, wrh_ref,
                     wrl_ref, br_ref, x1_ref, h_ref, lg_ref):
    o = of_ref[...] + ob_ref[...]
    heads = []
    for hd in range(DN_HEADS):
        oh = o[:, hd * DN_DV:(hd + 1) * DN_DV]
        heads.append(oh * lax.rsqrt(jnp.mean(oh * oh, axis=-1, keepdims=True) + EPS))
    z = z_ref[...]
    y_dn = jnp.concatenate(heads, axis=1) * nw_ref[...] * (z * jax.nn.sigmoid(z))
    x1 = (x_ref[...]
          + jnp.dot(yh_ref[...].astype(BF16), wh_ref[...], preferred_element_type=F32)
          + jnp.dot(y_dn.astype(BF16), wd_ref[...], preferred_element_type=F32))
    x1_ref[...] = x1
    h = _rms(x1, g_ref[...])
    h_hi = h.astype(BF16)
    h_lo = (h - h_hi.astype(F32)).astype(BF16)
    h_ref[...] = h
    lg_ref[...] = (jnp.dot(h_hi, wrh_ref[...], preferred_element_type=F32)
                   + jnp.dot(h_lo, wrh_ref[...], preferred_element_type=F32)
                   + jnp.dot(h_hi, wrl_ref[...], preferred_element_type=F32)
                   + br_ref[...])


def _out_proj(x2d, y_hy, o_f, o_b, z, dn_norm_w, w_oh, w_od, g, wr_hi, wr_lo, br):
    T = x2d.shape[0]
    const = lambda i: (0, 0)
    row = lambda i: (i, 0)
    return pl.pallas_call(
        _out_proj_kernel,
        grid=(T // ROW_TILE,),
        in_specs=[pl.BlockSpec((ROW_TILE, D_MODEL), row),
                  pl.BlockSpec((ROW_TILE, HY_CH), row),
                  pl.BlockSpec((ROW_TILE, DN_W), row),
                  pl.BlockSpec((ROW_TILE, DN_W), row),
                  pl.BlockSpec((ROW_TILE, DN_W), row),
                  pl.BlockSpec((1, DN_W), const),
                  pl.BlockSpec((HY_CH, D_MODEL), const),
                  pl.BlockSpec((DN_W, D_MODEL), const),
                  pl.BlockSpec((1, D_MODEL), const),
                  pl.BlockSpec((D_MODEL, LANES), const),
                  pl.BlockSpec((D_MODEL, LANES), const),
                  pl.BlockSpec((1, LANES), const)],
        out_specs=[pl.BlockSpec((ROW_TILE, D_MODEL), row),
                   pl.BlockSpec((ROW_TILE, D_MODEL), row),
                   pl.BlockSpec((ROW_TILE, LANES), row)],
        out_shape=[jax.ShapeDtypeStruct((T, D_MODEL), F32),
                   jax.ShapeDtypeStruct((T, D_MODEL), F32),
                   jax.ShapeDtypeStruct((T, LANES), F32)],
        compiler_params=pltpu.CompilerParams(dimension_semantics=("parallel",),
                                             vmem_limit_bytes=VMEM_LIMIT),
        name="out_proj_router",
    )(x2d, y_hy, o_f, o_b, z, jnp.tile(dn_norm_w, DN_HEADS).reshape(1, DN_W), w_oh, w_od,
      g.reshape(1, D_MODEL), wr_hi, wr_lo, br)


def _deint_kernel(w_ref, p_ref, og_ref, ol_ref):
    half = DEINT_COLS // 2
    sel = jnp.dot(w_ref[0].astype(BF16), p_ref[...], preferred_element_type=F32)
    og_ref[0] = sel[:, :half].astype(BF16)
    ol_ref[0] = sel[:, half:].astype(BF16)


def _deinterleave(w_gate_up):
    half = DEINT_COLS // 2
    r = np.arange(DEINT_COLS)[:, None]
    c = np.arange(DEINT_COLS)[None, :]
    perm = jnp.asarray(np.where(c < half, r == 2 * c, r == 2 * (c - half) + 1), BF16)
    out = jax.ShapeDtypeStruct((N_EXPERTS, D_MODEL, D_FF), BF16)
    return pl.pallas_call(
        _deint_kernel,
        grid=(N_EXPERTS, 2 * D_FF // DEINT_COLS),
        in_specs=[pl.BlockSpec((1, D_MODEL, DEINT_COLS), lambda e, j: (e, 0, j)),
                  pl.BlockSpec((DEINT_COLS, DEINT_COLS), lambda e, j: (0, 0))],
        out_specs=[pl.BlockSpec((1, D_MODEL, half), lambda e, j: (e, 0, j)),
                   pl.BlockSpec((1, D_MODEL, half), lambda e, j: (e, 0, j))],
        out_shape=[out, out],
        compiler_params=pltpu.CompilerParams(dimension_semantics=("parallel", "parallel")),
        name="deinterleave_gate_up",
    )(w_gate_up, perm)


def _dispatch_kernel(tail_ref, dest_ref, h_ref, xb_ref, zero_scr, sem):
    n = h_ref.shape[0]

    @pl.when(pl.program_id(0) == 0)
    def _():
        zero_scr[...] = jnp.zeros_like(zero_scr)
        for e in range(N_EXPERTS):
            t0 = pl.multiple_of(tail_ref[0, e], SUB)
            pltpu.make_async_copy(zero_scr, xb_ref.at[pl.ds(t0, MOE_BLOCK), :], sem).start()
        for e in range(N_EXPERTS):
            pltpu.make_async_copy(zero_scr, xb_ref.at[pl.ds(0, MOE_BLOCK), :], sem).wait()

    def issue(r, c):
        for k in range(TOP_K):
            d = dest_ref[0, r * TOP_K + k]
            pltpu.make_async_copy(h_ref.at[pl.ds(r, 1), :], xb_ref.at[pl.ds(d, 1), :],
                                  sem).start(priority=k % 2)
        return c

    lax.fori_loop(0, n, issue, 0, unroll=8)

    def drain(r, c):
        for k in range(TOP_K):
            pltpu.make_async_copy(h_ref.at[pl.ds(0, 1), :], xb_ref.at[pl.ds(0, 1), :], sem).wait()
        return c

    lax.fori_loop(0, n, drain, 0, unroll=8)


def _dispatch(h, dest, tail_start, n_rows):
    T = h.shape[0]
    tm = DISPATCH_TILE
    return pl.pallas_call(
        _dispatch_kernel,
        grid=(T // tm,),
        in_specs=[pl.BlockSpec((1, N_EXPERTS), lambda i: (0, 0), memory_space=pltpu.SMEM),
                  pl.BlockSpec((None, 1, tm * TOP_K), lambda i: (i, 0, 0), memory_space=pltpu.SMEM),
                  pl.BlockSpec((tm, D_MODEL), lambda i: (i, 0))],
        out_specs=pl.BlockSpec(memory_space=pl.ANY),
        out_shape=jax.ShapeDtypeStruct((n_rows, D_MODEL), F32),
        scratch_shapes=[pltpu.VMEM((MOE_BLOCK, D_MODEL), F32), pltpu.SemaphoreType.DMA],
        compiler_params=pltpu.CompilerParams(dimension_semantics=("arbitrary",)),
        name="moe_dispatch",
    )(tail_start.reshape(1, N_EXPERTS), dest.reshape(T // tm, 1, tm * TOP_K), h)


def _expert_kernel(be_ref, nb_ref, xb_ref, wg_ref, wl_ref, bg_ref, bl_ref, wd_ref, bd_ref, y_ref):
    i = pl.program_id(0)

    @pl.when(i < nb_ref[0])
    def _():
        xb = xb_ref[...].astype(BF16)
        hg = jnp.dot(xb, wg_ref[0], preferred_element_type=F32) + bg_ref[0]
        hl = jnp.dot(xb, wl_ref[0], preferred_element_type=F32) + bl_ref[0]
        x_glu = jnp.minimum(hg, SWIGLU_LIMIT)
        x_lin = jnp.clip(hl, -SWIGLU_LIMIT, SWIGLU_LIMIT)
        act = x_glu * jax.nn.sigmoid(SWIGLU_ALPHA * x_glu) * (x_lin + 1.0)
        y = jnp.dot(act.astype(BF16), wd_ref[0], preferred_element_type=F32) + bd_ref[0]
        y_ref[...] = y

    @pl.when(i >= nb_ref[0])
    def _():
        y_ref[...] = jnp.zeros_like(y_ref)


def _expert_mlp(xb, block_e, n_used, wg, wl, bg, bl, wd, bd):
    n_rows = xb.shape[0]
    n_blocks = n_rows // MOE_BLOCK
    rowm = lambda i, be, nb: (i, 0)
    exp3 = lambda i, be, nb: (be[i], 0, 0)
    grid_spec = pltpu.PrefetchScalarGridSpec(
        num_scalar_prefetch=2,
        grid=(n_blocks,),
        in_specs=[pl.BlockSpec((MOE_BLOCK, D_MODEL), rowm),
                  pl.BlockSpec((1, D_MODEL, D_FF), exp3),
                  pl.BlockSpec((1, D_MODEL, D_FF), exp3),
                  pl.BlockSpec((1, 1, D_FF), exp3),
                  pl.BlockSpec((1, 1, D_FF), exp3),
                  pl.BlockSpec((1, D_FF, D_MODEL), exp3),
                  pl.BlockSpec((1, 1, D_MODEL), exp3)],
        out_specs=pl.BlockSpec((MOE_BLOCK, D_MODEL), rowm),
    )
    return pl.pallas_call(
        _expert_kernel,
        grid_spec=grid_spec,
        out_shape=jax.ShapeDtypeStruct((n_rows, D_MODEL), F32),
        compiler_params=pltpu.CompilerParams(dimension_semantics=("arbitrary",),
                                             vmem_limit_bytes=VMEM_LIMIT),
        name="expert_mlp",
    )(block_e, n_used, xb, wg, wl, bg, bl, wd, bd)


def _moe(h, logits, wg, wl, bg, bl, wd, bd):
    T = h.shape[0]
    TK = T * TOP_K
    top_vals, top_idx = lax.top_k(logits, TOP_K)
    gates = jax.nn.softmax(top_vals, axis=-1)
    sel = jnp.sum(jax.nn.one_hot(top_idx, N_EXPERTS, dtype=jnp.int32), axis=1)
    before = jnp.cumsum(sel, axis=0) - sel
    counts = jnp.sum(sel, axis=0)
    padded = (counts + MOE_BLOCK - 1) // MOE_BLOCK * MOE_BLOCK
    pad_end = jnp.cumsum(padded)
    pad_start = pad_end - padded
    rank = jnp.take_along_axis(before, top_idx, axis=1)
    dest = (pad_start[top_idx] + rank).astype(jnp.int32)
    n_blocks = (TK + MOE_BLOCK - 1) // MOE_BLOCK + N_EXPERTS
    n_rows = n_blocks * MOE_BLOCK
    block_start = jnp.arange(n_blocks, dtype=jnp.int32) * MOE_BLOCK
    block_e = jnp.minimum(jnp.sum((block_start[:, None] >= pad_end[None, :]).astype(jnp.int32), axis=1),
                          N_EXPERTS - 1)
    n_used = (pad_end[-1] // MOE_BLOCK).astype(jnp.int32).reshape(1)
    tail_start = jnp.minimum((pad_start + counts) // SUB * SUB, n_rows - MOE_BLOCK).astype(jnp.int32)
    xb = _dispatch(h, dest, tail_start, n_rows)
    yb = _expert_mlp(xb, block_e, n_used, wg, wl, bg, bl, wd, bd)
    return yb, dest, gates


def _combine_kernel(dest_ref, gate_ref, x1_ref, g_ref, yb_ref, o_ref, buf, sem):
    n = x1_ref.shape[0]

    def issue(r, c):
        for k in range(TOP_K):
            d = dest_ref[0, r * TOP_K + k]
            pltpu.make_async_copy(yb_ref.at[pl.ds(d, 1), :], buf.at[k, pl.ds(r, 1), :],
                                  sem).start(priority=k % 2)
        return c

    lax.fori_loop(0, n, issue, 0, unroll=8)

    def drain(r, c):
        for k in range(TOP_K):
            pltpu.make_async_copy(yb_ref.at[pl.ds(0, 1), :], buf.at[0, pl.ds(0, 1), :], sem).wait()
        return c

    lax.fori_loop(0, n, drain, 0, unroll=8)
    x = x1_ref[...]
    for k in range(TOP_K):
        x = x + gate_ref[:, k:k + 1] * buf[k]
    o_ref[...] = _rms(x, g_ref[...])


def _combine_final(yb, dest, gates, x1, g):
    T = x1.shape[0]
    tm = COMBINE_TILE
    row = lambda i: (i, 0)
    return pl.pallas_call(
        _combine_kernel,
        grid=(T // tm,),
        in_specs=[pl.BlockSpec((None, 1, tm * TOP_K), lambda i: (i, 0, 0), memory_space=pltpu.SMEM),
                  pl.BlockSpec((tm, TOP_K), row),
                  pl.BlockSpec((tm, D_MODEL), row),
                  pl.BlockSpec((1, D_MODEL), lambda i: (0, 0)),
                  pl.BlockSpec(memory_space=pl.ANY)],
        out_specs=pl.BlockSpec((tm, D_MODEL), row),
        out_shape=jax.ShapeDtypeStruct((T, D_MODEL), F32),
        scratch_shapes=[pltpu.VMEM((TOP_K, tm, D_MODEL), F32), pltpu.SemaphoreType.DMA],
        compiler_params=pltpu.CompilerParams(dimension_semantics=("arbitrary",),
                                             vmem_limit_bytes=VMEM_LIMIT),
        name="moe_combine_final",
    )(dest.reshape(T // tm, 1, tm * TOP_K), gates, x1, g.reshape(1, D_MODEL), yb)


def _fft_tables(nc, nf):
    n = nc * nf
    kc = np.arange(nc, dtype=np.float64)
    a1 = 2.0 * np.pi * np.outer(kc, np.arange(nc // 2)) / nc
    c1, s1 = np.cos(a1), np.sin(a1)
    l1 = np.block([[c1, s1], [-s1, c1]])
    a2 = 2.0 * np.pi * np.outer(np.arange(nf), np.arange(nf)) / nf
    c2, s2 = np.cos(a2), np.sin(a2)
    m2 = np.block([[c2, s2], [-s2, c2]])
    m2i = np.block([[c2, -s2], [s2, c2]])
    a3 = 2.0 * np.pi * np.outer(np.arange(nc // 2), kc) / nc
    c3, s3 = np.cos(a3), np.sin(a3)
    l3 = np.block([[c3, -s3], [s3, c3]]) / n
    return tuple(jnp.asarray(m, F32).astype(BF16) for m in (l1, m2, m2i, l3))


def _twiddle(nc, nf):
    n = nc * nf
    ph = (jnp.arange(nf, dtype=jnp.int32)[:, None] * jnp.arange(nc, dtype=jnp.int32)[None, :]) % n
    ang = ph.astype(F32) * (2.0 * math.pi / n)
    tw = jnp.stack([jnp.cos(ang), jnp.sin(ang)], axis=1)
    return jnp.broadcast_to(tw[..., None], (nf, 2, nc, LANES))


def _lane_tile(t, width):
    return t if width == LANES else jnp.concatenate([t] * (width // LANES), axis=1)


def _fft1_kernel(z_ref, l1_ref, tw_ref, o_ref, *, nc):
    cb = o_ref.shape[-1]
    for j in range(SUB):
        rhs = jnp.concatenate([z_ref[0, :, j, :], z_ref[1, :, j, :]], axis=0).astype(BF16)
        a = jnp.dot(l1_ref[...], rhs, preferred_element_type=F32)
        ar, ai = a[:nc], a[nc:]
        twr = _lane_tile(tw_ref[j, 0], cb)
        twi = _lane_tile(tw_ref[j, 1], cb)
        o_ref[j, 0] = ar * twr + ai * twi
        o_ref[j, 1] = ai * twr - ar * twi


def _fft1(z, l1, tw, nc, nf):
    P, C = z.shape[0], z.shape[-1]
    cb = FFT_CB
    z_spec = pl.BlockSpec((None, 2, nc // 2, SUB, cb), lambda p, j, c: (p, 0, 0, j, c))
    return pl.pallas_call(
        functools.partial(_fft1_kernel, nc=nc),
        grid=(P, nf // SUB, C // cb),
        in_specs=[z_spec,
                  pl.BlockSpec(l1.shape, lambda p, j, c: (0, 0)),
                  pl.BlockSpec((SUB, 2, nc, LANES), lambda p, j, c: (j, 0, 0, 0))],
        out_specs=pl.BlockSpec((None, SUB, 2, nc, cb), lambda p, j, c: (p, j, 0, 0, c)),
        out_shape=jax.ShapeDtypeStruct((P, nf, 2, nc, C), F32),
        compiler_params=pltpu.CompilerParams(dimension_semantics=("parallel", "parallel", "parallel"),
                                             vmem_limit_bytes=VMEM_LIMIT),
        name="hyena_fft_stage1",
    )(z, l1, tw)


def _fft2_filter_kernel(a_ref, m2_ref, k_ref, *, nf):
    for j in range(SUB):
        rhs = jnp.concatenate([a_ref[:, 0, j, :], a_ref[:, 1, j, :]], axis=0).astype(BF16)
        x = jnp.dot(m2_ref[...], rhs, preferred_element_type=F32)
        k_ref[j, 0] = x[:nf]
        k_ref[j, 1] = x[nf:]


def _fft2_filter(a, m2, nc, nf):
    C = a.shape[-1]
    cb = FFT_CB
    return pl.pallas_call(
        functools.partial(_fft2_filter_kernel, nf=nf),
        grid=(nc // SUB, C // cb),
        in_specs=[pl.BlockSpec((None, nf, 2, SUB, cb), lambda k, c: (0, 0, 0, k, c)),
                  pl.BlockSpec(m2.shape, lambda k, c: (0, 0))],
        out_specs=pl.BlockSpec((SUB, 2, nf, cb), lambda k, c: (k, 0, 0, c)),
        out_shape=jax.ShapeDtypeStruct((nc, 2, nf, C), F32),
        compiler_params=pltpu.CompilerParams(dimension_semantics=("parallel", "parallel"),
                                             vmem_limit_bytes=VMEM_LIMIT),
        name="hyena_filter_spectrum",
    )(a, m2)


def _fft2_kernel(a_ref, k_ref, m2_ref, m2i_ref, o_ref, *, nf):
    for j in range(SUB):
        rhs = jnp.concatenate([a_ref[:, 0, j, :], a_ref[:, 1, j, :]], axis=0).astype(BF16)
        x = jnp.dot(m2_ref[...], rhs, preferred_element_type=F32)
        xr, xi = x[:nf], x[nf:]
        kr, ki = k_ref[j, 0], k_ref[j, 1]
        y = jnp.concatenate([xr * kr - xi * ki, xr * ki + xi * kr], axis=0).astype(BF16)
        b = jnp.dot(m2i_ref[...], y, preferred_element_type=F32)
        o_ref[:, 0, j, :] = b[:nf]
        o_ref[:, 1, j, :] = b[nf:]


def _fft2(a, kspec, m2, m2i, nc, nf):
    P, C = a.shape[0], a.shape[-1]
    cb = FFT_CB
    blk = pl.BlockSpec((None, nf, 2, SUB, cb), lambda p, k, c: (p, 0, 0, k, c))
    return pl.pallas_call(
        functools.partial(_fft2_kernel, nf=nf),
        grid=(P, nc // SUB, C // cb),
        in_specs=[blk,
                  pl.BlockSpec((SUB, 2, nf, cb), lambda p, k, c: (k, 0, 0, c)),
                  pl.BlockSpec(m2.shape, lambda p, k, c: (0, 0)),
                  pl.BlockSpec(m2i.shape, lambda p, k, c: (0, 0))],
        out_specs=blk,
        out_shape=jax.ShapeDtypeStruct(a.shape, F32),
        compiler_params=pltpu.CompilerParams(dimension_semantics=("parallel", "parallel", "parallel"),
                                             vmem_limit_bytes=VMEM_LIMIT),
        name="hyena_fft_stage2",
    )(a, kspec, m2, m2i)


def _fft3_kernel(b_ref, tw_ref, l3_ref, s_ref, x0_ref, bias_ref, o_ref, *, nc):
    cb = o_ref.shape[-1]
    half = nc // 2
    for j in range(SUB):
        br, bi = b_ref[j, 0], b_ref[j, 1]
        twr = _lane_tile(tw_ref[j, 0], cb)
        twi = _lane_tile(tw_ref[j, 1], cb)
        rhs = jnp.concatenate([br * twr - bi * twi, br * twi + bi * twr], axis=0).astype(BF16)
        y = jnp.dot(l3_ref[...], rhs, preferred_element_type=F32)
        for q in range(2):
            s = s_ref[q, :, j, :]
            o_ref[q, :, j, :] = x0_ref[q, :, j, :] * (y[q * half:(q + 1) * half] + s * bias_ref[...])


def _fft3(b, tw, l3, s5, x05, bias, nc, nf):
    P, C = b.shape[0], b.shape[-1]
    cb = FFT_CB
    seq = pl.BlockSpec((None, 2, nc // 2, SUB, cb), lambda p, j, c: (p, 0, 0, j, c))
    return pl.pallas_call(
        functools.partial(_fft3_kernel, nc=nc),
        grid=(P, nf // SUB, C // cb),
        in_specs=[pl.BlockSpec((None, SUB, 2, nc, cb), lambda p, j, c: (p, j, 0, 0, c)),
                  pl.BlockSpec((SUB, 2, nc, LANES), lambda p, j, c: (j, 0, 0, 0)),
                  pl.BlockSpec(l3.shape, lambda p, j, c: (0, 0)),
                  seq, seq,
                  pl.BlockSpec((1, cb), lambda p, j, c: (0, c))],
        out_specs=seq,
        out_shape=jax.ShapeDtypeStruct(s5.shape, F32),
        compiler_params=pltpu.CompilerParams(dimension_semantics=("parallel", "parallel", "parallel"),
                                             vmem_limit_bytes=VMEM_LIMIT),
        name="hyena_fft_stage3",
    )(b, tw, l3, s5, x05, bias)


def _fft_split(n):
    nf = 1 << (int(math.log2(n)) // 2)
    return n // nf, nf


def _filter_tables(nc):
    half = nc // 2
    kc = np.arange(nc, dtype=np.float64)[:, None]
    r = np.arange(half, dtype=np.float64)
    mats = []
    for rows_b, drop0 in ((half + (half - 1 - r), False), (nc - r, True)):
        a = 2.0 * np.pi * kc * np.concatenate([r, rows_b])[None, :] / nc
        m = np.concatenate([np.cos(a), -np.sin(a)], axis=0)
        if drop0:
            m[:, half] = 0.0
        mats.append(m)
    return tuple(jnp.asarray(m, F32).astype(BF16) for m in mats)


def _fft1_filter_kernel(hf_ref, hba_ref, hbb_ref, mb_ref, mb0_ref, tw_ref, o_ref, *, nc):
    cb = o_ref.shape[-1]
    m_first = jnp.where(pl.program_id(0) == 0, mb0_ref[...], mb_ref[...])
    for j in range(SUB):
        src = hbb_ref[:, 0, :] if j == 0 else hba_ref[:, SUB - j, :]
        rhs = jnp.concatenate([hf_ref[:, j, :], src], axis=0).astype(BF16)
        a = jnp.dot(m_first if j == 0 else mb_ref[...], rhs, preferred_element_type=F32)
        ar, ai = a[:nc], a[nc:]
        twr = _lane_tile(tw_ref[j, 0], cb)
        twi = _lane_tile(tw_ref[j, 1], cb)
        o_ref[j, 0] = ar * twr + ai * twi
        o_ref[j, 1] = ai * twr - ar * twi


def _fft1_filter(h_f, h_b, mb, mb0, tw, nc, nf):
    C = h_f.shape[-1]
    cb = FFT_CB
    nblk = nf // SUB
    hf3 = h_f.reshape(nc // 2, nf, C)
    hb3 = h_b.reshape(nc // 2, nf, C)
    blk = lambda f: pl.BlockSpec((nc // 2, SUB, cb), f)
    return pl.pallas_call(
        functools.partial(_fft1_filter_kernel, nc=nc),
        grid=(nblk, C // cb),
        in_specs=[blk(lambda j, c: (0, j, c)),
                  blk(lambda j, c: (0, nblk - 1 - j, c)),
                  blk(lambda j, c: (0, (nblk - j) % nblk, c)),
                  pl.BlockSpec(mb.shape, lambda j, c: (0, 0)),
                  pl.BlockSpec(mb0.shape, lambda j, c: (0, 0)),
                  pl.BlockSpec((SUB, 2, nc, LANES), lambda j, c: (j, 0, 0, 0))],
        out_specs=pl.BlockSpec((None, SUB, 2, nc, cb), lambda j, c: (0, j, 0, 0, c)),
        out_shape=jax.ShapeDtypeStruct((1, nf, 2, nc, C), F32),
        compiler_params=pltpu.CompilerParams(dimension_semantics=("parallel", "parallel"),
                                             vmem_limit_bytes=VMEM_LIMIT),
        name="hyena_filter_stage1",
    )(hf3, hb3, hb3, mb, mb0, tw)


def _hyena_conv(s, x0, h_f, h_b, bias):
    B, L, C = s.shape
    nc, nf = _fft_split(2 * L)
    tw = _twiddle(nc, nf)
    l1, m2, m2i, l3 = _fft_tables(nc, nf)
    mb, mb0 = _filter_tables(nc)
    kspec = _fft2_filter(_fft1_filter(h_f, h_b, mb, mb0, tw, nc, nf), m2, nc, nf)
    s5 = s.reshape(B // 2, 2, nc // 2, nf, C)
    x05 = x0.reshape(B // 2, 2, nc // 2, nf, C)
    a = _fft1(s5, l1, tw, nc, nf)
    b = _fft2(a, kspec, m2, m2i, nc, nf)
    return _fft3(b, tw, l3, s5, x05, bias.reshape(1, C), nc, nf).reshape(B, L, C)


def _hyena_filter(L, w1, b1, w2, b2, w3, freq):
    pos = jnp.arange(L, dtype=F32)
    t = jnp.linspace(0.0, 1.0, L, dtype=F32)[:, None]
    bands = jnp.linspace(1e-4, HY_BANDS - 1, HY_BANDS, dtype=F32)
    ang = (2.0 * math.pi / L) * pos[:, None] * bands[None, :]
    z = jnp.concatenate([t, jnp.cos(ang), -jnp.sin(ang)], axis=-1)
    h = jnp.sin(freq * (z @ w1 + b1))
    h = jnp.sin(freq * (h @ w2 + b2))
    h = h @ w3
    deltas = jnp.abs(jnp.linspace(math.log(HY_DECAY_TARGET) / HY_SLOW_PCT,
                                  math.log(HY_DECAY_TARGET) / HY_FAST_PCT, HY_CH, dtype=F32))
    window = jnp.exp(-t * deltas[None, :])
    h_f = h[:, :HY_CH] * window
    h_b = h[:, HY_CH:] * window
    l1 = jnp.sum(jnp.abs(h_f), axis=0) + jnp.sum(jnp.abs(h_b[1:]), axis=0)
    return h_f / l1, h_b / l1


def _split3(x):
    x1 = x.astype(BF16)
    r = x - x1.astype(F32)
    x2 = r.astype(BF16)
    x3 = (r - x2.astype(F32)).astype(BF16)
    return x1, x2, x3


def _mm(a, b):
    return jnp.dot(a.astype(BF16), b.astype(BF16), preferred_element_type=F32)


def _delta_kernel(qf_ref, kf_ref, vf_ref, gf_ref, qb_ref, kb_ref, vb_ref, gb_ref, of_ref, ob_ref, s_ref,
                  *, n_sub):
    @pl.when(pl.program_id(1) == 0)
    def _():
        s_ref[...] = jnp.zeros_like(s_ref)

    row = lax.broadcasted_iota(jnp.int32, (CHUNK, CHUNK), 0)
    col = lax.broadcasted_iota(jnp.int32, (CHUNK, CHUNK), 1)
    eye_f = jnp.where(row == col, 1.0, 0.0).astype(F32)
    dirs = ((qf_ref, kf_ref, vf_ref, gf_ref, of_ref, row >= col, row > col),
            (qb_ref, kb_ref, vb_ref, gb_ref, ob_ref, row <= col, row < col))

    ch = []
    for d, (q_ref, k_ref, v_ref, g_ref, _, incl, strict) in enumerate(dirs):
        tri = jnp.where(incl, 1.0, 0.0).astype(BF16)
        for c in range(n_sub):
            rows = slice(c * CHUNK, (c + 1) * CHUNK)
            gates = g_ref[rows, :]
            gl = 2 * DN_HEADS * d
            g1, g2, g3 = _split3(gates)
            gc_all = (jnp.dot(tri, g1, preferred_element_type=F32)
                      + jnp.dot(tri, g2, preferred_element_type=F32)
                      + jnp.dot(tri, g3, preferred_element_type=F32))
            gsum_all = jnp.sum(gates, axis=0, keepdims=True)
            for h in range(DN_HEADS):
                lanes = slice(h * DN_DK, (h + 1) * DN_DK)
                kh = k_ref[rows, lanes]
                beta = gates[:, gl + DN_HEADS + h:gl + DN_HEADS + h + 1]
                gc = gc_all[:, gl + h:gl + h + 1]
                g_last = gsum_all[:, gl + h:gl + h + 1]
                gc_b = jnp.broadcast_to(gc, (CHUNK, CHUNK))
                gc_row = jnp.sum(gc_b * eye_f, axis=0, keepdims=True)
                decay = jnp.where(incl, jnp.exp(jnp.minimum(gc_b - gc_row, 0.0)), 0.0)
                e_gc = jnp.exp(gc)
                kb = kh * beta
                ch.append(dict(d=d, c=c, h=h, lanes=lanes, rows=rows, strict=strict, decay=decay,
                               kh16=kh.astype(BF16), kb16=kb.astype(BF16),
                               q16=q_ref[rows, lanes].astype(BF16),
                               rhs=jnp.concatenate([v_ref[rows, lanes] * beta, kb * e_gc],
                                                   axis=1).astype(BF16),
                               qe=q_ref[rows, lanes] * e_gc,
                               kdT=(kh * jnp.exp(g_last - gc)).T.astype(BF16),
                               e_last=jnp.exp(g_last)))
    for x in ch:
        a = lax.dot_general(x["kb16"], x["kh16"], _NT, preferred_element_type=F32) * x["decay"]
        x["a"] = jnp.where(x["strict"], a, 0.0)
        x["qk"] = (lax.dot_general(x["q16"], x["kh16"], _NT, preferred_element_type=F32)
                   * x["decay"]).astype(BF16)
    for x in ch:
        x["t"] = eye_f - x["a"]
        x["p"] = _mm(x["a"], x["a"])
    for level in range(5):
        for x in ch:
            p16 = x["p"].astype(BF16)
            x["t"] = x["t"] + jnp.dot(x["t"].astype(BF16), p16, preferred_element_type=F32)
            if level < 4:
                x["p"] = jnp.dot(p16, p16, preferred_element_type=F32)
    for x in ch:
        uw = jnp.dot(x["t"].astype(BF16), x["rhs"], preferred_element_type=F32)
        x["u"] = uw[:, :DN_DV]
        x["wq"] = jnp.concatenate([uw[:, DN_DV:], x["qe"]], axis=0).astype(BF16)

    for step in range(n_sub):
        cur = [x for x in ch if x["c"] == (step if x["d"] == 0 else n_sub - 1 - step)]
        for x in cur:
            x["s"] = s_ref[x["d"], x["h"]]
            x["ws"] = jnp.dot(x["wq"], x["s"].astype(BF16), preferred_element_type=F32)
        for x in cur:
            x["vn"] = (x["u"] - x["ws"][:CHUNK]).astype(BF16)
        for x in cur:
            o = x["ws"][CHUNK:] + jnp.dot(x["qk"], x["vn"], preferred_element_type=F32)
            dirs[x["d"]][4][x["rows"], x["lanes"]] = o
            s_ref[x["d"], x["h"]] = (x["s"] * x["e_last"]
                                     + jnp.dot(x["kdT"], x["vn"], preferred_element_type=F32))


def _delta_scan(q, k, v, gates):
    B, L, _ = q.shape
    n_sub = DN_STEP_CHUNKS
    rows = n_sub * CHUNK
    nblk = L // rows
    fwd = pl.BlockSpec((None, rows, DN_W), lambda b, j: (b, j, 0))
    bwd = pl.BlockSpec((None, rows, DN_W), lambda b, j: (b, nblk - 1 - j, 0))
    gfwd = pl.BlockSpec((None, rows, LANES), lambda b, j: (b, j, 0))
    gbwd = pl.BlockSpec((None, rows, LANES), lambda b, j: (b, nblk - 1 - j, 0))
    out = jax.ShapeDtypeStruct((B, L, DN_W), F32)
    return pl.pallas_call(
        functools.partial(_delta_kernel, n_sub=n_sub),
        grid=(B, nblk),
        in_specs=[fwd, fwd, fwd, gfwd, bwd, bwd, bwd, gbwd],
        out_specs=[fwd, bwd],
        out_shape=[out, out],
        scratch_shapes=[pltpu.VMEM((2, DN_HEADS, DN_DK, DN_DV), F32)],
        compiler_params=pltpu.CompilerParams(dimension_semantics=("parallel", "arbitrary"),
                                             vmem_limit_bytes=VMEM_LIMIT),
        name="delta_scan",
    )(q, k, v, gates, q, k, v, gates)


def _trunk(x, w, hy_pos_w1, hy_pos_b1, hy_pos_w2, hy_pos_b2, hy_pos_w3, hy_sin_freq, hy_bias, dn_norm_w):
    B, L, D = x.shape
    T = B * L
    x2d = x.reshape(T, D)
    x0, s, q, k, v, z, dn_gates = _in_proj(x2d, L, w["norm_mix"], *w["in_proj"])
    seq = lambda a: a.reshape(B, L, a.shape[-1])
    h_f, h_b = _hyena_filter(L, hy_pos_w1, hy_pos_b1, hy_pos_w2, hy_pos_b2, hy_pos_w3, hy_sin_freq)
    y_hy = _hyena_conv(seq(s), seq(x0), h_f, h_b, hy_bias)
    o_f, o_b = _delta_scan(seq(q), seq(k), seq(v), seq(dn_gates))
    x1, h_ffn, logits = _out_proj(x2d, y_hy.reshape(T, HY_CH), o_f.reshape(T, DN_W), o_b.reshape(T, DN_W),
                                 z, dn_norm_w, w["out_hy"], w["out_dn"], w["norm_ffn"],
                                 w["router_hi"], w["router_lo"], w["router_b"])
    yb, dest, gates = _moe(h_ffn, logits[:, :N_EXPERTS], w["wg"], w["wl"], w["bg"], w["bl"], w["wd"],
                           w["bd"])
    return _combine_final(yb, dest, gates, x1, w["norm_final"]).reshape(B, L, D)


def kernel(x_prompt, x_sample, norm_mix_w, w_in, hy_conv_w, hy_conv_b, hy_pos_w1, hy_pos_b1, hy_pos_w2, hy_pos_b2, hy_pos_w3, hy_sin_freq, hy_bias, dn_conv_w, dn_a_log, dn_dt_bias, dn_norm_w, w_out, norm_ffn_w, w_router, b_router, w_gate_up, b_gate_up, w_down, b_down, norm_final_w):
    w_out16 = w_out[0].astype(BF16)
    wr = jnp.pad(w_router[0], ((0, 0), (0, LANES - N_EXPERTS)))
    wr_hi = wr.astype(BF16)
    wg, wl = _deinterleave(w_gate_up[0])
    w = {
        "norm_mix": norm_mix_w[0],
        "in_proj": _in_proj_params(w_in[0], hy_conv_w[0], hy_conv_b[0], dn_conv_w[0], dn_a_log[0],
                                   dn_dt_bias[0]),
        "out_hy": w_out16[:HY_CH],
        "out_dn": w_out16[HY_CH:],
        "norm_ffn": norm_ffn_w[0],
        "router_hi": wr_hi,
        "router_lo": (wr - wr_hi.astype(F32)).astype(BF16),
        "router_b": jnp.pad(b_router[0], (0, LANES - N_EXPERTS)).reshape(1, LANES),
        "wg": wg,
        "wl": wl,
        "bg": b_gate_up[0][:, 0::2].reshape(N_EXPERTS, 1, D_FF),
        "bl": b_gate_up[0][:, 1::2].reshape(N_EXPERTS, 1, D_FF),
        "wd": w_down[0].astype(BF16),
        "bd": b_down[0].reshape(N_EXPERTS, 1, D_MODEL),
        "norm_final": norm_final_w,
    }
    mix = (hy_pos_w1[0], hy_pos_b1[0], hy_pos_w2[0], hy_pos_b2[0], hy_pos_w3[0], hy_sin_freq[0],
           hy_bias[0], dn_norm_w[0])
    return (_trunk(x_prompt, w, *mix), _trunk(x_sample, w, *mix))
```

```python
import functools
import math

import jax
import jax.numpy as jnp
import numpy as np
from jax import lax
from jax.experimental import pallas as pl
from jax.experimental.pallas import tpu as pltpu
from jax.experimental.pallas import tpu_sc as plsc

D_MODEL = 1024
HY_CH = 512
DN_HEADS = 4
DN_DK = 128
DN_DV = 128
DN_QK = DN_HEADS * DN_DK
DN_W = DN_HEADS * DN_DV
HY_IN = 3 * HY_CH
DN_CONV = 2 * DN_QK + DN_W
N_GATE = 4 * DN_HEADS
SHORT_CONV = 3
CONV_COLS = HY_IN + DN_CONV
REST_COLS = DN_W + 128
HY_EMB = 33
HY_BANDS = (HY_EMB - 1) // 2
HY_DECAY_TARGET = 1e-2
HY_FAST_PCT = 0.3
HY_SLOW_PCT = 1.5
CHUNK = 64
N_EXPERTS = 32
TOP_K = 4
D_FF = D_MODEL
SWIGLU_ALPHA = 1.702
SWIGLU_LIMIT = 7.0
MOE_BLOCK = 512
EPS = 1e-6

LANES = 128
SUB = 8
ROW_TILE = 512
FFT_CB = 256
DN_STEP_CHUNKS = 4
DEINT_COLS = 512
DISPATCH_TILE = 512
COMBINE_TILE = 512
SC_WINDOW = 128
SC_ROW = 256
VMEM_LIMIT = 56 * 1024 * 1024

F32 = jnp.float32
BF16 = jnp.bfloat16

_NT = (((1,), (1,)), ((), ()))


def _rms(x, g):
    return x * lax.rsqrt(jnp.mean(x * x, axis=-1, keepdims=True) + EPS) * g


def _silu(x):
    return x * jax.nn.sigmoid(x)


def _head_l2norm(x):
    parts = []
    for hd in range(DN_HEADS):
        xh = x[:, hd * DN_DK:(hd + 1) * DN_DK]
        parts.append(xh * lax.rsqrt(jnp.sum(xh * xh, axis=-1, keepdims=True) + EPS))
    return jnp.concatenate(parts, axis=1)


def _in_proj_kernel(xp_ref, x_ref, xn_ref, g_ref, wc_ref, wr_ref, cw_ref, cb_ref, gt_ref,
                    x0_ref, s_ref, q_ref, k_ref, v_ref, z_ref, gate_ref, p_scr, *, tiles_per_seq):
    i = pl.program_id(0)
    first = (i % tiles_per_seq) == 0
    last = (i % tiles_per_seq) == tiles_per_seq - 1
    g = g_ref[...]
    hp = jnp.where(first, 0.0, _rms(xp_ref[...], g))
    hn = jnp.where(last, 0.0, _rms(xn_ref[...], g))
    h_all = jnp.concatenate([hp, _rms(x_ref[...], g), hn], axis=0).astype(BF16)
    h = h_all[SUB:SUB + ROW_TILE]

    def project(c0):
        cols = slice(c0, c0 + HY_CH)
        p_scr[:, cols] = jnp.dot(h_all, wc_ref[:, cols], preferred_element_type=F32)

    def conv(c0):
        cols = slice(c0, c0 + HY_CH)
        return (p_scr[pl.ds(SUB - 1, ROW_TILE), cols] * cw_ref[0:1, cols]
                + p_scr[pl.ds(SUB, ROW_TILE), cols] * cw_ref[1:2, cols]
                + p_scr[pl.ds(SUB + 1, ROW_TILE), cols] * cw_ref[2:3, cols])

    project(0)
    project(HY_CH)
    x0_ref[...] = conv(0) + cb_ref[:, 0:HY_CH]
    project(2 * HY_CH)
    project(HY_IN)
    s_ref[...] = (conv(HY_CH) + cb_ref[:, HY_CH:2 * HY_CH]) * (conv(2 * HY_CH) + cb_ref[:, 2 * HY_CH:])
    project(HY_IN + DN_QK)
    q_ref[...] = _head_l2norm(_silu(conv(HY_IN))) * (DN_DK ** -0.5)
    project(HY_IN + 2 * DN_QK)
    k_ref[...] = _head_l2norm(_silu(conv(HY_IN + DN_QK)))
    rest = jnp.dot(h, wr_ref[...], preferred_element_type=F32)
    v_ref[...] = _silu(conv(HY_IN + 2 * DN_QK))
    z_ref[...] = rest[:, :DN_W]
    a = rest[:, DN_W:] + gt_ref[1:2]
    softplus = jnp.maximum(a, 0.0) + jnp.log(1.0 + jnp.exp(-jnp.abs(a)))
    gate_ref[...] = jnp.where(gt_ref[2:3] > 0.5, -gt_ref[0:1] * softplus, jax.nn.sigmoid(rest[:, DN_W:]))


def _in_proj(x2d, seq_len, g, w_conv, w_rest, conv_w, conv_b, gate_tab):
    T = x2d.shape[0]
    per = ROW_TILE // SUB
    last_sub = T // SUB - 1
    const = lambda i: (0, 0)
    row = lambda i: (i, 0)
    o512 = pl.BlockSpec((ROW_TILE, HY_CH), row)
    s512 = jax.ShapeDtypeStruct((T, HY_CH), F32)
    return pl.pallas_call(
        functools.partial(_in_proj_kernel, tiles_per_seq=seq_len // ROW_TILE),
        grid=(T // ROW_TILE,),
        in_specs=[pl.BlockSpec((SUB, D_MODEL), lambda i: (jnp.maximum(i * per - 1, 0), 0)),
                  pl.BlockSpec((ROW_TILE, D_MODEL), row),
                  pl.BlockSpec((SUB, D_MODEL), lambda i: (jnp.minimum((i + 1) * per, last_sub), 0)),
                  pl.BlockSpec((1, D_MODEL), const),
                  pl.BlockSpec((D_MODEL, CONV_COLS), const),
                  pl.BlockSpec((D_MODEL, REST_COLS), const),
                  pl.BlockSpec((SHORT_CONV, CONV_COLS), const),
                  pl.BlockSpec((1, HY_IN), const),
                  pl.BlockSpec((3, LANES), const)],
        out_specs=[o512, o512, o512, o512, o512, o512, pl.BlockSpec((ROW_TILE, LANES), row)],
        out_shape=[s512, s512, s512, s512, s512, s512, jax.ShapeDtypeStruct((T, LANES), F32)],
        scratch_shapes=[pltpu.VMEM((ROW_TILE + 2 * SUB, CONV_COLS), F32)],
        compiler_params=pltpu.CompilerParams(dimension_semantics=("parallel",),
                                             vmem_limit_bytes=VMEM_LIMIT),
        name="in_proj",
    )(x2d, x2d, x2d, g.reshape(1, D_MODEL), w_conv, w_rest, conv_w, conv_b, gate_tab)


def _in_proj_params(w_in, hy_conv_w, hy_conv_b, dn_conv_w, dn_a_log, dn_dt_bias):
    H = DN_HEADS
    w16 = w_in.astype(BF16)
    gc = w16[:, CONV_COLS + DN_W:]
    gc = jnp.concatenate([gc[:, 0:H], gc[:, 2 * H:3 * H], gc[:, H:2 * H], gc[:, 3 * H:]], axis=1)
    w_rest = jnp.concatenate([w16[:, CONV_COLS:CONV_COLS + DN_W],
                              jnp.pad(gc, ((0, 0), (0, LANES - N_GATE)))], axis=1)
    zero, one, pad = jnp.zeros((H,), F32), jnp.ones((H,), F32), jnp.zeros((LANES - N_GATE,), F32)
    gate_tab = jnp.stack([jnp.concatenate([jnp.exp(dn_a_log[0]), zero, jnp.exp(dn_a_log[1]), zero, pad]),
                          jnp.concatenate([dn_dt_bias[0], zero, dn_dt_bias[1], zero, pad]),
                          jnp.concatenate([one, zero, one, zero, pad])])
    return (w16[:, :CONV_COLS], w_rest, jnp.concatenate([hy_conv_w, dn_conv_w], axis=1),
            hy_conv_b.reshape(1, HY_IN), gate_tab)


def _out_proj_kernel(x_ref, yh_ref, of_ref, ob_ref, z_ref, nw_ref, wh_ref, wd_ref, g_ref, wrh_ref,
                     wrl_ref, br_ref, x1_ref, h_ref, lg_ref):
    o = of_ref[...] + ob_ref[...]
    heads = []
    for hd in range(DN_HEADS):
        oh = o[:, hd * DN_DV:(hd + 1) * DN_DV]
        heads.append(oh * lax.rsqrt(jnp.mean(oh * oh, axis=-1, keepdims=True) + EPS))
    z = z_ref[...]
    y_dn = jnp.concatenate(heads, axis=1) * nw_ref[...] * (z * jax.nn.sigmoid(z))
    x1 = (x_ref[...]
          + jnp.dot(yh_ref[...].astype(BF16), wh_ref[...], preferred_element_type=F32)
          + jnp.dot(y_dn.astype(BF16), wd_ref[...], preferred_element_type=F32))
    x1_ref[...] = x1
    h = _rms(x1, g_ref[...])
    h_hi = h.astype(BF16)
    h_lo = (h - h_hi.astype(F32)).astype(BF16)
    h_ref[...] = h
    lg_ref[...] = (jnp.dot(h_hi, wrh_ref[...], preferred_element_type=F32)
                   + jnp.dot(h_lo, wrh_ref[...], preferred_element_type=F32)
                   + jnp.dot(h_hi, wrl_ref[...], preferred_element_type=F32)
                   + br_ref[...])


def _out_proj(x2d, y_hy, o_f, o_b, z, dn_norm_w, w_oh, w_od, g, wr_hi, wr_lo, br):
    T = x2d.shape[0]
    const = lambda i: (0, 0)
    row = lambda i: (i, 0)
    return pl.pallas_call(
        _out_proj_kernel,
        grid=(T // ROW_TILE,),
        in_specs=[pl.BlockSpec((ROW_TILE, D_MODEL), row),
                  pl.BlockSpec((ROW_TILE, HY_CH), row),
                  pl.BlockSpec((ROW_TILE, DN_W), row),
                  pl.BlockSpec((ROW_TILE, DN_W), row),
                  pl.BlockSpec((ROW_TILE, DN_W), row),
                  pl.BlockSpec((1, DN_W), const),
                  pl.BlockSpec((HY_CH, D_MODEL), const),
                  pl.BlockSpec((DN_W, D_MODEL), const),
                  pl.BlockSpec((1, D_MODEL), const),
                  pl.BlockSpec((D_MODEL, LANES), const),
                  pl.BlockSpec((D_MODEL, LANES), const),
                  pl.BlockSpec((1, LANES), const)],
        out_specs=[pl.BlockSpec((ROW_TILE, D_MODEL), row),
                   pl.BlockSpec((ROW_TILE, D_MODEL), row),
                   pl.BlockSpec((ROW_TILE, LANES), row)],
        out_shape=[jax.ShapeDtypeStruct((T, D_MODEL), F32),
                   jax.ShapeDtypeStruct((T, D_MODEL), F32),
                   jax.ShapeDtypeStruct((T, LANES), F32)],
        compiler_params=pltpu.CompilerParams(dimension_semantics=("parallel",),
                                             vmem_limit_bytes=VMEM_LIMIT),
        name="out_proj_router",
    )(x2d, y_hy, o_f, o_b, z, jnp.tile(dn_norm_w, DN_HEADS).reshape(1, DN_W), w_oh, w_od,
      g.reshape(1, D_MODEL), wr_hi, wr_lo, br)


def _deint_kernel(w_ref, p_ref, og_ref, ol_ref):
    half = DEINT_COLS // 2
    sel = jnp.dot(w_ref[0].astype(BF16), p_ref[...], preferred_element_type=F32)
    og_ref[0] = sel[:, :half].astype(BF16)
    ol_ref[0] = sel[:, half:].astype(BF16)


def _deinterleave(w_gate_up):
    half = DEINT_COLS // 2
    r = np.arange(DEINT_COLS)[:, None]
    c = np.arange(DEINT_COLS)[None, :]
    perm = jnp.asarray(np.where(c < half, r == 2 * c, r == 2 * (c - half) + 1), BF16)
    out = jax.ShapeDtypeStruct((N_EXPERTS, D_MODEL, D_FF), BF16)
    return pl.pallas_call(
        _deint_kernel,
        grid=(N_EXPERTS, 2 * D_FF // DEINT_COLS),
        in_specs=[pl.BlockSpec((1, D_MODEL, DEINT_COLS), lambda e, j: (e, 0, j)),
                  pl.BlockSpec((DEINT_COLS, DEINT_COLS), lambda e, j: (0, 0))],
        out_specs=[pl.BlockSpec((1, D_MODEL, half), lambda e, j: (e, 0, j)),
                   pl.BlockSpec((1, D_MODEL, half), lambda e, j: (e, 0, j))],
        out_shape=[out, out],
        compiler_params=pltpu.CompilerParams(dimension_semantics=("parallel", "parallel")),
        name="deinterleave_gate_up",
    )(w_gate_up, perm)


def _dispatch_kernel(tail_ref, dest_ref, h_ref, xb_ref, zero_scr, sem):
    n = h_ref.shape[0]

    @pl.when(pl.program_id(0) == 0)
    def _():
        zero_scr[...] = jnp.zeros_like(zero_scr)
        for e in range(N_EXPERTS):
            t0 = pl.multiple_of(tail_ref[0, e], SUB)
            pltpu.make_async_copy(zero_scr, xb_ref.at[pl.ds(t0, MOE_BLOCK), :], sem).start()
        for e in range(N_EXPERTS):
            pltpu.make_async_copy(zero_scr, xb_ref.at[pl.ds(0, MOE_BLOCK), :], sem).wait()

    def issue(r, c):
        for k in range(TOP_K):
            d = dest_ref[0, r * TOP_K + k]
            pltpu.make_async_copy(h_ref.at[pl.ds(r, 1), :], xb_ref.at[pl.ds(d, 1), :],
                                  sem).start(priority=k % 2)
        return c

    lax.fori_loop(0, n, issue, 0, unroll=8)

    def drain(r, c):
        for k in range(TOP_K):
            pltpu.make_async_copy(h_ref.at[pl.ds(0, 1), :], xb_ref.at[pl.ds(0, 1), :], sem).wait()
        return c

    lax.fori_loop(0, n, drain, 0, unroll=8)


def _dispatch(h, dest, tail_start, n_rows):
    T = h.shape[0]
    tm = DISPATCH_TILE
    return pl.pallas_call(
        _dispatch_kernel,
        grid=(T // tm,),
        in_specs=[pl.BlockSpec((1, N_EXPERTS), lambda i: (0, 0), memory_space=pltpu.SMEM),
                  pl.BlockSpec((None, 1, tm * TOP_K), lambda i: (i, 0, 0), memory_space=pltpu.SMEM),
                  pl.BlockSpec((tm, D_MODEL), lambda i: (i, 0))],
        out_specs=pl.BlockSpec(memory_space=pl.ANY),
        out_shape=jax.ShapeDtypeStruct((n_rows, D_MODEL), F32),
        scratch_shapes=[pltpu.VMEM((MOE_BLOCK, D_MODEL), F32), pltpu.SemaphoreType.DMA],
        compiler_params=pltpu.CompilerParams(dimension_semantics=("arbitrary",)),
        name="moe_dispatch",
    )(tail_start.reshape(1, N_EXPERTS), dest.reshape(T // tm, 1, tm * TOP_K), h)


def _expert_kernel(be_ref, nb_ref, xb_ref, wg_ref, wl_ref, bg_ref, bl_ref, wd_ref, bd_ref, y_ref):
    i = pl.program_id(0)

    @pl.when(i < nb_ref[0])
    def _():
        xb = xb_ref[...].astype(BF16)
        hg = jnp.dot(xb, wg_ref[0], preferred_element_type=F32) + bg_ref[0]
        hl = jnp.dot(xb, wl_ref[0], preferred_element_type=F32) + bl_ref[0]
        x_glu = jnp.minimum(hg, SWIGLU_LIMIT)
        x_lin = jnp.clip(hl, -SWIGLU_LIMIT, SWIGLU_LIMIT)
        act = x_glu * jax.nn.sigmoid(SWIGLU_ALPHA * x_glu) * (x_lin + 1.0)
        y = jnp.dot(act.astype(BF16), wd_ref[0], preferred_element_type=F32) + bd_ref[0]
        y_ref[...] = y

    @pl.when(i >= nb_ref[0])
    def _():
        y_ref[...] = jnp.zeros_like(y_ref)


def _expert_mlp(xb, block_e, n_used, wg, wl, bg, bl, wd, bd):
    n_rows = xb.shape[0]
    n_blocks = n_rows // MOE_BLOCK
    rowm = lambda i, be, nb: (i, 0)
    exp3 = lambda i, be, nb: (be[i], 0, 0)
    grid_spec = pltpu.PrefetchScalarGridSpec(
        num_scalar_prefetch=2,
        grid=(n_blocks,),
        in_specs=[pl.BlockSpec((MOE_BLOCK, D_MODEL), rowm),
                  pl.BlockSpec((1, D_MODEL, D_FF), exp3),
                  pl.BlockSpec((1, D_MODEL, D_FF), exp3),
                  pl.BlockSpec((1, 1, D_FF), exp3),
                  pl.BlockSpec((1, 1, D_FF), exp3),
                  pl.BlockSpec((1, D_FF, D_MODEL), exp3),
                  pl.BlockSpec((1, 1, D_MODEL), exp3)],
        out_specs=pl.BlockSpec((MOE_BLOCK, D_MODEL), rowm),
    )
    return pl.pallas_call(
        _expert_kernel,
        grid_spec=grid_spec,
        out_shape=jax.ShapeDtypeStruct((n_rows, D_MODEL), F32),
        compiler_params=pltpu.CompilerParams(dimension_semantics=("arbitrary",),
                                             vmem_limit_bytes=VMEM_LIMIT),
        name="expert_mlp",
    )(block_e, n_used, xb, wg, wl, bg, bl, wd, bd)


def _moe(h, logits, wg, wl, bg, bl, wd, bd):
    T = h.shape[0]
    TK = T * TOP_K
    top_vals, top_idx = lax.top_k(logits, TOP_K)
    gates = jax.nn.softmax(top_vals, axis=-1)
    sel = jnp.sum(jax.nn.one_hot(top_idx, N_EXPERTS, dtype=jnp.int32), axis=1)
    before = jnp.cumsum(sel, axis=0) - sel
    counts = jnp.sum(sel, axis=0)
    padded = (counts + MOE_BLOCK - 1) // MOE_BLOCK * MOE_BLOCK
    pad_end = jnp.cumsum(padded)
    pad_start = pad_end - padded
    rank = jnp.take_along_axis(before, top_idx, axis=1)
    dest = (pad_start[top_idx] + rank).astype(jnp.int32)
    n_blocks = (TK + MOE_BLOCK - 1) // MOE_BLOCK + N_EXPERTS
    n_rows = n_blocks * MOE_BLOCK
    block_start = jnp.arange(n_blocks, dtype=jnp.int32) * MOE_BLOCK
    block_e = jnp.minimum(jnp.sum((block_start[:, None] >= pad_end[None, :]).astype(jnp.int32), axis=1),
                          N_EXPERTS - 1)
    n_used = (pad_end[-1] // MOE_BLOCK).astype(jnp.int32).reshape(1)
    tail_start = jnp.minimum((pad_start + counts) // SUB * SUB, n_rows - MOE_BLOCK).astype(jnp.int32)
    xb = _dispatch(h, dest, tail_start, n_rows)
    yb = _expert_mlp(xb, block_e, n_used, wg, wl, bg, bl, wd, bd)
    return yb, dest, gates


def _sc_gather(x, indices):
    n = indices.shape[0]
    width = x.shape[1]
    mesh = plsc.VectorSubcoreMesh(core_axis_name="core", subcore_axis_name="subcore")

    @functools.partial(pl.kernel, out_type=jax.ShapeDtypeStruct((n, width), x.dtype), mesh=mesh)
    def gather(x_hbm, i_hbm, o_hbm):
        def body(i_vmem, o_vmem):
            pltpu.sync_copy(x_hbm.at[i_vmem.at[0]], o_vmem)

        pltpu.emit_pipeline(
            body,
            grid=(n // SC_WINDOW,),
            in_specs=[pl.BlockSpec((1, SC_WINDOW), index_map=lambda i: (0, i))],
            out_specs=[pl.BlockSpec((SC_WINDOW, width), index_map=lambda i: (i, 0))],
            core_axis_name=("core", "subcore"),
            dimension_semantics=(pltpu.PARALLEL,),
        )(i_hbm, o_hbm)

    return gather(x, indices.reshape(1, n))


def _combine_kernel(gate_ref, x1_ref, g_ref, y4_ref, o_ref):
    x = x1_ref[...]
    for k in range(TOP_K):
        x = x + gate_ref[:, k:k + 1] * y4_ref[:, k * D_MODEL:(k + 1) * D_MODEL]
    o_ref[...] = _rms(x, g_ref[...])


def _combine_final(yb, dest, gates, x1, g):
    T = x1.shape[0]
    tm = COMBINE_TILE
    pieces = D_MODEL // SC_ROW
    idx = (dest.reshape(T * TOP_K, 1) * pieces + jnp.arange(pieces, dtype=jnp.int32)[None, :]).reshape(-1)
    y4 = _sc_gather(yb.reshape(-1, SC_ROW), idx).reshape(T, TOP_K * D_MODEL)
    row = lambda i: (i, 0)
    return pl.pallas_call(
        _combine_kernel,
        grid=(T // tm,),
        in_specs=[pl.BlockSpec((tm, TOP_K), row),
                  pl.BlockSpec((tm, D_MODEL), row),
                  pl.BlockSpec((1, D_MODEL), lambda i: (0, 0)),
                  pl.BlockSpec((tm, TOP_K * D_MODEL), row)],
        out_specs=pl.BlockSpec((tm, D_MODEL), row),
        out_shape=jax.ShapeDtypeStruct((T, D_MODEL), F32),
        compiler_params=pltpu.CompilerParams(dimension_semantics=("parallel",),
                                             vmem_limit_bytes=VMEM_LIMIT),
        name="moe_combine_final",
    )(gates, x1, g.reshape(1, D_MODEL), y4)


def _fft_tables(nc, nf):
    n = nc * nf
    kc = np.arange(nc, dtype=np.float64)
    a1 = 2.0 * np.pi * np.outer(kc, np.arange(nc // 2)) / nc
    c1, s1 = np.cos(a1), np.sin(a1)
    l1 = np.block([[c1, s1], [-s1, c1]])
    a2 = 2.0 * np.pi * np.outer(np.arange(nf), np.arange(nf)) / nf
    c2, s2 = np.cos(a2), np.sin(a2)
    m2 = np.block([[c2, s2], [-s2, c2]])
    m2i = np.block([[c2, -s2], [s2, c2]])
    a3 = 2.0 * np.pi * np.outer(np.arange(nc // 2), kc) / nc
    c3, s3 = np.cos(a3), np.sin(a3)
    l3 = np.block([[c3, -s3], [s3, c3]]) / n
    return tuple(jnp.asarray(m, F32).astype(BF16) for m in (l1, m2, m2i, l3))


def _twiddle(nc, nf):
    n = nc * nf
    ph = (jnp.arange(nf, dtype=jnp.int32)[:, None] * jnp.arange(nc, dtype=jnp.int32)[None, :]) % n
    ang = ph.astype(F32) * (2.0 * math.pi / n)
    tw = jnp.stack([jnp.cos(ang), jnp.sin(ang)], axis=1)
    return jnp.broadcast_to(tw[..., None], (nf, 2, nc, LANES))


def _lane_tile(t, width):
    return t if width == LANES else jnp.concatenate([t] * (width // LANES), axis=1)


def _fft1_kernel(z_ref, l1_ref, tw_ref, o_ref, *, nc):
    cb = o_ref.shape[-1]
    for j in range(SUB):
        rhs = jnp.concatenate([z_ref[0, :, j, :], z_ref[1, :, j, :]], axis=0).astype(BF16)
        a = jnp.dot(l1_ref[...], rhs, preferred_element_type=F32)
        ar, ai = a[:nc], a[nc:]
        twr = _lane_tile(tw_ref[j, 0], cb)
        twi = _lane_tile(tw_ref[j, 1], cb)
        o_ref[j, 0] = ar * twr + ai * twi
        o_ref[j, 1] = ai * twr - ar * twi


def _fft1(z, l1, tw, nc, nf):
    P, C = z.shape[0], z.shape[-1]
    cb = FFT_CB
    z_spec = pl.BlockSpec((None, 2, nc // 2, SUB, cb), lambda p, j, c: (p, 0, 0, j, c))
    return pl.pallas_call(
        functools.partial(_fft1_kernel, nc=nc),
        grid=(P, nf // SUB, C // cb),
        in_specs=[z_spec,
                  pl.BlockSpec(l1.shape, lambda p, j, c: (0, 0)),
                  pl.BlockSpec((SUB, 2, nc, LANES), lambda p, j, c: (j, 0, 0, 0))],
        out_specs=pl.BlockSpec((None, SUB, 2, nc, cb), lambda p, j, c: (p, j, 0, 0, c)),
        out_shape=jax.ShapeDtypeStruct((P, nf, 2, nc, C), F32),
        compiler_params=pltpu.CompilerParams(dimension_semantics=("parallel", "parallel", "parallel"),
                                             vmem_limit_bytes=VMEM_LIMIT),
        name="hyena_fft_stage1",
    )(z, l1, tw)


def _fft2_filter_kernel(a_ref, m2_ref, k_ref, *, nf):
    for j in range(SUB):
        rhs = jnp.concatenate([a_ref[:, 0, j, :], a_ref[:, 1, j, :]], axis=0).astype(BF16)
        x = jnp.dot(m2_ref[...], rhs, preferred_element_type=F32)
        k_ref[j, 0] = x[:nf]
        k_ref[j, 1] = x[nf:]


def _fft2_filter(a, m2, nc, nf):
    C = a.shape[-1]
    cb = FFT_CB
    return pl.pallas_call(
        functools.partial(_fft2_filter_kernel, nf=nf),
        grid=(nc // SUB, C // cb),
        in_specs=[pl.BlockSpec((None, nf, 2, SUB, cb), lambda k, c: (0, 0, 0, k, c)),
                  pl.BlockSpec(m2.shape, lambda k, c: (0, 0))],
        out_specs=pl.BlockSpec((SUB, 2, nf, cb), lambda k, c: (k, 0, 0, c)),
        out_shape=jax.ShapeDtypeStruct((nc, 2, nf, C), F32),
        compiler_params=pltpu.CompilerParams(dimension_semantics=("parallel", "parallel"),
                                             vmem_limit_bytes=VMEM_LIMIT),
        name="hyena_filter_spectrum",
    )(a, m2)


def _fft2_kernel(a_ref, k_ref, m2_ref, m2i_ref, o_ref, *, nf):
    for j in range(SUB):
        rhs = jnp.concatenate([a_ref[:, 0, j, :], a_ref[:, 1, j, :]], axis=0).astype(BF16)
        x = jnp.dot(m2_ref[...], rhs, preferred_element_type=F32)
        xr, xi = x[:nf], x[nf:]
        kr, ki = k_ref[j, 0], k_ref[j, 1]
        y = jnp.concatenate([xr * kr - xi * ki, xr * ki + xi * kr], axis=0).astype(BF16)
        b = jnp.dot(m2i_ref[...], y, preferred_element_type=F32)
        o_ref[:, 0, j, :] = b[:nf]
        o_ref[:, 1, j, :] = b[nf:]


def _fft2(a, kspec, m2, m2i, nc, nf):
    P, C = a.shape[0], a.shape[-1]
    cb = FFT_CB
    blk = pl.BlockSpec((None, nf, 2, SUB, cb), lambda p, k, c: (p, 0, 0, k, c))
    return pl.pallas_call(
        functools.partial(_fft2_kernel, nf=nf),
        grid=(P, nc // SUB, C // cb),
        in_specs=[blk,
                  pl.BlockSpec((SUB, 2, nf, cb), lambda p, k, c: (k, 0, 0, c)),
                  pl.BlockSpec(m2.shape, lambda p, k, c: (0, 0)),
                  pl.BlockSpec(m2i.shape, lambda p, k, c: (0, 0))],
        out_specs=blk,
        out_shape=jax.ShapeDtypeStruct(a.shape, F32),
        compiler_params=pltpu.CompilerParams(dimension_semantics=("parallel", "parallel", "parallel"),
                                             vmem_limit_bytes=VMEM_LIMIT),
        name="hyena_fft_stage2",
    )(a, kspec, m2, m2i)


def _fft3_kernel(b_ref, tw_ref, l3_ref, s_ref, x0_ref, bias_ref, o_ref, *, nc):
    cb = o_ref.shape[-1]
    half = nc // 2
    for j in range(SUB):
        br, bi = b_ref[j, 0], b_ref[j, 1]
        twr = _lane_tile(tw_ref[j, 0], cb)
        twi = _lane_tile(tw_ref[j, 1], cb)
        rhs = jnp.concatenate([br * twr - bi * twi, br * twi + bi * twr], axis=0).astype(BF16)
        y = jnp.dot(l3_ref[...], rhs, preferred_element_type=F32)
        for q in range(2):
            s = s_ref[q, :, j, :]
            o_ref[q, :, j, :] = x0_ref[q, :, j, :] * (y[q * half:(q + 1) * half] + s * bias_ref[...])


def _fft3(b, tw, l3, s5, x05, bias, nc, nf):
    P, C = b.shape[0], b.shape[-1]
    cb = FFT_CB
    seq = pl.BlockSpec((None, 2, nc // 2, SUB, cb), lambda p, j, c: (p, 0, 0, j, c))
    return pl.pallas_call(
        functools.partial(_fft3_kernel, nc=nc),
        grid=(P, nf // SUB, C // cb),
        in_specs=[pl.BlockSpec((None, SUB, 2, nc, cb), lambda p, j, c: (p, j, 0, 0, c)),
                  pl.BlockSpec((SUB, 2, nc, LANES), lambda p, j, c: (j, 0, 0, 0)),
                  pl.BlockSpec(l3.shape, lambda p, j, c: (0, 0)),
                  seq, seq,
                  pl.BlockSpec((1, cb), lambda p, j, c: (0, c))],
        out_specs=seq,
        out_shape=jax.ShapeDtypeStruct(s5.shape, F32),
        compiler_params=pltpu.CompilerParams(dimension_semantics=("parallel", "parallel", "parallel"),
                                             vmem_limit_bytes=VMEM_LIMIT),
        name="hyena_fft_stage3",
    )(b, tw, l3, s5, x05, bias)


def _fft_split(n):
    nf = 1 << (int(math.log2(n)) // 2)
    return n // nf, nf


def _filter_tables(nc):
    half = nc // 2
    kc = np.arange(nc, dtype=np.float64)[:, None]
    r = np.arange(half, dtype=np.float64)
    mats = []
    for rows_b, drop0 in ((half + (half - 1 - r), False), (nc - r, True)):
        a = 2.0 * np.pi * kc * np.concatenate([r, rows_b])[None, :] / nc
        m = np.concatenate([np.cos(a), -np.sin(a)], axis=0)
        if drop0:
            m[:, half] = 0.0
        mats.append(m)
    return tuple(jnp.asarray(m, F32).astype(BF16) for m in mats)


def _fft1_filter_kernel(hf_ref, hba_ref, hbb_ref, mb_ref, mb0_ref, tw_ref, o_ref, *, nc):
    cb = o_ref.shape[-1]
    m_first = jnp.where(pl.program_id(0) == 0, mb0_ref[...], mb_ref[...])
    for j in range(SUB):
        src = hbb_ref[:, 0, :] if j == 0 else hba_ref[:, SUB - j, :]
        rhs = jnp.concatenate([hf_ref[:, j, :], src], axis=0).astype(BF16)
        a = jnp.dot(m_first if j == 0 else mb_ref[...], rhs, preferred_element_type=F32)
        ar, ai = a[:nc], a[nc:]
        twr = _lane_tile(tw_ref[j, 0], cb)
        twi = _lane_tile(tw_ref[j, 1], cb)
        o_ref[j, 0] = ar * twr + ai * twi
        o_ref[j, 1] = ai * twr - ar * twi


def _fft1_filter(h_f, h_b, mb, mb0, tw, nc, nf):
    C = h_f.shape[-1]
    cb = FFT_CB
    nblk = nf // SUB
    hf3 = h_f.reshape(nc // 2, nf, C)
    hb3 = h_b.reshape(nc // 2, nf, C)
    blk = lambda f: pl.BlockSpec((nc // 2, SUB, cb), f)
    return pl.pallas_call(
        functools.partial(_fft1_filter_kernel, nc=nc),
        grid=(nblk, C // cb),
        in_specs=[blk(lambda j, c: (0, j, c)),
                  blk(lambda j, c: (0, nblk - 1 - j, c)),
                  blk(lambda j, c: (0, (nblk - j) % nblk, c)),
                  pl.BlockSpec(mb.shape, lambda j, c: (0, 0)),
                  pl.BlockSpec(mb0.shape, lambda j, c: (0, 0)),
                  pl.BlockSpec((SUB, 2, nc, LANES), lambda j, c: (j, 0, 0, 0))],
        out_specs=pl.BlockSpec((None, SUB, 2, nc, cb), lambda j, c: (0, j, 0, 0, c)),
        out_shape=jax.ShapeDtypeStruct((1, nf, 2, nc, C), F32),
        compiler_params=pltpu.CompilerParams(dimension_semantics=("parallel", "parallel"),
                                             vmem_limit_bytes=VMEM_LIMIT),
        name="hyena_filter_stage1",
    )(hf3, hb3, hb3, mb, mb0, tw)


def _hyena_conv(s, x0, h_f, h_b, bias):
    B, L, C = s.shape
    nc, nf = _fft_split(2 * L)
    tw = _twiddle(nc, nf)
    l1, m2, m2i, l3 = _fft_tables(nc, nf)
    mb, mb0 = _filter_tables(nc)
    kspec = _fft2_filter(_fft1_filter(h_f, h_b, mb, mb0, tw, nc, nf), m2, nc, nf)
    s5 = s.reshape(B // 2, 2, nc // 2, nf, C)
    x05 = x0.reshape(B // 2, 2, nc // 2, nf, C)
    a = _fft1(s5, l1, tw, nc, nf)
    b = _fft2(a, kspec, m2, m2i, nc, nf)
    return _fft3(b, tw, l3, s5, x05, bias.reshape(1, C), nc, nf).reshape(B, L, C)


def _hyena_filter(L, w1, b1, w2, b2, w3, freq):
    pos = jnp.arange(L, dtype=F32)
    t = jnp.linspace(0.0, 1.0, L, dtype=F32)[:, None]
    bands = jnp.linspace(1e-4, HY_BANDS - 1, HY_BANDS, dtype=F32)
    ang = (2.0 * math.pi / L) * pos[:, None] * bands[None, :]
    z = jnp.concatenate([t, jnp.cos(ang), -jnp.sin(ang)], axis=-1)
    h = jnp.sin(freq * (z @ w1 + b1))
    h = jnp.sin(freq * (h @ w2 + b2))
    h = h @ w3
    deltas = jnp.abs(jnp.linspace(math.log(HY_DECAY_TARGET) / HY_SLOW_PCT,
                                  math.log(HY_DECAY_TARGET) / HY_FAST_PCT, HY_CH, dtype=F32))
    window = jnp.exp(-t * deltas[None, :])
    h_f = h[:, :HY_CH] * window
    h_b = h[:, HY_CH:] * window
    l1 = jnp.sum(jnp.abs(h_f), axis=0) + jnp.sum(jnp.abs(h_b[1:]), axis=0)
    return h_f / l1, h_b / l1


def _split3(x):
    x1 = x.astype(BF16)
    r = x - x1.astype(F32)
    x2 = r.astype(BF16)
    x3 = (r - x2.astype(F32)).astype(BF16)
    return x1, x2, x3


def _mm(a, b):
    return jnp.dot(a.astype(BF16), b.astype(BF16), preferred_element_type=F32)


def _delta_kernel(qf_ref, kf_ref, vf_ref, gf_ref, qb_ref, kb_ref, vb_ref, gb_ref, of_ref, ob_ref, s_ref,
                  *, n_sub):
    @pl.when(pl.program_id(1) == 0)
    def _():
        s_ref[...] = jnp.zeros_like(s_ref)

    row = lax.broadcasted_iota(jnp.int32, (CHUNK, CHUNK), 0)
    col = lax.broadcasted_iota(jnp.int32, (CHUNK, CHUNK), 1)
    eye_f = jnp.where(row == col, 1.0, 0.0).astype(F32)
    dirs = ((qf_ref, kf_ref, vf_ref, gf_ref, of_ref, row >= col, row > col),
            (qb_ref, kb_ref, vb_ref, gb_ref, ob_ref, row <= col, row < col))

    ch = []
    for d, (q_ref, k_ref, v_ref, g_ref, _, incl, strict) in enumerate(dirs):
        tri = jnp.where(incl, 1.0, 0.0).astype(BF16)
        for c in range(n_sub):
            rows = slice(c * CHUNK, (c + 1) * CHUNK)
            gates = g_ref[rows, :]
            gl = 2 * DN_HEADS * d
            g1, g2, g3 = _split3(gates)
            gc_all = (jnp.dot(tri, g1, preferred_element_type=F32)
                      + jnp.dot(tri, g2, preferred_element_type=F32)
                      + jnp.dot(tri, g3, preferred_element_type=F32))
            gsum_all = jnp.sum(gates, axis=0, keepdims=True)
            for h in range(DN_HEADS):
                lanes = slice(h * DN_DK, (h + 1) * DN_DK)
                kh = k_ref[rows, lanes]
                beta = gates[:, gl + DN_HEADS + h:gl + DN_HEADS + h + 1]
                gc = gc_all[:, gl + h:gl + h + 1]
                g_last = gsum_all[:, gl + h:gl + h + 1]
                gc_b = jnp.broadcast_to(gc, (CHUNK, CHUNK))
                gc_row = jnp.sum(gc_b * eye_f, axis=0, keepdims=True)
                decay = jnp.where(incl, jnp.exp(jnp.minimum(gc_b - gc_row, 0.0)), 0.0)
                e_gc = jnp.exp(gc)
                kb = kh * beta
                ch.append(dict(d=d, c=c, h=h, lanes=lanes, rows=rows, strict=strict, decay=decay,
                               kh16=kh.astype(BF16), kb16=kb.astype(BF16),
                               q16=q_ref[rows, lanes].astype(BF16),
                               rhs=jnp.concatenate([v_ref[rows, lanes] * beta, kb * e_gc],
                                                   axis=1).astype(BF16),
                               qe=q_ref[rows, lanes] * e_gc,
                               kdT=(kh * jnp.exp(g_last - gc)).T.astype(BF16),
                               e_last=jnp.exp(g_last)))
    for x in ch:
        a = lax.dot_general(x["kb16"], x["kh16"], _NT, preferred_element_type=F32) * x["decay"]
        x["a"] = jnp.where(x["strict"], a, 0.0)
        x["qk"] = (lax.dot_general(x["q16"], x["kh16"], _NT, preferred_element_type=F32)
                   * x["decay"]).astype(BF16)
    for x in ch:
        x["t"] = eye_f - x["a"]
        x["p"] = _mm(x["a"], x["a"])
    for level in range(5):
        for x in ch:
            p16 = x["p"].astype(BF16)
            x["t"] = x["t"] + jnp.dot(x["t"].astype(BF16), p16, preferred_element_type=F32)
            if level < 4:
                x["p"] = jnp.dot(p16, p16, preferred_element_type=F32)
    for x in ch:
        uw = jnp.dot(x["t"].astype(BF16), x["rhs"], preferred_element_type=F32)
        x["u"] = uw[:, :DN_DV]
        x["wq"] = jnp.concatenate([uw[:, DN_DV:], x["qe"]], axis=0).astype(BF16)

    for step in range(n_sub):
        cur = [x for x in ch if x["c"] == (step if x["d"] == 0 else n_sub - 1 - step)]
        for x in cur:
            x["s"] = s_ref[x["d"], x["h"]]
            x["ws"] = jnp.dot(x["wq"], x["s"].astype(BF16), preferred_element_type=F32)
        for x in cur:
            x["vn"] = (x["u"] - x["ws"][:CHUNK]).astype(BF16)
        for x in cur:
            o = x["ws"][CHUNK:] + jnp.dot(x["qk"], x["vn"], preferred_element_type=F32)
            dirs[x["d"]][4][x["rows"], x["lanes"]] = o
            s_ref[x["d"], x["h"]] = (x["s"] * x["e_last"]
                                     + jnp.dot(x["kdT"], x["vn"], preferred_element_type=F32))


def _delta_scan(q, k, v, gates):
    B, L, _ = q.shape
    n_sub = DN_STEP_CHUNKS
    rows = n_sub * CHUNK
    nblk = L // rows
    fwd = pl.BlockSpec((None, rows, DN_W), lambda b, j: (b, j, 0))
    bwd = pl.BlockSpec((None, rows, DN_W), lambda b, j: (b, nblk - 1 - j, 0))
    gfwd = pl.BlockSpec((None, rows, LANES), lambda b, j: (b, j, 0))
    gbwd = pl.BlockSpec((None, rows, LANES), lambda b, j: (b, nblk - 1 - j, 0))
    out = jax.ShapeDtypeStruct((B, L, DN_W), F32)
    return pl.pallas_call(
        functools.partial(_delta_kernel, n_sub=n_sub),
        grid=(B, nblk),
        in_specs=[fwd, fwd, fwd, gfwd, bwd, bwd, bwd, gbwd],
        out_specs=[fwd, bwd],
        out_shape=[out, out],
        scratch_shapes=[pltpu.VMEM((2, DN_HEADS, DN_DK, DN_DV), F32)],
        compiler_params=pltpu.CompilerParams(dimension_semantics=("parallel", "arbitrary"),
                                             vmem_limit_bytes=VMEM_LIMIT),
        name="delta_scan",
    )(q, k, v, gates, q, k, v, gates)


def _trunk(x, w, hy_pos_w1, hy_pos_b1, hy_pos_w2, hy_pos_b2, hy_pos_w3, hy_sin_freq, hy_bias, dn_norm_w):
    B, L, D = x.shape
    T = B * L
    x2d = x.reshape(T, D)
    x0, s, q, k, v, z, dn_gates = _in_proj(x2d, L, w["norm_mix"], *w["in_proj"])
    seq = lambda a: a.reshape(B, L, a.shape[-1])
    h_f, h_b = _hyena_filter(L, hy_pos_w1, hy_pos_b1, hy_pos_w2, hy_pos_b2, hy_pos_w3, hy_sin_freq)
    y_hy = _hyena_conv(seq(s), seq(x0), h_f, h_b, hy_bias)
    o_f, o_b = _delta_scan(seq(q), seq(k), seq(v), seq(dn_gates))
    x1, h_ffn, logits = _out_proj(x2d, y_hy.reshape(T, HY_CH), o_f.reshape(T, DN_W), o_b.reshape(T, DN_W),
                                 z, dn_norm_w, w["out_hy"], w["out_dn"], w["norm_ffn"],
                                 w["router_hi"], w["router_lo"], w["router_b"])
    yb, dest, gates = _moe(h_ffn, logits[:, :N_EXPERTS], w["wg"], w["wl"], w["bg"], w["bl"], w["wd"],
                           w["bd"])
    return _combine_final(yb, dest, gates, x1, w["norm_final"]).reshape(B, L, D)


def kernel(x_prompt, x_sample, norm_mix_w, w_in, hy_conv_w, hy_conv_b, hy_pos_w1, hy_pos_b1, hy_pos_w2, hy_pos_b2, hy_pos_w3, hy_sin_freq, hy_bias, dn_conv_w, dn_a_log, dn_dt_bias, dn_norm_w, w_out, norm_ffn_w, w_router, b_router, w_gate_up, b_gate_up, w_down, b_down, norm_final_w):
    w_out16 = w_out[0].astype(BF16)
    wr = jnp.pad(w_router[0], ((0, 0), (0, LANES - N_EXPERTS)))
    wr_hi = wr.astype(BF16)
    wg, wl = _deinterleave(w_gate_up[0])
    w = {
        "norm_mix": norm_mix_w[0],
        "in_proj": _in_proj_params(w_in[0], hy_conv_w[0], hy_conv_b[0], dn_conv_w[0], dn_a_log[0],
                                   dn_dt_bias[0]),
        "out_hy": w_out16[:HY_CH],
        "out_dn": w_out16[HY_CH:],
        "norm_ffn": norm_ffn_w[0],
        "router_hi": wr_hi,
        "router_lo": (wr - wr_hi.astype(F32)).astype(BF16),
        "router_b": jnp.pad(b_router[0], (0, LANES - N_EXPERTS)).reshape(1, LANES),
        "wg": wg,
        "wl": wl,
        "bg": b_gate_up[0][:, 0::2].reshape(N_EXPERTS, 1, D_FF),
        "bl": b_gate_up[0][:, 1::2].reshape(N_EXPERTS, 1, D_FF),
        "wd": w_down[0].astype(BF16),
        "bd": b_down[0].reshape(N_EXPERTS, 1, D_MODEL),
        "norm_final": norm_final_w,
    }
    mix = (hy_pos_w1[0], hy_pos_b1[0], hy_pos_w2[0], hy_pos_b2[0], hy_pos_w3[0], hy_sin_freq[0],
           hy_bias[0], dn_norm_w[0])
    return (_trunk(x_prompt, w, *mix), _trunk(x_sample, w, *mix))
```

```python
import functools
import math

import jax
import jax.numpy as jnp
import numpy as np
from jax import lax
from jax.experimental import pallas as pl
from jax.experimental.pallas import tpu as pltpu
from jax.experimental.pallas import tpu_sc as plsc

D_MODEL = 1024
HY_CH = 512
DN_HEADS = 4
DN_DK = 128
DN_DV = 128
DN_QK = DN_HEADS * DN_DK
DN_W = DN_HEADS * DN_DV
HY_IN = 3 * HY_CH
DN_CONV = 2 * DN_QK + DN_W
N_GATE = 4 * DN_HEADS
SHORT_CONV = 3
CONV_COLS = HY_IN + DN_CONV
REST_COLS = DN_W + 128
HY_EMB = 33
HY_BANDS = (HY_EMB - 1) // 2
HY_DECAY_TARGET = 1e-2
HY_FAST_PCT = 0.3
HY_SLOW_PCT = 1.5
CHUNK = 64
N_EXPERTS = 32
TOP_K = 4
D_FF = D_MODEL
SWIGLU_ALPHA = 1.702
SWIGLU_LIMIT = 7.0
MOE_BLOCK = 512
EPS = 1e-6

LANES = 128
SUB = 8
ROW_TILE = 512
FFT_CB = 256
DN_STEP_CHUNKS = 4
DEINT_COLS = 512
DISPATCH_TILE = 512
COMBINE_TILE = 512
SC_WINDOW = 128
SC_ROW = LANES
VMEM_LIMIT = 56 * 1024 * 1024

F32 = jnp.float32
BF16 = jnp.bfloat16

_NT = (((1,), (1,)), ((), ()))


def _rms(x, g):
    return x * lax.rsqrt(jnp.mean(x * x, axis=-1, keepdims=True) + EPS) * g


def _silu(x):
    return x * jax.nn.sigmoid(x)


def _head_l2norm(x):
    parts = []
    for hd in range(DN_HEADS):
        xh = x[:, hd * DN_DK:(hd + 1) * DN_DK]
        parts.append(xh * lax.rsqrt(jnp.sum(xh * xh, axis=-1, keepdims=True) + EPS))
    return jnp.concatenate(parts, axis=1)


def _in_proj_kernel(xp_ref, x_ref, xn_ref, g_ref, wc_ref, wr_ref, cw_ref, cb_ref, gt_ref,
                    x0_ref, s_ref, q_ref, k_ref, v_ref, z_ref, gate_ref, p_scr, *, tiles_per_seq):
    i = pl.program_id(0)
    first = (i % tiles_per_seq) == 0
    last = (i % tiles_per_seq) == tiles_per_seq - 1
    g = g_ref[...]
    hp = jnp.where(first, 0.0, _rms(xp_ref[...], g))
    hn = jnp.where(last, 0.0, _rms(xn_ref[...], g))
    h_all = jnp.concatenate([hp, _rms(x_ref[...], g), hn], axis=0).astype(BF16)
    h = h_all[SUB:SUB + ROW_TILE]

    def project(c0):
        cols = slice(c0, c0 + HY_CH)
        p_scr[:, cols] = jnp.dot(h_all, wc_ref[:, cols], preferred_element_type=F32)

    def conv(c0):
        cols = slice(c0, c0 + HY_CH)
        return (p_scr[pl.ds(SUB - 1, ROW_TILE), cols] * cw_ref[0:1, cols]
                + p_scr[pl.ds(SUB, ROW_TILE), cols] * cw_ref[1:2, cols]
                + p_scr[pl.ds(SUB + 1, ROW_TILE), cols] * cw_ref[2:3, cols])

    project(0)
    project(HY_CH)
    x0_ref[...] = conv(0) + cb_ref[:, 0:HY_CH]
    project(2 * HY_CH)
    project(HY_IN)
    s_ref[...] = (conv(HY_CH) + cb_ref[:, HY_CH:2 * HY_CH]) * (conv(2 * HY_CH) + cb_ref[:, 2 * HY_CH:])
    project(HY_IN + DN_QK)
    q_ref[...] = _head_l2norm(_silu(conv(HY_IN))) * (DN_DK ** -0.5)
    project(HY_IN + 2 * DN_QK)
    k_ref[...] = _head_l2norm(_silu(conv(HY_IN + DN_QK)))
    rest = jnp.dot(h, wr_ref[...], preferred_element_type=F32)
    v_ref[...] = _silu(conv(HY_IN + 2 * DN_QK))
    z_ref[...] = rest[:, :DN_W]
    a = rest[:, DN_W:] + gt_ref[1:2]
    softplus = jnp.maximum(a, 0.0) + jnp.log(1.0 + jnp.exp(-jnp.abs(a)))
    gate_ref[...] = jnp.where(gt_ref[2:3] > 0.5, -gt_ref[0:1] * softplus, jax.nn.sigmoid(rest[:, DN_W:]))


def _in_proj(x2d, seq_len, g, w_conv, w_rest, conv_w, conv_b, gate_tab):
    T = x2d.shape[0]
    per = ROW_TILE // SUB
    last_sub = T // SUB - 1
    const = lambda i: (0, 0)
    row = lambda i: (i, 0)
    o512 = pl.BlockSpec((ROW_TILE, HY_CH), row)
    s512 = jax.ShapeDtypeStruct((T, HY_CH), F32)
    return pl.pallas_call(
        functools.partial(_in_proj_kernel, tiles_per_seq=seq_len // ROW_TILE),
        grid=(T // ROW_TILE,),
        in_specs=[pl.BlockSpec((SUB, D_MODEL), lambda i: (jnp.maximum(i * per - 1, 0), 0)),
                  pl.BlockSpec((ROW_TILE, D_MODEL), row),
                  pl.BlockSpec((SUB, D_MODEL), lambda i: (jnp.minimum((i + 1) * per, last_sub), 0)),
                  pl.BlockSpec((1, D_MODEL), const),
                  pl.BlockSpec((D_MODEL, CONV_COLS), const),
                  pl.BlockSpec((D_MODEL, REST_COLS), const),
                  pl.BlockSpec((SHORT_CONV, CONV_COLS), const),
                  pl.BlockSpec((1, HY_IN), const),
                  pl.BlockSpec((3, LANES), const)],
        out_specs=[o512, o512, o512, o512, o512, o512, pl.BlockSpec((ROW_TILE, LANES), row)],
        out_shape=[s512, s512, s512, s512, s512, s512, jax.ShapeDtypeStruct((T, LANES), F32)],
        scratch_shapes=[pltpu.VMEM((ROW_TILE + 2 * SUB, CONV_COLS), F32)],
        compiler_params=pltpu.CompilerParams(dimension_semantics=("parallel",),
                                             vmem_limit_bytes=VMEM_LIMIT),
        name="in_proj",
    )(x2d, x2d, x2d, g.reshape(1, D_MODEL), w_conv, w_rest, conv_w, conv_b, gate_tab)


def _in_proj_params(w_in, hy_conv_w, hy_conv_b, dn_conv_w, dn_a_log, dn_dt_bias):
    H = DN_HEADS
    w16 = w_in.astype(BF16)
    gc = w16[:, CONV_COLS + DN_W:]
    gc = jnp.concatenate([gc[:, 0:H], gc[:, 2 * H:3 * H], gc[:, H:2 * H], gc[:, 3 * H:]], axis=1)
    w_rest = jnp.concatenate([w16[:, CONV_COLS:CONV_COLS + DN_W],
                              jnp.pad(gc, ((0, 0), (0, LANES - N_GATE)))], axis=1)
    zero, one, pad = jnp.zeros((H,), F32), jnp.ones((H,), F32), jnp.zeros((LANES - N_GATE,), F32)
    gate_tab = jnp.stack([jnp.concatenate([jnp.exp(dn_a_log[0]), zero, jnp.exp(dn_a_log[1]), zero, pad]),
                          jnp.concatenate([dn_dt_bias[0], zero, dn_dt_bias[1], zero, pad]),
                          jnp.concatenate([one, zero, one, zero, pad])])
    return (w16[:, :CONV_COLS], w_rest, jnp.concatenate([hy_conv_w, dn_conv_w], axis=1),
            hy_conv_b.reshape(1, HY_IN), gate_tab)


def _out_proj_kernel(x_ref, yh_ref, of_ref, ob_ref, z_ref, nw_ref, wh_ref, wd_ref, g_ref, wrh_ref,
                     wrl_ref, br_ref, x1_ref, h_ref, lg_ref):
    o = of_ref[...] + ob_ref[...]
    heads = []
    for hd in range(DN_HEADS):
        oh = o[:, hd * DN_DV:(hd + 1) * DN_DV]
        heads.append(oh * lax.rsqrt(jnp.mean(oh * oh, axis=-1, keepdims=True) + EPS))
    z = z_ref[...]
    y_dn = jnp.concatenate(heads, axis=1) * nw_ref[...] * (z * jax.nn.sigmoid(z))
    x1 = (x_ref[...]
          + jnp.dot(yh_ref[...].astype(BF16), wh_ref[...], preferred_element_type=F32)
          + jnp.dot(y_dn.astype(BF16), wd_ref[...], preferred_element_type=F32))
    x1_ref[...] = x1
    h = _rms(x1, g_ref[...])
    h_hi = h.astype(BF16)
    h_lo = (h - h_hi.astype(F32)).astype(BF16)
    h_ref[...] = h
    lg_ref[...] = (jnp.dot(h_hi, wrh_ref[...], preferred_element_type=F32)
                   + jnp.dot(h_lo, wrh_ref[...], preferred_element_type=F32)
                   + jnp.dot(h_hi, wrl_ref[...], preferred_element_type=F32)
                   + br_ref[...])


def _out_proj(x2d, y_hy, o_f, o_b, z, dn_norm_w, w_oh, w_od, g, wr_hi, wr_lo, br):
    T = x2d.shape[0]
    const = lambda i: (0, 0)
    row = lambda i: (i, 0)
    return pl.pallas_call(
        _out_proj_kernel,
        grid=(T // ROW_TILE,),
        in_specs=[pl.BlockSpec((ROW_TILE, D_MODEL), row),
                  pl.BlockSpec((ROW_TILE, HY_CH), row),
                  pl.BlockSpec((ROW_TILE, DN_W), row),
                  pl.BlockSpec((ROW_TILE, DN_W), row),
                  pl.BlockSpec((ROW_TILE, DN_W), row),
                  pl.BlockSpec((1, DN_W), const),
                  pl.BlockSpec((HY_CH, D_MODEL), const),
                  pl.BlockSpec((DN_W, D_MODEL), const),
                  pl.BlockSpec((1, D_MODEL), const),
                  pl.BlockSpec((D_MODEL, LANES), const),
                  pl.BlockSpec((D_MODEL, LANES), const),
                  pl.BlockSpec((1, LANES), const)],
        out_specs=[pl.BlockSpec((ROW_TILE, D_MODEL), row),
                   pl.BlockSpec((ROW_TILE, D_MODEL), row),
                   pl.BlockSpec((ROW_TILE, LANES), row)],
        out_shape=[jax.ShapeDtypeStruct((T, D_MODEL), F32),
                   jax.ShapeDtypeStruct((T, D_MODEL), F32),
                   jax.ShapeDtypeStruct((T, LANES), F32)],
        compiler_params=pltpu.CompilerParams(dimension_semantics=("parallel",),
                                             vmem_limit_bytes=VMEM_LIMIT),
        name="out_proj_router",
    )(x2d, y_hy, o_f, o_b, z, jnp.tile(dn_norm_w, DN_HEADS).reshape(1, DN_W), w_oh, w_od,
      g.reshape(1, D_MODEL), wr_hi, wr_lo, br)


def _deint_kernel(w_ref, p_ref, og_ref, ol_ref):
    half = DEINT_COLS // 2
    sel = jnp.dot(w_ref[0].astype(BF16), p_ref[...], preferred_element_type=F32)
    og_ref[0] = sel[:, :half].astype(BF16)
    ol_ref[0] = sel[:, half:].astype(BF16)


def _deinterleave(w_gate_up):
    half = DEINT_COLS // 2
    r = np.arange(DEINT_COLS)[:, None]
    c = np.arange(DEINT_COLS)[None, :]
    perm = jnp.asarray(np.where(c < half, r == 2 * c, r == 2 * (c - half) + 1), BF16)
    out = jax.ShapeDtypeStruct((N_EXPERTS, D_MODEL, D_FF), BF16)
    return pl.pallas_call(
        _deint_kernel,
        grid=(N_EXPERTS, 2 * D_FF // DEINT_COLS),
        in_specs=[pl.BlockSpec((1, D_MODEL, DEINT_COLS), lambda e, j: (e, 0, j)),
                  pl.BlockSpec((DEINT_COLS, DEINT_COLS), lambda e, j: (0, 0))],
        out_specs=[pl.BlockSpec((1, D_MODEL, half), lambda e, j: (e, 0, j)),
                   pl.BlockSpec((1, D_MODEL, half), lambda e, j: (e, 0, j))],
        out_shape=[out, out],
        compiler_params=pltpu.CompilerParams(dimension_semantics=("parallel", "parallel")),
        name="deinterleave_gate_up",
    )(w_gate_up, perm)


def _dispatch_kernel(tail_ref, dest_ref, h_ref, xb_ref, zero_scr, sem):
    n = h_ref.shape[0]

    @pl.when(pl.program_id(0) == 0)
    def _():
        zero_scr[...] = jnp.zeros_like(zero_scr)
        for e in range(N_EXPERTS):
            t0 = pl.multiple_of(tail_ref[0, e], SUB)
            pltpu.make_async_copy(zero_scr, xb_ref.at[pl.ds(t0, MOE_BLOCK), :], sem).start()
        for e in range(N_EXPERTS):
            pltpu.make_async_copy(zero_scr, xb_ref.at[pl.ds(0, MOE_BLOCK), :], sem).wait()

    def issue(r, c):
        for k in range(TOP_K):
            d = dest_ref[0, r * TOP_K + k]
            pltpu.make_async_copy(h_ref.at[pl.ds(r, 1), :], xb_ref.at[pl.ds(d, 1), :],
                                  sem).start(priority=k % 2)
        return c

    lax.fori_loop(0, n, issue, 0, unroll=8)

    def drain(r, c):
        for k in range(TOP_K):
            pltpu.make_async_copy(h_ref.at[pl.ds(0, 1), :], xb_ref.at[pl.ds(0, 1), :], sem).wait()
        return c

    lax.fori_loop(0, n, drain, 0, unroll=8)


def _dispatch(h, dest, tail_start, n_rows):
    T = h.shape[0]
    tm = DISPATCH_TILE
    return pl.pallas_call(
        _dispatch_kernel,
        grid=(T // tm,),
        in_specs=[pl.BlockSpec((1, N_EXPERTS), lambda i: (0, 0), memory_space=pltpu.SMEM),
                  pl.BlockSpec((None, 1, tm * TOP_K), lambda i: (i, 0, 0), memory_space=pltpu.SMEM),
                  pl.BlockSpec((tm, D_MODEL), lambda i: (i, 0))],
        out_specs=pl.BlockSpec(memory_space=pl.ANY),
        out_shape=jax.ShapeDtypeStruct((n_rows, D_MODEL), F32),
        scratch_shapes=[pltpu.VMEM((MOE_BLOCK, D_MODEL), F32), pltpu.SemaphoreType.DMA],
        compiler_params=pltpu.CompilerParams(dimension_semantics=("arbitrary",)),
        name="moe_dispatch",
    )(tail_start.reshape(1, N_EXPERTS), dest.reshape(T // tm, 1, tm * TOP_K), h)


def _expert_kernel(be_ref, nb_ref, xb_ref, wg_ref, wl_ref, bg_ref, bl_ref, wd_ref, bd_ref, y_ref):
    i = pl.program_id(0)

    @pl.when(i < nb_ref[0])
    def _():
        xb = xb_ref[...].astype(BF16)
        hg = jnp.dot(xb, wg_ref[0], preferred_element_type=F32) + bg_ref[0]
        hl = jnp.dot(xb, wl_ref[0], preferred_element_type=F32) + bl_ref[0]
        x_glu = jnp.minimum(hg, SWIGLU_LIMIT)
        x_lin = jnp.clip(hl, -SWIGLU_LIMIT, SWIGLU_LIMIT)
        act = x_glu * jax.nn.sigmoid(SWIGLU_ALPHA * x_glu) * (x_lin + 1.0)
        y = jnp.dot(act.astype(BF16), wd_ref[0], preferred_element_type=F32) + bd_ref[0]
        pieces = D_MODEL // SC_ROW
        for c in range(pieces):
            y_ref[pl.ds(c, MOE_BLOCK, stride=pieces), :] = y[:, c * SC_ROW:(c + 1) * SC_ROW]

    @pl.when(i >= nb_ref[0])
    def _():
        y_ref[...] = jnp.zeros_like(y_ref)


def _expert_mlp(xb, block_e, n_used, wg, wl, bg, bl, wd, bd):
    n_rows = xb.shape[0]
    n_blocks = n_rows // MOE_BLOCK
    rowm = lambda i, be, nb: (i, 0)
    exp3 = lambda i, be, nb: (be[i], 0, 0)
    grid_spec = pltpu.PrefetchScalarGridSpec(
        num_scalar_prefetch=2,
        grid=(n_blocks,),
        in_specs=[pl.BlockSpec((MOE_BLOCK, D_MODEL), rowm),
                  pl.BlockSpec((1, D_MODEL, D_FF), exp3),
                  pl.BlockSpec((1, D_MODEL, D_FF), exp3),
                  pl.BlockSpec((1, 1, D_FF), exp3),
                  pl.BlockSpec((1, 1, D_FF), exp3),
                  pl.BlockSpec((1, D_FF, D_MODEL), exp3),
                  pl.BlockSpec((1, 1, D_MODEL), exp3)],
        out_specs=pl.BlockSpec((MOE_BLOCK * D_MODEL // SC_ROW, SC_ROW), rowm),
    )
    return pl.pallas_call(
        _expert_kernel,
        grid_spec=grid_spec,
        out_shape=jax.ShapeDtypeStruct((n_rows * D_MODEL // SC_ROW, SC_ROW), F32),
        compiler_params=pltpu.CompilerParams(dimension_semantics=("arbitrary",),
                                             vmem_limit_bytes=VMEM_LIMIT),
        name="expert_mlp",
    )(block_e, n_used, xb, wg, wl, bg, bl, wd, bd)


def _moe(h, logits, wg, wl, bg, bl, wd, bd):
    T = h.shape[0]
    TK = T * TOP_K
    top_vals, top_idx = lax.top_k(logits, TOP_K)
    gates = jax.nn.softmax(top_vals, axis=-1)
    sel = jnp.sum(jax.nn.one_hot(top_idx, N_EXPERTS, dtype=jnp.int32), axis=1)
    before = jnp.cumsum(sel, axis=0) - sel
    counts = jnp.sum(sel, axis=0)
    padded = (counts + MOE_BLOCK - 1) // MOE_BLOCK * MOE_BLOCK
    pad_end = jnp.cumsum(padded)
    pad_start = pad_end - padded
    rank = jnp.take_along_axis(before, top_idx, axis=1)
    dest = (pad_start[top_idx] + rank).astype(jnp.int32)
    n_blocks = (TK + MOE_BLOCK - 1) // MOE_BLOCK + N_EXPERTS
    n_rows = n_blocks * MOE_BLOCK
    block_start = jnp.arange(n_blocks, dtype=jnp.int32) * MOE_BLOCK
    block_e = jnp.minimum(jnp.sum((block_start[:, None] >= pad_end[None, :]).astype(jnp.int32), axis=1),
                          N_EXPERTS - 1)
    n_used = (pad_end[-1] // MOE_BLOCK).astype(jnp.int32).reshape(1)
    tail_start = jnp.minimum((pad_start + counts) // SUB * SUB, n_rows - MOE_BLOCK).astype(jnp.int32)
    xb = _dispatch(h, dest, tail_start, n_rows)
    yb = _expert_mlp(xb, block_e, n_used, wg, wl, bg, bl, wd, bd)
    return yb, dest, gates


def _sc_gather(x, indices):
    n = indices.shape[0]
    width = x.shape[1]
    mesh = plsc.VectorSubcoreMesh(core_axis_name="core", subcore_axis_name="subcore")

    @functools.partial(pl.kernel, out_type=jax.ShapeDtypeStruct((n, width), x.dtype), mesh=mesh)
    def gather(x_hbm, i_hbm, o_hbm):
        def body(i_vmem, o_vmem):
            pltpu.sync_copy(x_hbm.at[i_vmem.at[0]], o_vmem)

        pltpu.emit_pipeline(
            body,
            grid=(n // SC_WINDOW,),
            in_specs=[pl.BlockSpec((1, SC_WINDOW), index_map=lambda i: (0, i))],
            out_specs=[pl.BlockSpec((SC_WINDOW, width), index_map=lambda i: (i, 0))],
            core_axis_name=("core", "subcore"),
            dimension_semantics=(pltpu.PARALLEL,),
        )(i_hbm, o_hbm)

    return gather(x, indices.reshape(1, n))


def _combine_kernel(gate_ref, x1_ref, g_ref, y4_ref, o_ref):
    tm = x1_ref.shape[0]
    pieces = D_MODEL // SC_ROW
    cols = []
    for c in range(pieces):
        acc = x1_ref[:, c * SC_ROW:(c + 1) * SC_ROW]
        for k in range(TOP_K):
            acc = acc + gate_ref[:, k:k + 1] * y4_ref[pl.ds((k * pieces + c) * tm, tm), :]
        cols.append(acc)
    o_ref[...] = _rms(jnp.concatenate(cols, axis=1), g_ref[...])


def _combine_final(yb, dest, gates, x1, g):
    T = x1.shape[0]
    tm = COMBINE_TILE
    pieces = D_MODEL // SC_ROW
    d = dest.reshape(T // tm, tm, TOP_K).transpose(0, 2, 1)
    idx = (d[:, :, None, :] * pieces + jnp.arange(pieces, dtype=jnp.int32)[None, None, :, None]).reshape(-1)
    y4 = _sc_gather(yb, idx)
    row = lambda i: (i, 0)
    return pl.pallas_call(
        _combine_kernel,
        grid=(T // tm,),
        in_specs=[pl.BlockSpec((tm, TOP_K), row),
                  pl.BlockSpec((tm, D_MODEL), row),
                  pl.BlockSpec((1, D_MODEL), lambda i: (0, 0)),
                  pl.BlockSpec((tm * TOP_K * pieces, SC_ROW), row)],
        out_specs=pl.BlockSpec((tm, D_MODEL), row),
        out_shape=jax.ShapeDtypeStruct((T, D_MODEL), F32),
        compiler_params=pltpu.CompilerParams(dimension_semantics=("parallel",),
                                             vmem_limit_bytes=VMEM_LIMIT),
        name="moe_combine_final",
    )(gates, x1, g.reshape(1, D_MODEL), y4)


def _fft_tables(nc, nf):
    n = nc * nf
    kc = np.arange(nc, dtype=np.float64)
    a1 = 2.0 * np.pi * np.outer(kc, np.arange(nc // 2)) / nc
    c1, s1 = np.cos(a1), np.sin(a1)
    l1 = np.block([[c1, s1], [-s1, c1]])
    a2 = 2.0 * np.pi * np.outer(np.arange(nf), np.arange(nf)) / nf
    c2, s2 = np.cos(a2), np.sin(a2)
    m2 = np.block([[c2, s2], [-s2, c2]])
    m2i = np.block([[c2, -s2], [s2, c2]])
    a3 = 2.0 * np.pi * np.outer(np.arange(nc // 2), kc) / nc
    c3, s3 = np.cos(a3), np.sin(a3)
    l3 = np.block([[c3, -s3], [s3, c3]]) / n
    return tuple(jnp.asarray(m, F32).astype(BF16) for m in (l1, m2, m2i, l3))


def _twiddle(nc, nf):
    n = nc * nf
    ph = (jnp.arange(nf, dtype=jnp.int32)[:, None] * jnp.arange(nc, dtype=jnp.int32)[None, :]) % n
    ang = ph.astype(F32) * (2.0 * math.pi / n)
    tw = jnp.stack([jnp.cos(ang), jnp.sin(ang)], axis=1)
    return jnp.broadcast_to(tw[..., None], (nf, 2, nc, LANES))


def _lane_tile(t, width):
    return t if width == LANES else jnp.concatenate([t] * (width // LANES), axis=1)


def _fft1_kernel(z_ref, l1_ref, tw_ref, o_ref, *, nc):
    cb = o_ref.shape[-1]
    for j in range(SUB):
        rhs = jnp.concatenate([z_ref[0, :, j, :], z_ref[1, :, j, :]], axis=0).astype(BF16)
        a = jnp.dot(l1_ref[...], rhs, preferred_element_type=F32)
        ar, ai = a[:nc], a[nc:]
        twr = _lane_tile(tw_ref[j, 0], cb)
        twi = _lane_tile(tw_ref[j, 1], cb)
        o_ref[j, 0] = ar * twr + ai * twi
        o_ref[j, 1] = ai * twr - ar * twi


def _fft1(z, l1, tw, nc, nf):
    P, C = z.shape[0], z.shape[-1]
    cb = FFT_CB
    z_spec = pl.BlockSpec((None, 2, nc // 2, SUB, cb), lambda p, j, c: (p, 0, 0, j, c))
    return pl.pallas_call(
        functools.partial(_fft1_kernel, nc=nc),
        grid=(P, nf // SUB, C // cb),
        in_specs=[z_spec,
                  pl.BlockSpec(l1.shape, lambda p, j, c: (0, 0)),
                  pl.BlockSpec((SUB, 2, nc, LANES), lambda p, j, c: (j, 0, 0, 0))],
        out_specs=pl.BlockSpec((None, SUB, 2, nc, cb), lambda p, j, c: (p, j, 0, 0, c)),
        out_shape=jax.ShapeDtypeStruct((P, nf, 2, nc, C), F32),
        compiler_params=pltpu.CompilerParams(dimension_semantics=("parallel", "parallel", "parallel"),
                                             vmem_limit_bytes=VMEM_LIMIT),
        name="hyena_fft_stage1",
    )(z, l1, tw)


def _fft2_filter_kernel(a_ref, m2_ref, k_ref, *, nf):
    for j in range(SUB):
        rhs = jnp.concatenate([a_ref[:, 0, j, :], a_ref[:, 1, j, :]], axis=0).astype(BF16)
        x = jnp.dot(m2_ref[...], rhs, preferred_element_type=F32)
        k_ref[j, 0] = x[:nf]
        k_ref[j, 1] = x[nf:]


def _fft2_filter(a, m2, nc, nf):
    C = a.shape[-1]
    cb = FFT_CB
    return pl.pallas_call(
        functools.partial(_fft2_filter_kernel, nf=nf),
        grid=(nc // SUB, C // cb),
        in_specs=[pl.BlockSpec((None, nf, 2, SUB, cb), lambda k, c: (0, 0, 0, k, c)),
                  pl.BlockSpec(m2.shape, lambda k, c: (0, 0))],
        out_specs=pl.BlockSpec((SUB, 2, nf, cb), lambda k, c: (k, 0, 0, c)),
        out_shape=jax.ShapeDtypeStruct((nc, 2, nf, C), F32),
        compiler_params=pltpu.CompilerParams(dimension_semantics=("parallel", "parallel"),
                                             vmem_limit_bytes=VMEM_LIMIT),
        name="hyena_filter_spectrum",
    )(a, m2)


def _fft2_kernel(a_ref, k_ref, m2_ref, m2i_ref, o_ref, *, nf):
    for j in range(SUB):
        rhs = jnp.concatenate([a_ref[:, 0, j, :], a_ref[:, 1, j, :]], axis=0).astype(BF16)
        x = jnp.dot(m2_ref[...], rhs, preferred_element_type=F32)
        xr, xi = x[:nf], x[nf:]
        kr, ki = k_ref[j, 0], k_ref[j, 1]
        y = jnp.concatenate([xr * kr - xi * ki, xr * ki + xi * kr], axis=0).astype(BF16)
        b = jnp.dot(m2i_ref[...], y, preferred_element_type=F32)
        o_ref[:, 0, j, :] = b[:nf]
        o_ref[:, 1, j, :] = b[nf:]


def _fft2(a, kspec, m2, m2i, nc, nf):
    P, C = a.shape[0], a.shape[-1]
    cb = FFT_CB
    blk = pl.BlockSpec((None, nf, 2, SUB, cb), lambda p, k, c: (p, 0, 0, k, c))
    return pl.pallas_call(
        functools.partial(_fft2_kernel, nf=nf),
        grid=(P, nc // SUB, C // cb),
        in_specs=[blk,
                  pl.BlockSpec((SUB, 2, nf, cb), lambda p, k, c: (k, 0, 0, c)),
                  pl.BlockSpec(m2.shape, lambda p, k, c: (0, 0)),
                  pl.BlockSpec(m2i.shape, lambda p, k, c: (0, 0))],
        out_specs=blk,
        out_shape=jax.ShapeDtypeStruct(a.shape, F32),
        compiler_params=pltpu.CompilerParams(dimension_semantics=("parallel", "parallel", "parallel"),
                                             vmem_limit_bytes=VMEM_LIMIT),
        name="hyena_fft_stage2",
    )(a, kspec, m2, m2i)


def _fft3_kernel(b_ref, tw_ref, l3_ref, s_ref, x0_ref, bias_ref, o_ref, *, nc):
    cb = o_ref.shape[-1]
    half = nc // 2
    for j in range(SUB):
        br, bi = b_ref[j, 0], b_ref[j, 1]
        twr = _lane_tile(tw_ref[j, 0], cb)
        twi = _lane_tile(tw_ref[j, 1], cb)
        rhs = jnp.concatenate([br * twr - bi * twi, br * twi + bi * twr], axis=0).astype(BF16)
        y = jnp.dot(l3_ref[...], rhs, preferred_element_type=F32)
        for q in range(2):
            s = s_ref[q, :, j, :]
            o_ref[q, :, j, :] = x0_ref[q, :, j, :] * (y[q * half:(q + 1) * half] + s * bias_ref[...])


def _fft3(b, tw, l3, s5, x05, bias, nc, nf):
    P, C = b.shape[0], b.shape[-1]
    cb = FFT_CB
    seq = pl.BlockSpec((None, 2, nc // 2, SUB, cb), lambda p, j, c: (p, 0, 0, j, c))
    return pl.pallas_call(
        functools.partial(_fft3_kernel, nc=nc),
        grid=(P, nf // SUB, C // cb),
        in_specs=[pl.BlockSpec((None, SUB, 2, nc, cb), lambda p, j, c: (p, j, 0, 0, c)),
                  pl.BlockSpec((SUB, 2, nc, LANES), lambda p, j, c: (j, 0, 0, 0)),
                  pl.BlockSpec(l3.shape, lambda p, j, c: (0, 0)),
                  seq, seq,
                  pl.BlockSpec((1, cb), lambda p, j, c: (0, c))],
        out_specs=seq,
        out_shape=jax.ShapeDtypeStruct(s5.shape, F32),
        compiler_params=pltpu.CompilerParams(dimension_semantics=("parallel", "parallel", "parallel"),
                                             vmem_limit_bytes=VMEM_LIMIT),
        name="hyena_fft_stage3",
    )(b, tw, l3, s5, x05, bias)


def _fft_split(n):
    nf = 1 << (int(math.log2(n)) // 2)
    return n // nf, nf


def _filter_tables(nc):
    half = nc // 2
    kc = np.arange(nc, dtype=np.float64)[:, None]
    r = np.arange(half, dtype=np.float64)
    mats = []
    for rows_b, drop0 in ((half + (half - 1 - r), False), (nc - r, True)):
        a = 2.0 * np.pi * kc * np.concatenate([r, rows_b])[None, :] / nc
        m = np.concatenate([np.cos(a), -np.sin(a)], axis=0)
        if drop0:
            m[:, half] = 0.0
        mats.append(m)
    return tuple(jnp.asarray(m, F32).astype(BF16) for m in mats)


def _fft1_filter_kernel(hf_ref, hba_ref, hbb_ref, mb_ref, mb0_ref, tw_ref, o_ref, *, nc):
    cb = o_ref.shape[-1]
    m_first = jnp.where(pl.program_id(0) == 0, mb0_ref[...], mb_ref[...])
    for j in range(SUB):
        src = hbb_ref[:, 0, :] if j == 0 else hba_ref[:, SUB - j, :]
        rhs = jnp.concatenate([hf_ref[:, j, :], src], axis=0).astype(BF16)
        a = jnp.dot(m_first if j == 0 else mb_ref[...], rhs, preferred_element_type=F32)
        ar, ai = a[:nc], a[nc:]
        twr = _lane_tile(tw_ref[j, 0], cb)
        twi = _lane_tile(tw_ref[j, 1], cb)
        o_ref[j, 0] = ar * twr + ai * twi
        o_ref[j, 1] = ai * twr - ar * twi


def _fft1_filter(h_f, h_b, mb, mb0, tw, nc, nf):
    C = h_f.shape[-1]
    cb = FFT_CB
    nblk = nf // SUB
    hf3 = h_f.reshape(nc // 2, nf, C)
    hb3 = h_b.reshape(nc // 2, nf, C)
    blk = lambda f: pl.BlockSpec((nc // 2, SUB, cb), f)
    return pl.pallas_call(
        functools.partial(_fft1_filter_kernel, nc=nc),
        grid=(nblk, C // cb),
        in_specs=[blk(lambda j, c: (0, j, c)),
                  blk(lambda j, c: (0, nblk - 1 - j, c)),
                  blk(lambda j, c: (0, (nblk - j) % nblk, c)),
                  pl.BlockSpec(mb.shape, lambda j, c: (0, 0)),
                  pl.BlockSpec(mb0.shape, lambda j, c: (0, 0)),
                  pl.BlockSpec((SUB, 2, nc, LANES), lambda j, c: (j, 0, 0, 0))],
        out_specs=pl.BlockSpec((None, SUB, 2, nc, cb), lambda j, c: (0, j, 0, 0, c)),
        out_shape=jax.ShapeDtypeStruct((1, nf, 2, nc, C), F32),
        compiler_params=pltpu.CompilerParams(dimension_semantics=("parallel", "parallel"),
                                             vmem_limit_bytes=VMEM_LIMIT),
        name="hyena_filter_stage1",
    )(hf3, hb3, hb3, mb, mb0, tw)


def _hyena_conv(s, x0, h_f, h_b, bias):
    B, L, C = s.shape
    nc, nf = _fft_split(2 * L)
    tw = _twiddle(nc, nf)
    l1, m2, m2i, l3 = _fft_tables(nc, nf)
    mb, mb0 = _filter_tables(nc)
    kspec = _fft2_filter(_fft1_filter(h_f, h_b, mb, mb0, tw, nc, nf), m2, nc, nf)
    s5 = s.reshape(B // 2, 2, nc // 2, nf, C)
    x05 = x0.reshape(B // 2, 2, nc // 2, nf, C)
    a = _fft1(s5, l1, tw, nc, nf)
    b = _fft2(a, kspec, m2, m2i, nc, nf)
    return _fft3(b, tw, l3, s5, x05, bias.reshape(1, C), nc, nf).reshape(B, L, C)


def _hyena_filter(L, w1, b1, w2, b2, w3, freq):
    pos = jnp.arange(L, dtype=F32)
    t = jnp.linspace(0.0, 1.0, L, dtype=F32)[:, None]
    bands = jnp.linspace(1e-4, HY_BANDS - 1, HY_BANDS, dtype=F32)
    ang = (2.0 * math.pi / L) * pos[:, None] * bands[None, :]
    z = jnp.concatenate([t, jnp.cos(ang), -jnp.sin(ang)], axis=-1)
    h = jnp.sin(freq * (z @ w1 + b1))
    h = jnp.sin(freq * (h @ w2 + b2))
    h = h @ w3
    deltas = jnp.abs(jnp.linspace(math.log(HY_DECAY_TARGET) / HY_SLOW_PCT,
                                  math.log(HY_DECAY_TARGET) / HY_FAST_PCT, HY_CH, dtype=F32))
    window = jnp.exp(-t * deltas[None, :])
    h_f = h[:, :HY_CH] * window
    h_b = h[:, HY_CH:] * window
    l1 = jnp.sum(jnp.abs(h_f), axis=0) + jnp.sum(jnp.abs(h_b[1:]), axis=0)
    return h_f / l1, h_b / l1


def _split3(x):
    x1 = x.astype(BF16)
    r = x - x1.astype(F32)
    x2 = r.astype(BF16)
    x3 = (r - x2.astype(F32)).astype(BF16)
    return x1, x2, x3


def _mm(a, b):
    return jnp.dot(a.astype(BF16), b.astype(BF16), preferred_element_type=F32)


def _delta_kernel(qf_ref, kf_ref, vf_ref, gf_ref, qb_ref, kb_ref, vb_ref, gb_ref, of_ref, ob_ref, s_ref,
                  *, n_sub):
    @pl.when(pl.program_id(1) == 0)
    def _():
        s_ref[...] = jnp.zeros_like(s_ref)

    row = lax.broadcasted_iota(jnp.int32, (CHUNK, CHUNK), 0)
    col = lax.broadcasted_iota(jnp.int32, (CHUNK, CHUNK), 1)
    eye_f = jnp.where(row == col, 1.0, 0.0).astype(F32)
    dirs = ((qf_ref, kf_ref, vf_ref, gf_ref, of_ref, row >= col, row > col),
            (qb_ref, kb_ref, vb_ref, gb_ref, ob_ref, row <= col, row < col))

    ch = []
    for d, (q_ref, k_ref, v_ref, g_ref, _, incl, strict) in enumerate(dirs):
        tri = jnp.where(incl, 1.0, 0.0).astype(BF16)
        for c in range(n_sub):
            rows = slice(c * CHUNK, (c + 1) * CHUNK)
            gates = g_ref[rows, :]
            gl = 2 * DN_HEADS * d
            g1, g2, g3 = _split3(gates)
            gc_all = (jnp.dot(tri, g1, preferred_element_type=F32)
                      + jnp.dot(tri, g2, preferred_element_type=F32)
                      + jnp.dot(tri, g3, preferred_element_type=F32))
            gsum_all = jnp.sum(gates, axis=0, keepdims=True)
            for h in range(DN_HEADS):
                lanes = slice(h * DN_DK, (h + 1) * DN_DK)
                kh = k_ref[rows, lanes]
                beta = gates[:, gl + DN_HEADS + h:gl + DN_HEADS + h + 1]
                gc = gc_all[:, gl + h:gl + h + 1]
                g_last = gsum_all[:, gl + h:gl + h + 1]
                gc_b = jnp.broadcast_to(gc, (CHUNK, CHUNK))
                gc_row = jnp.sum(gc_b * eye_f, axis=0, keepdims=True)
                decay = jnp.where(incl, jnp.exp(jnp.minimum(gc_b - gc_row, 0.0)), 0.0)
                e_gc = jnp.exp(gc)
                kb = kh * beta
                ch.append(dict(d=d, c=c, h=h, lanes=lanes, rows=rows, strict=strict, decay=decay,
                               kh16=kh.astype(BF16), kb16=kb.astype(BF16),
                               q16=q_ref[rows, lanes].astype(BF16),
                               rhs=jnp.concatenate([v_ref[rows, lanes] * beta, kb * e_gc],
                                                   axis=1).astype(BF16),
                               qe=q_ref[rows, lanes] * e_gc,
                               kdT=(kh * jnp.exp(g_last - gc)).T.astype(BF16),
                               e_last=jnp.exp(g_last)))
    for x in ch:
        a = lax.dot_general(x["kb16"], x["kh16"], _NT, preferred_element_type=F32) * x["decay"]
        x["a"] = jnp.where(x["strict"], a, 0.0)
        x["qk"] = (lax.dot_general(x["q16"], x["kh16"], _NT, preferred_element_type=F32)
                   * x["decay"]).astype(BF16)
    for x in ch:
        x["t"] = eye_f - x["a"]
        x["p"] = _mm(x["a"], x["a"])
    for level in range(5):
        for x in ch:
            p16 = x["p"].astype(BF16)
            x["t"] = x["t"] + jnp.dot(x["t"].astype(BF16), p16, preferred_element_type=F32)
            if level < 4:
                x["p"] = jnp.dot(p16, p16, preferred_element_type=F32)
    for x in ch:
        uw = jnp.dot(x["t"].astype(BF16), x["rhs"], preferred_element_type=F32)
        x["u"] = uw[:, :DN_DV]
        x["wq"] = jnp.concatenate([uw[:, DN_DV:], x["qe"]], axis=0).astype(BF16)

    for step in range(n_sub):
        cur = [x for x in ch if x["c"] == (step if x["d"] == 0 else n_sub - 1 - step)]
        for x in cur:
            x["s"] = s_ref[x["d"], x["h"]]
            x["ws"] = jnp.dot(x["wq"], x["s"].astype(BF16), preferred_element_type=F32)
        for x in cur:
            x["vn"] = (x["u"] - x["ws"][:CHUNK]).astype(BF16)
        for x in cur:
            o = x["ws"][CHUNK:] + jnp.dot(x["qk"], x["vn"], preferred_element_type=F32)
            dirs[x["d"]][4][x["rows"], x["lanes"]] = o
            s_ref[x["d"], x["h"]] = (x["s"] * x["e_last"]
                                     + jnp.dot(x["kdT"], x["vn"], preferred_element_type=F32))


def _delta_scan(q, k, v, gates):
    B, L, _ = q.shape
    n_sub = DN_STEP_CHUNKS
    rows = n_sub * CHUNK
    nblk = L // rows
    fwd = pl.BlockSpec((None, rows, DN_W), lambda b, j: (b, j, 0))
    bwd = pl.BlockSpec((None, rows, DN_W), lambda b, j: (b, nblk - 1 - j, 0))
    gfwd = pl.BlockSpec((None, rows, LANES), lambda b, j: (b, j, 0))
    gbwd = pl.BlockSpec((None, rows, LANES), lambda b, j: (b, nblk - 1 - j, 0))
    out = jax.ShapeDtypeStruct((B, L, DN_W), F32)
    return pl.pallas_call(
        functools.partial(_delta_kernel, n_sub=n_sub),
        grid=(B, nblk),
        in_specs=[fwd, fwd, fwd, gfwd, bwd, bwd, bwd, gbwd],
        out_specs=[fwd, bwd],
        out_shape=[out, out],
        scratch_shapes=[pltpu.VMEM((2, DN_HEADS, DN_DK, DN_DV), F32)],
        compiler_params=pltpu.CompilerParams(dimension_semantics=("parallel", "arbitrary"),
                                             vmem_limit_bytes=VMEM_LIMIT),
        name="delta_scan",
    )(q, k, v, gates, q, k, v, gates)


def _trunk(x, w, hy_pos_w1, hy_pos_b1, hy_pos_w2, hy_pos_b2, hy_pos_w3, hy_sin_freq, hy_bias, dn_norm_w):
    B, L, D = x.shape
    T = B * L
    x2d = x.reshape(T, D)
    x0, s, q, k, v, z, dn_gates = _in_proj(x2d, L, w["norm_mix"], *w["in_proj"])
    seq = lambda a: a.reshape(B, L, a.shape[-1])
    h_f, h_b = _hyena_filter(L, hy_pos_w1, hy_pos_b1, hy_pos_w2, hy_pos_b2, hy_pos_w3, hy_sin_freq)
    y_hy = _hyena_conv(seq(s), seq(x0), h_f, h_b, hy_bias)
    o_f, o_b = _delta_scan(seq(q), seq(k), seq(v), seq(dn_gates))
    x1, h_ffn, logits = _out_proj(x2d, y_hy.reshape(T, HY_CH), o_f.reshape(T, DN_W), o_b.reshape(T, DN_W),
                                 z, dn_norm_w, w["out_hy"], w["out_dn"], w["norm_ffn"],
                                 w["router_hi"], w["router_lo"], w["router_b"])
    yb, dest, gates = _moe(h_ffn, logits[:, :N_EXPERTS], w["wg"], w["wl"], w["bg"], w["bl"], w["wd"],
                           w["bd"])
    return _combine_final(yb, dest, gates, x1, w["norm_final"]).reshape(B, L, D)


def kernel(x_prompt, x_sample, norm_mix_w, w_in, hy_conv_w, hy_conv_b, hy_pos_w1, hy_pos_b1, hy_pos_w2, hy_pos_b2, hy_pos_w3, hy_sin_freq, hy_bias, dn_conv_w, dn_a_log, dn_dt_bias, dn_norm_w, w_out, norm_ffn_w, w_router, b_router, w_gate_up, b_gate_up, w_down, b_down, norm_final_w):
    w_out16 = w_out[0].astype(BF16)
    wr = jnp.pad(w_router[0], ((0, 0), (0, LANES - N_EXPERTS)))
    wr_hi = wr.astype(BF16)
    wg, wl = _deinterleave(w_gate_up[0])
    w = {
        "norm_mix": norm_mix_w[0],
        "in_proj": _in_proj_params(w_in[0], hy_conv_w[0], hy_conv_b[0], dn_conv_w[0], dn_a_log[0],
                                   dn_dt_bias[0]),
        "out_hy": w_out16[:HY_CH],
        "out_dn": w_out16[HY_CH:],
        "norm_ffn": norm_ffn_w[0],
        "router_hi": wr_hi,
        "router_lo": (wr - wr_hi.astype(F32)).astype(BF16),
        "router_b": jnp.pad(b_router[0], (0, LANES - N_EXPERTS)).reshape(1, LANES),
        "wg": wg,
        "wl": wl,
        "bg": b_gate_up[0][:, 0::2].reshape(N_EXPERTS, 1, D_FF),
        "bl": b_gate_up[0][:, 1::2].reshape(N_EXPERTS, 1, D_FF),
        "wd": w_down[0].astype(BF16),
        "bd": b_down[0].reshape(N_EXPERTS, 1, D_MODEL),
        "norm_final": norm_final_w,
    }
    mix = (hy_pos_w1[0], hy_pos_b1[0], hy_pos_w2[0], hy_pos_b2[0], hy_pos_w3[0], hy_sin_freq[0],
           hy_bias[0], dn_norm_w[0])
    return (_trunk(x_prompt, w, *mix), _trunk(x_sample, w, *mix))
```

```python
import functools
import math

import jax
import jax.numpy as jnp
import numpy as np
from jax import lax
from jax.experimental import pallas as pl
from jax.experimental.pallas import tpu as pltpu
from jax.experimental.pallas import tpu_sc as plsc

D_MODEL = 1024
HY_CH = 512
DN_HEADS = 4
DN_DK = 128
DN_DV = 128
DN_QK = DN_HEADS * DN_DK
DN_W = DN_HEADS * DN_DV
HY_IN = 3 * HY_CH
DN_CONV = 2 * DN_QK + DN_W
N_GATE = 4 * DN_HEADS
SHORT_CONV = 3
CONV_COLS = HY_IN + DN_CONV
REST_COLS = DN_W + 128
HY_EMB = 33
HY_BANDS = (HY_EMB - 1) // 2
HY_DECAY_TARGET = 1e-2
HY_FAST_PCT = 0.3
HY_SLOW_PCT = 1.5
CHUNK = 64
N_EXPERTS = 32
TOP_K = 4
D_FF = D_MODEL
SWIGLU_ALPHA = 1.702
SWIGLU_LIMIT = 7.0
MOE_BLOCK = 512
EPS = 1e-6

LANES = 128
SUB = 8
ROW_TILE = 512
FFT_CB = 256
DN_STEP_CHUNKS = 4
DEINT_COLS = 512
COMBINE_TILE = 512
SC_WINDOW = 128
SC_ROW = LANES
SC_PIECES = D_MODEL // SC_ROW
VMEM_LIMIT = 56 * 1024 * 1024

F32 = jnp.float32
BF16 = jnp.bfloat16

_NT = (((1,), (1,)), ((), ()))


def _rms(x, g):
    return x * lax.rsqrt(jnp.mean(x * x, axis=-1, keepdims=True) + EPS) * g


def _silu(x):
    return x * jax.nn.sigmoid(x)


def _head_l2norm(x):
    parts = []
    for hd in range(DN_HEADS):
        xh = x[:, hd * DN_DK:(hd + 1) * DN_DK]
        parts.append(xh * lax.rsqrt(jnp.sum(xh * xh, axis=-1, keepdims=True) + EPS))
    return jnp.concatenate(parts, axis=1)


def _in_proj_kernel(xp_ref, x_ref, xn_ref, g_ref, wc_ref, wr_ref, cw_ref, cb_ref, gt_ref,
                    x0_ref, s_ref, q_ref, k_ref, v_ref, z_ref, gate_ref, p_scr, *, tiles_per_seq):
    i = pl.program_id(0)
    first = (i % tiles_per_seq) == 0
    last = (i % tiles_per_seq) == tiles_per_seq - 1
    g = g_ref[...]
    hp = jnp.where(first, 0.0, _rms(xp_ref[...], g))
    hn = jnp.where(last, 0.0, _rms(xn_ref[...], g))
    h_all = jnp.concatenate([hp, _rms(x_ref[...], g), hn], axis=0).astype(BF16)
    h = h_all[SUB:SUB + ROW_TILE]

    def project(c0):
        cols = slice(c0, c0 + HY_CH)
        p_scr[:, cols] = jnp.dot(h_all, wc_ref[:, cols], preferred_element_type=F32)

    def conv(c0):
        cols = slice(c0, c0 + HY_CH)
        return (p_scr[pl.ds(SUB - 1, ROW_TILE), cols] * cw_ref[0:1, cols]
                + p_scr[pl.ds(SUB, ROW_TILE), cols] * cw_ref[1:2, cols]
                + p_scr[pl.ds(SUB + 1, ROW_TILE), cols] * cw_ref[2:3, cols])

    project(0)
    project(HY_CH)
    x0_ref[...] = conv(0) + cb_ref[:, 0:HY_CH]
    project(2 * HY_CH)
    project(HY_IN)
    s_ref[...] = (conv(HY_CH) + cb_ref[:, HY_CH:2 * HY_CH]) * (conv(2 * HY_CH) + cb_ref[:, 2 * HY_CH:])
    project(HY_IN + DN_QK)
    q_ref[...] = _head_l2norm(_silu(conv(HY_IN))) * (DN_DK ** -0.5)
    project(HY_IN + 2 * DN_QK)
    k_ref[...] = _head_l2norm(_silu(conv(HY_IN + DN_QK)))
    rest = jnp.dot(h, wr_ref[...], preferred_element_type=F32)
    v_ref[...] = _silu(conv(HY_IN + 2 * DN_QK))
    z_ref[...] = rest[:, :DN_W]
    a = rest[:, DN_W:] + gt_ref[1:2]
    softplus = jnp.maximum(a, 0.0) + jnp.log(1.0 + jnp.exp(-jnp.abs(a)))
    gate_ref[...] = jnp.where(gt_ref[2:3] > 0.5, -gt_ref[0:1] * softplus, jax.nn.sigmoid(rest[:, DN_W:]))


def _in_proj(x2d, seq_len, g, w_conv, w_rest, conv_w, conv_b, gate_tab):
    T = x2d.shape[0]
    per = ROW_TILE // SUB
    last_sub = T // SUB - 1
    const = lambda i: (0, 0)
    row = lambda i: (i, 0)
    o512 = pl.BlockSpec((ROW_TILE, HY_CH), row)
    s512 = jax.ShapeDtypeStruct((T, HY_CH), F32)
    return pl.pallas_call(
        functools.partial(_in_proj_kernel, tiles_per_seq=seq_len // ROW_TILE),
        grid=(T // ROW_TILE,),
        in_specs=[pl.BlockSpec((SUB, D_MODEL), lambda i: (jnp.maximum(i * per - 1, 0), 0)),
                  pl.BlockSpec((ROW_TILE, D_MODEL), row),
                  pl.BlockSpec((SUB, D_MODEL), lambda i: (jnp.minimum((i + 1) * per, last_sub), 0)),
                  pl.BlockSpec((1, D_MODEL), const),
                  pl.BlockSpec((D_MODEL, CONV_COLS), const),
                  pl.BlockSpec((D_MODEL, REST_COLS), const),
                  pl.BlockSpec((SHORT_CONV, CONV_COLS), const),
                  pl.BlockSpec((1, HY_IN), const),
                  pl.BlockSpec((3, LANES), const)],
        out_specs=[o512, o512, o512, o512, o512, o512, pl.BlockSpec((ROW_TILE, LANES), row)],
        out_shape=[s512, s512, s512, s512, s512, s512, jax.ShapeDtypeStruct((T, LANES), F32)],
        scratch_shapes=[pltpu.VMEM((ROW_TILE + 2 * SUB, CONV_COLS), F32)],
        compiler_params=pltpu.CompilerParams(dimension_semantics=("parallel",),
                                             vmem_limit_bytes=VMEM_LIMIT),
        name="in_proj",
    )(x2d, x2d, x2d, g.reshape(1, D_MODEL), w_conv, w_rest, conv_w, conv_b, gate_tab)


def _in_proj_params(w_in, hy_conv_w, hy_conv_b, dn_conv_w, dn_a_log, dn_dt_bias):
    H = DN_HEADS
    w16 = w_in.astype(BF16)
    gc = w16[:, CONV_COLS + DN_W:]
    gc = jnp.concatenate([gc[:, 0:H], gc[:, 2 * H:3 * H], gc[:, H:2 * H], gc[:, 3 * H:]], axis=1)
    w_rest = jnp.concatenate([w16[:, CONV_COLS:CONV_COLS + DN_W],
                              jnp.pad(gc, ((0, 0), (0, LANES - N_GATE)))], axis=1)
    zero, one, pad = jnp.zeros((H,), F32), jnp.ones((H,), F32), jnp.zeros((LANES - N_GATE,), F32)
    gate_tab = jnp.stack([jnp.concatenate([jnp.exp(dn_a_log[0]), zero, jnp.exp(dn_a_log[1]), zero, pad]),
                          jnp.concatenate([dn_dt_bias[0], zero, dn_dt_bias[1], zero, pad]),
                          jnp.concatenate([one, zero, one, zero, pad])])
    return (w16[:, :CONV_COLS], w_rest, jnp.concatenate([hy_conv_w, dn_conv_w], axis=1),
            hy_conv_b.reshape(1, HY_IN), gate_tab)


def _out_proj_kernel(x_ref, yh_ref, of_ref, ob_ref, z_ref, nw_ref, wh_ref, wd_ref, g_ref, wrh_ref,
                     wrl_ref, br_ref, x1_ref, h_ref, lg_ref):
    o = of_ref[...] + ob_ref[...]
    heads = []
    for hd in range(DN_HEADS):
        oh = o[:, hd * DN_DV:(hd + 1) * DN_DV]
        heads.append(oh * lax.rsqrt(jnp.mean(oh * oh, axis=-1, keepdims=True) + EPS))
    z = z_ref[...]
    y_dn = jnp.concatenate(heads, axis=1) * nw_ref[...] * (z * jax.nn.sigmoid(z))
    x1 = (x_ref[...]
          + jnp.dot(yh_ref[...].astype(BF16), wh_ref[...], preferred_element_type=F32)
          + jnp.dot(y_dn.astype(BF16), wd_ref[...], preferred_element_type=F32))
    x1_ref[...] = x1
    h = _rms(x1, g_ref[...])
    h_hi = h.astype(BF16)
    h_lo = (h - h_hi.astype(F32)).astype(BF16)
    _store_pieces(h_ref, h)
    lg_ref[...] = (jnp.dot(h_hi, wrh_ref[...], preferred_element_type=F32)
                   + jnp.dot(h_lo, wrh_ref[...], preferred_element_type=F32)
                   + jnp.dot(h_hi, wrl_ref[...], preferred_element_type=F32)
                   + br_ref[...])


def _out_proj(x2d, y_hy, o_f, o_b, z, dn_norm_w, w_oh, w_od, g, wr_hi, wr_lo, br):
    T = x2d.shape[0]
    const = lambda i: (0, 0)
    row = lambda i: (i, 0)
    return pl.pallas_call(
        _out_proj_kernel,
        grid=(T // ROW_TILE,),
        in_specs=[pl.BlockSpec((ROW_TILE, D_MODEL), row),
                  pl.BlockSpec((ROW_TILE, HY_CH), row),
                  pl.BlockSpec((ROW_TILE, DN_W), row),
                  pl.BlockSpec((ROW_TILE, DN_W), row),
                  pl.BlockSpec((ROW_TILE, DN_W), row),
                  pl.BlockSpec((1, DN_W), const),
                  pl.BlockSpec((HY_CH, D_MODEL), const),
                  pl.BlockSpec((DN_W, D_MODEL), const),
                  pl.BlockSpec((1, D_MODEL), const),
                  pl.BlockSpec((D_MODEL, LANES), const),
                  pl.BlockSpec((D_MODEL, LANES), const),
                  pl.BlockSpec((1, LANES), const)],
        out_specs=[pl.BlockSpec((ROW_TILE, D_MODEL), row),
                   pl.BlockSpec((ROW_TILE * SC_PIECES, SC_ROW), row),
                   pl.BlockSpec((ROW_TILE, LANES), row)],
        out_shape=[jax.ShapeDtypeStruct((T, D_MODEL), F32),
                   jax.ShapeDtypeStruct((T * SC_PIECES, SC_ROW), F32),
                   jax.ShapeDtypeStruct((T, LANES), F32)],
        compiler_params=pltpu.CompilerParams(dimension_semantics=("parallel",),
                                             vmem_limit_bytes=VMEM_LIMIT),
        name="out_proj_router",
    )(x2d, y_hy, o_f, o_b, z, jnp.tile(dn_norm_w, DN_HEADS).reshape(1, DN_W), w_oh, w_od,
      g.reshape(1, D_MODEL), wr_hi, wr_lo, br)


def _deint_kernel(w_ref, p_ref, og_ref, ol_ref):
    half = DEINT_COLS // 2
    sel = jnp.dot(w_ref[0].astype(BF16), p_ref[...], preferred_element_type=F32)
    og_ref[0] = sel[:, :half].astype(BF16)
    ol_ref[0] = sel[:, half:].astype(BF16)


def _deinterleave(w_gate_up):
    half = DEINT_COLS // 2
    r = np.arange(DEINT_COLS)[:, None]
    c = np.arange(DEINT_COLS)[None, :]
    perm = jnp.asarray(np.where(c < half, r == 2 * c, r == 2 * (c - half) + 1), BF16)
    out = jax.ShapeDtypeStruct((N_EXPERTS, D_MODEL, D_FF), BF16)
    return pl.pallas_call(
        _deint_kernel,
        grid=(N_EXPERTS, 2 * D_FF // DEINT_COLS),
        in_specs=[pl.BlockSpec((1, D_MODEL, DEINT_COLS), lambda e, j: (e, 0, j)),
                  pl.BlockSpec((DEINT_COLS, DEINT_COLS), lambda e, j: (0, 0))],
        out_specs=[pl.BlockSpec((1, D_MODEL, half), lambda e, j: (e, 0, j)),
                   pl.BlockSpec((1, D_MODEL, half), lambda e, j: (e, 0, j))],
        out_shape=[out, out],
        compiler_params=pltpu.CompilerParams(dimension_semantics=("parallel", "parallel")),
        name="deinterleave_gate_up",
    )(w_gate_up, perm)


def _store_pieces(ref, x):
    rows = x.shape[0]
    for c in range(SC_PIECES):
        ref[pl.ds(c, rows, stride=SC_PIECES), :] = x[:, c * SC_ROW:(c + 1) * SC_ROW]


def _load_pieces(ref, rows):
    return jnp.concatenate([ref[pl.ds(c, rows, stride=SC_PIECES), :] for c in range(SC_PIECES)], axis=1)


def _sc_scatter(x, indices, n_out):
    n = indices.shape[0]
    rows, width = x.shape
    nsrc = rows // SC_WINDOW
    mesh = plsc.VectorSubcoreMesh(core_axis_name="core", subcore_axis_name="subcore")

    @functools.partial(pl.kernel, out_type=jax.ShapeDtypeStruct((n_out, width), x.dtype), mesh=mesh)
    def scatter(x_hbm, i_hbm, o_hbm):
        def body(x_vmem, i_vmem):
            pltpu.sync_copy(x_vmem, o_hbm.at[i_vmem.at[0]])

        pltpu.emit_pipeline(
            body,
            grid=(n // SC_WINDOW,),
            in_specs=[pl.BlockSpec((SC_WINDOW, width), index_map=lambda i: (i % nsrc, 0)),
                      pl.BlockSpec((1, SC_WINDOW), index_map=lambda i: (0, i))],
            out_specs=[],
            core_axis_name=("core", "subcore"),
            dimension_semantics=(pltpu.PARALLEL,),
        )(x_hbm, i_hbm)

    return scatter(x, indices.reshape(1, n))


def _expert_kernel(be_ref, nb_ref, nv_ref, xb_ref, wg_ref, wl_ref, bg_ref, bl_ref, wd_ref, bd_ref, y_ref):
    i = pl.program_id(0)

    @pl.when(i < nb_ref[0])
    def _():
        row_id = lax.broadcasted_iota(jnp.int32, (MOE_BLOCK, 1), 0)
        xb = jnp.where(row_id < nv_ref[i], _load_pieces(xb_ref, MOE_BLOCK), 0.0).astype(BF16)
        hg = jnp.dot(xb, wg_ref[0], preferred_element_type=F32) + bg_ref[0]
        hl = jnp.dot(xb, wl_ref[0], preferred_element_type=F32) + bl_ref[0]
        x_glu = jnp.minimum(hg, SWIGLU_LIMIT)
        x_lin = jnp.clip(hl, -SWIGLU_LIMIT, SWIGLU_LIMIT)
        act = x_glu * jax.nn.sigmoid(SWIGLU_ALPHA * x_glu) * (x_lin + 1.0)
        y = jnp.dot(act.astype(BF16), wd_ref[0], preferred_element_type=F32) + bd_ref[0]
        _store_pieces(y_ref, y)

    @pl.when(i >= nb_ref[0])
    def _():
        y_ref[...] = jnp.zeros_like(y_ref)


def _expert_mlp(xb, block_e, n_used, n_valid, wg, wl, bg, bl, wd, bd):
    n_blocks = xb.shape[0] // (MOE_BLOCK * SC_PIECES)
    n_rows = n_blocks * MOE_BLOCK
    rowm = lambda i, be, nb, nv: (i, 0)
    exp3 = lambda i, be, nb, nv: (be[i], 0, 0)
    grid_spec = pltpu.PrefetchScalarGridSpec(
        num_scalar_prefetch=3,
        grid=(n_blocks,),
        in_specs=[pl.BlockSpec((MOE_BLOCK * SC_PIECES, SC_ROW), rowm),
                  pl.BlockSpec((1, D_MODEL, D_FF), exp3),
                  pl.BlockSpec((1, D_MODEL, D_FF), exp3),
                  pl.BlockSpec((1, 1, D_FF), exp3),
                  pl.BlockSpec((1, 1, D_FF), exp3),
                  pl.BlockSpec((1, D_FF, D_MODEL), exp3),
                  pl.BlockSpec((1, 1, D_MODEL), exp3)],
        out_specs=pl.BlockSpec((MOE_BLOCK * SC_PIECES, SC_ROW), rowm),
    )
    return pl.pallas_call(
        _expert_kernel,
        grid_spec=grid_spec,
        out_shape=jax.ShapeDtypeStruct((n_rows * SC_PIECES, SC_ROW), F32),
        compiler_params=pltpu.CompilerParams(dimension_semantics=("arbitrary",),
                                             vmem_limit_bytes=VMEM_LIMIT),
        name="expert_mlp",
    )(block_e, n_used, n_valid, xb, wg, wl, bg, bl, wd, bd)


def _moe(h, logits, wg, wl, bg, bl, wd, bd):
    T = logits.shape[0]
    TK = T * TOP_K
    top_vals, top_idx = lax.top_k(logits, TOP_K)
    gates = jax.nn.softmax(top_vals, axis=-1)
    sel = jnp.sum(jax.nn.one_hot(top_idx, N_EXPERTS, dtype=jnp.int32), axis=1)
    before = jnp.cumsum(sel, axis=0) - sel
    counts = jnp.sum(sel, axis=0)
    padded = (counts + MOE_BLOCK - 1) // MOE_BLOCK * MOE_BLOCK
    pad_end = jnp.cumsum(padded)
    pad_start = pad_end - padded
    rank = jnp.take_along_axis(before, top_idx, axis=1)
    dest = (pad_start[top_idx] + rank).astype(jnp.int32)
    n_blocks = (TK + MOE_BLOCK - 1) // MOE_BLOCK + N_EXPERTS
    n_rows = n_blocks * MOE_BLOCK
    block_start = jnp.arange(n_blocks, dtype=jnp.int32) * MOE_BLOCK
    block_e = jnp.minimum(jnp.sum((block_start[:, None] >= pad_end[None, :]).astype(jnp.int32), axis=1),
                          N_EXPERTS - 1)
    n_used = (pad_end[-1] // MOE_BLOCK).astype(jnp.int32).reshape(1)
    n_valid = jnp.clip(counts[block_e] - (block_start - pad_start[block_e]), 0, MOE_BLOCK).astype(jnp.int32)
    idx = (dest.T[:, :, None] * SC_PIECES + jnp.arange(SC_PIECES, dtype=jnp.int32)[None, None, :]).reshape(-1)
    xb = _sc_scatter(h, idx, n_rows * SC_PIECES)
    yb = _expert_mlp(xb, block_e, n_used, n_valid, wg, wl, bg, bl, wd, bd)
    return yb, dest, gates


def _sc_gather(x, indices):
    n = indices.shape[0]
    width = x.shape[1]
    mesh = plsc.VectorSubcoreMesh(core_axis_name="core", subcore_axis_name="subcore")

    @functools.partial(pl.kernel, out_type=jax.ShapeDtypeStruct((n, width), x.dtype), mesh=mesh)
    def gather(x_hbm, i_hbm, o_hbm):
        def body(i_vmem, o_vmem):
            pltpu.sync_copy(x_hbm.at[i_vmem.at[0]], o_vmem)

        pltpu.emit_pipeline(
            body,
            grid=(n // SC_WINDOW,),
            in_specs=[pl.BlockSpec((1, SC_WINDOW), index_map=lambda i: (0, i))],
            out_specs=[pl.BlockSpec((SC_WINDOW, width), index_map=lambda i: (i, 0))],
            core_axis_name=("core", "subcore"),
            dimension_semantics=(pltpu.PARALLEL,),
        )(i_hbm, o_hbm)

    return gather(x, indices.reshape(1, n))


def _combine_kernel(gate_ref, x1_ref, g_ref, y4_ref, o_ref):
    tm = x1_ref.shape[0]
    pieces = SC_PIECES
    cols = []
    for c in range(pieces):
        acc = x1_ref[:, c * SC_ROW:(c + 1) * SC_ROW]
        for k in range(TOP_K):
            acc = acc + gate_ref[:, k:k + 1] * y4_ref[pl.ds((k * pieces + c) * tm, tm), :]
        cols.append(acc)
    o_ref[...] = _rms(jnp.concatenate(cols, axis=1), g_ref[...])


def _combine_final(yb, dest, gates, x1, g):
    T = x1.shape[0]
    tm = COMBINE_TILE
    pieces = SC_PIECES
    d = dest.reshape(T // tm, tm, TOP_K).transpose(0, 2, 1)
    idx = (d[:, :, None, :] * pieces + jnp.arange(pieces, dtype=jnp.int32)[None, None, :, None]).reshape(-1)
    y4 = _sc_gather(yb, idx)
    row = lambda i: (i, 0)
    return pl.pallas_call(
        _combine_kernel,
        grid=(T // tm,),
        in_specs=[pl.BlockSpec((tm, TOP_K), row),
                  pl.BlockSpec((tm, D_MODEL), row),
                  pl.BlockSpec((1, D_MODEL), lambda i: (0, 0)),
                  pl.BlockSpec((tm * TOP_K * pieces, SC_ROW), row)],
        out_specs=pl.BlockSpec((tm, D_MODEL), row),
        out_shape=jax.ShapeDtypeStruct((T, D_MODEL), F32),
        compiler_params=pltpu.CompilerParams(dimension_semantics=("parallel",),
                                             vmem_limit_bytes=VMEM_LIMIT),
        name="moe_combine_final",
    )(gates, x1, g.reshape(1, D_MODEL), y4)


def _fft_tables(nc, nf):
    n = nc * nf
    kc = np.arange(nc, dtype=np.float64)
    a1 = 2.0 * np.pi * np.outer(kc, np.arange(nc // 2)) / nc
    c1, s1 = np.cos(a1), np.sin(a1)
    l1 = np.block([[c1, s1], [-s1, c1]])
    a2 = 2.0 * np.pi * np.outer(np.arange(nf), np.arange(nf)) / nf
    c2, s2 = np.cos(a2), np.sin(a2)
    m2 = np.block([[c2, s2], [-s2, c2]])
    m2i = np.block([[c2, -s2], [s2, c2]])
    a3 = 2.0 * np.pi * np.outer(np.arange(nc // 2), kc) / nc
    c3, s3 = np.cos(a3), np.sin(a3)
    l3 = np.block([[c3, -s3], [s3, c3]]) / n
    return tuple(jnp.asarray(m, F32).astype(BF16) for m in (l1, m2, m2i, l3))


def _twiddle(nc, nf):
    n = nc * nf
    ph = (jnp.arange(nf, dtype=jnp.int32)[:, None] * jnp.arange(nc, dtype=jnp.int32)[None, :]) % n
    ang = ph.astype(F32) * (2.0 * math.pi / n)
    tw = jnp.stack([jnp.cos(ang), jnp.sin(ang)], axis=1)
    return jnp.broadcast_to(tw[..., None], (nf, 2, nc, LANES))


def _lane_tile(t, width):
    return t if width == LANES else jnp.concatenate([t] * (width // LANES), axis=1)


def _fft1_kernel(z_ref, l1_ref, tw_ref, o_ref, *, nc):
    cb = o_ref.shape[-1]
    for j in range(SUB):
        rhs = jnp.concatenate([z_ref[0, :, j, :], z_ref[1, :, j, :]], axis=0).astype(BF16)
        a = jnp.dot(l1_ref[...], rhs, preferred_element_type=F32)
        ar, ai = a[:nc], a[nc:]
        twr = _lane_tile(tw_ref[j, 0], cb)
        twi = _lane_tile(tw_ref[j, 1], cb)
        o_ref[j, 0] = ar * twr + ai * twi
        o_ref[j, 1] = ai * twr - ar * twi


def _fft1(z, l1, tw, nc, nf):
    P, C = z.shape[0], z.shape[-1]
    cb = FFT_CB
    z_spec = pl.BlockSpec((None, 2, nc // 2, SUB, cb), lambda p, j, c: (p, 0, 0, j, c))
    return pl.pallas_call(
        functools.partial(_fft1_kernel, nc=nc),
        grid=(P, nf // SUB, C // cb),
        in_specs=[z_spec,
                  pl.BlockSpec(l1.shape, lambda p, j, c: (0, 0)),
                  pl.BlockSpec((SUB, 2, nc, LANES), lambda p, j, c: (j, 0, 0, 0))],
        out_specs=pl.BlockSpec((None, SUB, 2, nc, cb), lambda p, j, c: (p, j, 0, 0, c)),
        out_shape=jax.ShapeDtypeStruct((P, nf, 2, nc, C), F32),
        compiler_params=pltpu.CompilerParams(dimension_semantics=("parallel", "parallel", "parallel"),
                                             vmem_limit_bytes=VMEM_LIMIT),
        name="hyena_fft_stage1",
    )(z, l1, tw)


def _fft2_filter_kernel(a_ref, m2_ref, k_ref, *, nf):
    for j in range(SUB):
        rhs = jnp.concatenate([a_ref[:, 0, j, :], a_ref[:, 1, j, :]], axis=0).astype(BF16)
        x = jnp.dot(m2_ref[...], rhs, preferred_element_type=F32)
        k_ref[j, 0] = x[:nf]
        k_ref[j, 1] = x[nf:]


def _fft2_filter(a, m2, nc, nf):
    C = a.shape[-1]
    cb = FFT_CB
    return pl.pallas_call(
        functools.partial(_fft2_filter_kernel, nf=nf),
        grid=(nc // SUB, C // cb),
        in_specs=[pl.BlockSpec((None, nf, 2, SUB, cb), lambda k, c: (0, 0, 0, k, c)),
                  pl.BlockSpec(m2.shape, lambda k, c: (0, 0))],
        out_specs=pl.BlockSpec((SUB, 2, nf, cb), lambda k, c: (k, 0, 0, c)),
        out_shape=jax.ShapeDtypeStruct((nc, 2, nf, C), F32),
        compiler_params=pltpu.CompilerParams(dimension_semantics=("parallel", "parallel"),
                                             vmem_limit_bytes=VMEM_LIMIT),
        name="hyena_filter_spectrum",
    )(a, m2)


def _fft2_kernel(a_ref, k_ref, m2_ref, m2i_ref, o_ref, *, nf):
    for j in range(SUB):
        rhs = jnp.concatenate([a_ref[:, 0, j, :], a_ref[:, 1, j, :]], axis=0).astype(BF16)
        x = jnp.dot(m2_ref[...], rhs, preferred_element_type=F32)
        xr, xi = x[:nf], x[nf:]
        kr, ki = k_ref[j, 0], k_ref[j, 1]
        y = jnp.concatenate([xr * kr - xi * ki, xr * ki + xi * kr], axis=0).astype(BF16)
        b = jnp.dot(m2i_ref[...], y, preferred_element_type=F32)
        o_ref[:, 0, j, :] = b[:nf]
        o_ref[:, 1, j, :] = b[nf:]


def _fft2(a, kspec, m2, m2i, nc, nf):
    P, C = a.shape[0], a.shape[-1]
    cb = FFT_CB
    blk = pl.BlockSpec((None, nf, 2, SUB, cb), lambda p, k, c: (p, 0, 0, k, c))
    return pl.pallas_call(
        functools.partial(_fft2_kernel, nf=nf),
        grid=(P, nc // SUB, C // cb),
        in_specs=[blk,
                  pl.BlockSpec((SUB, 2, nf, cb), lambda p, k, c: (k, 0, 0, c)),
                  pl.BlockSpec(m2.shape, lambda p, k, c: (0, 0)),
                  pl.BlockSpec(m2i.shape, lambda p, k, c: (0, 0))],
        out_specs=blk,
        out_shape=jax.ShapeDtypeStruct(a.shape, F32),
        compiler_params=pltpu.CompilerParams(dimension_semantics=("parallel", "parallel", "parallel"),
                                             vmem_limit_bytes=VMEM_LIMIT),
        name="hyena_fft_stage2",
    )(a, kspec, m2, m2i)


def _fft3_kernel(b_ref, tw_ref, l3_ref, s_ref, x0_ref, bias_ref, o_ref, *, nc):
    cb = o_ref.shape[-1]
    half = nc // 2
    for j in range(SUB):
        br, bi = b_ref[j, 0], b_ref[j, 1]
        twr = _lane_tile(tw_ref[j, 0], cb)
        twi = _lane_tile(tw_ref[j, 1], cb)
        rhs = jnp.concatenate([br * twr - bi * twi, br * twi + bi * twr], axis=0).astype(BF16)
        y = jnp.dot(l3_ref[...], rhs, preferred_element_type=F32)
        for q in range(2):
            s = s_ref[q, :, j, :]
            o_ref[q, :, j, :] = x0_ref[q, :, j, :] * (y[q * half:(q + 1) * half] + s * bias_ref[...])


def _fft3(b, tw, l3, s5, x05, bias, nc, nf):
    P, C = b.shape[0], b.shape[-1]
    cb = FFT_CB
    seq = pl.BlockSpec((None, 2, nc // 2, SUB, cb), lambda p, j, c: (p, 0, 0, j, c))
    return pl.pallas_call(
        functools.partial(_fft3_kernel, nc=nc),
        grid=(P, nf // SUB, C // cb),
        in_specs=[pl.BlockSpec((None, SUB, 2, nc, cb), lambda p, j, c: (p, j, 0, 0, c)),
                  pl.BlockSpec((SUB, 2, nc, LANES), lambda p, j, c: (j, 0, 0, 0)),
                  pl.BlockSpec(l3.shape, lambda p, j, c: (0, 0)),
                  seq, seq,
                  pl.BlockSpec((1, cb), lambda p, j, c: (0, c))],
        out_specs=seq,
        out_shape=jax.ShapeDtypeStruct(s5.shape, F32),
        compiler_params=pltpu.CompilerParams(dimension_semantics=("parallel", "parallel", "parallel"),
                                             vmem_limit_bytes=VMEM_LIMIT),
        name="hyena_fft_stage3",
    )(b, tw, l3, s5, x05, bias)


def _fft_split(n):
    nf = 1 << (int(math.log2(n)) // 2)
    return n // nf, nf


def _filter_tables(nc):
    half = nc // 2
    kc = np.arange(nc, dtype=np.float64)[:, None]
    r = np.arange(half, dtype=np.float64)
    mats = []
    for rows_b, drop0 in ((half + (half - 1 - r), False), (nc - r, True)):
        a = 2.0 * np.pi * kc * np.concatenate([r, rows_b])[None, :] / nc
        m = np.concatenate([np.cos(a), -np.sin(a)], axis=0)
        if drop0:
            m[:, half] = 0.0
        mats.append(m)
    return tuple(jnp.asarray(m, F32).astype(BF16) for m in mats)


def _fft1_filter_kernel(hf_ref, hba_ref, hbb_ref, mb_ref, mb0_ref, tw_ref, o_ref, *, nc):
    cb = o_ref.shape[-1]
    m_first = jnp.where(pl.program_id(0) == 0, mb0_ref[...], mb_ref[...])
    for j in range(SUB):
        src = hbb_ref[:, 0, :] if j == 0 else hba_ref[:, SUB - j, :]
        rhs = jnp.concatenate([hf_ref[:, j, :], src], axis=0).astype(BF16)
        a = jnp.dot(m_first if j == 0 else mb_ref[...], rhs, preferred_element_type=F32)
        ar, ai = a[:nc], a[nc:]
        twr = _lane_tile(tw_ref[j, 0], cb)
        twi = _lane_tile(tw_ref[j, 1], cb)
        o_ref[j, 0] = ar * twr + ai * twi
        o_ref[j, 1] = ai * twr - ar * twi


def _fft1_filter(h_f, h_b, mb, mb0, tw, nc, nf):
    C = h_f.shape[-1]
    cb = FFT_CB
    nblk = nf // SUB
    hf3 = h_f.reshape(nc // 2, nf, C)
    hb3 = h_b.reshape(nc // 2, nf, C)
    blk = lambda f: pl.BlockSpec((nc // 2, SUB, cb), f)
    return pl.pallas_call(
        functools.partial(_fft1_filter_kernel, nc=nc),
        grid=(nblk, C // cb),
        in_specs=[blk(lambda j, c: (0, j, c)),
                  blk(lambda j, c: (0, nblk - 1 - j, c)),
                  blk(lambda j, c: (0, (nblk - j) % nblk, c)),
                  pl.BlockSpec(mb.shape, lambda j, c: (0, 0)),
                  pl.BlockSpec(mb0.shape, lambda j, c: (0, 0)),
                  pl.BlockSpec((SUB, 2, nc, LANES), lambda j, c: (j, 0, 0, 0))],
        out_specs=pl.BlockSpec((None, SUB, 2, nc, cb), lambda j, c: (0, j, 0, 0, c)),
        out_shape=jax.ShapeDtypeStruct((1, nf, 2, nc, C), F32),
        compiler_params=pltpu.CompilerParams(dimension_semantics=("parallel", "parallel"),
                                             vmem_limit_bytes=VMEM_LIMIT),
        name="hyena_filter_stage1",
    )(hf3, hb3, hb3, mb, mb0, tw)


def _hyena_conv(s, x0, h_f, h_b, bias):
    B, L, C = s.shape
    nc, nf = _fft_split(2 * L)
    tw = _twiddle(nc, nf)
    l1, m2, m2i, l3 = _fft_tables(nc, nf)
    mb, mb0 = _filter_tables(nc)
    kspec = _fft2_filter(_fft1_filter(h_f, h_b, mb, mb0, tw, nc, nf), m2, nc, nf)
    s5 = s.reshape(B // 2, 2, nc // 2, nf, C)
    x05 = x0.reshape(B // 2, 2, nc // 2, nf, C)
    a = _fft1(s5, l1, tw, nc, nf)
    b = _fft2(a, kspec, m2, m2i, nc, nf)
    return _fft3(b, tw, l3, s5, x05, bias.reshape(1, C), nc, nf).reshape(B, L, C)


def _hyena_filter(L, w1, b1, w2, b2, w3, freq):
    pos = jnp.arange(L, dtype=F32)
    t = jnp.linspace(0.0, 1.0, L, dtype=F32)[:, None]
    bands = jnp.linspace(1e-4, HY_BANDS - 1, HY_BANDS, dtype=F32)
    ang = (2.0 * math.pi / L) * pos[:, None] * bands[None, :]
    z = jnp.concatenate([t, jnp.cos(ang), -jnp.sin(ang)], axis=-1)
    h = jnp.sin(freq * (z @ w1 + b1))
    h = jnp.sin(freq * (h @ w2 + b2))
    h = h @ w3
    deltas = jnp.abs(jnp.linspace(math.log(HY_DECAY_TARGET) / HY_SLOW_PCT,
                                  math.log(HY_DECAY_TARGET) / HY_FAST_PCT, HY_CH, dtype=F32))
    window = jnp.exp(-t * deltas[None, :])
    h_f = h[:, :HY_CH] * window
    h_b = h[:, HY_CH:] * window
    l1 = jnp.sum(jnp.abs(h_f), axis=0) + jnp.sum(jnp.abs(h_b[1:]), axis=0)
    return h_f / l1, h_b / l1


def _split3(x):
    x1 = x.astype(BF16)
    r = x - x1.astype(F32)
    x2 = r.astype(BF16)
    x3 = (r - x2.astype(F32)).astype(BF16)
    return x1, x2, x3


def _mm(a, b):
    return jnp.dot(a.astype(BF16), b.astype(BF16), preferred_element_type=F32)


def _delta_kernel(qf_ref, kf_ref, vf_ref, gf_ref, qb_ref, kb_ref, vb_ref, gb_ref, of_ref, ob_ref, s_ref,
                  *, n_sub):
    @pl.when(pl.program_id(1) == 0)
    def _():
        s_ref[...] = jnp.zeros_like(s_ref)

    row = lax.broadcasted_iota(jnp.int32, (CHUNK, CHUNK), 0)
    col = lax.broadcasted_iota(jnp.int32, (CHUNK, CHUNK), 1)
    eye_f = jnp.where(row == col, 1.0, 0.0).astype(F32)
    dirs = ((qf_ref, kf_ref, vf_ref, gf_ref, of_ref, row >= col, row > col),
            (qb_ref, kb_ref, vb_ref, gb_ref, ob_ref, row <= col, row < col))

    ch = []
    for d, (q_ref, k_ref, v_ref, g_ref, _, incl, strict) in enumerate(dirs):
        tri = jnp.where(incl, 1.0, 0.0).astype(BF16)
        for c in range(n_sub):
            rows = slice(c * CHUNK, (c + 1) * CHUNK)
            gates = g_ref[rows, :]
            gl = 2 * DN_HEADS * d
            g1, g2, g3 = _split3(gates)
            gc_all = (jnp.dot(tri, g1, preferred_element_type=F32)
                      + jnp.dot(tri, g2, preferred_element_type=F32)
                      + jnp.dot(tri, g3, preferred_element_type=F32))
            gsum_all = jnp.sum(gates, axis=0, keepdims=True)
            for h in range(DN_HEADS):
                lanes = slice(h * DN_DK, (h + 1) * DN_DK)
                kh = k_ref[rows, lanes]
                beta = gates[:, gl + DN_HEADS + h:gl + DN_HEADS + h + 1]
                gc = gc_all[:, gl + h:gl + h + 1]
                g_last = gsum_all[:, gl + h:gl + h + 1]
                gc_b = jnp.broadcast_to(gc, (CHUNK, CHUNK))
                gc_row = jnp.sum(gc_b * eye_f, axis=0, keepdims=True)
                decay = jnp.where(incl, jnp.exp(jnp.minimum(gc_b - gc_row, 0.0)), 0.0)
                e_gc = jnp.exp(gc)
                kb = kh * beta
                ch.append(dict(d=d, c=c, h=h, lanes=lanes, rows=rows, strict=strict, decay=decay,
                               kh16=kh.astype(BF16), kb16=kb.astype(BF16),
                               q16=q_ref[rows, lanes].astype(BF16),
                               rhs=jnp.concatenate([v_ref[rows, lanes] * beta, kb * e_gc],
                                                   axis=1).astype(BF16),
                               qe=q_ref[rows, lanes] * e_gc,
                               kdT=(kh * jnp.exp(g_last - gc)).T.astype(BF16),
                               e_last=jnp.exp(g_last)))
    for x in ch:
        a = lax.dot_general(x["kb16"], x["kh16"], _NT, preferred_element_type=F32) * x["decay"]
        x["a"] = jnp.where(x["strict"], a, 0.0)
        x["qk"] = (lax.dot_general(x["q16"], x["kh16"], _NT, preferred_element_type=F32)
                   * x["decay"]).astype(BF16)
    for x in ch:
        x["t"] = eye_f - x["a"]
        x["p"] = _mm(x["a"], x["a"])
    for level in range(5):
        for x in ch:
            p16 = x["p"].astype(BF16)
            x["t"] = x["t"] + jnp.dot(x["t"].astype(BF16), p16, preferred_element_type=F32)
            if level < 4:
                x["p"] = jnp.dot(p16, p16, preferred_element_type=F32)
    for x in ch:
        uw = jnp.dot(x["t"].astype(BF16), x["rhs"], preferred_element_type=F32)
        x["u"] = uw[:, :DN_DV]
        x["wq"] = jnp.concatenate([uw[:, DN_DV:], x["qe"]], axis=0).astype(BF16)

    for step in range(n_sub):
        cur = [x for x in ch if x["c"] == (step if x["d"] == 0 else n_sub - 1 - step)]
        for x in cur:
            x["s"] = s_ref[x["d"], x["h"]]
            x["ws"] = jnp.dot(x["wq"], x["s"].astype(BF16), preferred_element_type=F32)
        for x in cur:
            x["vn"] = (x["u"] - x["ws"][:CHUNK]).astype(BF16)
        for x in cur:
            o = x["ws"][CHUNK:] + jnp.dot(x["qk"], x["vn"], preferred_element_type=F32)
            dirs[x["d"]][4][x["rows"], x["lanes"]] = o
            s_ref[x["d"], x["h"]] = (x["s"] * x["e_last"]
                                     + jnp.dot(x["kdT"], x["vn"], preferred_element_type=F32))


def _delta_scan(q, k, v, gates):
    B, L, _ = q.shape
    n_sub = DN_STEP_CHUNKS
    rows = n_sub * CHUNK
    nblk = L // rows
    fwd = pl.BlockSpec((None, rows, DN_W), lambda b, j: (b, j, 0))
    bwd = pl.BlockSpec((None, rows, DN_W), lambda b, j: (b, nblk - 1 - j, 0))
    gfwd = pl.BlockSpec((None, rows, LANES), lambda b, j: (b, j, 0))
    gbwd = pl.BlockSpec((None, rows, LANES), lambda b, j: (b, nblk - 1 - j, 0))
    out = jax.ShapeDtypeStruct((B, L, DN_W), F32)
    return pl.pallas_call(
        functools.partial(_delta_kernel, n_sub=n_sub),
        grid=(B, nblk),
        in_specs=[fwd, fwd, fwd, gfwd, bwd, bwd, bwd, gbwd],
        out_specs=[fwd, bwd],
        out_shape=[out, out],
        scratch_shapes=[pltpu.VMEM((2, DN_HEADS, DN_DK, DN_DV), F32)],
        compiler_params=pltpu.CompilerParams(dimension_semantics=("parallel", "arbitrary"),
                                             vmem_limit_bytes=VMEM_LIMIT),
        name="delta_scan",
    )(q, k, v, gates, q, k, v, gates)


def _trunk(x, w, hy_pos_w1, hy_pos_b1, hy_pos_w2, hy_pos_b2, hy_pos_w3, hy_sin_freq, hy_bias, dn_norm_w):
    B, L, D = x.shape
    T = B * L
    x2d = x.reshape(T, D)
    x0, s, q, k, v, z, dn_gates = _in_proj(x2d, L, w["norm_mix"], *w["in_proj"])
    seq = lambda a: a.reshape(B, L, a.shape[-1])
    h_f, h_b = _hyena_filter(L, hy_pos_w1, hy_pos_b1, hy_pos_w2, hy_pos_b2, hy_pos_w3, hy_sin_freq)
    y_hy = _hyena_conv(seq(s), seq(x0), h_f, h_b, hy_bias)
    o_f, o_b = _delta_scan(seq(q), seq(k), seq(v), seq(dn_gates))
    x1, h_ffn, logits = _out_proj(x2d, y_hy.reshape(T, HY_CH), o_f.reshape(T, DN_W), o_b.reshape(T, DN_W),
                                 z, dn_norm_w, w["out_hy"], w["out_dn"], w["norm_ffn"],
                                 w["router_hi"], w["router_lo"], w["router_b"])
    yb, dest, gates = _moe(h_ffn, logits[:, :N_EXPERTS], w["wg"], w["wl"], w["bg"], w["bl"], w["wd"],
                           w["bd"])
    return _combine_final(yb, dest, gates, x1, w["norm_final"]).reshape(B, L, D)


def kernel(x_prompt, x_sample, norm_mix_w, w_in, hy_conv_w, hy_conv_b, hy_pos_w1, hy_pos_b1, hy_pos_w2, hy_pos_b2, hy_pos_w3, hy_sin_freq, hy_bias, dn_conv_w, dn_a_log, dn_dt_bias, dn_norm_w, w_out, norm_ffn_w, w_router, b_router, w_gate_up, b_gate_up, w_down, b_down, norm_final_w):
    w_out16 = w_out[0].astype(BF16)
    wr = jnp.pad(w_router[0], ((0, 0), (0, LANES - N_EXPERTS)))
    wr_hi = wr.astype(BF16)
    wg, wl = _deinterleave(w_gate_up[0])
    w = {
        "norm_mix": norm_mix_w[0],
        "in_proj": _in_proj_params(w_in[0], hy_conv_w[0], hy_conv_b[0], dn_conv_w[0], dn_a_log[0],
                                   dn_dt_bias[0]),
        "out_hy": w_out16[:HY_CH],
        "out_dn": w_out16[HY_CH:],
        "norm_ffn": norm_ffn_w[0],
        "router_hi": wr_hi,
        "router_lo": (wr - wr_hi.astype(F32)).astype(BF16),
        "router_b": jnp.pad(b_router[0], (0, LANES - N_EXPERTS)).reshape(1, LANES),
        "wg": wg,
        "wl": wl,
        "bg": b_gate_up[0][:, 0::2].reshape(N_EXPERTS, 1, D_FF),
        "bl": b_gate_up[0][:, 1::2].reshape(N_EXPERTS, 1, D_FF),
        "wd": w_down[0].astype(BF16),
        "bd": b_down[0].reshape(N_EXPERTS, 1, D_MODEL),
        "norm_final": norm_final_w,
    }
    mix = (hy_pos_w1[0], hy_pos_b1[0], hy_pos_w2[0], hy_pos_b2[0], hy_pos_w3[0], hy_sin_freq[0],
           hy_bias[0], dn_norm_w[0])
    return (_trunk(x_prompt, w, *mix), _trunk(x_sample, w, *mix))
```

```python
import functools
import math

import jax
import jax.numpy as jnp
import numpy as np
from jax import lax
from jax.experimental import pallas as pl
from jax.experimental.pallas import tpu as pltpu
from jax.experimental.pallas import tpu_sc as plsc

D_MODEL = 1024
HY_CH = 512
DN_HEADS = 4
DN_DK = 128
DN_DV = 128
DN_QK = DN_HEADS * DN_DK
DN_W = DN_HEADS * DN_DV
HY_IN = 3 * HY_CH
DN_CONV = 2 * DN_QK + DN_W
N_GATE = 4 * DN_HEADS
SHORT_CONV = 3
CONV_COLS = HY_IN + DN_CONV
REST_COLS = DN_W + 128
HY_EMB = 33
HY_BANDS = (HY_EMB - 1) // 2
HY_DECAY_TARGET = 1e-2
HY_FAST_PCT = 0.3
HY_SLOW_PCT = 1.5
CHUNK = 64
N_EXPERTS = 32
TOP_K = 4
D_FF = D_MODEL
SWIGLU_ALPHA = 1.702
SWIGLU_LIMIT = 7.0
MOE_BLOCK = 512
EPS = 1e-6

LANES = 128
SUB = 8
ROW_TILE = 512
FFT_CB = 256
DN_STEP_CHUNKS = 4
DEINT_COLS = 512
COMBINE_TILE = 512
SC_WINDOW = 128
SC_ROW = LANES
SC_PIECES = D_MODEL // SC_ROW
VMEM_LIMIT = 56 * 1024 * 1024

F32 = jnp.float32
BF16 = jnp.bfloat16

_NT = (((1,), (1,)), ((), ()))


def _rms(x, g):
    return x * lax.rsqrt(jnp.mean(x * x, axis=-1, keepdims=True) + EPS) * g


def _silu(x):
    return x * jax.nn.sigmoid(x)


def _head_l2norm(x):
    parts = []
    for hd in range(DN_HEADS):
        xh = x[:, hd * DN_DK:(hd + 1) * DN_DK]
        parts.append(xh * lax.rsqrt(jnp.sum(xh * xh, axis=-1, keepdims=True) + EPS))
    return jnp.concatenate(parts, axis=1)


def _in_proj_kernel(xp_ref, x_ref, xn_ref, g_ref, wc_ref, wr_ref, cw_ref, cb_ref, gt_ref,
                    x0_ref, s_ref, q_ref, k_ref, v_ref, z_ref, gate_ref, p_scr, *, tiles_per_seq):
    i = pl.program_id(0)
    first = (i % tiles_per_seq) == 0
    last = (i % tiles_per_seq) == tiles_per_seq - 1
    g = g_ref[...]
    hp = jnp.where(first, 0.0, _rms(xp_ref[...], g))
    hn = jnp.where(last, 0.0, _rms(xn_ref[...], g))
    h_all = jnp.concatenate([hp, _rms(x_ref[...], g), hn], axis=0).astype(BF16)
    h = h_all[SUB:SUB + ROW_TILE]

    def project(c0):
        cols = slice(c0, c0 + HY_CH)
        p_scr[:, cols] = jnp.dot(h_all, wc_ref[:, cols], preferred_element_type=F32)

    def conv(c0):
        cols = slice(c0, c0 + HY_CH)
        return (p_scr[pl.ds(SUB - 1, ROW_TILE), cols] * cw_ref[0:1, cols]
                + p_scr[pl.ds(SUB, ROW_TILE), cols] * cw_ref[1:2, cols]
                + p_scr[pl.ds(SUB + 1, ROW_TILE), cols] * cw_ref[2:3, cols])

    project(0)
    project(HY_CH)
    x0_ref[...] = conv(0) + cb_ref[:, 0:HY_CH]
    project(2 * HY_CH)
    project(HY_IN)
    s_ref[...] = (conv(HY_CH) + cb_ref[:, HY_CH:2 * HY_CH]) * (conv(2 * HY_CH) + cb_ref[:, 2 * HY_CH:])
    project(HY_IN + DN_QK)
    q_ref[...] = _head_l2norm(_silu(conv(HY_IN))) * (DN_DK ** -0.5)
    project(HY_IN + 2 * DN_QK)
    k_ref[...] = _head_l2norm(_silu(conv(HY_IN + DN_QK)))
    rest = jnp.dot(h, wr_ref[...], preferred_element_type=F32)
    v_ref[...] = _silu(conv(HY_IN + 2 * DN_QK))
    z_ref[...] = rest[:, :DN_W]
    a = rest[:, DN_W:] + gt_ref[1:2]
    softplus = jnp.maximum(a, 0.0) + jnp.log(1.0 + jnp.exp(-jnp.abs(a)))
    gate_ref[...] = jnp.where(gt_ref[2:3] > 0.5, -gt_ref[0:1] * softplus, jax.nn.sigmoid(rest[:, DN_W:]))


def _in_proj(x2d, seq_len, g, w_conv, w_rest, conv_w, conv_b, gate_tab):
    T = x2d.shape[0]
    per = ROW_TILE // SUB
    last_sub = T // SUB - 1
    const = lambda i: (0, 0)
    row = lambda i: (i, 0)
    o512 = pl.BlockSpec((ROW_TILE, HY_CH), row)
    s512 = jax.ShapeDtypeStruct((T, HY_CH), F32)
    return pl.pallas_call(
        functools.partial(_in_proj_kernel, tiles_per_seq=seq_len // ROW_TILE),
        grid=(T // ROW_TILE,),
        in_specs=[pl.BlockSpec((SUB, D_MODEL), lambda i: (jnp.maximum(i * per - 1, 0), 0)),
                  pl.BlockSpec((ROW_TILE, D_MODEL), row),
                  pl.BlockSpec((SUB, D_MODEL), lambda i: (jnp.minimum((i + 1) * per, last_sub), 0)),
                  pl.BlockSpec((1, D_MODEL), const),
                  pl.BlockSpec((D_MODEL, CONV_COLS), const),
                  pl.BlockSpec((D_MODEL, REST_COLS), const),
                  pl.BlockSpec((SHORT_CONV, CONV_COLS), const),
                  pl.BlockSpec((1, HY_IN), const),
                  pl.BlockSpec((3, LANES), const)],
        out_specs=[o512, o512, o512, o512, o512, o512, pl.BlockSpec((ROW_TILE, LANES), row)],
        out_shape=[s512, s512, s512, s512, s512, s512, jax.ShapeDtypeStruct((T, LANES), F32)],
        scratch_shapes=[pltpu.VMEM((ROW_TILE + 2 * SUB, CONV_COLS), F32)],
        compiler_params=pltpu.CompilerParams(dimension_semantics=("parallel",),
                                             vmem_limit_bytes=VMEM_LIMIT),
        name="in_proj",
    )(x2d, x2d, x2d, g.reshape(1, D_MODEL), w_conv, w_rest, conv_w, conv_b, gate_tab)


def _in_proj_params(w_in, hy_conv_w, hy_conv_b, dn_conv_w, dn_a_log, dn_dt_bias):
    H = DN_HEADS
    w16 = w_in.astype(BF16)
    gc = w16[:, CONV_COLS + DN_W:]
    gc = jnp.concatenate([gc[:, 0:H], gc[:, 2 * H:3 * H], gc[:, H:2 * H], gc[:, 3 * H:]], axis=1)
    w_rest = jnp.concatenate([w16[:, CONV_COLS:CONV_COLS + DN_W],
                              jnp.pad(gc, ((0, 0), (0, LANES - N_GATE)))], axis=1)
    zero, one, pad = jnp.zeros((H,), F32), jnp.ones((H,), F32), jnp.zeros((LANES - N_GATE,), F32)
    gate_tab = jnp.stack([jnp.concatenate([jnp.exp(dn_a_log[0]), zero, jnp.exp(dn_a_log[1]), zero, pad]),
                          jnp.concatenate([dn_dt_bias[0], zero, dn_dt_bias[1], zero, pad]),
                          jnp.concatenate([one, zero, one, zero, pad])])
    return (w16[:, :CONV_COLS], w_rest, jnp.concatenate([hy_conv_w, dn_conv_w], axis=1),
            hy_conv_b.reshape(1, HY_IN), gate_tab)


def _out_proj_kernel(x_ref, yh_ref, of_ref, ob_ref, z_ref, nw_ref, wh_ref, wd_ref, g_ref, wrh_ref,
                     wrl_ref, br_ref, x1_ref, h_ref, lg_ref):
    o = of_ref[...] + ob_ref[...]
    heads = []
    for hd in range(DN_HEADS):
        oh = o[:, hd * DN_DV:(hd + 1) * DN_DV]
        heads.append(oh * lax.rsqrt(jnp.mean(oh * oh, axis=-1, keepdims=True) + EPS))
    z = z_ref[...]
    y_dn = jnp.concatenate(heads, axis=1) * nw_ref[...] * (z * jax.nn.sigmoid(z))
    x1 = (x_ref[...]
          + jnp.dot(yh_ref[...].astype(BF16), wh_ref[...], preferred_element_type=F32)
          + jnp.dot(y_dn.astype(BF16), wd_ref[...], preferred_element_type=F32))
    x1_ref[...] = x1
    h = _rms(x1, g_ref[...])
    h_hi = h.astype(BF16)
    h_lo = (h - h_hi.astype(F32)).astype(BF16)
    _store_pieces(h_ref, h)
    lg_ref[...] = (jnp.dot(h_hi, wrh_ref[...], preferred_element_type=F32)
                   + jnp.dot(h_lo, wrh_ref[...], preferred_element_type=F32)
                   + jnp.dot(h_hi, wrl_ref[...], preferred_element_type=F32)
                   + br_ref[...])


def _out_proj(x2d, y_hy, o_f, o_b, z, dn_norm_w, w_oh, w_od, g, wr_hi, wr_lo, br):
    T = x2d.shape[0]
    const = lambda i: (0, 0)
    row = lambda i: (i, 0)
    return pl.pallas_call(
        _out_proj_kernel,
        grid=(T // ROW_TILE,),
        in_specs=[pl.BlockSpec((ROW_TILE, D_MODEL), row),
                  pl.BlockSpec((ROW_TILE, HY_CH), row),
                  pl.BlockSpec((ROW_TILE, DN_W), row),
                  pl.BlockSpec((ROW_TILE, DN_W), row),
                  pl.BlockSpec((ROW_TILE, DN_W), row),
                  pl.BlockSpec((1, DN_W), const),
                  pl.BlockSpec((HY_CH, D_MODEL), const),
                  pl.BlockSpec((DN_W, D_MODEL), const),
                  pl.BlockSpec((1, D_MODEL), const),
                  pl.BlockSpec((D_MODEL, LANES), const),
                  pl.BlockSpec((D_MODEL, LANES), const),
                  pl.BlockSpec((1, LANES), const)],
        out_specs=[pl.BlockSpec((ROW_TILE, D_MODEL), row),
                   pl.BlockSpec((ROW_TILE * SC_PIECES, SC_ROW), row),
                   pl.BlockSpec((ROW_TILE, LANES), row)],
        out_shape=[jax.ShapeDtypeStruct((T, D_MODEL), F32),
                   jax.ShapeDtypeStruct((T * SC_PIECES, SC_ROW), F32),
                   jax.ShapeDtypeStruct((T, LANES), F32)],
        compiler_params=pltpu.CompilerParams(dimension_semantics=("parallel",),
                                             vmem_limit_bytes=VMEM_LIMIT),
        name="out_proj_router",
    )(x2d, y_hy, o_f, o_b, z, jnp.tile(dn_norm_w, DN_HEADS).reshape(1, DN_W), w_oh, w_od,
      g.reshape(1, D_MODEL), wr_hi, wr_lo, br)


def _deint_kernel(w_ref, p_ref, og_ref, ol_ref):
    half = DEINT_COLS // 2
    sel = jnp.dot(w_ref[0].astype(BF16), p_ref[...], preferred_element_type=F32)
    og_ref[0] = sel[:, :half].astype(BF16)
    ol_ref[0] = sel[:, half:].astype(BF16)


def _deinterleave(w_gate_up):
    half = DEINT_COLS // 2
    r = np.arange(DEINT_COLS)[:, None]
    c = np.arange(DEINT_COLS)[None, :]
    perm = jnp.asarray(np.where(c < half, r == 2 * c, r == 2 * (c - half) + 1), BF16)
    out = jax.ShapeDtypeStruct((N_EXPERTS, D_MODEL, D_FF), BF16)
    return pl.pallas_call(
        _deint_kernel,
        grid=(N_EXPERTS, 2 * D_FF // DEINT_COLS),
        in_specs=[pl.BlockSpec((1, D_MODEL, DEINT_COLS), lambda e, j: (e, 0, j)),
                  pl.BlockSpec((DEINT_COLS, DEINT_COLS), lambda e, j: (0, 0))],
        out_specs=[pl.BlockSpec((1, D_MODEL, half), lambda e, j: (e, 0, j)),
                   pl.BlockSpec((1, D_MODEL, half), lambda e, j: (e, 0, j))],
        out_shape=[out, out],
        compiler_params=pltpu.CompilerParams(dimension_semantics=("parallel", "parallel")),
        name="deinterleave_gate_up",
    )(w_gate_up, perm)


def _store_pieces(ref, x):
    rows = x.shape[0]
    for c in range(SC_PIECES):
        ref[pl.ds(c, rows, stride=SC_PIECES), :] = x[:, c * SC_ROW:(c + 1) * SC_ROW]


def _load_pieces(ref, rows):
    return jnp.concatenate([ref[pl.ds(c, rows, stride=SC_PIECES), :] for c in range(SC_PIECES)], axis=1)


def _sc_scatter(x, indices, n_out):
    n = indices.shape[0]
    rows, width = x.shape
    nsrc = rows // SC_WINDOW
    mesh = plsc.VectorSubcoreMesh(core_axis_name="core", subcore_axis_name="subcore")

    @functools.partial(pl.kernel, out_type=jax.ShapeDtypeStruct((n_out, width), x.dtype), mesh=mesh)
    def scatter(x_hbm, i_hbm, o_hbm):
        def body(x_vmem, i_vmem):
            pltpu.sync_copy(x_vmem, o_hbm.at[i_vmem.at[0]])

        pltpu.emit_pipeline(
            body,
            grid=(n // SC_WINDOW,),
            in_specs=[pl.BlockSpec((SC_WINDOW, width), index_map=lambda i: (i % nsrc, 0)),
                      pl.BlockSpec((1, SC_WINDOW), index_map=lambda i: (0, i))],
            out_specs=[],
            core_axis_name=("core", "subcore"),
            dimension_semantics=(pltpu.PARALLEL,),
        )(x_hbm, i_hbm)

    return scatter(x, indices.reshape(1, n))


def _expert_kernel(be_ref, nb_ref, nv_ref, xb_ref, wg_ref, wl_ref, bg_ref, bl_ref, wd_ref, bd_ref, y_ref):
    i = pl.program_id(0)

    @pl.when(i < nb_ref[0])
    def _():
        row_id = lax.broadcasted_iota(jnp.int32, (MOE_BLOCK, 1), 0)
        xb = jnp.where(row_id < nv_ref[i], _load_pieces(xb_ref, MOE_BLOCK), 0.0).astype(BF16)
        hg = jnp.dot(xb, wg_ref[0], preferred_element_type=F32) + bg_ref[0]
        hl = jnp.dot(xb, wl_ref[0], preferred_element_type=F32) + bl_ref[0]
        x_glu = jnp.minimum(hg, SWIGLU_LIMIT)
        x_lin = jnp.clip(hl, -SWIGLU_LIMIT, SWIGLU_LIMIT)
        act = x_glu * jax.nn.sigmoid(SWIGLU_ALPHA * x_glu) * (x_lin + 1.0)
        y = jnp.dot(act.astype(BF16), wd_ref[0], preferred_element_type=F32) + bd_ref[0]
        _store_pieces(y_ref, y)

    @pl.when(i >= nb_ref[0])
    def _():
        y_ref[...] = jnp.zeros_like(y_ref)


def _expert_mlp(xb, block_e, n_used, n_valid, wg, wl, bg, bl, wd, bd):
    n_blocks = xb.shape[0] // (MOE_BLOCK * SC_PIECES)
    n_rows = n_blocks * MOE_BLOCK
    rowm = lambda i, be, nb, nv: (i, 0)
    exp3 = lambda i, be, nb, nv: (be[i], 0, 0)
    grid_spec = pltpu.PrefetchScalarGridSpec(
        num_scalar_prefetch=3,
        grid=(n_blocks,),
        in_specs=[pl.BlockSpec((MOE_BLOCK * SC_PIECES, SC_ROW), rowm),
                  pl.BlockSpec((1, D_MODEL, D_FF), exp3),
                  pl.BlockSpec((1, D_MODEL, D_FF), exp3),
                  pl.BlockSpec((1, 1, D_FF), exp3),
                  pl.BlockSpec((1, 1, D_FF), exp3),
                  pl.BlockSpec((1, D_FF, D_MODEL), exp3),
                  pl.BlockSpec((1, 1, D_MODEL), exp3)],
        out_specs=pl.BlockSpec((MOE_BLOCK * SC_PIECES, SC_ROW), rowm),
    )
    return pl.pallas_call(
        _expert_kernel,
        grid_spec=grid_spec,
        out_shape=jax.ShapeDtypeStruct((n_rows * SC_PIECES, SC_ROW), F32),
        compiler_params=pltpu.CompilerParams(dimension_semantics=("arbitrary",),
                                             vmem_limit_bytes=VMEM_LIMIT),
        name="expert_mlp",
    )(block_e, n_used, n_valid, xb, wg, wl, bg, bl, wd, bd)


def _moe(h, logits, wg, wl, bg, bl, wd, bd):
    T = logits.shape[0]
    TK = T * TOP_K
    top_vals, top_idx = lax.top_k(logits, TOP_K)
    gates = jax.nn.softmax(top_vals, axis=-1)
    onehot = jax.nn.one_hot(top_idx, N_EXPERTS, dtype=jnp.int32)
    sel = jnp.sum(onehot, axis=1)
    before = jnp.cumsum(sel, axis=0) - sel
    counts = jnp.sum(sel, axis=0)
    padded = (counts + MOE_BLOCK - 1) // MOE_BLOCK * MOE_BLOCK
    pad_end = jnp.cumsum(padded)
    pad_start = pad_end - padded
    dest = jnp.sum(onehot * (before + pad_start[None, :])[:, None, :], axis=-1).astype(jnp.int32)
    n_blocks = (TK + MOE_BLOCK - 1) // MOE_BLOCK + N_EXPERTS
    n_rows = n_blocks * MOE_BLOCK
    block_start = jnp.arange(n_blocks, dtype=jnp.int32) * MOE_BLOCK
    block_e = jnp.minimum(jnp.sum((block_start[:, None] >= pad_end[None, :]).astype(jnp.int32), axis=1),
                          N_EXPERTS - 1)
    n_used = (pad_end[-1] // MOE_BLOCK).astype(jnp.int32).reshape(1)
    be_onehot = jax.nn.one_hot(block_e, N_EXPERTS, dtype=jnp.int32)
    n_valid = jnp.clip(jnp.sum(be_onehot * (counts + pad_start)[None, :], axis=1) - block_start,
                       0, MOE_BLOCK).astype(jnp.int32)
    idx = (dest.T[:, :, None] * SC_PIECES + jnp.arange(SC_PIECES, dtype=jnp.int32)[None, None, :]).reshape(-1)
    xb = _sc_scatter(h, idx, n_rows * SC_PIECES)
    yb = _expert_mlp(xb, block_e, n_used, n_valid, wg, wl, bg, bl, wd, bd)
    return yb, dest, gates


def _sc_gather(x, indices):
    n = indices.shape[0]
    width = x.shape[1]
    mesh = plsc.VectorSubcoreMesh(core_axis_name="core", subcore_axis_name="subcore")

    @functools.partial(pl.kernel, out_type=jax.ShapeDtypeStruct((n, width), x.dtype), mesh=mesh)
    def gather(x_hbm, i_hbm, o_hbm):
        def body(i_vmem, o_vmem):
            pltpu.sync_copy(x_hbm.at[i_vmem.at[0]], o_vmem)

        pltpu.emit_pipeline(
            body,
            grid=(n // SC_WINDOW,),
            in_specs=[pl.BlockSpec((1, SC_WINDOW), index_map=lambda i: (0, i))],
            out_specs=[pl.BlockSpec((SC_WINDOW, width), index_map=lambda i: (i, 0))],
            core_axis_name=("core", "subcore"),
            dimension_semantics=(pltpu.PARALLEL,),
        )(i_hbm, o_hbm)

    return gather(x, indices.reshape(1, n))


def _combine_kernel(gate_ref, x1_ref, g_ref, y4_ref, o_ref):
    tm = x1_ref.shape[0]
    pieces = SC_PIECES
    cols = []
    for c in range(pieces):
        acc = x1_ref[:, c * SC_ROW:(c + 1) * SC_ROW]
        for k in range(TOP_K):
            acc = acc + gate_ref[:, k:k + 1] * y4_ref[pl.ds((k * pieces + c) * tm, tm), :]
        cols.append(acc)
    o_ref[...] = _rms(jnp.concatenate(cols, axis=1), g_ref[...])


def _combine_final(yb, dest, gates, x1, g):
    T = x1.shape[0]
    tm = COMBINE_TILE
    pieces = SC_PIECES
    d = dest.reshape(T // tm, tm, TOP_K).transpose(0, 2, 1)
    idx = (d[:, :, None, :] * pieces + jnp.arange(pieces, dtype=jnp.int32)[None, None, :, None]).reshape(-1)
    y4 = _sc_gather(yb, idx)
    row = lambda i: (i, 0)
    return pl.pallas_call(
        _combine_kernel,
        grid=(T // tm,),
        in_specs=[pl.BlockSpec((tm, TOP_K), row),
                  pl.BlockSpec((tm, D_MODEL), row),
                  pl.BlockSpec((1, D_MODEL), lambda i: (0, 0)),
                  pl.BlockSpec((tm * TOP_K * pieces, SC_ROW), row)],
        out_specs=pl.BlockSpec((tm, D_MODEL), row),
        out_shape=jax.ShapeDtypeStruct((T, D_MODEL), F32),
        compiler_params=pltpu.CompilerParams(dimension_semantics=("parallel",),
                                             vmem_limit_bytes=VMEM_LIMIT),
        name="moe_combine_final",
    )(gates, x1, g.reshape(1, D_MODEL), y4)


def _fft_tables(nc, nf):
    n = nc * nf
    kc = np.arange(nc, dtype=np.float64)
    a1 = 2.0 * np.pi * np.outer(kc, np.arange(nc // 2)) / nc
    c1, s1 = np.cos(a1), np.sin(a1)
    l1 = np.block([[c1, s1], [-s1, c1]])
    a2 = 2.0 * np.pi * np.outer(np.arange(nf), np.arange(nf)) / nf
    c2, s2 = np.cos(a2), np.sin(a2)
    m2 = np.block([[c2, s2], [-s2, c2]])
    m2i = np.block([[c2, -s2], [s2, c2]])
    a3 = 2.0 * np.pi * np.outer(np.arange(nc // 2), kc) / nc
    c3, s3 = np.cos(a3), np.sin(a3)
    l3 = np.block([[c3, -s3], [s3, c3]]) / n
    return tuple(jnp.asarray(m, F32).astype(BF16) for m in (l1, m2, m2i, l3))


def _twiddle(nc, nf):
    n = nc * nf
    ph = (jnp.arange(nf, dtype=jnp.int32)[:, None] * jnp.arange(nc, dtype=jnp.int32)[None, :]) % n
    ang = ph.astype(F32) * (2.0 * math.pi / n)
    tw = jnp.stack([jnp.cos(ang), jnp.sin(ang)], axis=1)
    return jnp.broadcast_to(tw[..., None], (nf, 2, nc, LANES))


def _lane_tile(t, width):
    return t if width == LANES else jnp.concatenate([t] * (width // LANES), axis=1)


def _fft1_kernel(z_ref, l1_ref, tw_ref, o_ref, *, nc):
    cb = o_ref.shape[-1]
    for j in range(SUB):
        rhs = jnp.concatenate([z_ref[0, :, j, :], z_ref[1, :, j, :]], axis=0).astype(BF16)
        a = jnp.dot(l1_ref[...], rhs, preferred_element_type=F32)
        ar, ai = a[:nc], a[nc:]
        twr = _lane_tile(tw_ref[j, 0], cb)
        twi = _lane_tile(tw_ref[j, 1], cb)
        o_ref[j, 0] = ar * twr + ai * twi
        o_ref[j, 1] = ai * twr - ar * twi


def _fft1(z, l1, tw, nc, nf):
    P, C = z.shape[0], z.shape[-1]
    cb = FFT_CB
    z_spec = pl.BlockSpec((None, 2, nc // 2, SUB, cb), lambda p, j, c: (p, 0, 0, j, c))
    return pl.pallas_call(
        functools.partial(_fft1_kernel, nc=nc),
        grid=(P, nf // SUB, C // cb),
        in_specs=[z_spec,
                  pl.BlockSpec(l1.shape, lambda p, j, c: (0, 0)),
                  pl.BlockSpec((SUB, 2, nc, LANES), lambda p, j, c: (j, 0, 0, 0))],
        out_specs=pl.BlockSpec((None, SUB, 2, nc, cb), lambda p, j, c: (p, j, 0, 0, c)),
        out_shape=jax.ShapeDtypeStruct((P, nf, 2, nc, C), F32),
        compiler_params=pltpu.CompilerParams(dimension_semantics=("parallel", "parallel", "parallel"),
                                             vmem_limit_bytes=VMEM_LIMIT),
        name="hyena_fft_stage1",
    )(z, l1, tw)


def _fft2_filter_kernel(a_ref, m2_ref, k_ref, *, nf):
    for j in range(SUB):
        rhs = jnp.concatenate([a_ref[:, 0, j, :], a_ref[:, 1, j, :]], axis=0).astype(BF16)
        x = jnp.dot(m2_ref[...], rhs, preferred_element_type=F32)
        k_ref[j, 0] = x[:nf]
        k_ref[j, 1] = x[nf:]


def _fft2_filter(a, m2, nc, nf):
    C = a.shape[-1]
    cb = FFT_CB
    return pl.pallas_call(
        functools.partial(_fft2_filter_kernel, nf=nf),
        grid=(nc // SUB, C // cb),
        in_specs=[pl.BlockSpec((None, nf, 2, SUB, cb), lambda k, c: (0, 0, 0, k, c)),
                  pl.BlockSpec(m2.shape, lambda k, c: (0, 0))],
        out_specs=pl.BlockSpec((SUB, 2, nf, cb), lambda k, c: (k, 0, 0, c)),
        out_shape=jax.ShapeDtypeStruct((nc, 2, nf, C), F32),
        compiler_params=pltpu.CompilerParams(dimension_semantics=("parallel", "parallel"),
                                             vmem_limit_bytes=VMEM_LIMIT),
        name="hyena_filter_spectrum",
    )(a, m2)


def _fft2_kernel(a_ref, k_ref, m2_ref, m2i_ref, o_ref, *, nf):
    for j in range(SUB):
        rhs = jnp.concatenate([a_ref[:, 0, j, :], a_ref[:, 1, j, :]], axis=0).astype(BF16)
        x = jnp.dot(m2_ref[...], rhs, preferred_element_type=F32)
        xr, xi = x[:nf], x[nf:]
        kr, ki = k_ref[j, 0], k_ref[j, 1]
        y = jnp.concatenate([xr * kr - xi * ki, xr * ki + xi * kr], axis=0).astype(BF16)
        b = jnp.dot(m2i_ref[...], y, preferred_element_type=F32)
        o_ref[:, 0, j, :] = b[:nf]
        o_ref[:, 1, j, :] = b[nf:]


def _fft2(a, kspec, m2, m2i, nc, nf):
    P, C = a.shape[0], a.shape[-1]
    cb = FFT_CB
    blk = pl.BlockSpec((None, nf, 2, SUB, cb), lambda p, k, c: (p, 0, 0, k, c))
    return pl.pallas_call(
        functools.partial(_fft2_kernel, nf=nf),
        grid=(P, nc // SUB, C // cb),
        in_specs=[blk,
                  pl.BlockSpec((SUB, 2, nf, cb), lambda p, k, c: (k, 0, 0, c)),
                  pl.BlockSpec(m2.shape, lambda p, k, c: (0, 0)),
                  pl.BlockSpec(m2i.shape, lambda p, k, c: (0, 0))],
        out_specs=blk,
        out_shape=jax.ShapeDtypeStruct(a.shape, F32),
        compiler_params=pltpu.CompilerParams(dimension_semantics=("parallel", "parallel", "parallel"),
                                             vmem_limit_bytes=VMEM_LIMIT),
        name="hyena_fft_stage2",
    )(a, kspec, m2, m2i)


def _fft3_kernel(b_ref, tw_ref, l3_ref, s_ref, x0_ref, bias_ref, o_ref, *, nc):
    cb = o_ref.shape[-1]
    half = nc // 2
    for j in range(SUB):
        br, bi = b_ref[j, 0], b_ref[j, 1]
        twr = _lane_tile(tw_ref[j, 0], cb)
        twi = _lane_tile(tw_ref[j, 1], cb)
        rhs = jnp.concatenate([br * twr - bi * twi, br * twi + bi * twr], axis=0).astype(BF16)
        y = jnp.dot(l3_ref[...], rhs, preferred_element_type=F32)
        for q in range(2):
            s = s_ref[q, :, j, :]
            o_ref[q, :, j, :] = x0_ref[q, :, j, :] * (y[q * half:(q + 1) * half] + s * bias_ref[...])


def _fft3(b, tw, l3, s5, x05, bias, nc, nf):
    P, C = b.shape[0], b.shape[-1]
    cb = FFT_CB
    seq = pl.BlockSpec((None, 2, nc // 2, SUB, cb), lambda p, j, c: (p, 0, 0, j, c))
    return pl.pallas_call(
        functools.partial(_fft3_kernel, nc=nc),
        grid=(P, nf // SUB, C // cb),
        in_specs=[pl.BlockSpec((None, SUB, 2, nc, cb), lambda p, j, c: (p, j, 0, 0, c)),
                  pl.BlockSpec((SUB, 2, nc, LANES), lambda p, j, c: (j, 0, 0, 0)),
                  pl.BlockSpec(l3.shape, lambda p, j, c: (0, 0)),
                  seq, seq,
                  pl.BlockSpec((1, cb), lambda p, j, c: (0, c))],
        out_specs=seq,
        out_shape=jax.ShapeDtypeStruct(s5.shape, F32),
        compiler_params=pltpu.CompilerParams(dimension_semantics=("parallel", "parallel", "parallel"),
                                             vmem_limit_bytes=VMEM_LIMIT),
        name="hyena_fft_stage3",
    )(b, tw, l3, s5, x05, bias)


def _fft_split(n):
    nf = 1 << (int(math.log2(n)) // 2)
    return n // nf, nf


def _filter_tables(nc):
    half = nc // 2
    kc = np.arange(nc, dtype=np.float64)[:, None]
    r = np.arange(half, dtype=np.float64)
    mats = []
    for rows_b, drop0 in ((half + (half - 1 - r), False), (nc - r, True)):
        a = 2.0 * np.pi * kc * np.concatenate([r, rows_b])[None, :] / nc
        m = np.concatenate([np.cos(a), -np.sin(a)], axis=0)
        if drop0:
            m[:, half] = 0.0
        mats.append(m)
    return tuple(jnp.asarray(m, F32).astype(BF16) for m in mats)


def _fft1_filter_kernel(hf_ref, hba_ref, hbb_ref, mb_ref, mb0_ref, tw_ref, o_ref, *, nc):
    cb = o_ref.shape[-1]
    m_first = jnp.where(pl.program_id(0) == 0, mb0_ref[...], mb_ref[...])
    for j in range(SUB):
        src = hbb_ref[:, 0, :] if j == 0 else hba_ref[:, SUB - j, :]
        rhs = jnp.concatenate([hf_ref[:, j, :], src], axis=0).astype(BF16)
        a = jnp.dot(m_first if j == 0 else mb_ref[...], rhs, preferred_element_type=F32)
        ar, ai = a[:nc], a[nc:]
        twr = _lane_tile(tw_ref[j, 0], cb)
        twi = _lane_tile(tw_ref[j, 1], cb)
        o_ref[j, 0] = ar * twr + ai * twi
        o_ref[j, 1] = ai * twr - ar * twi


def _fft1_filter(h_f, h_b, mb, mb0, tw, nc, nf):
    C = h_f.shape[-1]
    cb = FFT_CB
    nblk = nf // SUB
    hf3 = h_f.reshape(nc // 2, nf, C)
    hb3 = h_b.reshape(nc // 2, nf, C)
    blk = lambda f: pl.BlockSpec((nc // 2, SUB, cb), f)
    return pl.pallas_call(
        functools.partial(_fft1_filter_kernel, nc=nc),
        grid=(nblk, C // cb),
        in_specs=[blk(lambda j, c: (0, j, c)),
                  blk(lambda j, c: (0, nblk - 1 - j, c)),
                  blk(lambda j, c: (0, (nblk - j) % nblk, c)),
                  pl.BlockSpec(mb.shape, lambda j, c: (0, 0)),
                  pl.BlockSpec(mb0.shape, lambda j, c: (0, 0)),
                  pl.BlockSpec((SUB, 2, nc, LANES), lambda j, c: (j, 0, 0, 0))],
        out_specs=pl.BlockSpec((None, SUB, 2, nc, cb), lambda j, c: (0, j, 0, 0, c)),
        out_shape=jax.ShapeDtypeStruct((1, nf, 2, nc, C), F32),
        compiler_params=pltpu.CompilerParams(dimension_semantics=("parallel", "parallel"),
                                             vmem_limit_bytes=VMEM_LIMIT),
        name="hyena_filter_stage1",
    )(hf3, hb3, hb3, mb, mb0, tw)


def _hyena_conv(s, x0, h_f, h_b, bias):
    B, L, C = s.shape
    nc, nf = _fft_split(2 * L)
    tw = _twiddle(nc, nf)
    l1, m2, m2i, l3 = _fft_tables(nc, nf)
    mb, mb0 = _filter_tables(nc)
    kspec = _fft2_filter(_fft1_filter(h_f, h_b, mb, mb0, tw, nc, nf), m2, nc, nf)
    s5 = s.reshape(B // 2, 2, nc // 2, nf, C)
    x05 = x0.reshape(B // 2, 2, nc // 2, nf, C)
    a = _fft1(s5, l1, tw, nc, nf)
    b = _fft2(a, kspec, m2, m2i, nc, nf)
    return _fft3(b, tw, l3, s5, x05, bias.reshape(1, C), nc, nf).reshape(B, L, C)


def _hyena_filter(L, w1, b1, w2, b2, w3, freq):
    pos = jnp.arange(L, dtype=F32)
    t = jnp.linspace(0.0, 1.0, L, dtype=F32)[:, None]
    bands = jnp.linspace(1e-4, HY_BANDS - 1, HY_BANDS, dtype=F32)
    ang = (2.0 * math.pi / L) * pos[:, None] * bands[None, :]
    z = jnp.concatenate([t, jnp.cos(ang), -jnp.sin(ang)], axis=-1)
    h = jnp.sin(freq * (z @ w1 + b1))
    h = jnp.sin(freq * (h @ w2 + b2))
    h = h @ w3
    deltas = jnp.abs(jnp.linspace(math.log(HY_DECAY_TARGET) / HY_SLOW_PCT,
                                  math.log(HY_DECAY_TARGET) / HY_FAST_PCT, HY_CH, dtype=F32))
    window = jnp.exp(-t * deltas[None, :])
    h_f = h[:, :HY_CH] * window
    h_b = h[:, HY_CH:] * window
    l1 = jnp.sum(jnp.abs(h_f), axis=0) + jnp.sum(jnp.abs(h_b[1:]), axis=0)
    return h_f / l1, h_b / l1


def _split3(x):
    x1 = x.astype(BF16)
    r = x - x1.astype(F32)
    x2 = r.astype(BF16)
    x3 = (r - x2.astype(F32)).astype(BF16)
    return x1, x2, x3


def _mm(a, b):
    return jnp.dot(a.astype(BF16), b.astype(BF16), preferred_element_type=F32)


def _delta_kernel(qf_ref, kf_ref, vf_ref, gf_ref, qb_ref, kb_ref, vb_ref, gb_ref, of_ref, ob_ref, s_ref,
                  *, n_sub):
    @pl.when(pl.program_id(1) == 0)
    def _():
        s_ref[...] = jnp.zeros_like(s_ref)

    row = lax.broadcasted_iota(jnp.int32, (CHUNK, CHUNK), 0)
    col = lax.broadcasted_iota(jnp.int32, (CHUNK, CHUNK), 1)
    eye_f = jnp.where(row == col, 1.0, 0.0).astype(F32)
    dirs = ((qf_ref, kf_ref, vf_ref, gf_ref, of_ref, row >= col, row > col),
            (qb_ref, kb_ref, vb_ref, gb_ref, ob_ref, row <= col, row < col))

    ch = []
    for d, (q_ref, k_ref, v_ref, g_ref, _, incl, strict) in enumerate(dirs):
        tri = jnp.where(incl, 1.0, 0.0).astype(BF16)
        for c in range(n_sub):
            rows = slice(c * CHUNK, (c + 1) * CHUNK)
            gates = g_ref[rows, :]
            gl = 2 * DN_HEADS * d
            g1, g2, g3 = _split3(gates)
            gc_all = (jnp.dot(tri, g1, preferred_element_type=F32)
                      + jnp.dot(tri, g2, preferred_element_type=F32)
                      + jnp.dot(tri, g3, preferred_element_type=F32))
            gsum_all = jnp.sum(gates, axis=0, keepdims=True)
            for h in range(DN_HEADS):
                lanes = slice(h * DN_DK, (h + 1) * DN_DK)
                kh = k_ref[rows, lanes]
                beta = gates[:, gl + DN_HEADS + h:gl + DN_HEADS + h + 1]
                gc = gc_all[:, gl + h:gl + h + 1]
                g_last = gsum_all[:, gl + h:gl + h + 1]
                gc_b = jnp.broadcast_to(gc, (CHUNK, CHUNK))
                gc_row = jnp.sum(gc_b * eye_f, axis=0, keepdims=True)
                decay = jnp.where(incl, jnp.exp(jnp.minimum(gc_b - gc_row, 0.0)), 0.0)
                e_gc = jnp.exp(gc)
                kb = kh * beta
                ch.append(dict(d=d, c=c, h=h, lanes=lanes, rows=rows, strict=strict, decay=decay,
                               kh16=kh.astype(BF16), kb16=kb.astype(BF16),
                               q16=q_ref[rows, lanes].astype(BF16),
                               rhs=jnp.concatenate([v_ref[rows, lanes] * beta, kb * e_gc],
                                                   axis=1).astype(BF16),
                               qe=q_ref[rows, lanes] * e_gc,
                               kdT=(kh * jnp.exp(g_last - gc)).T.astype(BF16),
                               e_last=jnp.exp(g_last)))
    for x in ch:
        a = lax.dot_general(x["kb16"], x["kh16"], _NT, preferred_element_type=F32) * x["decay"]
        x["a"] = jnp.where(x["strict"], a, 0.0)
        x["qk"] = (lax.dot_general(x["q16"], x["kh16"], _NT, preferred_element_type=F32)
                   * x["decay"]).astype(BF16)
    for x in ch:
        x["t"] = eye_f - x["a"]
        x["p"] = _mm(x["a"], x["a"])
    for level in range(5):
        for x in ch:
            p16 = x["p"].astype(BF16)
            x["t"] = x["t"] + jnp.dot(x["t"].astype(BF16), p16, preferred_element_type=F32)
            if level < 4:
                x["p"] = jnp.dot(p16, p16, preferred_element_type=F32)
    for x in ch:
        uw = jnp.dot(x["t"].astype(BF16), x["rhs"], preferred_element_type=F32)
        x["u"] = uw[:, :DN_DV]
        x["wq"] = jnp.concatenate([uw[:, DN_DV:], x["qe"]], axis=0).astype(BF16)

    for step in range(n_sub):
        cur = [x for x in ch if x["c"] == (step if x["d"] == 0 else n_sub - 1 - step)]
        for x in cur:
            x["s"] = s_ref[x["d"], x["h"]]
            x["ws"] = jnp.dot(x["wq"], x["s"].astype(BF16), preferred_element_type=F32)
        for x in cur:
            x["vn"] = (x["u"] - x["ws"][:CHUNK]).astype(BF16)
        for x in cur:
            o = x["ws"][CHUNK:] + jnp.dot(x["qk"], x["vn"], preferred_element_type=F32)
            dirs[x["d"]][4][x["rows"], x["lanes"]] = o
            s_ref[x["d"], x["h"]] = (x["s"] * x["e_last"]
                                     + jnp.dot(x["kdT"], x["vn"], preferred_element_type=F32))


def _delta_scan(q, k, v, gates):
    B, L, _ = q.shape
    n_sub = DN_STEP_CHUNKS
    rows = n_sub * CHUNK
    nblk = L // rows
    fwd = pl.BlockSpec((None, rows, DN_W), lambda b, j: (b, j, 0))
    bwd = pl.BlockSpec((None, rows, DN_W), lambda b, j: (b, nblk - 1 - j, 0))
    gfwd = pl.BlockSpec((None, rows, LANES), lambda b, j: (b, j, 0))
    gbwd = pl.BlockSpec((None, rows, LANES), lambda b, j: (b, nblk - 1 - j, 0))
    out = jax.ShapeDtypeStruct((B, L, DN_W), F32)
    return pl.pallas_call(
        functools.partial(_delta_kernel, n_sub=n_sub),
        grid=(B, nblk),
        in_specs=[fwd, fwd, fwd, gfwd, bwd, bwd, bwd, gbwd],
        out_specs=[fwd, bwd],
        out_shape=[out, out],
        scratch_shapes=[pltpu.VMEM((2, DN_HEADS, DN_DK, DN_DV), F32)],
        compiler_params=pltpu.CompilerParams(dimension_semantics=("parallel", "arbitrary"),
                                             vmem_limit_bytes=VMEM_LIMIT),
        name="delta_scan",
    )(q, k, v, gates, q, k, v, gates)


def _trunk(x, w, hy_pos_w1, hy_pos_b1, hy_pos_w2, hy_pos_b2, hy_pos_w3, hy_sin_freq, hy_bias, dn_norm_w):
    B, L, D = x.shape
    T = B * L
    x2d = x.reshape(T, D)
    x0, s, q, k, v, z, dn_gates = _in_proj(x2d, L, w["norm_mix"], *w["in_proj"])
    seq = lambda a: a.reshape(B, L, a.shape[-1])
    h_f, h_b = _hyena_filter(L, hy_pos_w1, hy_pos_b1, hy_pos_w2, hy_pos_b2, hy_pos_w3, hy_sin_freq)
    y_hy = _hyena_conv(seq(s), seq(x0), h_f, h_b, hy_bias)
    o_f, o_b = _delta_scan(seq(q), seq(k), seq(v), seq(dn_gates))
    x1, h_ffn, logits = _out_proj(x2d, y_hy.reshape(T, HY_CH), o_f.reshape(T, DN_W), o_b.reshape(T, DN_W),
                                 z, dn_norm_w, w["out_hy"], w["out_dn"], w["norm_ffn"],
                                 w["router_hi"], w["router_lo"], w["router_b"])
    yb, dest, gates = _moe(h_ffn, logits[:, :N_EXPERTS], w["wg"], w["wl"], w["bg"], w["bl"], w["wd"],
                           w["bd"])
    return _combine_final(yb, dest, gates, x1, w["norm_final"]).reshape(B, L, D)


def kernel(x_prompt, x_sample, norm_mix_w, w_in, hy_conv_w, hy_conv_b, hy_pos_w1, hy_pos_b1, hy_pos_w2, hy_pos_b2, hy_pos_w3, hy_sin_freq, hy_bias, dn_conv_w, dn_a_log, dn_dt_bias, dn_norm_w, w_out, norm_ffn_w, w_router, b_router, w_gate_up, b_gate_up, w_down, b_down, norm_final_w):
    w_out16 = w_out[0].astype(BF16)
    wr = jnp.pad(w_router[0], ((0, 0), (0, LANES - N_EXPERTS)))
    wr_hi = wr.astype(BF16)
    wg, wl = _deinterleave(w_gate_up[0])
    w = {
        "norm_mix": norm_mix_w[0],
        "in_proj": _in_proj_params(w_in[0], hy_conv_w[0], hy_conv_b[0], dn_conv_w[0], dn_a_log[0],
                                   dn_dt_bias[0]),
        "out_hy": w_out16[:HY_CH],
        "out_dn": w_out16[HY_CH:],
        "norm_ffn": norm_ffn_w[0],
        "router_hi": wr_hi,
        "router_lo": (wr - wr_hi.astype(F32)).astype(BF16),
        "router_b": jnp.pad(b_router[0], (0, LANES - N_EXPERTS)).reshape(1, LANES),
        "wg": wg,
        "wl": wl,
        "bg": b_gate_up[0][:, 0::2].reshape(N_EXPERTS, 1, D_FF),
        "bl": b_gate_up[0][:, 1::2].reshape(N_EXPERTS, 1, D_FF),
        "wd": w_down[0].astype(BF16),
        "bd": b_down[0].reshape(N_EXPERTS, 1, D_MODEL),
        "norm_final": norm_final_w,
    }
    mix = (hy_pos_w1[0], hy_pos_b1[0], hy_pos_w2[0], hy_pos_b2[0], hy_pos_w3[0], hy_sin_freq[0],
           hy_bias[0], dn_norm_w[0])
    return (_trunk(x_prompt, w, *mix), _trunk(x_sample, w, *mix))
```

```python
import functools
import math

import jax
import jax.numpy as jnp
import numpy as np
from jax import lax
from jax.experimental import pallas as pl
from jax.experimental.pallas import tpu as pltpu
from jax.experimental.pallas import tpu_sc as plsc

D_MODEL = 1024
HY_CH = 512
DN_HEADS = 4
DN_DK = 128
DN_DV = 128
DN_QK = DN_HEADS * DN_DK
DN_W = DN_HEADS * DN_DV
HY_IN = 3 * HY_CH
DN_CONV = 2 * DN_QK + DN_W
N_GATE = 4 * DN_HEADS
SHORT_CONV = 3
CONV_COLS = HY_IN + DN_CONV
REST_COLS = DN_W + 128
HY_EMB = 33
HY_BANDS = (HY_EMB - 1) // 2
HY_DECAY_TARGET = 1e-2
HY_FAST_PCT = 0.3
HY_SLOW_PCT = 1.5
CHUNK = 64
N_EXPERTS = 32
TOP_K = 4
D_FF = D_MODEL
SWIGLU_ALPHA = 1.702
SWIGLU_LIMIT = 7.0
MOE_BLOCK = 512
EPS = 1e-6

LANES = 128
SUB = 8
ROW_TILE = 512
FFT_CB = 256
DN_STEP_CHUNKS = 4
DEINT_COLS = 512
COMBINE_TILE = 512
SC_WINDOW = 128
SC_ROW = LANES
SC_PIECES = D_MODEL // 2 // SC_ROW
VMEM_LIMIT = 56 * 1024 * 1024

F32 = jnp.float32
BF16 = jnp.bfloat16
U32 = jnp.uint32

_NT = (((1,), (1,)), ((), ()))


def _rms(x, g):
    return x * lax.rsqrt(jnp.mean(x * x, axis=-1, keepdims=True) + EPS) * g


def _silu(x):
    return x * jax.nn.sigmoid(x)


def _head_l2norm(x):
    parts = []
    for hd in range(DN_HEADS):
        xh = x[:, hd * DN_DK:(hd + 1) * DN_DK]
        parts.append(xh * lax.rsqrt(jnp.sum(xh * xh, axis=-1, keepdims=True) + EPS))
    return jnp.concatenate(parts, axis=1)


def _in_proj_kernel(xp_ref, x_ref, xn_ref, g_ref, wc_ref, wr_ref, cw_ref, cb_ref, gt_ref,
                    x0_ref, s_ref, q_ref, k_ref, v_ref, z_ref, gate_ref, p_scr, *, tiles_per_seq):
    i = pl.program_id(0)
    first = (i % tiles_per_seq) == 0
    last = (i % tiles_per_seq) == tiles_per_seq - 1
    g = g_ref[...]
    hp = jnp.where(first, 0.0, _rms(xp_ref[...], g))
    hn = jnp.where(last, 0.0, _rms(xn_ref[...], g))
    h_all = jnp.concatenate([hp, _rms(x_ref[...], g), hn], axis=0).astype(BF16)
    h = h_all[SUB:SUB + ROW_TILE]

    def project(c0):
        cols = slice(c0, c0 + HY_CH)
        p_scr[:, cols] = jnp.dot(h_all, wc_ref[:, cols], preferred_element_type=F32)

    def conv(c0):
        cols = slice(c0, c0 + HY_CH)
        return (p_scr[pl.ds(SUB - 1, ROW_TILE), cols] * cw_ref[0:1, cols]
                + p_scr[pl.ds(SUB, ROW_TILE), cols] * cw_ref[1:2, cols]
                + p_scr[pl.ds(SUB + 1, ROW_TILE), cols] * cw_ref[2:3, cols])

    project(0)
    project(HY_CH)
    x0_ref[...] = conv(0) + cb_ref[:, 0:HY_CH]
    project(2 * HY_CH)
    project(HY_IN)
    s_ref[...] = (conv(HY_CH) + cb_ref[:, HY_CH:2 * HY_CH]) * (conv(2 * HY_CH) + cb_ref[:, 2 * HY_CH:])
    project(HY_IN + DN_QK)
    q_ref[...] = _head_l2norm(_silu(conv(HY_IN))) * (DN_DK ** -0.5)
    project(HY_IN + 2 * DN_QK)
    k_ref[...] = _head_l2norm(_silu(conv(HY_IN + DN_QK)))
    rest = jnp.dot(h, wr_ref[...], preferred_element_type=F32)
    v_ref[...] = _silu(conv(HY_IN + 2 * DN_QK))
    z_ref[...] = rest[:, :DN_W]
    a = rest[:, DN_W:] + gt_ref[1:2]
    softplus = jnp.maximum(a, 0.0) + jnp.log(1.0 + jnp.exp(-jnp.abs(a)))
    gate_ref[...] = jnp.where(gt_ref[2:3] > 0.5, -gt_ref[0:1] * softplus, jax.nn.sigmoid(rest[:, DN_W:]))


def _in_proj(x2d, seq_len, g, w_conv, w_rest, conv_w, conv_b, gate_tab):
    T = x2d.shape[0]
    per = ROW_TILE // SUB
    last_sub = T // SUB - 1
    const = lambda i: (0, 0)
    row = lambda i: (i, 0)
    o512 = pl.BlockSpec((ROW_TILE, HY_CH), row)
    s512 = jax.ShapeDtypeStruct((T, HY_CH), F32)
    return pl.pallas_call(
        functools.partial(_in_proj_kernel, tiles_per_seq=seq_len // ROW_TILE),
        grid=(T // ROW_TILE,),
        in_specs=[pl.BlockSpec((SUB, D_MODEL), lambda i: (jnp.maximum(i * per - 1, 0), 0)),
                  pl.BlockSpec((ROW_TILE, D_MODEL), row),
                  pl.BlockSpec((SUB, D_MODEL), lambda i: (jnp.minimum((i + 1) * per, last_sub), 0)),
                  pl.BlockSpec((1, D_MODEL), const),
                  pl.BlockSpec((D_MODEL, CONV_COLS), const),
                  pl.BlockSpec((D_MODEL, REST_COLS), const),
                  pl.BlockSpec((SHORT_CONV, CONV_COLS), const),
                  pl.BlockSpec((1, HY_IN), const),
                  pl.BlockSpec((3, LANES), const)],
        out_specs=[o512, o512, o512, o512, o512, o512, pl.BlockSpec((ROW_TILE, LANES), row)],
        out_shape=[s512, s512, s512, s512, s512, s512, jax.ShapeDtypeStruct((T, LANES), F32)],
        scratch_shapes=[pltpu.VMEM((ROW_TILE + 2 * SUB, CONV_COLS), F32)],
        compiler_params=pltpu.CompilerParams(dimension_semantics=("parallel",),
                                             vmem_limit_bytes=VMEM_LIMIT),
        name="in_proj",
    )(x2d, x2d, x2d, g.reshape(1, D_MODEL), w_conv, w_rest, conv_w, conv_b, gate_tab)


def _in_proj_params(w_in, hy_conv_w, hy_conv_b, dn_conv_w, dn_a_log, dn_dt_bias):
    H = DN_HEADS
    w16 = w_in.astype(BF16)
    gc = w16[:, CONV_COLS + DN_W:]
    gc = jnp.concatenate([gc[:, 0:H], gc[:, 2 * H:3 * H], gc[:, H:2 * H], gc[:, 3 * H:]], axis=1)
    w_rest = jnp.concatenate([w16[:, CONV_COLS:CONV_COLS + DN_W],
                              jnp.pad(gc, ((0, 0), (0, LANES - N_GATE)))], axis=1)
    zero, one, pad = jnp.zeros((H,), F32), jnp.ones((H,), F32), jnp.zeros((LANES - N_GATE,), F32)
    gate_tab = jnp.stack([jnp.concatenate([jnp.exp(dn_a_log[0]), zero, jnp.exp(dn_a_log[1]), zero, pad]),
                          jnp.concatenate([dn_dt_bias[0], zero, dn_dt_bias[1], zero, pad]),
                          jnp.concatenate([one, zero, one, zero, pad])])
    return (w16[:, :CONV_COLS], w_rest, jnp.concatenate([hy_conv_w, dn_conv_w], axis=1),
            hy_conv_b.reshape(1, HY_IN), gate_tab)


def _out_proj_kernel(x_ref, yh_ref, of_ref, ob_ref, z_ref, nw_ref, wh_ref, wd_ref, g_ref, wrh_ref,
                     wrl_ref, br_ref, x1_ref, h_ref, lg_ref):
    o = of_ref[...] + ob_ref[...]
    heads = []
    for hd in range(DN_HEADS):
        oh = o[:, hd * DN_DV:(hd + 1) * DN_DV]
        heads.append(oh * lax.rsqrt(jnp.mean(oh * oh, axis=-1, keepdims=True) + EPS))
    z = z_ref[...]
    y_dn = jnp.concatenate(heads, axis=1) * nw_ref[...] * (z * jax.nn.sigmoid(z))
    x1 = (x_ref[...]
          + jnp.dot(yh_ref[...].astype(BF16), wh_ref[...], preferred_element_type=F32)
          + jnp.dot(y_dn.astype(BF16), wd_ref[...], preferred_element_type=F32))
    x1_ref[...] = x1
    h = _rms(x1, g_ref[...])
    h_hi = h.astype(BF16)
    h_lo = (h - h_hi.astype(F32)).astype(BF16)
    _store_pieces(h_ref, _pack_bf16_pairs(h))
    lg_ref[...] = (jnp.dot(h_hi, wrh_ref[...], preferred_element_type=F32)
                   + jnp.dot(h_lo, wrh_ref[...], preferred_element_type=F32)
                   + jnp.dot(h_hi, wrl_ref[...], preferred_element_type=F32)
                   + br_ref[...])


def _out_proj(x2d, y_hy, o_f, o_b, z, dn_norm_w, w_oh, w_od, g, wr_hi, wr_lo, br):
    T = x2d.shape[0]
    const = lambda i: (0, 0)
    row = lambda i: (i, 0)
    return pl.pallas_call(
        _out_proj_kernel,
        grid=(T // ROW_TILE,),
        in_specs=[pl.BlockSpec((ROW_TILE, D_MODEL), row),
                  pl.BlockSpec((ROW_TILE, HY_CH), row),
                  pl.BlockSpec((ROW_TILE, DN_W), row),
                  pl.BlockSpec((ROW_TILE, DN_W), row),
                  pl.BlockSpec((ROW_TILE, DN_W), row),
                  pl.BlockSpec((1, DN_W), const),
                  pl.BlockSpec((HY_CH, D_MODEL), const),
                  pl.BlockSpec((DN_W, D_MODEL), const),
                  pl.BlockSpec((1, D_MODEL), const),
                  pl.BlockSpec((D_MODEL, LANES), const),
                  pl.BlockSpec((D_MODEL, LANES), const),
                  pl.BlockSpec((1, LANES), const)],
        out_specs=[pl.BlockSpec((ROW_TILE, D_MODEL), row),
                   pl.BlockSpec((ROW_TILE * SC_PIECES, SC_ROW), row),
                   pl.BlockSpec((ROW_TILE, LANES), row)],
        out_shape=[jax.ShapeDtypeStruct((T, D_MODEL), F32),
                   jax.ShapeDtypeStruct((T * SC_PIECES, SC_ROW), U32),
                   jax.ShapeDtypeStruct((T, LANES), F32)],
        compiler_params=pltpu.CompilerParams(dimension_semantics=("parallel",),
                                             vmem_limit_bytes=VMEM_LIMIT),
        name="out_proj_router",
    )(x2d, y_hy, o_f, o_b, z, jnp.tile(dn_norm_w, DN_HEADS).reshape(1, DN_W), w_oh, w_od,
      g.reshape(1, D_MODEL), wr_hi, wr_lo, br)


def _deint_kernel(w_ref, p_ref, og_ref, ol_ref):
    half = DEINT_COLS // 2
    sel = jnp.dot(w_ref[0].astype(BF16), p_ref[...], preferred_element_type=F32)
    og_ref[0] = sel[:, :half].astype(BF16)
    ol_ref[0] = sel[:, half:].astype(BF16)


def _deinterleave(w_gate_up):
    half = DEINT_COLS // 2
    r = np.arange(DEINT_COLS)[:, None]
    c = np.arange(DEINT_COLS)[None, :]
    perm = jnp.asarray(np.where(c < half, r == 2 * c, r == 2 * (c - half) + 1), BF16)
    out = jax.ShapeDtypeStruct((N_EXPERTS, D_MODEL, D_FF), BF16)
    return pl.pallas_call(
        _deint_kernel,
        grid=(N_EXPERTS, 2 * D_FF // DEINT_COLS),
        in_specs=[pl.BlockSpec((1, D_MODEL, DEINT_COLS), lambda e, j: (e, 0, j)),
                  pl.BlockSpec((DEINT_COLS, DEINT_COLS), lambda e, j: (0, 0))],
        out_specs=[pl.BlockSpec((1, D_MODEL, half), lambda e, j: (e, 0, j)),
                   pl.BlockSpec((1, D_MODEL, half), lambda e, j: (e, 0, j))],
        out_shape=[out, out],
        compiler_params=pltpu.CompilerParams(dimension_semantics=("parallel", "parallel")),
        name="deinterleave_gate_up",
    )(w_gate_up, perm)


def _pack_bf16_pairs(x):
    half = x.shape[1] // 2
    lo = lax.bitcast_convert_type(x[:, :half].astype(BF16).astype(F32), U32)
    hi = lax.bitcast_convert_type(x[:, half:].astype(BF16).astype(F32), U32)
    return (hi & jnp.uint32(0xFFFF0000)) | (lo >> 16)


def _unpack_bf16_pairs(w):
    lo = lax.bitcast_convert_type(w << 16, F32)
    hi = lax.bitcast_convert_type(w & jnp.uint32(0xFFFF0000), F32)
    return jnp.concatenate([lo, hi], axis=1)


def _store_pieces(ref, x):
    rows = x.shape[0]
    for c in range(SC_PIECES):
        ref[pl.ds(c, rows, stride=SC_PIECES), :] = x[:, c * SC_ROW:(c + 1) * SC_ROW]


def _load_pieces(ref, rows):
    return jnp.concatenate([ref[pl.ds(c, rows, stride=SC_PIECES), :] for c in range(SC_PIECES)], axis=1)


def _sc_scatter(x, indices, n_out):
    n = indices.shape[0]
    rows, width = x.shape
    nsrc = rows // SC_WINDOW
    mesh = plsc.VectorSubcoreMesh(core_axis_name="core", subcore_axis_name="subcore")

    @functools.partial(pl.kernel, out_type=jax.ShapeDtypeStruct((n_out, width), x.dtype), mesh=mesh)
    def scatter(x_hbm, i_hbm, o_hbm):
        def body(x_vmem, i_vmem):
            pltpu.sync_copy(x_vmem, o_hbm.at[i_vmem.at[0]])

        pltpu.emit_pipeline(
            body,
            grid=(n // SC_WINDOW,),
            in_specs=[pl.BlockSpec((SC_WINDOW, width), index_map=lambda i: (i % nsrc, 0)),
                      pl.BlockSpec((1, SC_WINDOW), index_map=lambda i: (0, i))],
            out_specs=[],
            core_axis_name=("core", "subcore"),
            dimension_semantics=(pltpu.PARALLEL,),
        )(x_hbm, i_hbm)

    return scatter(x, indices.reshape(1, n))


def _expert_kernel(be_ref, nb_ref, nv_ref, xb_ref, wg_ref, wl_ref, bg_ref, bl_ref, wd_ref, bd_ref, y_ref):
    i = pl.program_id(0)

    @pl.when(i < nb_ref[0])
    def _():
        row_id = lax.broadcasted_iota(jnp.int32, (MOE_BLOCK, 1), 0)
        xb = _unpack_bf16_pairs(_load_pieces(xb_ref, MOE_BLOCK))
        xb = jnp.where(row_id < nv_ref[i], xb, 0.0).astype(BF16)
        hg = jnp.dot(xb, wg_ref[0], preferred_element_type=F32) + bg_ref[0]
        hl = jnp.dot(xb, wl_ref[0], preferred_element_type=F32) + bl_ref[0]
        x_glu = jnp.minimum(hg, SWIGLU_LIMIT)
        x_lin = jnp.clip(hl, -SWIGLU_LIMIT, SWIGLU_LIMIT)
        act = x_glu * jax.nn.sigmoid(SWIGLU_ALPHA * x_glu) * (x_lin + 1.0)
        y = jnp.dot(act.astype(BF16), wd_ref[0], preferred_element_type=F32) + bd_ref[0]
        _store_pieces(y_ref, _pack_bf16_pairs(y))

    @pl.when(i >= nb_ref[0])
    def _():
        y_ref[...] = jnp.zeros_like(y_ref)


def _expert_mlp(xb, block_e, n_used, n_valid, wg, wl, bg, bl, wd, bd):
    n_blocks = xb.shape[0] // (MOE_BLOCK * SC_PIECES)
    n_rows = n_blocks * MOE_BLOCK
    rowm = lambda i, be, nb, nv: (i, 0)
    exp3 = lambda i, be, nb, nv: (be[i], 0, 0)
    grid_spec = pltpu.PrefetchScalarGridSpec(
        num_scalar_prefetch=3,
        grid=(n_blocks,),
        in_specs=[pl.BlockSpec((MOE_BLOCK * SC_PIECES, SC_ROW), rowm),
                  pl.BlockSpec((1, D_MODEL, D_FF), exp3),
                  pl.BlockSpec((1, D_MODEL, D_FF), exp3),
                  pl.BlockSpec((1, 1, D_FF), exp3),
                  pl.BlockSpec((1, 1, D_FF), exp3),
                  pl.BlockSpec((1, D_FF, D_MODEL), exp3),
                  pl.BlockSpec((1, 1, D_MODEL), exp3)],
        out_specs=pl.BlockSpec((MOE_BLOCK * SC_PIECES, SC_ROW), rowm),
    )
    return pl.pallas_call(
        _expert_kernel,
        grid_spec=grid_spec,
        out_shape=jax.ShapeDtypeStruct((n_rows * SC_PIECES, SC_ROW), U32),
        compiler_params=pltpu.CompilerParams(dimension_semantics=("arbitrary",),
                                             vmem_limit_bytes=VMEM_LIMIT),
        name="expert_mlp",
    )(block_e, n_used, n_valid, xb, wg, wl, bg, bl, wd, bd)


def _moe(h, logits, wg, wl, bg, bl, wd, bd):
    T = logits.shape[0]
    TK = T * TOP_K
    top_vals, top_idx = lax.top_k(logits, TOP_K)
    gates = jax.nn.softmax(top_vals, axis=-1)
    onehot = jax.nn.one_hot(top_idx, N_EXPERTS, dtype=jnp.int32)
    sel = jnp.sum(onehot, axis=1)
    before = jnp.cumsum(sel, axis=0) - sel
    counts = jnp.sum(sel, axis=0)
    padded = (counts + MOE_BLOCK - 1) // MOE_BLOCK * MOE_BLOCK
    pad_end = jnp.cumsum(padded)
    pad_start = pad_end - padded
    dest = jnp.sum(onehot * (before + pad_start[None, :])[:, None, :], axis=-1).astype(jnp.int32)
    n_blocks = (TK + MOE_BLOCK - 1) // MOE_BLOCK + N_EXPERTS
    n_rows = n_blocks * MOE_BLOCK
    block_start = jnp.arange(n_blocks, dtype=jnp.int32) * MOE_BLOCK
    block_e = jnp.minimum(jnp.sum((block_start[:, None] >= pad_end[None, :]).astype(jnp.int32), axis=1),
                          N_EXPERTS - 1)
    n_used = (pad_end[-1] // MOE_BLOCK).astype(jnp.int32).reshape(1)
    be_onehot = jax.nn.one_hot(block_e, N_EXPERTS, dtype=jnp.int32)
    n_valid = jnp.clip(jnp.sum(be_onehot * (counts + pad_start)[None, :], axis=1) - block_start,
                       0, MOE_BLOCK).astype(jnp.int32)
    idx = (dest.T[:, :, None] * SC_PIECES + jnp.arange(SC_PIECES, dtype=jnp.int32)[None, None, :]).reshape(-1)
    xb = _sc_scatter(h, idx, n_rows * SC_PIECES)
    yb = _expert_mlp(xb, block_e, n_used, n_valid, wg, wl, bg, bl, wd, bd)
    return yb, dest, gates


def _sc_gather(x, indices):
    n = indices.shape[0]
    width = x.shape[1]
    mesh = plsc.VectorSubcoreMesh(core_axis_name="core", subcore_axis_name="subcore")

    @functools.partial(pl.kernel, out_type=jax.ShapeDtypeStruct((n, width), x.dtype), mesh=mesh)
    def gather(x_hbm, i_hbm, o_hbm):
        def body(i_vmem, o_vmem):
            pltpu.sync_copy(x_hbm.at[i_vmem.at[0]], o_vmem)

        pltpu.emit_pipeline(
            body,
            grid=(n // SC_WINDOW,),
            in_specs=[pl.BlockSpec((1, SC_WINDOW), index_map=lambda i: (0, i))],
            out_specs=[pl.BlockSpec((SC_WINDOW, width), index_map=lambda i: (i, 0))],
            core_axis_name=("core", "subcore"),
            dimension_semantics=(pltpu.PARALLEL,),
        )(i_hbm, o_hbm)

    return gather(x, indices.reshape(1, n))


def _combine_kernel(gate_ref, x1_ref, g_ref, y4_ref, o_ref):
    tm = x1_ref.shape[0]
    x = x1_ref[...]
    for k in range(TOP_K):
        words = jnp.concatenate([y4_ref[pl.ds((k * SC_PIECES + c) * tm, tm), :] for c in range(SC_PIECES)],
                                axis=1)
        x = x + gate_ref[:, k:k + 1] * _unpack_bf16_pairs(words)
    o_ref[...] = _rms(x, g_ref[...])


def _combine_final(yb, dest, gates, x1, g):
    T = x1.shape[0]
    tm = COMBINE_TILE
    pieces = SC_PIECES
    d = dest.reshape(T // tm, tm, TOP_K).transpose(0, 2, 1)
    idx = (d[:, :, None, :] * pieces + jnp.arange(pieces, dtype=jnp.int32)[None, None, :, None]).reshape(-1)
    y4 = _sc_gather(yb, idx)
    row = lambda i: (i, 0)
    return pl.pallas_call(
        _combine_kernel,
        grid=(T // tm,),
        in_specs=[pl.BlockSpec((tm, TOP_K), row),
                  pl.BlockSpec((tm, D_MODEL), row),
                  pl.BlockSpec((1, D_MODEL), lambda i: (0, 0)),
                  pl.BlockSpec((tm * TOP_K * pieces, SC_ROW), row)],
        out_specs=pl.BlockSpec((tm, D_MODEL), row),
        out_shape=jax.ShapeDtypeStruct((T, D_MODEL), F32),
        compiler_params=pltpu.CompilerParams(dimension_semantics=("parallel",),
                                             vmem_limit_bytes=VMEM_LIMIT),
        name="moe_combine_final",
    )(gates, x1, g.reshape(1, D_MODEL), y4)


def _fft_tables(nc, nf):
    n = nc * nf
    kc = np.arange(nc, dtype=np.float64)
    a1 = 2.0 * np.pi * np.outer(kc, np.arange(nc // 2)) / nc
    c1, s1 = np.cos(a1), np.sin(a1)
    l1 = np.block([[c1, s1], [-s1, c1]])
    a2 = 2.0 * np.pi * np.outer(np.arange(nf), np.arange(nf)) / nf
    c2, s2 = np.cos(a2), np.sin(a2)
    m2 = np.block([[c2, s2], [-s2, c2]])
    m2i = np.block([[c2, -s2], [s2, c2]])
    a3 = 2.0 * np.pi * np.outer(np.arange(nc // 2), kc) / nc
    c3, s3 = np.cos(a3), np.sin(a3)
    l3 = np.block([[c3, -s3], [s3, c3]]) / n
    return tuple(jnp.asarray(m, F32).astype(BF16) for m in (l1, m2, m2i, l3))


def _twiddle(nc, nf):
    n = nc * nf
    ph = (jnp.arange(nf, dtype=jnp.int32)[:, None] * jnp.arange(nc, dtype=jnp.int32)[None, :]) % n
    ang = ph.astype(F32) * (2.0 * math.pi / n)
    tw = jnp.stack([jnp.cos(ang), jnp.sin(ang)], axis=1)
    return jnp.broadcast_to(tw[..., None], (nf, 2, nc, LANES))


def _lane_tile(t, width):
    return t if width == LANES else jnp.concatenate([t] * (width // LANES), axis=1)


def _fft1_kernel(z_ref, l1_ref, tw_ref, o_ref, *, nc):
    cb = o_ref.shape[-1]
    for j in range(SUB):
        rhs = jnp.concatenate([z_ref[0, :, j, :], z_ref[1, :, j, :]], axis=0).astype(BF16)
        a = jnp.dot(l1_ref[...], rhs, preferred_element_type=F32)
        ar, ai = a[:nc], a[nc:]
        twr = _lane_tile(tw_ref[j, 0], cb)
        twi = _lane_tile(tw_ref[j, 1], cb)
        o_ref[j, 0] = ar * twr + ai * twi
        o_ref[j, 1] = ai * twr - ar * twi


def _fft1(z, l1, tw, nc, nf):
    P, C = z.shape[0], z.shape[-1]
    cb = FFT_CB
    z_spec = pl.BlockSpec((None, 2, nc // 2, SUB, cb), lambda p, j, c: (p, 0, 0, j, c))
    return pl.pallas_call(
        functools.partial(_fft1_kernel, nc=nc),
        grid=(P, nf // SUB, C // cb),
        in_specs=[z_spec,
                  pl.BlockSpec(l1.shape, lambda p, j, c: (0, 0)),
                  pl.BlockSpec((SUB, 2, nc, LANES), lambda p, j, c: (j, 0, 0, 0))],
        out_specs=pl.BlockSpec((None, SUB, 2, nc, cb), lambda p, j, c: (p, j, 0, 0, c)),
        out_shape=jax.ShapeDtypeStruct((P, nf, 2, nc, C), F32),
        compiler_params=pltpu.CompilerParams(dimension_semantics=("parallel", "parallel", "parallel"),
                                             vmem_limit_bytes=VMEM_LIMIT),
        name="hyena_fft_stage1",
    )(z, l1, tw)


def _fft2_filter_kernel(a_ref, m2_ref, k_ref, *, nf):
    for j in range(SUB):
        rhs = jnp.concatenate([a_ref[:, 0, j, :], a_ref[:, 1, j, :]], axis=0).astype(BF16)
        x = jnp.dot(m2_ref[...], rhs, preferred_element_type=F32)
        k_ref[j, 0] = x[:nf]
        k_ref[j, 1] = x[nf:]


def _fft2_filter(a, m2, nc, nf):
    C = a.shape[-1]
    cb = FFT_CB
    return pl.pallas_call(
        functools.partial(_fft2_filter_kernel, nf=nf),
        grid=(nc // SUB, C // cb),
        in_specs=[pl.BlockSpec((None, nf, 2, SUB, cb), lambda k, c: (0, 0, 0, k, c)),
                  pl.BlockSpec(m2.shape, lambda k, c: (0, 0))],
        out_specs=pl.BlockSpec((SUB, 2, nf, cb), lambda k, c: (k, 0, 0, c)),
        out_shape=jax.ShapeDtypeStruct((nc, 2, nf, C), F32),
        compiler_params=pltpu.CompilerParams(dimension_semantics=("parallel", "parallel"),
                                             vmem_limit_bytes=VMEM_LIMIT),
        name="hyena_filter_spectrum",
    )(a, m2)


def _fft2_kernel(a_ref, k_ref, m2_ref, m2i_ref, o_ref, *, nf):
    for j in range(SUB):
        rhs = jnp.concatenate([a_ref[:, 0, j, :], a_ref[:, 1, j, :]], axis=0).astype(BF16)
        x = jnp.dot(m2_ref[...], rhs, preferred_element_type=F32)
        xr, xi = x[:nf], x[nf:]
        kr, ki = k_ref[j, 0], k_ref[j, 1]
        y = jnp.concatenate([xr * kr - xi * ki, xr * ki + xi * kr], axis=0).astype(BF16)
        b = jnp.dot(m2i_ref[...], y, preferred_element_type=F32)
        o_ref[:, 0, j, :] = b[:nf]
        o_ref[:, 1, j, :] = b[nf:]


def _fft2(a, kspec, m2, m2i, nc, nf):
    P, C = a.shape[0], a.shape[-1]
    cb = FFT_CB
    blk = pl.BlockSpec((None, nf, 2, SUB, cb), lambda p, k, c: (p, 0, 0, k, c))
    return pl.pallas_call(
        functools.partial(_fft2_kernel, nf=nf),
        grid=(P, nc // SUB, C // cb),
        in_specs=[blk,
                  pl.BlockSpec((SUB, 2, nf, cb), lambda p, k, c: (k, 0, 0, c)),
                  pl.BlockSpec(m2.shape, lambda p, k, c: (0, 0)),
                  pl.BlockSpec(m2i.shape, lambda p, k, c: (0, 0))],
        out_specs=blk,
        out_shape=jax.ShapeDtypeStruct(a.shape, F32),
        compiler_params=pltpu.CompilerParams(dimension_semantics=("parallel", "parallel", "parallel"),
                                             vmem_limit_bytes=VMEM_LIMIT),
        name="hyena_fft_stage2",
    )(a, kspec, m2, m2i)


def _fft3_kernel(b_ref, tw_ref, l3_ref, s_ref, x0_ref, bias_ref, o_ref, *, nc):
    cb = o_ref.shape[-1]
    half = nc // 2
    for j in range(SUB):
        br, bi = b_ref[j, 0], b_ref[j, 1]
        twr = _lane_tile(tw_ref[j, 0], cb)
        twi = _lane_tile(tw_ref[j, 1], cb)
        rhs = jnp.concatenate([br * twr - bi * twi, br * twi + bi * twr], axis=0).astype(BF16)
        y = jnp.dot(l3_ref[...], rhs, preferred_element_type=F32)
        for q in range(2):
            s = s_ref[q, :, j, :]
            o_ref[q, :, j, :] = x0_ref[q, :, j, :] * (y[q * half:(q + 1) * half] + s * bias_ref[...])


def _fft3(b, tw, l3, s5, x05, bias, nc, nf):
    P, C = b.shape[0], b.shape[-1]
    cb = FFT_CB
    seq = pl.BlockSpec((None, 2, nc // 2, SUB, cb), lambda p, j, c: (p, 0, 0, j, c))
    return pl.pallas_call(
        functools.partial(_fft3_kernel, nc=nc),
        grid=(P, nf // SUB, C // cb),
        in_specs=[pl.BlockSpec((None, SUB, 2, nc, cb), lambda p, j, c: (p, j, 0, 0, c)),
                  pl.BlockSpec((SUB, 2, nc, LANES), lambda p, j, c: (j, 0, 0, 0)),
                  pl.BlockSpec(l3.shape, lambda p, j, c: (0, 0)),
                  seq, seq,
                  pl.BlockSpec((1, cb), lambda p, j, c: (0, c))],
        out_specs=seq,
        out_shape=jax.ShapeDtypeStruct(s5.shape, F32),
        compiler_params=pltpu.CompilerParams(dimension_semantics=("parallel", "parallel", "parallel"),
                                             vmem_limit_bytes=VMEM_LIMIT),
        name="hyena_fft_stage3",
    )(b, tw, l3, s5, x05, bias)


def _fft_split(n):
    nf = 1 << (int(math.log2(n)) // 2)
    return n // nf, nf


def _filter_tables(nc):
    half = nc // 2
    kc = np.arange(nc, dtype=np.float64)[:, None]
    r = np.arange(half, dtype=np.float64)
    mats = []
    for rows_b, drop0 in ((half + (half - 1 - r), False), (nc - r, True)):
        a = 2.0 * np.pi * kc * np.concatenate([r, rows_b])[None, :] / nc
        m = np.concatenate([np.cos(a), -np.sin(a)], axis=0)
        if drop0:
            m[:, half] = 0.0
        mats.append(m)
    return tuple(jnp.asarray(m, F32).astype(BF16) for m in mats)


def _fft1_filter_kernel(hf_ref, hba_ref, hbb_ref, mb_ref, mb0_ref, tw_ref, o_ref, *, nc):
    cb = o_ref.shape[-1]
    m_first = jnp.where(pl.program_id(0) == 0, mb0_ref[...], mb_ref[...])
    for j in range(SUB):
        src = hbb_ref[:, 0, :] if j == 0 else hba_ref[:, SUB - j, :]
        rhs = jnp.concatenate([hf_ref[:, j, :], src], axis=0).astype(BF16)
        a = jnp.dot(m_first if j == 0 else mb_ref[...], rhs, preferred_element_type=F32)
        ar, ai = a[:nc], a[nc:]
        twr = _lane_tile(tw_ref[j, 0], cb)
        twi = _lane_tile(tw_ref[j, 1], cb)
        o_ref[j, 0] = ar * twr + ai * twi
        o_ref[j, 1] = ai * twr - ar * twi


def _fft1_filter(h_f, h_b, mb, mb0, tw, nc, nf):
    C = h_f.shape[-1]
    cb = FFT_CB
    nblk = nf // SUB
    hf3 = h_f.reshape(nc // 2, nf, C)
    hb3 = h_b.reshape(nc // 2, nf, C)
    blk = lambda f: pl.BlockSpec((nc // 2, SUB, cb), f)
    return pl.pallas_call(
        functools.partial(_fft1_filter_kernel, nc=nc),
        grid=(nblk, C // cb),
        in_specs=[blk(lambda j, c: (0, j, c)),
                  blk(lambda j, c: (0, nblk - 1 - j, c)),
                  blk(lambda j, c: (0, (nblk - j) % nblk, c)),
                  pl.BlockSpec(mb.shape, lambda j, c: (0, 0)),
                  pl.BlockSpec(mb0.shape, lambda j, c: (0, 0)),
                  pl.BlockSpec((SUB, 2, nc, LANES), lambda j, c: (j, 0, 0, 0))],
        out_specs=pl.BlockSpec((None, SUB, 2, nc, cb), lambda j, c: (0, j, 0, 0, c)),
        out_shape=jax.ShapeDtypeStruct((1, nf, 2, nc, C), F32),
        compiler_params=pltpu.CompilerParams(dimension_semantics=("parallel", "parallel"),
                                             vmem_limit_bytes=VMEM_LIMIT),
        name="hyena_filter_stage1",
    )(hf3, hb3, hb3, mb, mb0, tw)


def _hyena_conv(s, x0, h_f, h_b, bias):
    B, L, C = s.shape
    nc, nf = _fft_split(2 * L)
    tw = _twiddle(nc, nf)
    l1, m2, m2i, l3 = _fft_tables(nc, nf)
    mb, mb0 = _filter_tables(nc)
    kspec = _fft2_filter(_fft1_filter(h_f, h_b, mb, mb0, tw, nc, nf), m2, nc, nf)
    s5 = s.reshape(B // 2, 2, nc // 2, nf, C)
    x05 = x0.reshape(B // 2, 2, nc // 2, nf, C)
    a = _fft1(s5, l1, tw, nc, nf)
    b = _fft2(a, kspec, m2, m2i, nc, nf)
    return _fft3(b, tw, l3, s5, x05, bias.reshape(1, C), nc, nf).reshape(B, L, C)


def _hyena_filter(L, w1, b1, w2, b2, w3, freq):
    pos = jnp.arange(L, dtype=F32)
    t = jnp.linspace(0.0, 1.0, L, dtype=F32)[:, None]
    bands = jnp.linspace(1e-4, HY_BANDS - 1, HY_BANDS, dtype=F32)
    ang = (2.0 * math.pi / L) * pos[:, None] * bands[None, :]
    z = jnp.concatenate([t, jnp.cos(ang), -jnp.sin(ang)], axis=-1)
    h = jnp.sin(freq * (z @ w1 + b1))
    h = jnp.sin(freq * (h @ w2 + b2))
    h = h @ w3
    deltas = jnp.abs(jnp.linspace(math.log(HY_DECAY_TARGET) / HY_SLOW_PCT,
                                  math.log(HY_DECAY_TARGET) / HY_FAST_PCT, HY_CH, dtype=F32))
    window = jnp.exp(-t * deltas[None, :])
    h_f = h[:, :HY_CH] * window
    h_b = h[:, HY_CH:] * window
    l1 = jnp.sum(jnp.abs(h_f), axis=0) + jnp.sum(jnp.abs(h_b[1:]), axis=0)
    return h_f / l1, h_b / l1


def _split3(x):
    x1 = x.astype(BF16)
    r = x - x1.astype(F32)
    x2 = r.astype(BF16)
    x3 = (r - x2.astype(F32)).astype(BF16)
    return x1, x2, x3


def _mm(a, b):
    return jnp.dot(a.astype(BF16), b.astype(BF16), preferred_element_type=F32)


def _delta_kernel(qf_ref, kf_ref, vf_ref, gf_ref, qb_ref, kb_ref, vb_ref, gb_ref, of_ref, ob_ref, s_ref,
                  *, n_sub):
    @pl.when(pl.program_id(1) == 0)
    def _():
        s_ref[...] = jnp.zeros_like(s_ref)

    row = lax.broadcasted_iota(jnp.int32, (CHUNK, CHUNK), 0)
    col = lax.broadcasted_iota(jnp.int32, (CHUNK, CHUNK), 1)
    eye_f = jnp.where(row == col, 1.0, 0.0).astype(F32)
    dirs = ((qf_ref, kf_ref, vf_ref, gf_ref, of_ref, row >= col, row > col),
            (qb_ref, kb_ref, vb_ref, gb_ref, ob_ref, row <= col, row < col))

    ch = []
    for d, (q_ref, k_ref, v_ref, g_ref, _, incl, strict) in enumerate(dirs):
        tri = jnp.where(incl, 1.0, 0.0).astype(BF16)
        for c in range(n_sub):
            rows = slice(c * CHUNK, (c + 1) * CHUNK)
            gates = g_ref[rows, :]
            gl = 2 * DN_HEADS * d
            g1, g2, g3 = _split3(gates)
            gc_all = (jnp.dot(tri, g1, preferred_element_type=F32)
                      + jnp.dot(tri, g2, preferred_element_type=F32)
                      + jnp.dot(tri, g3, preferred_element_type=F32))
            gsum_all = jnp.sum(gates, axis=0, keepdims=True)
            for h in range(DN_HEADS):
                lanes = slice(h * DN_DK, (h + 1) * DN_DK)
                kh = k_ref[rows, lanes]
                beta = gates[:, gl + DN_HEADS + h:gl + DN_HEADS + h + 1]
                gc = gc_all[:, gl + h:gl + h + 1]
                g_last = gsum_all[:, gl + h:gl + h + 1]
                gc_b = jnp.broadcast_to(gc, (CHUNK, CHUNK))
                gc_row = jnp.sum(gc_b * eye_f, axis=0, keepdims=True)
                decay = jnp.where(incl, jnp.exp(jnp.minimum(gc_b - gc_row, 0.0)), 0.0)
                e_gc = jnp.exp(gc)
                kb = kh * beta
                ch.append(dict(d=d, c=c, h=h, lanes=lanes, rows=rows, strict=strict, decay=decay,
                               kh16=kh.astype(BF16), kb16=kb.astype(BF16),
                               q16=q_ref[rows, lanes].astype(BF16),
                               rhs=jnp.concatenate([v_ref[rows, lanes] * beta, kb * e_gc],
                                                   axis=1).astype(BF16),
                               qe=q_ref[rows, lanes] * e_gc,
                               kdT=(kh * jnp.exp(g_last - gc)).T.astype(BF16),
                               e_last=jnp.exp(g_last)))
    for x in ch:
        a = lax.dot_general(x["kb16"], x["kh16"], _NT, preferred_element_type=F32) * x["decay"]
        x["a"] = jnp.where(x["strict"], a, 0.0)
        x["qk"] = (lax.dot_general(x["q16"], x["kh16"], _NT, preferred_element_type=F32)
                   * x["decay"]).astype(BF16)
    for x in ch:
        x["t"] = eye_f - x["a"]
        x["p"] = _mm(x["a"], x["a"])
    for level in range(5):
        for x in ch:
            p16 = x["p"].astype(BF16)
            x["t"] = x["t"] + jnp.dot(x["t"].astype(BF16), p16, preferred_element_type=F32)
            if level < 4:
                x["p"] = jnp.dot(p16, p16, preferred_element_type=F32)
    for x in ch:
        uw = jnp.dot(x["t"].astype(BF16), x["rhs"], preferred_element_type=F32)
        x["u"] = uw[:, :DN_DV]
        x["wq"] = jnp.concatenate([uw[:, DN_DV:], x["qe"]], axis=0).astype(BF16)

    for step in range(n_sub):
        cur = [x for x in ch if x["c"] == (step if x["d"] == 0 else n_sub - 1 - step)]
        for x in cur:
            x["s"] = s_ref[x["d"], x["h"]]
            x["ws"] = jnp.dot(x["wq"], x["s"].astype(BF16), preferred_element_type=F32)
        for x in cur:
            x["vn"] = (x["u"] - x["ws"][:CHUNK]).astype(BF16)
        for x in cur:
            o = x["ws"][CHUNK:] + jnp.dot(x["qk"], x["vn"], preferred_element_type=F32)
            dirs[x["d"]][4][x["rows"], x["lanes"]] = o
            s_ref[x["d"], x["h"]] = (x["s"] * x["e_last"]
                                     + jnp.dot(x["kdT"], x["vn"], preferred_element_type=F32))


def _delta_scan(q, k, v, gates):
    B, L, _ = q.shape
    n_sub = DN_STEP_CHUNKS
    rows = n_sub * CHUNK
    nblk = L // rows
    fwd = pl.BlockSpec((None, rows, DN_W), lambda b, j: (b, j, 0))
    bwd = pl.BlockSpec((None, rows, DN_W), lambda b, j: (b, nblk - 1 - j, 0))
    gfwd = pl.BlockSpec((None, rows, LANES), lambda b, j: (b, j, 0))
    gbwd = pl.BlockSpec((None, rows, LANES), lambda b, j: (b, nblk - 1 - j, 0))
    out = jax.ShapeDtypeStruct((B, L, DN_W), F32)
    return pl.pallas_call(
        functools.partial(_delta_kernel, n_sub=n_sub),
        grid=(B, nblk),
        in_specs=[fwd, fwd, fwd, gfwd, bwd, bwd, bwd, gbwd],
        out_specs=[fwd, bwd],
        out_shape=[out, out],
        scratch_shapes=[pltpu.VMEM((2, DN_HEADS, DN_DK, DN_DV), F32)],
        compiler_params=pltpu.CompilerParams(dimension_semantics=("parallel", "arbitrary"),
                                             vmem_limit_bytes=VMEM_LIMIT),
        name="delta_scan",
    )(q, k, v, gates, q, k, v, gates)


def _trunk(x, w, hy_pos_w1, hy_pos_b1, hy_pos_w2, hy_pos_b2, hy_pos_w3, hy_sin_freq, hy_bias, dn_norm_w):
    B, L, D = x.shape
    T = B * L
    x2d = x.reshape(T, D)
    x0, s, q, k, v, z, dn_gates = _in_proj(x2d, L, w["norm_mix"], *w["in_proj"])
    seq = lambda a: a.reshape(B, L, a.shape[-1])
    h_f, h_b = _hyena_filter(L, hy_pos_w1, hy_pos_b1, hy_pos_w2, hy_pos_b2, hy_pos_w3, hy_sin_freq)
    y_hy = _hyena_conv(seq(s), seq(x0), h_f, h_b, hy_bias)
    o_f, o_b = _delta_scan(seq(q), seq(k), seq(v), seq(dn_gates))
    x1, h_ffn, logits = _out_proj(x2d, y_hy.reshape(T, HY_CH), o_f.reshape(T, DN_W), o_b.reshape(T, DN_W),
                                 z, dn_norm_w, w["out_hy"], w["out_dn"], w["norm_ffn"],
                                 w["router_hi"], w["router_lo"], w["router_b"])
    yb, dest, gates = _moe(h_ffn, logits[:, :N_EXPERTS], w["wg"], w["wl"], w["bg"], w["bl"], w["wd"],
                           w["bd"])
    return _combine_final(yb, dest, gates, x1, w["norm_final"]).reshape(B, L, D)


def kernel(x_prompt, x_sample, norm_mix_w, w_in, hy_conv_w, hy_conv_b, hy_pos_w1, hy_pos_b1, hy_pos_w2, hy_pos_b2, hy_pos_w3, hy_sin_freq, hy_bias, dn_conv_w, dn_a_log, dn_dt_bias, dn_norm_w, w_out, norm_ffn_w, w_router, b_router, w_gate_up, b_gate_up, w_down, b_down, norm_final_w):
    w_out16 = w_out[0].astype(BF16)
    wr = jnp.pad(w_router[0], ((0, 0), (0, LANES - N_EXPERTS)))
    wr_hi = wr.astype(BF16)
    wg, wl = _deinterleave(w_gate_up[0])
    w = {
        "norm_mix": norm_mix_w[0],
        "in_proj": _in_proj_params(w_in[0], hy_conv_w[0], hy_conv_b[0], dn_conv_w[0], dn_a_log[0],
                                   dn_dt_bias[0]),
        "out_hy": w_out16[:HY_CH],
        "out_dn": w_out16[HY_CH:],
        "norm_ffn": norm_ffn_w[0],
        "router_hi": wr_hi,
        "router_lo": (wr - wr_hi.astype(F32)).astype(BF16),
        "router_b": jnp.pad(b_router[0], (0, LANES - N_EXPERTS)).reshape(1, LANES),
        "wg": wg,
        "wl": wl,
        "bg": b_gate_up[0][:, 0::2].reshape(N_EXPERTS, 1, D_FF),
        "bl": b_gate_up[0][:, 1::2].reshape(N_EXPERTS, 1, D_FF),
        "wd": w_down[0].astype(BF16),
        "bd": b_down[0].reshape(N_EXPERTS, 1, D_MODEL),
        "norm_final": norm_final_w,
    }
    mix = (hy_pos_w1[0], hy_pos_b1[0], hy_pos_w2[0], hy_pos_b2[0], hy_pos_w3[0], hy_sin_freq[0],
           hy_bias[0], dn_norm_w[0])
    return (_trunk(x_prompt, w, *mix), _trunk(x_sample, w, *mix))
```

```python
import functools
import math

import jax
import jax.numpy as jnp
import numpy as np
from jax import lax
from jax.experimental import pallas as pl
from jax.experimental.pallas import tpu as pltpu
from jax.experimental.pallas import tpu_sc as plsc

D_MODEL = 1024
HY_CH = 512
DN_HEADS = 4
DN_DK = 128
DN_DV = 128
DN_QK = DN_HEADS * DN_DK
DN_W = DN_HEADS * DN_DV
HY_IN = 3 * HY_CH
DN_CONV = 2 * DN_QK + DN_W
N_GATE = 4 * DN_HEADS
SHORT_CONV = 3
CONV_COLS = HY_IN + DN_CONV
REST_COLS = DN_W + 128
HY_EMB = 33
HY_BANDS = (HY_EMB - 1) // 2
HY_DECAY_TARGET = 1e-2
HY_FAST_PCT = 0.3
HY_SLOW_PCT = 1.5
CHUNK = 64
N_EXPERTS = 32
TOP_K = 4
D_FF = D_MODEL
SWIGLU_ALPHA = 1.702
SWIGLU_LIMIT = 7.0
MOE_BLOCK = 512
EPS = 1e-6

LANES = 128
SUB = 8
ROW_TILE = 512
FFT_CB = 256
DN_STEP_CHUNKS = 4
DEINT_COLS = 512
COMBINE_TILE = 512
SC_WINDOW = 128
SC_STREAMS = 4
SC_ROW = LANES
SC_PIECES = D_MODEL // 2 // SC_ROW
VMEM_LIMIT = 56 * 1024 * 1024

F32 = jnp.float32
BF16 = jnp.bfloat16
U32 = jnp.uint32

_NT = (((1,), (1,)), ((), ()))


def _rms(x, g):
    return x * lax.rsqrt(jnp.mean(x * x, axis=-1, keepdims=True) + EPS) * g


def _silu(x):
    return x * jax.nn.sigmoid(x)


def _head_l2norm(x):
    parts = []
    for hd in range(DN_HEADS):
        xh = x[:, hd * DN_DK:(hd + 1) * DN_DK]
        parts.append(xh * lax.rsqrt(jnp.sum(xh * xh, axis=-1, keepdims=True) + EPS))
    return jnp.concatenate(parts, axis=1)


def _in_proj_kernel(xp_ref, x_ref, xn_ref, g_ref, wc_ref, wr_ref, cw_ref, cb_ref, gt_ref,
                    x0_ref, s_ref, q_ref, k_ref, v_ref, z_ref, gate_ref, p_scr, *, tiles_per_seq):
    i = pl.program_id(0)
    first = (i % tiles_per_seq) == 0
    last = (i % tiles_per_seq) == tiles_per_seq - 1
    g = g_ref[...]
    hp = jnp.where(first, 0.0, _rms(xp_ref[...], g))
    hn = jnp.where(last, 0.0, _rms(xn_ref[...], g))
    h_all = jnp.concatenate([hp, _rms(x_ref[...], g), hn], axis=0).astype(BF16)
    h = h_all[SUB:SUB + ROW_TILE]

    def project(c0):
        cols = slice(c0, c0 + HY_CH)
        p_scr[:, cols] = jnp.dot(h_all, wc_ref[:, cols], preferred_element_type=F32)

    def conv(c0):
        cols = slice(c0, c0 + HY_CH)
        return (p_scr[pl.ds(SUB - 1, ROW_TILE), cols] * cw_ref[0:1, cols]
                + p_scr[pl.ds(SUB, ROW_TILE), cols] * cw_ref[1:2, cols]
                + p_scr[pl.ds(SUB + 1, ROW_TILE), cols] * cw_ref[2:3, cols])

    project(0)
    project(HY_CH)
    x0_ref[...] = conv(0) + cb_ref[:, 0:HY_CH]
    project(2 * HY_CH)
    project(HY_IN)
    s_ref[...] = (conv(HY_CH) + cb_ref[:, HY_CH:2 * HY_CH]) * (conv(2 * HY_CH) + cb_ref[:, 2 * HY_CH:])
    project(HY_IN + DN_QK)
    q_ref[...] = _head_l2norm(_silu(conv(HY_IN))) * (DN_DK ** -0.5)
    project(HY_IN + 2 * DN_QK)
    k_ref[...] = _head_l2norm(_silu(conv(HY_IN + DN_QK)))
    rest = jnp.dot(h, wr_ref[...], preferred_element_type=F32)
    v_ref[...] = _silu(conv(HY_IN + 2 * DN_QK))
    z_ref[...] = rest[:, :DN_W]
    a = rest[:, DN_W:] + gt_ref[1:2]
    softplus = jnp.maximum(a, 0.0) + jnp.log(1.0 + jnp.exp(-jnp.abs(a)))
    gate_ref[...] = jnp.where(gt_ref[2:3] > 0.5, -gt_ref[0:1] * softplus, jax.nn.sigmoid(rest[:, DN_W:]))


def _in_proj(x2d, seq_len, g, w_conv, w_rest, conv_w, conv_b, gate_tab):
    T = x2d.shape[0]
    per = ROW_TILE // SUB
    last_sub = T // SUB - 1
    const = lambda i: (0, 0)
    row = lambda i: (i, 0)
    o512 = pl.BlockSpec((ROW_TILE, HY_CH), row)
    s512 = jax.ShapeDtypeStruct((T, HY_CH), F32)
    return pl.pallas_call(
        functools.partial(_in_proj_kernel, tiles_per_seq=seq_len // ROW_TILE),
        grid=(T // ROW_TILE,),
        in_specs=[pl.BlockSpec((SUB, D_MODEL), lambda i: (jnp.maximum(i * per - 1, 0), 0)),
                  pl.BlockSpec((ROW_TILE, D_MODEL), row),
                  pl.BlockSpec((SUB, D_MODEL), lambda i: (jnp.minimum((i + 1) * per, last_sub), 0)),
                  pl.BlockSpec((1, D_MODEL), const),
                  pl.BlockSpec((D_MODEL, CONV_COLS), const),
                  pl.BlockSpec((D_MODEL, REST_COLS), const),
                  pl.BlockSpec((SHORT_CONV, CONV_COLS), const),
                  pl.BlockSpec((1, HY_IN), const),
                  pl.BlockSpec((3, LANES), const)],
        out_specs=[o512, o512, o512, o512, o512, o512, pl.BlockSpec((ROW_TILE, LANES), row)],
        out_shape=[s512, s512, s512, s512, s512, s512, jax.ShapeDtypeStruct((T, LANES), F32)],
        scratch_shapes=[pltpu.VMEM((ROW_TILE + 2 * SUB, CONV_COLS), F32)],
        compiler_params=pltpu.CompilerParams(dimension_semantics=("parallel",),
                                             vmem_limit_bytes=VMEM_LIMIT),
        name="in_proj",
    )(x2d, x2d, x2d, g.reshape(1, D_MODEL), w_conv, w_rest, conv_w, conv_b, gate_tab)


def _in_proj_params(w_in, hy_conv_w, hy_conv_b, dn_conv_w, dn_a_log, dn_dt_bias):
    H = DN_HEADS
    w16 = w_in.astype(BF16)
    gc = w16[:, CONV_COLS + DN_W:]
    gc = jnp.concatenate([gc[:, 0:H], gc[:, 2 * H:3 * H], gc[:, H:2 * H], gc[:, 3 * H:]], axis=1)
    w_rest = jnp.concatenate([w16[:, CONV_COLS:CONV_COLS + DN_W],
                              jnp.pad(gc, ((0, 0), (0, LANES - N_GATE)))], axis=1)
    zero, one, pad = jnp.zeros((H,), F32), jnp.ones((H,), F32), jnp.zeros((LANES - N_GATE,), F32)
    gate_tab = jnp.stack([jnp.concatenate([jnp.exp(dn_a_log[0]), zero, jnp.exp(dn_a_log[1]), zero, pad]),
                          jnp.concatenate([dn_dt_bias[0], zero, dn_dt_bias[1], zero, pad]),
                          jnp.concatenate([one, zero, one, zero, pad])])
    return (w16[:, :CONV_COLS], w_rest, jnp.concatenate([hy_conv_w, dn_conv_w], axis=1),
            hy_conv_b.reshape(1, HY_IN), gate_tab)


def _out_proj_kernel(x_ref, yh_ref, of_ref, ob_ref, z_ref, nw_ref, wh_ref, wd_ref, g_ref, wrh_ref,
                     wrl_ref, br_ref, x1_ref, h_ref, lg_ref):
    o = of_ref[...] + ob_ref[...]
    heads = []
    for hd in range(DN_HEADS):
        oh = o[:, hd * DN_DV:(hd + 1) * DN_DV]
        heads.append(oh * lax.rsqrt(jnp.mean(oh * oh, axis=-1, keepdims=True) + EPS))
    z = z_ref[...]
    y_dn = jnp.concatenate(heads, axis=1) * nw_ref[...] * (z * jax.nn.sigmoid(z))
    x1 = (x_ref[...]
          + jnp.dot(yh_ref[...].astype(BF16), wh_ref[...], preferred_element_type=F32)
          + jnp.dot(y_dn.astype(BF16), wd_ref[...], preferred_element_type=F32))
    x1_ref[...] = x1
    h = _rms(x1, g_ref[...])
    h_hi = h.astype(BF16)
    h_lo = (h - h_hi.astype(F32)).astype(BF16)
    _store_pieces(h_ref, _pack_bf16_pairs(h))
    lg_ref[...] = (jnp.dot(h_hi, wrh_ref[...], preferred_element_type=F32)
                   + jnp.dot(h_lo, wrh_ref[...], preferred_element_type=F32)
                   + jnp.dot(h_hi, wrl_ref[...], preferred_element_type=F32)
                   + br_ref[...])


def _out_proj(x2d, y_hy, o_f, o_b, z, dn_norm_w, w_oh, w_od, g, wr_hi, wr_lo, br):
    T = x2d.shape[0]
    const = lambda i: (0, 0)
    row = lambda i: (i, 0)
    return pl.pallas_call(
        _out_proj_kernel,
        grid=(T // ROW_TILE,),
        in_specs=[pl.BlockSpec((ROW_TILE, D_MODEL), row),
                  pl.BlockSpec((ROW_TILE, HY_CH), row),
                  pl.BlockSpec((ROW_TILE, DN_W), row),
                  pl.BlockSpec((ROW_TILE, DN_W), row),
                  pl.BlockSpec((ROW_TILE, DN_W), row),
                  pl.BlockSpec((1, DN_W), const),
                  pl.BlockSpec((HY_CH, D_MODEL), const),
                  pl.BlockSpec((DN_W, D_MODEL), const),
                  pl.BlockSpec((1, D_MODEL), const),
                  pl.BlockSpec((D_MODEL, LANES), const),
                  pl.BlockSpec((D_MODEL, LANES), const),
                  pl.BlockSpec((1, LANES), const)],
        out_specs=[pl.BlockSpec((ROW_TILE, D_MODEL), row),
                   pl.BlockSpec((ROW_TILE * SC_PIECES, SC_ROW), row),
                   pl.BlockSpec((ROW_TILE, LANES), row)],
        out_shape=[jax.ShapeDtypeStruct((T, D_MODEL), F32),
                   jax.ShapeDtypeStruct((T * SC_PIECES, SC_ROW), U32),
                   jax.ShapeDtypeStruct((T, LANES), F32)],
        compiler_params=pltpu.CompilerParams(dimension_semantics=("parallel",),
                                             vmem_limit_bytes=VMEM_LIMIT),
        name="out_proj_router",
    )(x2d, y_hy, o_f, o_b, z, jnp.tile(dn_norm_w, DN_HEADS).reshape(1, DN_W), w_oh, w_od,
      g.reshape(1, D_MODEL), wr_hi, wr_lo, br)


def _deint_kernel(w_ref, p_ref, og_ref, ol_ref):
    half = DEINT_COLS // 2
    sel = jnp.dot(w_ref[0].astype(BF16), p_ref[...], preferred_element_type=F32)
    og_ref[0] = sel[:, :half].astype(BF16)
    ol_ref[0] = sel[:, half:].astype(BF16)


def _deinterleave(w_gate_up):
    half = DEINT_COLS // 2
    r = np.arange(DEINT_COLS)[:, None]
    c = np.arange(DEINT_COLS)[None, :]
    perm = jnp.asarray(np.where(c < half, r == 2 * c, r == 2 * (c - half) + 1), BF16)
    out = jax.ShapeDtypeStruct((N_EXPERTS, D_MODEL, D_FF), BF16)
    return pl.pallas_call(
        _deint_kernel,
        grid=(N_EXPERTS, 2 * D_FF // DEINT_COLS),
        in_specs=[pl.BlockSpec((1, D_MODEL, DEINT_COLS), lambda e, j: (e, 0, j)),
                  pl.BlockSpec((DEINT_COLS, DEINT_COLS), lambda e, j: (0, 0))],
        out_specs=[pl.BlockSpec((1, D_MODEL, half), lambda e, j: (e, 0, j)),
                   pl.BlockSpec((1, D_MODEL, half), lambda e, j: (e, 0, j))],
        out_shape=[out, out],
        compiler_params=pltpu.CompilerParams(dimension_semantics=("parallel", "parallel")),
        name="deinterleave_gate_up",
    )(w_gate_up, perm)


def _pack_bf16_pairs(x):
    half = x.shape[1] // 2
    lo = lax.bitcast_convert_type(x[:, :half].astype(BF16).astype(F32), U32)
    hi = lax.bitcast_convert_type(x[:, half:].astype(BF16).astype(F32), U32)
    return (hi & jnp.uint32(0xFFFF0000)) | (lo >> 16)


def _unpack_bf16_pairs(w):
    lo = lax.bitcast_convert_type(w << 16, F32)
    hi = lax.bitcast_convert_type(w & jnp.uint32(0xFFFF0000), F32)
    return jnp.concatenate([lo, hi], axis=1)


def _store_pieces(ref, x):
    rows = x.shape[0]
    for c in range(SC_PIECES):
        ref[pl.ds(c, rows, stride=SC_PIECES), :] = x[:, c * SC_ROW:(c + 1) * SC_ROW]


def _load_pieces(ref, rows):
    return jnp.concatenate([ref[pl.ds(c, rows, stride=SC_PIECES), :] for c in range(SC_PIECES)], axis=1)


def _sc_scatter(x, indices, n_out):
    n = indices.shape[0]
    rows, width = x.shape
    nsrc = rows // SC_WINDOW
    mesh = plsc.VectorSubcoreMesh(core_axis_name="core", subcore_axis_name="subcore")

    sub = SC_WINDOW // SC_STREAMS

    @functools.partial(pl.kernel, out_type=jax.ShapeDtypeStruct((n_out, width), x.dtype), mesh=mesh,
                       scratch_types=[pltpu.SemaphoreType.DMA((SC_STREAMS,))])
    def scatter(x_hbm, i_hbm, o_hbm, sems):
        def body(x_vmem, i_vmem):
            copies = [pltpu.async_copy(x_vmem.at[pl.ds(j * sub, sub)], o_hbm.at[i_vmem.at[j]], sems.at[j])
                      for j in range(SC_STREAMS)]
            for cp in copies:
                cp.wait()

        pltpu.emit_pipeline(
            body,
            grid=(n // SC_WINDOW,),
            in_specs=[pl.BlockSpec((SC_WINDOW, width), index_map=lambda i: (i % nsrc, 0)),
                      pl.BlockSpec((SC_STREAMS, sub), index_map=lambda i: (i, 0))],
            out_specs=[],
            core_axis_name=("core", "subcore"),
            dimension_semantics=(pltpu.PARALLEL,),
        )(x_hbm, i_hbm)

    return scatter(x, indices.reshape(n // sub, sub))


def _expert_kernel(be_ref, nb_ref, nv_ref, xb_ref, wg_ref, wl_ref, bg_ref, bl_ref, wd_ref, bd_ref, y_ref):
    i = pl.program_id(0)

    @pl.when(i < nb_ref[0])
    def _():
        row_id = lax.broadcasted_iota(jnp.int32, (MOE_BLOCK, 1), 0)
        xb = _unpack_bf16_pairs(_load_pieces(xb_ref, MOE_BLOCK))
        xb = jnp.where(row_id < nv_ref[i], xb, 0.0).astype(BF16)
        hg = jnp.dot(xb, wg_ref[0], preferred_element_type=F32) + bg_ref[0]
        hl = jnp.dot(xb, wl_ref[0], preferred_element_type=F32) + bl_ref[0]
        x_glu = jnp.minimum(hg, SWIGLU_LIMIT)
        x_lin = jnp.clip(hl, -SWIGLU_LIMIT, SWIGLU_LIMIT)
        act = x_glu * jax.nn.sigmoid(SWIGLU_ALPHA * x_glu) * (x_lin + 1.0)
        y = jnp.dot(act.astype(BF16), wd_ref[0], preferred_element_type=F32) + bd_ref[0]
        _store_pieces(y_ref, _pack_bf16_pairs(y))

    @pl.when(i >= nb_ref[0])
    def _():
        y_ref[...] = jnp.zeros_like(y_ref)


def _expert_mlp(xb, block_e, n_used, n_valid, wg, wl, bg, bl, wd, bd):
    n_blocks = xb.shape[0] // (MOE_BLOCK * SC_PIECES)
    n_rows = n_blocks * MOE_BLOCK
    rowm = lambda i, be, nb, nv: (i, 0)
    exp3 = lambda i, be, nb, nv: (be[i], 0, 0)
    grid_spec = pltpu.PrefetchScalarGridSpec(
        num_scalar_prefetch=3,
        grid=(n_blocks,),
        in_specs=[pl.BlockSpec((MOE_BLOCK * SC_PIECES, SC_ROW), rowm),
                  pl.BlockSpec((1, D_MODEL, D_FF), exp3),
                  pl.BlockSpec((1, D_MODEL, D_FF), exp3),
                  pl.BlockSpec((1, 1, D_FF), exp3),
                  pl.BlockSpec((1, 1, D_FF), exp3),
                  pl.BlockSpec((1, D_FF, D_MODEL), exp3),
                  pl.BlockSpec((1, 1, D_MODEL), exp3)],
        out_specs=pl.BlockSpec((MOE_BLOCK * SC_PIECES, SC_ROW), rowm),
    )
    return pl.pallas_call(
        _expert_kernel,
        grid_spec=grid_spec,
        out_shape=jax.ShapeDtypeStruct((n_rows * SC_PIECES, SC_ROW), U32),
        compiler_params=pltpu.CompilerParams(dimension_semantics=("arbitrary",),
                                             vmem_limit_bytes=VMEM_LIMIT),
        name="expert_mlp",
    )(block_e, n_used, n_valid, xb, wg, wl, bg, bl, wd, bd)


def _moe(h, logits, wg, wl, bg, bl, wd, bd):
    T = logits.shape[0]
    TK = T * TOP_K
    top_vals, top_idx = lax.top_k(logits, TOP_K)
    gates = jax.nn.softmax(top_vals, axis=-1)
    onehot = jax.nn.one_hot(top_idx, N_EXPERTS, dtype=jnp.int32)
    sel = jnp.sum(onehot, axis=1)
    before = jnp.cumsum(sel, axis=0) - sel
    counts = jnp.sum(sel, axis=0)
    padded = (counts + MOE_BLOCK - 1) // MOE_BLOCK * MOE_BLOCK
    pad_end = jnp.cumsum(padded)
    pad_start = pad_end - padded
    dest = jnp.sum(onehot * (before + pad_start[None, :])[:, None, :], axis=-1).astype(jnp.int32)
    n_blocks = (TK + MOE_BLOCK - 1) // MOE_BLOCK + N_EXPERTS
    n_rows = n_blocks * MOE_BLOCK
    block_start = jnp.arange(n_blocks, dtype=jnp.int32) * MOE_BLOCK
    block_e = jnp.minimum(jnp.sum((block_start[:, None] >= pad_end[None, :]).astype(jnp.int32), axis=1),
                          N_EXPERTS - 1)
    n_used = (pad_end[-1] // MOE_BLOCK).astype(jnp.int32).reshape(1)
    be_onehot = jax.nn.one_hot(block_e, N_EXPERTS, dtype=jnp.int32)
    n_valid = jnp.clip(jnp.sum(be_onehot * (counts + pad_start)[None, :], axis=1) - block_start,
                       0, MOE_BLOCK).astype(jnp.int32)
    idx = (dest.T[:, :, None] * SC_PIECES + jnp.arange(SC_PIECES, dtype=jnp.int32)[None, None, :]).reshape(-1)
    xb = _sc_scatter(h, idx, n_rows * SC_PIECES)
    yb = _expert_mlp(xb, block_e, n_used, n_valid, wg, wl, bg, bl, wd, bd)
    return yb, dest, gates


def _sc_gather(x, indices):
    n = indices.shape[0]
    width = x.shape[1]
    mesh = plsc.VectorSubcoreMesh(core_axis_name="core", subcore_axis_name="subcore")

    sub = SC_WINDOW // SC_STREAMS

    @functools.partial(pl.kernel, out_type=jax.ShapeDtypeStruct((n, width), x.dtype), mesh=mesh,
                       scratch_types=[pltpu.SemaphoreType.DMA((SC_STREAMS,))])
    def gather(x_hbm, i_hbm, o_hbm, sems):
        def body(i_vmem, o_vmem):
            copies = [pltpu.async_copy(x_hbm.at[i_vmem.at[j]], o_vmem.at[pl.ds(j * sub, sub)], sems.at[j])
                      for j in range(SC_STREAMS)]
            for cp in copies:
                cp.wait()

        pltpu.emit_pipeline(
            body,
            grid=(n // SC_WINDOW,),
            in_specs=[pl.BlockSpec((SC_STREAMS, sub), index_map=lambda i: (i, 0))],
            out_specs=[pl.BlockSpec((SC_WINDOW, width), index_map=lambda i: (i, 0))],
            core_axis_name=("core", "subcore"),
            dimension_semantics=(pltpu.PARALLEL,),
        )(i_hbm, o_hbm)

    return gather(x, indices.reshape(n // sub, sub))


def _combine_kernel(gate_ref, x1_ref, g_ref, y4_ref, o_ref):
    tm = x1_ref.shape[0]
    x = x1_ref[...]
    for k in range(TOP_K):
        words = jnp.concatenate([y4_ref[pl.ds((k * SC_PIECES + c) * tm, tm), :] for c in range(SC_PIECES)],
                                axis=1)
        x = x + gate_ref[:, k:k + 1] * _unpack_bf16_pairs(words)
    o_ref[...] = _rms(x, g_ref[...])


def _combine_final(yb, dest, gates, x1, g):
    T = x1.shape[0]
    tm = COMBINE_TILE
    pieces = SC_PIECES
    d = dest.reshape(T // tm, tm, TOP_K).transpose(0, 2, 1)
    idx = (d[:, :, None, :] * pieces + jnp.arange(pieces, dtype=jnp.int32)[None, None, :, None]).reshape(-1)
    y4 = _sc_gather(yb, idx)
    row = lambda i: (i, 0)
    return pl.pallas_call(
        _combine_kernel,
        grid=(T // tm,),
        in_specs=[pl.BlockSpec((tm, TOP_K), row),
                  pl.BlockSpec((tm, D_MODEL), row),
                  pl.BlockSpec((1, D_MODEL), lambda i: (0, 0)),
                  pl.BlockSpec((tm * TOP_K * pieces, SC_ROW), row)],
        out_specs=pl.BlockSpec((tm, D_MODEL), row),
        out_shape=jax.ShapeDtypeStruct((T, D_MODEL), F32),
        compiler_params=pltpu.CompilerParams(dimension_semantics=("parallel",),
                                             vmem_limit_bytes=VMEM_LIMIT),
        name="moe_combine_final",
    )(gates, x1, g.reshape(1, D_MODEL), y4)


def _fft_tables(nc, nf):
    n = nc * nf
    kc = np.arange(nc, dtype=np.float64)
    a1 = 2.0 * np.pi * np.outer(kc, np.arange(nc // 2)) / nc
    c1, s1 = np.cos(a1), np.sin(a1)
    l1 = np.block([[c1, s1], [-s1, c1]])
    a2 = 2.0 * np.pi * np.outer(np.arange(nf), np.arange(nf)) / nf
    c2, s2 = np.cos(a2), np.sin(a2)
    m2 = np.block([[c2, s2], [-s2, c2]])
    m2i = np.block([[c2, -s2], [s2, c2]])
    a3 = 2.0 * np.pi * np.outer(np.arange(nc // 2), kc) / nc
    c3, s3 = np.cos(a3), np.sin(a3)
    l3 = np.block([[c3, -s3], [s3, c3]]) / n
    return tuple(jnp.asarray(m, F32).astype(BF16) for m in (l1, m2, m2i, l3))


def _twiddle(nc, nf):
    n = nc * nf
    ph = (jnp.arange(nf, dtype=jnp.int32)[:, None] * jnp.arange(nc, dtype=jnp.int32)[None, :]) % n
    ang = ph.astype(F32) * (2.0 * math.pi / n)
    tw = jnp.stack([jnp.cos(ang), jnp.sin(ang)], axis=1)
    return jnp.broadcast_to(tw[..., None], (nf, 2, nc, LANES))


def _lane_tile(t, width):
    return t if width == LANES else jnp.concatenate([t] * (width // LANES), axis=1)


def _fft1_kernel(z_ref, l1_ref, tw_ref, o_ref, *, nc):
    cb = o_ref.shape[-1]
    for j in range(SUB):
        rhs = jnp.concatenate([z_ref[0, :, j, :], z_ref[1, :, j, :]], axis=0).astype(BF16)
        a = jnp.dot(l1_ref[...], rhs, preferred_element_type=F32)
        ar, ai = a[:nc], a[nc:]
        twr = _lane_tile(tw_ref[j, 0], cb)
        twi = _lane_tile(tw_ref[j, 1], cb)
        o_ref[j, 0] = ar * twr + ai * twi
        o_ref[j, 1] = ai * twr - ar * twi


def _fft1(z, l1, tw, nc, nf):
    P, C = z.shape[0], z.shape[-1]
    cb = FFT_CB
    z_spec = pl.BlockSpec((None, 2, nc // 2, SUB, cb), lambda p, j, c: (p, 0, 0, j, c))
    return pl.pallas_call(
        functools.partial(_fft1_kernel, nc=nc),
        grid=(P, nf // SUB, C // cb),
        in_specs=[z_spec,
                  pl.BlockSpec(l1.shape, lambda p, j, c: (0, 0)),
                  pl.BlockSpec((SUB, 2, nc, LANES), lambda p, j, c: (j, 0, 0, 0))],
        out_specs=pl.BlockSpec((None, SUB, 2, nc, cb), lambda p, j, c: (p, j, 0, 0, c)),
        out_shape=jax.ShapeDtypeStruct((P, nf, 2, nc, C), F32),
        compiler_params=pltpu.CompilerParams(dimension_semantics=("parallel", "parallel", "parallel"),
                                             vmem_limit_bytes=VMEM_LIMIT),
        name="hyena_fft_stage1",
    )(z, l1, tw)


def _fft2_filter_kernel(a_ref, m2_ref, k_ref, *, nf):
    for j in range(SUB):
        rhs = jnp.concatenate([a_ref[:, 0, j, :], a_ref[:, 1, j, :]], axis=0).astype(BF16)
        x = jnp.dot(m2_ref[...], rhs, preferred_element_type=F32)
        k_ref[j, 0] = x[:nf]
        k_ref[j, 1] = x[nf:]


def _fft2_filter(a, m2, nc, nf):
    C = a.shape[-1]
    cb = FFT_CB
    return pl.pallas_call(
        functools.partial(_fft2_filter_kernel, nf=nf),
        grid=(nc // SUB, C // cb),
        in_specs=[pl.BlockSpec((None, nf, 2, SUB, cb), lambda k, c: (0, 0, 0, k, c)),
                  pl.BlockSpec(m2.shape, lambda k, c: (0, 0))],
        out_specs=pl.BlockSpec((SUB, 2, nf, cb), lambda k, c: (k, 0, 0, c)),
        out_shape=jax.ShapeDtypeStruct((nc, 2, nf, C), F32),
        compiler_params=pltpu.CompilerParams(dimension_semantics=("parallel", "parallel"),
                                             vmem_limit_bytes=VMEM_LIMIT),
        name="hyena_filter_spectrum",
    )(a, m2)


def _fft2_kernel(a_ref, k_ref, m2_ref, m2i_ref, o_ref, *, nf):
    for j in range(SUB):
        rhs = jnp.concatenate([a_ref[:, 0, j, :], a_ref[:, 1, j, :]], axis=0).astype(BF16)
        x = jnp.dot(m2_ref[...], rhs, preferred_element_type=F32)
        xr, xi = x[:nf], x[nf:]
        kr, ki = k_ref[j, 0], k_ref[j, 1]
        y = jnp.concatenate([xr * kr - xi * ki, xr * ki + xi * kr], axis=0).astype(BF16)
        b = jnp.dot(m2i_ref[...], y, preferred_element_type=F32)
        o_ref[:, 0, j, :] = b[:nf]
        o_ref[:, 1, j, :] = b[nf:]


def _fft2(a, kspec, m2, m2i, nc, nf):
    P, C = a.shape[0], a.shape[-1]
    cb = FFT_CB
    blk = pl.BlockSpec((None, nf, 2, SUB, cb), lambda p, k, c: (p, 0, 0, k, c))
    return pl.pallas_call(
        functools.partial(_fft2_kernel, nf=nf),
        grid=(P, nc // SUB, C // cb),
        in_specs=[blk,
                  pl.BlockSpec((SUB, 2, nf, cb), lambda p, k, c: (k, 0, 0, c)),
                  pl.BlockSpec(m2.shape, lambda p, k, c: (0, 0)),
                  pl.BlockSpec(m2i.shape, lambda p, k, c: (0, 0))],
        out_specs=blk,
        out_shape=jax.ShapeDtypeStruct(a.shape, F32),
        compiler_params=pltpu.CompilerParams(dimension_semantics=("parallel", "parallel", "parallel"),
                                             vmem_limit_bytes=VMEM_LIMIT),
        name="hyena_fft_stage2",
    )(a, kspec, m2, m2i)


def _fft3_kernel(b_ref, tw_ref, l3_ref, s_ref, x0_ref, bias_ref, o_ref, *, nc):
    cb = o_ref.shape[-1]
    half = nc // 2
    for j in range(SUB):
        br, bi = b_ref[j, 0], b_ref[j, 1]
        twr = _lane_tile(tw_ref[j, 0], cb)
        twi = _lane_tile(tw_ref[j, 1], cb)
        rhs = jnp.concatenate([br * twr - bi * twi, br * twi + bi * twr], axis=0).astype(BF16)
        y = jnp.dot(l3_ref[...], rhs, preferred_element_type=F32)
        for q in range(2):
            s = s_ref[q, :, j, :]
            o_ref[q, :, j, :] = x0_ref[q, :, j, :] * (y[q * half:(q + 1) * half] + s * bias_ref[...])


def _fft3(b, tw, l3, s5, x05, bias, nc, nf):
    P, C = b.shape[0], b.shape[-1]
    cb = FFT_CB
    seq = pl.BlockSpec((None, 2, nc // 2, SUB, cb), lambda p, j, c: (p, 0, 0, j, c))
    return pl.pallas_call(
        functools.partial(_fft3_kernel, nc=nc),
        grid=(P, nf // SUB, C // cb),
        in_specs=[pl.BlockSpec((None, SUB, 2, nc, cb), lambda p, j, c: (p, j, 0, 0, c)),
                  pl.BlockSpec((SUB, 2, nc, LANES), lambda p, j, c: (j, 0, 0, 0)),
                  pl.BlockSpec(l3.shape, lambda p, j, c: (0, 0)),
                  seq, seq,
                  pl.BlockSpec((1, cb), lambda p, j, c: (0, c))],
        out_specs=seq,
        out_shape=jax.ShapeDtypeStruct(s5.shape, F32),
        compiler_params=pltpu.CompilerParams(dimension_semantics=("parallel", "parallel", "parallel"),
                                             vmem_limit_bytes=VMEM_LIMIT),
        name="hyena_fft_stage3",
    )(b, tw, l3, s5, x05, bias)


def _fft_split(n):
    nf = 1 << (int(math.log2(n)) // 2)
    return n // nf, nf


def _filter_tables(nc):
    half = nc // 2
    kc = np.arange(nc, dtype=np.float64)[:, None]
    r = np.arange(half, dtype=np.float64)
    mats = []
    for rows_b, drop0 in ((half + (half - 1 - r), False), (nc - r, True)):
        a = 2.0 * np.pi * kc * np.concatenate([r, rows_b])[None, :] / nc
        m = np.concatenate([np.cos(a), -np.sin(a)], axis=0)
        if drop0:
            m[:, half] = 0.0
        mats.append(m)
    return tuple(jnp.asarray(m, F32).astype(BF16) for m in mats)


def _fft1_filter_kernel(hf_ref, hba_ref, hbb_ref, mb_ref, mb0_ref, tw_ref, o_ref, *, nc):
    cb = o_ref.shape[-1]
    m_first = jnp.where(pl.program_id(0) == 0, mb0_ref[...], mb_ref[...])
    for j in range(SUB):
        src = hbb_ref[:, 0, :] if j == 0 else hba_ref[:, SUB - j, :]
        rhs = jnp.concatenate([hf_ref[:, j, :], src], axis=0).astype(BF16)
        a = jnp.dot(m_first if j == 0 else mb_ref[...], rhs, preferred_element_type=F32)
        ar, ai = a[:nc], a[nc:]
        twr = _lane_tile(tw_ref[j, 0], cb)
        twi = _lane_tile(tw_ref[j, 1], cb)
        o_ref[j, 0] = ar * twr + ai * twi
        o_ref[j, 1] = ai * twr - ar * twi


def _fft1_filter(h_f, h_b, mb, mb0, tw, nc, nf):
    C = h_f.shape[-1]
    cb = FFT_CB
    nblk = nf // SUB
    hf3 = h_f.reshape(nc // 2, nf, C)
    hb3 = h_b.reshape(nc // 2, nf, C)
    blk = lambda f: pl.BlockSpec((nc // 2, SUB, cb), f)
    return pl.pallas_call(
        functools.partial(_fft1_filter_kernel, nc=nc),
        grid=(nblk, C // cb),
        in_specs=[blk(lambda j, c: (0, j, c)),
                  blk(lambda j, c: (0, nblk - 1 - j, c)),
                  blk(lambda j, c: (0, (nblk - j) % nblk, c)),
                  pl.BlockSpec(mb.shape, lambda j, c: (0, 0)),
                  pl.BlockSpec(mb0.shape, lambda j, c: (0, 0)),
                  pl.BlockSpec((SUB, 2, nc, LANES), lambda j, c: (j, 0, 0, 0))],
        out_specs=pl.BlockSpec((None, SUB, 2, nc, cb), lambda j, c: (0, j, 0, 0, c)),
        out_shape=jax.ShapeDtypeStruct((1, nf, 2, nc, C), F32),
        compiler_params=pltpu.CompilerParams(dimension_semantics=("parallel", "parallel"),
                                             vmem_limit_bytes=VMEM_LIMIT),
        name="hyena_filter_stage1",
    )(hf3, hb3, hb3, mb, mb0, tw)


def _hyena_conv(s, x0, h_f, h_b, bias):
    B, L, C = s.shape
    nc, nf = _fft_split(2 * L)
    tw = _twiddle(nc, nf)
    l1, m2, m2i, l3 = _fft_tables(nc, nf)
    mb, mb0 = _filter_tables(nc)
    kspec = _fft2_filter(_fft1_filter(h_f, h_b, mb, mb0, tw, nc, nf), m2, nc, nf)
    s5 = s.reshape(B // 2, 2, nc // 2, nf, C)
    x05 = x0.reshape(B // 2, 2, nc // 2, nf, C)
    a = _fft1(s5, l1, tw, nc, nf)
    b = _fft2(a, kspec, m2, m2i, nc, nf)
    return _fft3(b, tw, l3, s5, x05, bias.reshape(1, C), nc, nf).reshape(B, L, C)


def _hyena_filter(L, w1, b1, w2, b2, w3, freq):
    pos = jnp.arange(L, dtype=F32)
    t = jnp.linspace(0.0, 1.0, L, dtype=F32)[:, None]
    bands = jnp.linspace(1e-4, HY_BANDS - 1, HY_BANDS, dtype=F32)
    ang = (2.0 * math.pi / L) * pos[:, None] * bands[None, :]
    z = jnp.concatenate([t, jnp.cos(ang), -jnp.sin(ang)], axis=-1)
    h = jnp.sin(freq * (z @ w1 + b1))
    h = jnp.sin(freq * (h @ w2 + b2))
    h = h @ w3
    deltas = jnp.abs(jnp.linspace(math.log(HY_DECAY_TARGET) / HY_SLOW_PCT,
                                  math.log(HY_DECAY_TARGET) / HY_FAST_PCT, HY_CH, dtype=F32))
    window = jnp.exp(-t * deltas[None, :])
    h_f = h[:, :HY_CH] * window
    h_b = h[:, HY_CH:] * window
    l1 = jnp.sum(jnp.abs(h_f), axis=0) + jnp.sum(jnp.abs(h_b[1:]), axis=0)
    return h_f / l1, h_b / l1


def _split3(x):
    x1 = x.astype(BF16)
    r = x - x1.astype(F32)
    x2 = r.astype(BF16)
    x3 = (r - x2.astype(F32)).astype(BF16)
    return x1, x2, x3


def _mm(a, b):
    return jnp.dot(a.astype(BF16), b.astype(BF16), preferred_element_type=F32)


def _delta_kernel(qf_ref, kf_ref, vf_ref, gf_ref, qb_ref, kb_ref, vb_ref, gb_ref, of_ref, ob_ref, s_ref,
                  *, n_sub):
    @pl.when(pl.program_id(1) == 0)
    def _():
        s_ref[...] = jnp.zeros_like(s_ref)

    row = lax.broadcasted_iota(jnp.int32, (CHUNK, CHUNK), 0)
    col = lax.broadcasted_iota(jnp.int32, (CHUNK, CHUNK), 1)
    eye_f = jnp.where(row == col, 1.0, 0.0).astype(F32)
    dirs = ((qf_ref, kf_ref, vf_ref, gf_ref, of_ref, row >= col, row > col),
            (qb_ref, kb_ref, vb_ref, gb_ref, ob_ref, row <= col, row < col))

    ch = []
    for d, (q_ref, k_ref, v_ref, g_ref, _, incl, strict) in enumerate(dirs):
        tri = jnp.where(incl, 1.0, 0.0).astype(BF16)
        for c in range(n_sub):
            rows = slice(c * CHUNK, (c + 1) * CHUNK)
            gates = g_ref[rows, :]
            gl = 2 * DN_HEADS * d
            g1, g2, g3 = _split3(gates)
            gc_all = (jnp.dot(tri, g1, preferred_element_type=F32)
                      + jnp.dot(tri, g2, preferred_element_type=F32)
                      + jnp.dot(tri, g3, preferred_element_type=F32))
            gsum_all = jnp.sum(gates, axis=0, keepdims=True)
            for h in range(DN_HEADS):
                lanes = slice(h * DN_DK, (h + 1) * DN_DK)
                kh = k_ref[rows, lanes]
                beta = gates[:, gl + DN_HEADS + h:gl + DN_HEADS + h + 1]
                gc = gc_all[:, gl + h:gl + h + 1]
                g_last = gsum_all[:, gl + h:gl + h + 1]
                gc_b = jnp.broadcast_to(gc, (CHUNK, CHUNK))
                gc_row = jnp.sum(gc_b * eye_f, axis=0, keepdims=True)
                decay = jnp.where(incl, jnp.exp(jnp.minimum(gc_b - gc_row, 0.0)), 0.0)
                e_gc = jnp.exp(gc)
                kb = kh * beta
                ch.append(dict(d=d, c=c, h=h, lanes=lanes, rows=rows, strict=strict, decay=decay,
                               kh16=kh.astype(BF16), kb16=kb.astype(BF16),
                               q16=q_ref[rows, lanes].astype(BF16),
                               rhs=jnp.concatenate([v_ref[rows, lanes] * beta, kb * e_gc],
                                                   axis=1).astype(BF16),
                               qe=q_ref[rows, lanes] * e_gc,
                               kdT=(kh * jnp.exp(g_last - gc)).T.astype(BF16),
                               e_last=jnp.exp(g_last)))
    for x in ch:
        a = lax.dot_general(x["kb16"], x["kh16"], _NT, preferred_element_type=F32) * x["decay"]
        x["a"] = jnp.where(x["strict"], a, 0.0)
        x["qk"] = (lax.dot_general(x["q16"], x["kh16"], _NT, preferred_element_type=F32)
                   * x["decay"]).astype(BF16)
    for x in ch:
        x["t"] = eye_f - x["a"]
        x["p"] = _mm(x["a"], x["a"])
    for level in range(5):
        for x in ch:
            p16 = x["p"].astype(BF16)
            x["t"] = x["t"] + jnp.dot(x["t"].astype(BF16), p16, preferred_element_type=F32)
            if level < 4:
                x["p"] = jnp.dot(p16, p16, preferred_element_type=F32)
    for x in ch:
        uw = jnp.dot(x["t"].astype(BF16), x["rhs"], preferred_element_type=F32)
        x["u"] = uw[:, :DN_DV]
        x["wq"] = jnp.concatenate([uw[:, DN_DV:], x["qe"]], axis=0).astype(BF16)

    for step in range(n_sub):
        cur = [x for x in ch if x["c"] == (step if x["d"] == 0 else n_sub - 1 - step)]
        for x in cur:
            x["s"] = s_ref[x["d"], x["h"]]
            x["ws"] = jnp.dot(x["wq"], x["s"].astype(BF16), preferred_element_type=F32)
        for x in cur:
            x["vn"] = (x["u"] - x["ws"][:CHUNK]).astype(BF16)
        for x in cur:
            o = x["ws"][CHUNK:] + jnp.dot(x["qk"], x["vn"], preferred_element_type=F32)
            dirs[x["d"]][4][x["rows"], x["lanes"]] = o
            s_ref[x["d"], x["h"]] = (x["s"] * x["e_last"]
                                     + jnp.dot(x["kdT"], x["vn"], preferred_element_type=F32))


def _delta_scan(q, k, v, gates):
    B, L, _ = q.shape
    n_sub = DN_STEP_CHUNKS
    rows = n_sub * CHUNK
    nblk = L // rows
    fwd = pl.BlockSpec((None, rows, DN_W), lambda b, j: (b, j, 0))
    bwd = pl.BlockSpec((None, rows, DN_W), lambda b, j: (b, nblk - 1 - j, 0))
    gfwd = pl.BlockSpec((None, rows, LANES), lambda b, j: (b, j, 0))
    gbwd = pl.BlockSpec((None, rows, LANES), lambda b, j: (b, nblk - 1 - j, 0))
    out = jax.ShapeDtypeStruct((B, L, DN_W), F32)
    return pl.pallas_call(
        functools.partial(_delta_kernel, n_sub=n_sub),
        grid=(B, nblk),
        in_specs=[fwd, fwd, fwd, gfwd, bwd, bwd, bwd, gbwd],
        out_specs=[fwd, bwd],
        out_shape=[out, out],
        scratch_shapes=[pltpu.VMEM((2, DN_HEADS, DN_DK, DN_DV), F32)],
        compiler_params=pltpu.CompilerParams(dimension_semantics=("parallel", "arbitrary"),
                                             vmem_limit_bytes=VMEM_LIMIT),
        name="delta_scan",
    )(q, k, v, gates, q, k, v, gates)


def _trunk(x, w, hy_pos_w1, hy_pos_b1, hy_pos_w2, hy_pos_b2, hy_pos_w3, hy_sin_freq, hy_bias, dn_norm_w):
    B, L, D = x.shape
    T = B * L
    x2d = x.reshape(T, D)
    x0, s, q, k, v, z, dn_gates = _in_proj(x2d, L, w["norm_mix"], *w["in_proj"])
    seq = lambda a: a.reshape(B, L, a.shape[-1])
    h_f, h_b = _hyena_filter(L, hy_pos_w1, hy_pos_b1, hy_pos_w2, hy_pos_b2, hy_pos_w3, hy_sin_freq)
    y_hy = _hyena_conv(seq(s), seq(x0), h_f, h_b, hy_bias)
    o_f, o_b = _delta_scan(seq(q), seq(k), seq(v), seq(dn_gates))
    x1, h_ffn, logits = _out_proj(x2d, y_hy.reshape(T, HY_CH), o_f.reshape(T, DN_W), o_b.reshape(T, DN_W),
                                 z, dn_norm_w, w["out_hy"], w["out_dn"], w["norm_ffn"],
                                 w["router_hi"], w["router_lo"], w["router_b"])
    yb, dest, gates = _moe(h_ffn, logits[:, :N_EXPERTS], w["wg"], w["wl"], w["bg"], w["bl"], w["wd"],
                           w["bd"])
    return _combine_final(yb, dest, gates, x1, w["norm_final"]).reshape(B, L, D)


def kernel(x_prompt, x_sample, norm_mix_w, w_in, hy_conv_w, hy_conv_b, hy_pos_w1, hy_pos_b1, hy_pos_w2, hy_pos_b2, hy_pos_w3, hy_sin_freq, hy_bias, dn_conv_w, dn_a_log, dn_dt_bias, dn_norm_w, w_out, norm_ffn_w, w_router, b_router, w_gate_up, b_gate_up, w_down, b_down, norm_final_w):
    w_out16 = w_out[0].astype(BF16)
    wr = jnp.pad(w_router[0], ((0, 0), (0, LANES - N_EXPERTS)))
    wr_hi = wr.astype(BF16)
    wg, wl = _deinterleave(w_gate_up[0])
    w = {
        "norm_mix": norm_mix_w[0],
        "in_proj": _in_proj_params(w_in[0], hy_conv_w[0], hy_conv_b[0], dn_conv_w[0], dn_a_log[0],
                                   dn_dt_bias[0]),
        "out_hy": w_out16[:HY_CH],
        "out_dn": w_out16[HY_CH:],
        "norm_ffn": norm_ffn_w[0],
        "router_hi": wr_hi,
        "router_lo": (wr - wr_hi.astype(F32)).astype(BF16),
        "router_b": jnp.pad(b_router[0], (0, LANES - N_EXPERTS)).reshape(1, LANES),
        "wg": wg,
        "wl": wl,
        "bg": b_gate_up[0][:, 0::2].reshape(N_EXPERTS, 1, D_FF),
        "bl": b_gate_up[0][:, 1::2].reshape(N_EXPERTS, 1, D_FF),
        "wd": w_down[0].astype(BF16),
        "bd": b_down[0].reshape(N_EXPERTS, 1, D_MODEL),
        "norm_final": norm_final_w,
    }
    mix = (hy_pos_w1[0], hy_pos_b1[0], hy_pos_w2[0], hy_pos_b2[0], hy_pos_w3[0], hy_sin_freq[0],
           hy_bias[0], dn_norm_w[0])
    return (_trunk(x_prompt, w, *mix), _trunk(x_sample, w, *mix))
```

```python
import functools
import math

import jax
import jax.numpy as jnp
import numpy as np
from jax import lax
from jax.experimental import pallas as pl
from jax.experimental.pallas import tpu as pltpu
from jax.experimental.pallas import tpu_sc as plsc

D_MODEL = 1024
HY_CH = 512
DN_HEADS = 4
DN_DK = 128
DN_DV = 128
DN_QK = DN_HEADS * DN_DK
DN_W = DN_HEADS * DN_DV
HY_IN = 3 * HY_CH
DN_CONV = 2 * DN_QK + DN_W
N_GATE = 4 * DN_HEADS
SHORT_CONV = 3
CONV_COLS = HY_IN + DN_CONV
REST_COLS = DN_W + 128
HY_EMB = 33
HY_BANDS = (HY_EMB - 1) // 2
HY_DECAY_TARGET = 1e-2
HY_FAST_PCT = 0.3
HY_SLOW_PCT = 1.5
CHUNK = 64
N_EXPERTS = 32
TOP_K = 4
D_FF = D_MODEL
SWIGLU_ALPHA = 1.702
SWIGLU_LIMIT = 7.0
MOE_BLOCK = 512
EPS = 1e-6

LANES = 128
SUB = 8
ROW_TILE = 512
FFT_CB = 256
DN_STEP_CHUNKS = 4
DEINT_COLS = 512
COMBINE_TILE = 512
SC_WINDOW = 32
VMEM_LIMIT = 56 * 1024 * 1024

F32 = jnp.float32
BF16 = jnp.bfloat16

_NT = (((1,), (1,)), ((), ()))


def _rms(x, g):
    return x * lax.rsqrt(jnp.mean(x * x, axis=-1, keepdims=True) + EPS) * g


def _silu(x):
    return x * jax.nn.sigmoid(x)


def _head_l2norm(x):
    parts = []
    for hd in range(DN_HEADS):
        xh = x[:, hd * DN_DK:(hd + 1) * DN_DK]
        parts.append(xh * lax.rsqrt(jnp.sum(xh * xh, axis=-1, keepdims=True) + EPS))
    return jnp.concatenate(parts, axis=1)


def _in_proj_kernel(xp_ref, x_ref, xn_ref, g_ref, wc_ref, wr_ref, cw_ref, cb_ref, gt_ref,
                    x0_ref, s_ref, q_ref, k_ref, v_ref, z_ref, gate_ref, p_scr, *, tiles_per_seq):
    i = pl.program_id(0)
    first = (i % tiles_per_seq) == 0
    last = (i % tiles_per_seq) == tiles_per_seq - 1
    g = g_ref[...]
    hp = jnp.where(first, 0.0, _rms(xp_ref[...], g))
    hn = jnp.where(last, 0.0, _rms(xn_ref[...], g))
    h_all = jnp.concatenate([hp, _rms(x_ref[...], g), hn], axis=0).astype(BF16)
    h = h_all[SUB:SUB + ROW_TILE]

    def project(c0):
        cols = slice(c0, c0 + HY_CH)
        p_scr[:, cols] = jnp.dot(h_all, wc_ref[:, cols], preferred_element_type=F32)

    def conv(c0):
        cols = slice(c0, c0 + HY_CH)
        return (p_scr[pl.ds(SUB - 1, ROW_TILE), cols] * cw_ref[0:1, cols]
                + p_scr[pl.ds(SUB, ROW_TILE), cols] * cw_ref[1:2, cols]
                + p_scr[pl.ds(SUB + 1, ROW_TILE), cols] * cw_ref[2:3, cols])

    project(0)
    project(HY_CH)
    x0_ref[...] = conv(0) + cb_ref[:, 0:HY_CH]
    project(2 * HY_CH)
    project(HY_IN)
    s_ref[...] = (conv(HY_CH) + cb_ref[:, HY_CH:2 * HY_CH]) * (conv(2 * HY_CH) + cb_ref[:, 2 * HY_CH:])
    project(HY_IN + DN_QK)
    q_ref[...] = _head_l2norm(_silu(conv(HY_IN))) * (DN_DK ** -0.5)
    project(HY_IN + 2 * DN_QK)
    k_ref[...] = _head_l2norm(_silu(conv(HY_IN + DN_QK)))
    rest = jnp.dot(h, wr_ref[...], preferred_element_type=F32)
    v_ref[...] = _silu(conv(HY_IN + 2 * DN_QK))
    z_ref[...] = rest[:, :DN_W]
    a = rest[:, DN_W:] + gt_ref[1:2]
    softplus = jnp.maximum(a, 0.0) + jnp.log(1.0 + jnp.exp(-jnp.abs(a)))
    gate_ref[...] = jnp.where(gt_ref[2:3] > 0.5, -gt_ref[0:1] * softplus, jax.nn.sigmoid(rest[:, DN_W:]))


def _in_proj(x2d, seq_len, g, w_conv, w_rest, conv_w, conv_b, gate_tab):
    T = x2d.shape[0]
    per = ROW_TILE // SUB
    last_sub = T // SUB - 1
    const = lambda i: (0, 0)
    row = lambda i: (i, 0)
    o512 = pl.BlockSpec((ROW_TILE, HY_CH), row)
    s512 = jax.ShapeDtypeStruct((T, HY_CH), F32)
    return pl.pallas_call(
        functools.partial(_in_proj_kernel, tiles_per_seq=seq_len // ROW_TILE),
        grid=(T // ROW_TILE,),
        in_specs=[pl.BlockSpec((SUB, D_MODEL), lambda i: (jnp.maximum(i * per - 1, 0), 0)),
                  pl.BlockSpec((ROW_TILE, D_MODEL), row),
                  pl.BlockSpec((SUB, D_MODEL), lambda i: (jnp.minimum((i + 1) * per, last_sub), 0)),
                  pl.BlockSpec((1, D_MODEL), const),
                  pl.BlockSpec((D_MODEL, CONV_COLS), const),
                  pl.BlockSpec((D_MODEL, REST_COLS), const),
                  pl.BlockSpec((SHORT_CONV, CONV_COLS), const),
                  pl.BlockSpec((1, HY_IN), const),
                  pl.BlockSpec((3, LANES), const)],
        out_specs=[o512, o512, o512, o512, o512, o512, pl.BlockSpec((ROW_TILE, LANES), row)],
        out_shape=[s512, s512, s512, s512, s512, s512, jax.ShapeDtypeStruct((T, LANES), F32)],
        scratch_shapes=[pltpu.VMEM((ROW_TILE + 2 * SUB, CONV_COLS), F32)],
        compiler_params=pltpu.CompilerParams(dimension_semantics=("parallel",),
                                             vmem_limit_bytes=VMEM_LIMIT),
        name="in_proj",
    )(x2d, x2d, x2d, g.reshape(1, D_MODEL), w_conv, w_rest, conv_w, conv_b, gate_tab)


def _in_proj_params(w_in, hy_conv_w, hy_conv_b, dn_conv_w, dn_a_log, dn_dt_bias):
    H = DN_HEADS
    w16 = w_in.astype(BF16)
    gc = w16[:, CONV_COLS + DN_W:]
    gc = jnp.concatenate([gc[:, 0:H], gc[:, 2 * H:3 * H], gc[:, H:2 * H], gc[:, 3 * H:]], axis=1)
    w_rest = jnp.concatenate([w16[:, CONV_COLS:CONV_COLS + DN_W],
                              jnp.pad(gc, ((0, 0), (0, LANES - N_GATE)))], axis=1)
    zero, one, pad = jnp.zeros((H,), F32), jnp.ones((H,), F32), jnp.zeros((LANES - N_GATE,), F32)
    gate_tab = jnp.stack([jnp.concatenate([jnp.exp(dn_a_log[0]), zero, jnp.exp(dn_a_log[1]), zero, pad]),
                          jnp.concatenate([dn_dt_bias[0], zero, dn_dt_bias[1], zero, pad]),
                          jnp.concatenate([one, zero, one, zero, pad])])
    return (w16[:, :CONV_COLS], w_rest, jnp.concatenate([hy_conv_w, dn_conv_w], axis=1),
            hy_conv_b.reshape(1, HY_IN), gate_tab)


def _out_proj_kernel(x_ref, yh_ref, of_ref, ob_ref, z_ref, nw_ref, wh_ref, wd_ref, g_ref, wrh_ref,
                     wrl_ref, br_ref, x1_ref, h_ref, lg_ref):
    o = of_ref[...] + ob_ref[...]
    heads = []
    for hd in range(DN_HEADS):
        oh = o[:, hd * DN_DV:(hd + 1) * DN_DV]
        heads.append(oh * lax.rsqrt(jnp.mean(oh * oh, axis=-1, keepdims=True) + EPS))
    z = z_ref[...]
    y_dn = jnp.concatenate(heads, axis=1) * nw_ref[...] * (z * jax.nn.sigmoid(z))
    x1 = (x_ref[...]
          + jnp.dot(yh_ref[...].astype(BF16), wh_ref[...], preferred_element_type=F32)
          + jnp.dot(y_dn.astype(BF16), wd_ref[...], preferred_element_type=F32))
    x1_ref[...] = x1
    h = _rms(x1, g_ref[...])
    h_hi = h.astype(BF16)
    h_lo = (h - h_hi.astype(F32)).astype(BF16)
    h_ref[...] = h
    lg_ref[...] = (jnp.dot(h_hi, wrh_ref[...], preferred_element_type=F32)
                   + jnp.dot(h_lo, wrh_ref[...], preferred_element_type=F32)
                   + jnp.dot(h_hi, wrl_ref[...], preferred_element_type=F32)
                   + br_ref[...])


def _out_proj(x2d, y_hy, o_f, o_b, z, dn_norm_w, w_oh, w_od, g, wr_hi, wr_lo, br):
    T = x2d.shape[0]
    const = lambda i: (0, 0)
    row = lambda i: (i, 0)
    return pl.pallas_call(
        _out_proj_kernel,
        grid=(T // ROW_TILE,),
        in_specs=[pl.BlockSpec((ROW_TILE, D_MODEL), row),
                  pl.BlockSpec((ROW_TILE, HY_CH), row),
                  pl.BlockSpec((ROW_TILE, DN_W), row),
                  pl.BlockSpec((ROW_TILE, DN_W), row),
                  pl.BlockSpec((ROW_TILE, DN_W), row),
                  pl.BlockSpec((1, DN_W), const),
                  pl.BlockSpec((HY_CH, D_MODEL), const),
                  pl.BlockSpec((DN_W, D_MODEL), const),
                  pl.BlockSpec((1, D_MODEL), const),
                  pl.BlockSpec((D_MODEL, LANES), const),
                  pl.BlockSpec((D_MODEL, LANES), const),
                  pl.BlockSpec((1, LANES), const)],
        out_specs=[pl.BlockSpec((ROW_TILE, D_MODEL), row),
                   pl.BlockSpec((ROW_TILE, D_MODEL), row),
                   pl.BlockSpec((ROW_TILE, LANES), row)],
        out_shape=[jax.ShapeDtypeStruct((T, D_MODEL), F32),
                   jax.ShapeDtypeStruct((T, D_MODEL), F32),
                   jax.ShapeDtypeStruct((T, LANES), F32)],
        compiler_params=pltpu.CompilerParams(dimension_semantics=("parallel",),
                                             vmem_limit_bytes=VMEM_LIMIT),
        name="out_proj_router",
    )(x2d, y_hy, o_f, o_b, z, jnp.tile(dn_norm_w, DN_HEADS).reshape(1, DN_W), w_oh, w_od,
      g.reshape(1, D_MODEL), wr_hi, wr_lo, br)


def _deint_kernel(w_ref, p_ref, og_ref, ol_ref):
    half = DEINT_COLS // 2
    sel = jnp.dot(w_ref[0].astype(BF16), p_ref[...], preferred_element_type=F32)
    og_ref[0] = sel[:, :half].astype(BF16)
    ol_ref[0] = sel[:, half:].astype(BF16)


def _deinterleave(w_gate_up):
    half = DEINT_COLS // 2
    r = np.arange(DEINT_COLS)[:, None]
    c = np.arange(DEINT_COLS)[None, :]
    perm = jnp.asarray(np.where(c < half, r == 2 * c, r == 2 * (c - half) + 1), BF16)
    out = jax.ShapeDtypeStruct((N_EXPERTS, D_MODEL, D_FF), BF16)
    return pl.pallas_call(
        _deint_kernel,
        grid=(N_EXPERTS, 2 * D_FF // DEINT_COLS),
        in_specs=[pl.BlockSpec((1, D_MODEL, DEINT_COLS), lambda e, j: (e, 0, j)),
                  pl.BlockSpec((DEINT_COLS, DEINT_COLS), lambda e, j: (0, 0))],
        out_specs=[pl.BlockSpec((1, D_MODEL, half), lambda e, j: (e, 0, j)),
                   pl.BlockSpec((1, D_MODEL, half), lambda e, j: (e, 0, j))],
        out_shape=[out, out],
        compiler_params=pltpu.CompilerParams(dimension_semantics=("parallel", "parallel")),
        name="deinterleave_gate_up",
    )(w_gate_up, perm)


def _sc_scatter(x, indices, n_out):
    n = indices.shape[0]
    rows, width = x.shape
    nsrc = rows // SC_WINDOW
    mesh = plsc.VectorSubcoreMesh(core_axis_name="core", subcore_axis_name="subcore")

    @functools.partial(pl.kernel, out_type=jax.ShapeDtypeStruct((n_out, width), x.dtype), mesh=mesh)
    def scatter(x_hbm, i_hbm, o_hbm):
        def body(x_vmem, i_vmem):
            pltpu.sync_copy(x_vmem, o_hbm.at[i_vmem.at[0]])

        pltpu.emit_pipeline(
            body,
            grid=(n // SC_WINDOW,),
            in_specs=[pl.BlockSpec((SC_WINDOW, width), index_map=lambda i: (i % nsrc, 0)),
                      pl.BlockSpec((1, SC_WINDOW), index_map=lambda i: (i, 0))],
            out_specs=[],
            core_axis_name=("core", "subcore"),
            dimension_semantics=(pltpu.PARALLEL,),
        )(x_hbm, i_hbm)

    return scatter(x, indices.reshape(n // SC_WINDOW, SC_WINDOW))


def _expert_kernel(be_ref, nb_ref, nv_ref, xb_ref, wg_ref, wl_ref, bg_ref, bl_ref, wd_ref, bd_ref, y_ref):
    i = pl.program_id(0)

    @pl.when(i < nb_ref[0])
    def _():
        row_id = lax.broadcasted_iota(jnp.int32, (MOE_BLOCK, 1), 0)
        xb = jnp.where(row_id < nv_ref[i], xb_ref[...], 0.0).astype(BF16)
        hg = jnp.dot(xb, wg_ref[0], preferred_element_type=F32) + bg_ref[0]
        hl = jnp.dot(xb, wl_ref[0], preferred_element_type=F32) + bl_ref[0]
        x_glu = jnp.minimum(hg, SWIGLU_LIMIT)
        x_lin = jnp.clip(hl, -SWIGLU_LIMIT, SWIGLU_LIMIT)
        act = x_glu * jax.nn.sigmoid(SWIGLU_ALPHA * x_glu) * (x_lin + 1.0)
        y = jnp.dot(act.astype(BF16), wd_ref[0], preferred_element_type=F32) + bd_ref[0]
        y_ref[...] = y

    @pl.when(i >= nb_ref[0])
    def _():
        y_ref[...] = jnp.zeros_like(y_ref)


def _expert_mlp(xb, block_e, n_used, n_valid, wg, wl, bg, bl, wd, bd):
    n_rows = xb.shape[0]
    n_blocks = n_rows // MOE_BLOCK
    rowm = lambda i, be, nb, nv: (i, 0)
    exp3 = lambda i, be, nb, nv: (be[i], 0, 0)
    grid_spec = pltpu.PrefetchScalarGridSpec(
        num_scalar_prefetch=3,
        grid=(n_blocks,),
        in_specs=[pl.BlockSpec((MOE_BLOCK, D_MODEL), rowm),
                  pl.BlockSpec((1, D_MODEL, D_FF), exp3),
                  pl.BlockSpec((1, D_MODEL, D_FF), exp3),
                  pl.BlockSpec((1, 1, D_FF), exp3),
                  pl.BlockSpec((1, 1, D_FF), exp3),
                  pl.BlockSpec((1, D_FF, D_MODEL), exp3),
                  pl.BlockSpec((1, 1, D_MODEL), exp3)],
        out_specs=pl.BlockSpec((MOE_BLOCK, D_MODEL), rowm),
    )
    return pl.pallas_call(
        _expert_kernel,
        grid_spec=grid_spec,
        out_shape=jax.ShapeDtypeStruct((n_rows, D_MODEL), F32),
        compiler_params=pltpu.CompilerParams(dimension_semantics=("arbitrary",),
                                             vmem_limit_bytes=VMEM_LIMIT),
        name="expert_mlp",
    )(block_e, n_used, n_valid, xb, wg, wl, bg, bl, wd, bd)


def _moe(h, logits, wg, wl, bg, bl, wd, bd):
    T = logits.shape[0]
    TK = T * TOP_K
    top_vals, top_idx = lax.top_k(logits, TOP_K)
    gates = jax.nn.softmax(top_vals, axis=-1)
    onehot = jax.nn.one_hot(top_idx, N_EXPERTS, dtype=jnp.int32)
    sel = jnp.sum(onehot, axis=1)
    before = jnp.cumsum(sel, axis=0) - sel
    counts = jnp.sum(sel, axis=0)
    padded = (counts + MOE_BLOCK - 1) // MOE_BLOCK * MOE_BLOCK
    pad_end = jnp.cumsum(padded)
    pad_start = pad_end - padded
    dest = jnp.sum(onehot * (before + pad_start[None, :])[:, None, :], axis=-1).astype(jnp.int32)
    n_blocks = (TK + MOE_BLOCK - 1) // MOE_BLOCK + N_EXPERTS
    n_rows = n_blocks * MOE_BLOCK
    block_start = jnp.arange(n_blocks, dtype=jnp.int32) * MOE_BLOCK
    block_e = jnp.minimum(jnp.sum((block_start[:, None] >= pad_end[None, :]).astype(jnp.int32), axis=1),
                          N_EXPERTS - 1)
    n_used = (pad_end[-1] // MOE_BLOCK).astype(jnp.int32).reshape(1)
    be_onehot = jax.nn.one_hot(block_e, N_EXPERTS, dtype=jnp.int32)
    n_valid = jnp.clip(jnp.sum(be_onehot * (counts + pad_start)[None, :], axis=1) - block_start,
                       0, MOE_BLOCK).astype(jnp.int32)
    xb = _sc_scatter(h, dest.T.reshape(-1), n_rows)
    yb = _expert_mlp(xb, block_e, n_used, n_valid, wg, wl, bg, bl, wd, bd)
    return yb, dest, gates


def _sc_gather(x, indices):
    n = indices.shape[0]
    width = x.shape[1]
    mesh = plsc.VectorSubcoreMesh(core_axis_name="core", subcore_axis_name="subcore")

    @functools.partial(pl.kernel, out_type=jax.ShapeDtypeStruct((n, width), x.dtype), mesh=mesh)
    def gather(x_hbm, i_hbm, o_hbm):
        def body(i_vmem, o_vmem):
            pltpu.sync_copy(x_hbm.at[i_vmem.at[0]], o_vmem)

        pltpu.emit_pipeline(
            body,
            grid=(n // SC_WINDOW,),
            in_specs=[pl.BlockSpec((1, SC_WINDOW), index_map=lambda i: (i, 0))],
            out_specs=[pl.BlockSpec((SC_WINDOW, width), index_map=lambda i: (i, 0))],
            core_axis_name=("core", "subcore"),
            dimension_semantics=(pltpu.PARALLEL,),
        )(i_hbm, o_hbm)

    return gather(x, indices.reshape(n // SC_WINDOW, SC_WINDOW))


def _combine_kernel(gate_ref, x1_ref, g_ref, y4_ref, o_ref):
    tm = x1_ref.shape[0]
    x = x1_ref[...]
    for k in range(TOP_K):
        x = x + gate_ref[:, k:k + 1] * y4_ref[pl.ds(k * tm, tm), :]
    o_ref[...] = _rms(x, g_ref[...])


def _combine_final(yb, dest, gates, x1, g):
    T = x1.shape[0]
    tm = COMBINE_TILE
    idx = dest.reshape(T // tm, tm, TOP_K).transpose(0, 2, 1).reshape(-1)
    y4 = _sc_gather(yb, idx)
    row = lambda i: (i, 0)
    return pl.pallas_call(
        _combine_kernel,
        grid=(T // tm,),
        in_specs=[pl.BlockSpec((tm, TOP_K), row),
                  pl.BlockSpec((tm, D_MODEL), row),
                  pl.BlockSpec((1, D_MODEL), lambda i: (0, 0)),
                  pl.BlockSpec((tm * TOP_K, D_MODEL), row)],
        out_specs=pl.BlockSpec((tm, D_MODEL), row),
        out_shape=jax.ShapeDtypeStruct((T, D_MODEL), F32),
        compiler_params=pltpu.CompilerParams(dimension_semantics=("parallel",),
                                             vmem_limit_bytes=VMEM_LIMIT),
        name="moe_combine_final",
    )(gates, x1, g.reshape(1, D_MODEL), y4)


def _fft_tables(nc, nf):
    n = nc * nf
    kc = np.arange(nc, dtype=np.float64)
    a1 = 2.0 * np.pi * np.outer(kc, np.arange(nc // 2)) / nc
    c1, s1 = np.cos(a1), np.sin(a1)
    l1 = np.block([[c1, s1], [-s1, c1]])
    a2 = 2.0 * np.pi * np.outer(np.arange(nf), np.arange(nf)) / nf
    c2, s2 = np.cos(a2), np.sin(a2)
    m2 = np.block([[c2, s2], [-s2, c2]])
    m2i = np.block([[c2, -s2], [s2, c2]])
    a3 = 2.0 * np.pi * np.outer(np.arange(nc // 2), kc) / nc
    c3, s3 = np.cos(a3), np.sin(a3)
    l3 = np.block([[c3, -s3], [s3, c3]]) / n
    return tuple(jnp.asarray(m, F32).astype(BF16) for m in (l1, m2, m2i, l3))


def _twiddle(nc, nf):
    n = nc * nf
    ph = (jnp.arange(nf, dtype=jnp.int32)[:, None] * jnp.arange(nc, dtype=jnp.int32)[None, :]) % n
    ang = ph.astype(F32) * (2.0 * math.pi / n)
    tw = jnp.stack([jnp.cos(ang), jnp.sin(ang)], axis=1)
    return jnp.broadcast_to(tw[..., None], (nf, 2, nc, LANES))


def _lane_tile(t, width):
    return t if width == LANES else jnp.concatenate([t] * (width // LANES), axis=1)


def _fft1_kernel(z_ref, l1_ref, tw_ref, o_ref, *, nc):
    cb = o_ref.shape[-1]
    for j in range(SUB):
        rhs = jnp.concatenate([z_ref[0, :, j, :], z_ref[1, :, j, :]], axis=0).astype(BF16)
        a = jnp.dot(l1_ref[...], rhs, preferred_element_type=F32)
        ar, ai = a[:nc], a[nc:]
        twr = _lane_tile(tw_ref[j, 0], cb)
        twi = _lane_tile(tw_ref[j, 1], cb)
        o_ref[j, 0] = ar * twr + ai * twi
        o_ref[j, 1] = ai * twr - ar * twi


def _fft1(z, l1, tw, nc, nf):
    P, C = z.shape[0], z.shape[-1]
    cb = FFT_CB
    z_spec = pl.BlockSpec((None, 2, nc // 2, SUB, cb), lambda p, j, c: (p, 0, 0, j, c))
    return pl.pallas_call(
        functools.partial(_fft1_kernel, nc=nc),
        grid=(P, nf // SUB, C // cb),
        in_specs=[z_spec,
                  pl.BlockSpec(l1.shape, lambda p, j, c: (0, 0)),
                  pl.BlockSpec((SUB, 2, nc, LANES), lambda p, j, c: (j, 0, 0, 0))],
        out_specs=pl.BlockSpec((None, SUB, 2, nc, cb), lambda p, j, c: (p, j, 0, 0, c)),
        out_shape=jax.ShapeDtypeStruct((P, nf, 2, nc, C), F32),
        compiler_params=pltpu.CompilerParams(dimension_semantics=("parallel", "parallel", "parallel"),
                                             vmem_limit_bytes=VMEM_LIMIT),
        name="hyena_fft_stage1",
    )(z, l1, tw)


def _fft2_filter_kernel(a_ref, m2_ref, k_ref, *, nf):
    for j in range(SUB):
        rhs = jnp.concatenate([a_ref[:, 0, j, :], a_ref[:, 1, j, :]], axis=0).astype(BF16)
        x = jnp.dot(m2_ref[...], rhs, preferred_element_type=F32)
        k_ref[j, 0] = x[:nf]
        k_ref[j, 1] = x[nf:]


def _fft2_filter(a, m2, nc, nf):
    C = a.shape[-1]
    cb = FFT_CB
    return pl.pallas_call(
        functools.partial(_fft2_filter_kernel, nf=nf),
        grid=(nc // SUB, C // cb),
        in_specs=[pl.BlockSpec((None, nf, 2, SUB, cb), lambda k, c: (0, 0, 0, k, c)),
                  pl.BlockSpec(m2.shape, lambda k, c: (0, 0))],
        out_specs=pl.BlockSpec((SUB, 2, nf, cb), lambda k, c: (k, 0, 0, c)),
        out_shape=jax.ShapeDtypeStruct((nc, 2, nf, C), F32),
        compiler_params=pltpu.CompilerParams(dimension_semantics=("parallel", "parallel"),
                                             vmem_limit_bytes=VMEM_LIMIT),
        name="hyena_filter_spectrum",
    )(a, m2)


def _fft2_kernel(a_ref, k_ref, m2_ref, m2i_ref, o_ref, *, nf):
    for j in range(SUB):
        rhs = jnp.concatenate([a_ref[:, 0, j, :], a_ref[:, 1, j, :]], axis=0).astype(BF16)
        x = jnp.dot(m2_ref[...], rhs, preferred_element_type=F32)
        xr, xi = x[:nf], x[nf:]
        kr, ki = k_ref[j, 0], k_ref[j, 1]
        y = jnp.concatenate([xr * kr - xi * ki, xr * ki + xi * kr], axis=0).astype(BF16)
        b = jnp.dot(m2i_ref[...], y, preferred_element_type=F32)
        o_ref[:, 0, j, :] = b[:nf]
        o_ref[:, 1, j, :] = b[nf:]


def _fft2(a, kspec, m2, m2i, nc, nf):
    P, C = a.shape[0], a.shape[-1]
    cb = FFT_CB
    blk = pl.BlockSpec((None, nf, 2, SUB, cb), lambda p, k, c: (p, 0, 0, k, c))
    return pl.pallas_call(
        functools.partial(_fft2_kernel, nf=nf),
        grid=(P, nc // SUB, C // cb),
        in_specs=[blk,
                  pl.BlockSpec((SUB, 2, nf, cb), lambda p, k, c: (k, 0, 0, c)),
                  pl.BlockSpec(m2.shape, lambda p, k, c: (0, 0)),
                  pl.BlockSpec(m2i.shape, lambda p, k, c: (0, 0))],
        out_specs=blk,
        out_shape=jax.ShapeDtypeStruct(a.shape, F32),
        compiler_params=pltpu.CompilerParams(dimension_semantics=("parallel", "parallel", "parallel"),
                                             vmem_limit_bytes=VMEM_LIMIT),
        name="hyena_fft_stage2",
    )(a, kspec, m2, m2i)


def _fft3_kernel(b_ref, tw_ref, l3_ref, s_ref, x0_ref, bias_ref, o_ref, *, nc):
    cb = o_ref.shape[-1]
    half = nc // 2
    for j in range(SUB):
        br, bi = b_ref[j, 0], b_ref[j, 1]
        twr = _lane_tile(tw_ref[j, 0], cb)
        twi = _lane_tile(tw_ref[j, 1], cb)
        rhs = jnp.concatenate([br * twr - bi * twi, br * twi + bi * twr], axis=0).astype(BF16)
        y = jnp.dot(l3_ref[...], rhs, preferred_element_type=F32)
        for q in range(2):
            s = s_ref[q, :, j, :]
            o_ref[q, :, j, :] = x0_ref[q, :, j, :] * (y[q * half:(q + 1) * half] + s * bias_ref[...])


def _fft3(b, tw, l3, s5, x05, bias, nc, nf):
    P, C = b.shape[0], b.shape[-1]
    cb = FFT_CB
    seq = pl.BlockSpec((None, 2, nc // 2, SUB, cb), lambda p, j, c: (p, 0, 0, j, c))
    return pl.pallas_call(
        functools.partial(_fft3_kernel, nc=nc),
        grid=(P, nf // SUB, C // cb),
        in_specs=[pl.BlockSpec((None, SUB, 2, nc, cb), lambda p, j, c: (p, j, 0, 0, c)),
                  pl.BlockSpec((SUB, 2, nc, LANES), lambda p, j, c: (j, 0, 0, 0)),
                  pl.BlockSpec(l3.shape, lambda p, j, c: (0, 0)),
                  seq, seq,
                  pl.BlockSpec((1, cb), lambda p, j, c: (0, c))],
        out_specs=seq,
        out_shape=jax.ShapeDtypeStruct(s5.shape, F32),
        compiler_params=pltpu.CompilerParams(dimension_semantics=("parallel", "parallel", "parallel"),
                                             vmem_limit_bytes=VMEM_LIMIT),
        name="hyena_fft_stage3",
    )(b, tw, l3, s5, x05, bias)


def _fft_split(n):
    nf = 1 << (int(math.log2(n)) // 2)
    return n // nf, nf


def _filter_tables(nc):
    half = nc // 2
    kc = np.arange(nc, dtype=np.float64)[:, None]
    r = np.arange(half, dtype=np.float64)
    mats = []
    for rows_b, drop0 in ((half + (half - 1 - r), False), (nc - r, True)):
        a = 2.0 * np.pi * kc * np.concatenate([r, rows_b])[None, :] / nc
        m = np.concatenate([np.cos(a), -np.sin(a)], axis=0)
        if drop0:
            m[:, half] = 0.0
        mats.append(m)
    return tuple(jnp.asarray(m, F32).astype(BF16) for m in mats)


def _fft1_filter_kernel(hf_ref, hba_ref, hbb_ref, mb_ref, mb0_ref, tw_ref, o_ref, *, nc):
    cb = o_ref.shape[-1]
    m_first = jnp.where(pl.program_id(0) == 0, mb0_ref[...], mb_ref[...])
    for j in range(SUB):
        src = hbb_ref[:, 0, :] if j == 0 else hba_ref[:, SUB - j, :]
        rhs = jnp.concatenate([hf_ref[:, j, :], src], axis=0).astype(BF16)
        a = jnp.dot(m_first if j == 0 else mb_ref[...], rhs, preferred_element_type=F32)
        ar, ai = a[:nc], a[nc:]
        twr = _lane_tile(tw_ref[j, 0], cb)
        twi = _lane_tile(tw_ref[j, 1], cb)
        o_ref[j, 0] = ar * twr + ai * twi
        o_ref[j, 1] = ai * twr - ar * twi


def _fft1_filter(h_f, h_b, mb, mb0, tw, nc, nf):
    C = h_f.shape[-1]
    cb = FFT_CB
    nblk = nf // SUB
    hf3 = h_f.reshape(nc // 2, nf, C)
    hb3 = h_b.reshape(nc // 2, nf, C)
    blk = lambda f: pl.BlockSpec((nc // 2, SUB, cb), f)
    return pl.pallas_call(
        functools.partial(_fft1_filter_kernel, nc=nc),
        grid=(nblk, C // cb),
        in_specs=[blk(lambda j, c: (0, j, c)),
                  blk(lambda j, c: (0, nblk - 1 - j, c)),
                  blk(lambda j, c: (0, (nblk - j) % nblk, c)),
                  pl.BlockSpec(mb.shape, lambda j, c: (0, 0)),
                  pl.BlockSpec(mb0.shape, lambda j, c: (0, 0)),
                  pl.BlockSpec((SUB, 2, nc, LANES), lambda j, c: (j, 0, 0, 0))],
        out_specs=pl.BlockSpec((None, SUB, 2, nc, cb), lambda j, c: (0, j, 0, 0, c)),
        out_shape=jax.ShapeDtypeStruct((1, nf, 2, nc, C), F32),
        compiler_params=pltpu.CompilerParams(dimension_semantics=("parallel", "parallel"),
                                             vmem_limit_bytes=VMEM_LIMIT),
        name="hyena_filter_stage1",
    )(hf3, hb3, hb3, mb, mb0, tw)


def _hyena_conv(s, x0, h_f, h_b, bias):
    B, L, C = s.shape
    nc, nf = _fft_split(2 * L)
    tw = _twiddle(nc, nf)
    l1, m2, m2i, l3 = _fft_tables(nc, nf)
    mb, mb0 = _filter_tables(nc)
    kspec = _fft2_filter(_fft1_filter(h_f, h_b, mb, mb0, tw, nc, nf), m2, nc, nf)
    s5 = s.reshape(B // 2, 2, nc // 2, nf, C)
    x05 = x0.reshape(B // 2, 2, nc // 2, nf, C)
    a = _fft1(s5, l1, tw, nc, nf)
    b = _fft2(a, kspec, m2, m2i, nc, nf)
    return _fft3(b, tw, l3, s5, x05, bias.reshape(1, C), nc, nf).reshape(B, L, C)


def _hyena_filter(L, w1, b1, w2, b2, w3, freq):
    pos = jnp.arange(L, dtype=F32)
    t = jnp.linspace(0.0, 1.0, L, dtype=F32)[:, None]
    bands = jnp.linspace(1e-4, HY_BANDS - 1, HY_BANDS, dtype=F32)
    ang = (2.0 * math.pi / L) * pos[:, None] * bands[None, :]
    z = jnp.concatenate([t, jnp.cos(ang), -jnp.sin(ang)], axis=-1)
    h = jnp.sin(freq * (z @ w1 + b1))
    h = jnp.sin(freq * (h @ w2 + b2))
    h = h @ w3
    deltas = jnp.abs(jnp.linspace(math.log(HY_DECAY_TARGET) / HY_SLOW_PCT,
                                  math.log(HY_DECAY_TARGET) / HY_FAST_PCT, HY_CH, dtype=F32))
    window = jnp.exp(-t * deltas[None, :])
    h_f = h[:, :HY_CH] * window
    h_b = h[:, HY_CH:] * window
    l1 = jnp.sum(jnp.abs(h_f), axis=0) + jnp.sum(jnp.abs(h_b[1:]), axis=0)
    return h_f / l1, h_b / l1


def _split3(x):
    x1 = x.astype(BF16)
    r = x - x1.astype(F32)
    x2 = r.astype(BF16)
    x3 = (r - x2.astype(F32)).astype(BF16)
    return x1, x2, x3


def _mm(a, b):
    return jnp.dot(a.astype(BF16), b.astype(BF16), preferred_element_type=F32)


def _delta_kernel(qf_ref, kf_ref, vf_ref, gf_ref, qb_ref, kb_ref, vb_ref, gb_ref, of_ref, ob_ref, s_ref,
                  *, n_sub):
    @pl.when(pl.program_id(1) == 0)
    def _():
        s_ref[...] = jnp.zeros_like(s_ref)

    row = lax.broadcasted_iota(jnp.int32, (CHUNK, CHUNK), 0)
    col = lax.broadcasted_iota(jnp.int32, (CHUNK, CHUNK), 1)
    eye_f = jnp.where(row == col, 1.0, 0.0).astype(F32)
    dirs = ((qf_ref, kf_ref, vf_ref, gf_ref, of_ref, row >= col, row > col),
            (qb_ref, kb_ref, vb_ref, gb_ref, ob_ref, row <= col, row < col))

    ch = []
    for d, (q_ref, k_ref, v_ref, g_ref, _, incl, strict) in enumerate(dirs):
        tri = jnp.where(incl, 1.0, 0.0).astype(BF16)
        for c in range(n_sub):
            rows = slice(c * CHUNK, (c + 1) * CHUNK)
            gates = g_ref[rows, :]
            gl = 2 * DN_HEADS * d
            g1, g2, g3 = _split3(gates)
            gc_all = (jnp.dot(tri, g1, preferred_element_type=F32)
                      + jnp.dot(tri, g2, preferred_element_type=F32)
                      + jnp.dot(tri, g3, preferred_element_type=F32))
            gsum_all = jnp.sum(gates, axis=0, keepdims=True)
            for h in range(DN_HEADS):
                lanes = slice(h * DN_DK, (h + 1) * DN_DK)
                kh = k_ref[rows, lanes]
                beta = gates[:, gl + DN_HEADS + h:gl + DN_HEADS + h + 1]
                gc = gc_all[:, gl + h:gl + h + 1]
                g_last = gsum_all[:, gl + h:gl + h + 1]
                gc_b = jnp.broadcast_to(gc, (CHUNK, CHUNK))
                gc_row = jnp.sum(gc_b * eye_f, axis=0, keepdims=True)
                decay = jnp.where(incl, jnp.exp(jnp.minimum(gc_b - gc_row, 0.0)), 0.0)
                e_gc = jnp.exp(gc)
                kb = kh * beta
                ch.append(dict(d=d, c=c, h=h, lanes=lanes, rows=rows, strict=strict, decay=decay,
                               kh16=kh.astype(BF16), kb16=kb.astype(BF16),
                               q16=q_ref[rows, lanes].astype(BF16),
                               rhs=jnp.concatenate([v_ref[rows, lanes] * beta, kb * e_gc],
                                                   axis=1).astype(BF16),
                               qe=q_ref[rows, lanes] * e_gc,
                               kdT=(kh * jnp.exp(g_last - gc)).T.astype(BF16),
                               e_last=jnp.exp(g_last)))
    for x in ch:
        a = lax.dot_general(x["kb16"], x["kh16"], _NT, preferred_element_type=F32) * x["decay"]
        x["a"] = jnp.where(x["strict"], a, 0.0)
        x["qk"] = (lax.dot_general(x["q16"], x["kh16"], _NT, preferred_element_type=F32)
                   * x["decay"]).astype(BF16)
    for x in ch:
        x["t"] = eye_f - x["a"]
        x["p"] = _mm(x["a"], x["a"])
    for level in range(5):
        for x in ch:
            p16 = x["p"].astype(BF16)
            x["t"] = x["t"] + jnp.dot(x["t"].astype(BF16), p16, preferred_element_type=F32)
            if level < 4:
                x["p"] = jnp.dot(p16, p16, preferred_element_type=F32)
    for x in ch:
        uw = jnp.dot(x["t"].astype(BF16), x["rhs"], preferred_element_type=F32)
        x["u"] = uw[:, :DN_DV]
        x["wq"] = jnp.concatenate([uw[:, DN_DV:], x["qe"]], axis=0).astype(BF16)

    for step in range(n_sub):
        cur = [x for x in ch if x["c"] == (step if x["d"] == 0 else n_sub - 1 - step)]
        for x in cur:
            x["s"] = s_ref[x["d"], x["h"]]
            x["ws"] = jnp.dot(x["wq"], x["s"].astype(BF16), preferred_element_type=F32)
        for x in cur:
            x["vn"] = (x["u"] - x["ws"][:CHUNK]).astype(BF16)
        for x in cur:
            o = x["ws"][CHUNK:] + jnp.dot(x["qk"], x["vn"], preferred_element_type=F32)
            dirs[x["d"]][4][x["rows"], x["lanes"]] = o
            s_ref[x["d"], x["h"]] = (x["s"] * x["e_last"]
                                     + jnp.dot(x["kdT"], x["vn"], preferred_element_type=F32))


def _delta_scan(q, k, v, gates):
    B, L, _ = q.shape
    n_sub = DN_STEP_CHUNKS
    rows = n_sub * CHUNK
    nblk = L // rows
    fwd = pl.BlockSpec((None, rows, DN_W), lambda b, j: (b, j, 0))
    bwd = pl.BlockSpec((None, rows, DN_W), lambda b, j: (b, nblk - 1 - j, 0))
    gfwd = pl.BlockSpec((None, rows, LANES), lambda b, j: (b, j, 0))
    gbwd = pl.BlockSpec((None, rows, LANES), lambda b, j: (b, nblk - 1 - j, 0))
    out = jax.ShapeDtypeStruct((B, L, DN_W), F32)
    return pl.pallas_call(
        functools.partial(_delta_kernel, n_sub=n_sub),
        grid=(B, nblk),
        in_specs=[fwd, fwd, fwd, gfwd, bwd, bwd, bwd, gbwd],
        out_specs=[fwd, bwd],
        out_shape=[out, out],
        scratch_shapes=[pltpu.VMEM((2, DN_HEADS, DN_DK, DN_DV), F32)],
        compiler_params=pltpu.CompilerParams(dimension_semantics=("parallel", "arbitrary"),
                                             vmem_limit_bytes=VMEM_LIMIT),
        name="delta_scan",
    )(q, k, v, gates, q, k, v, gates)


def _trunk(x, w, hy_pos_w1, hy_pos_b1, hy_pos_w2, hy_pos_b2, hy_pos_w3, hy_sin_freq, hy_bias, dn_norm_w):
    B, L, D = x.shape
    T = B * L
    x2d = x.reshape(T, D)
    x0, s, q, k, v, z, dn_gates = _in_proj(x2d, L, w["norm_mix"], *w["in_proj"])
    seq = lambda a: a.reshape(B, L, a.shape[-1])
    h_f, h_b = _hyena_filter(L, hy_pos_w1, hy_pos_b1, hy_pos_w2, hy_pos_b2, hy_pos_w3, hy_sin_freq)
    y_hy = _hyena_conv(seq(s), seq(x0), h_f, h_b, hy_bias)
    o_f, o_b = _delta_scan(seq(q), seq(k), seq(v), seq(dn_gates))
    x1, h_ffn, logits = _out_proj(x2d, y_hy.reshape(T, HY_CH), o_f.reshape(T, DN_W), o_b.reshape(T, DN_W),
                                 z, dn_norm_w, w["out_hy"], w["out_dn"], w["norm_ffn"],
                                 w["router_hi"], w["router_lo"], w["router_b"])
    yb, dest, gates = _moe(h_ffn, logits[:, :N_EXPERTS], w["wg"], w["wl"], w["bg"], w["bl"], w["wd"],
                           w["bd"])
    return _combine_final(yb, dest, gates, x1, w["norm_final"]).reshape(B, L, D)


def kernel(x_prompt, x_sample, norm_mix_w, w_in, hy_conv_w, hy_conv_b, hy_pos_w1, hy_pos_b1, hy_pos_w2, hy_pos_b2, hy_pos_w3, hy_sin_freq, hy_bias, dn_conv_w, dn_a_log, dn_dt_bias, dn_norm_w, w_out, norm_ffn_w, w_router, b_router, w_gate_up, b_gate_up, w_down, b_down, norm_final_w):
    w_out16 = w_out[0].astype(BF16)
    wr = jnp.pad(w_router[0], ((0, 0), (0, LANES - N_EXPERTS)))
    wr_hi = wr.astype(BF16)
    wg, wl = _deinterleave(w_gate_up[0])
    w = {
        "norm_mix": norm_mix_w[0],
        "in_proj": _in_proj_params(w_in[0], hy_conv_w[0], hy_conv_b[0], dn_conv_w[0], dn_a_log[0],
                                   dn_dt_bias[0]),
        "out_hy": w_out16[:HY_CH],
        "out_dn": w_out16[HY_CH:],
        "norm_ffn": norm_ffn_w[0],
        "router_hi": wr_hi,
        "router_lo": (wr - wr_hi.astype(F32)).astype(BF16),
        "router_b": jnp.pad(b_router[0], (0, LANES - N_EXPERTS)).reshape(1, LANES),
        "wg": wg,
        "wl": wl,
        "bg": b_gate_up[0][:, 0::2].reshape(N_EXPERTS, 1, D_FF),
        "bl": b_gate_up[0][:, 1::2].reshape(N_EXPERTS, 1, D_FF),
        "wd": w_down[0].astype(BF16),
        "bd": b_down[0].reshape(N_EXPERTS, 1, D_MODEL),
        "norm_final": norm_final_w,
    }
    mix = (hy_pos_w1[0], hy_pos_b1[0], hy_pos_w2[0], hy_pos_b2[0], hy_pos_w3[0], hy_sin_freq[0],
           hy_bias[0], dn_norm_w[0])
    return (_trunk(x_prompt, w, *mix), _trunk(x_sample, w, *mix))
```

```python
import functools
import math

import jax
import jax.numpy as jnp
import numpy as np
from jax import lax
from jax.experimental import pallas as pl
from jax.experimental.pallas import tpu as pltpu
from jax.experimental.pallas import tpu_sc as plsc

D_MODEL = 1024
HY_CH = 512
DN_HEADS = 4
DN_DK = 128
DN_DV = 128
DN_QK = DN_HEADS * DN_DK
DN_W = DN_HEADS * DN_DV
HY_IN = 3 * HY_CH
DN_CONV = 2 * DN_QK + DN_W
N_GATE = 4 * DN_HEADS
SHORT_CONV = 3
CONV_COLS = HY_IN + DN_CONV
REST_COLS = DN_W + 128
HY_EMB = 33
HY_BANDS = (HY_EMB - 1) // 2
HY_DECAY_TARGET = 1e-2
HY_FAST_PCT = 0.3
HY_SLOW_PCT = 1.5
CHUNK = 64
N_EXPERTS = 32
TOP_K = 4
D_FF = D_MODEL
SWIGLU_ALPHA = 1.702
SWIGLU_LIMIT = 7.0
MOE_BLOCK = 512
EPS = 1e-6

LANES = 128
SUB = 8
ROW_TILE = 512
FFT_CB = 256
DN_STEP_CHUNKS = 4
DEINT_COLS = 512
COMBINE_TILE = 512
SC_WINDOW = 128
SC_ROW = LANES
SC_PIECES = D_MODEL // SC_ROW
VMEM_LIMIT = 56 * 1024 * 1024

F32 = jnp.float32
BF16 = jnp.bfloat16

_NT = (((1,), (1,)), ((), ()))


def _rms(x, g):
    return x * lax.rsqrt(jnp.mean(x * x, axis=-1, keepdims=True) + EPS) * g


def _silu(x):
    return x * jax.nn.sigmoid(x)


def _head_l2norm(x):
    parts = []
    for hd in range(DN_HEADS):
        xh = x[:, hd * DN_DK:(hd + 1) * DN_DK]
        parts.append(xh * lax.rsqrt(jnp.sum(xh * xh, axis=-1, keepdims=True) + EPS))
    return jnp.concatenate(parts, axis=1)


def _in_proj_kernel(xp_ref, x_ref, xn_ref, g_ref, wc_ref, wr_ref, cw_ref, cb_ref, gt_ref,
                    x0_ref, s_ref, q_ref, k_ref, v_ref, z_ref, gate_ref, p_scr, *, tiles_per_seq):
    i = pl.program_id(0)
    first = (i % tiles_per_seq) == 0
    last = (i % tiles_per_seq) == tiles_per_seq - 1
    g = g_ref[...]
    hp = jnp.where(first, 0.0, _rms(xp_ref[...], g))
    hn = jnp.where(last, 0.0, _rms(xn_ref[...], g))
    h_all = jnp.concatenate([hp, _rms(x_ref[...], g), hn], axis=0).astype(BF16)
    h = h_all[SUB:SUB + ROW_TILE]

    def project(c0):
        cols = slice(c0, c0 + HY_CH)
        p_scr[:, cols] = jnp.dot(h_all, wc_ref[:, cols], preferred_element_type=F32)

    def conv(c0):
        cols = slice(c0, c0 + HY_CH)
        return (p_scr[pl.ds(SUB - 1, ROW_TILE), cols] * cw_ref[0:1, cols]
                + p_scr[pl.ds(SUB, ROW_TILE), cols] * cw_ref[1:2, cols]
                + p_scr[pl.ds(SUB + 1, ROW_TILE), cols] * cw_ref[2:3, cols])

    project(0)
    project(HY_CH)
    x0_ref[...] = conv(0) + cb_ref[:, 0:HY_CH]
    project(2 * HY_CH)
    project(HY_IN)
    s_ref[...] = (conv(HY_CH) + cb_ref[:, HY_CH:2 * HY_CH]) * (conv(2 * HY_CH) + cb_ref[:, 2 * HY_CH:])
    project(HY_IN + DN_QK)
    q_ref[...] = _head_l2norm(_silu(conv(HY_IN))) * (DN_DK ** -0.5)
    project(HY_IN + 2 * DN_QK)
    k_ref[...] = _head_l2norm(_silu(conv(HY_IN + DN_QK)))
    rest = jnp.dot(h, wr_ref[...], preferred_element_type=F32)
    v_ref[...] = _silu(conv(HY_IN + 2 * DN_QK))
    z_ref[...] = rest[:, :DN_W]
    a = rest[:, DN_W:] + gt_ref[1:2]
    softplus = jnp.maximum(a, 0.0) + jnp.log(1.0 + jnp.exp(-jnp.abs(a)))
    gate_ref[...] = jnp.where(gt_ref[2:3] > 0.5, -gt_ref[0:1] * softplus, jax.nn.sigmoid(rest[:, DN_W:]))


def _in_proj(x2d, seq_len, g, w_conv, w_rest, conv_w, conv_b, gate_tab):
    T = x2d.shape[0]
    per = ROW_TILE // SUB
    last_sub = T // SUB - 1
    const = lambda i: (0, 0)
    row = lambda i: (i, 0)
    o512 = pl.BlockSpec((ROW_TILE, HY_CH), row)
    s512 = jax.ShapeDtypeStruct((T, HY_CH), F32)
    return pl.pallas_call(
        functools.partial(_in_proj_kernel, tiles_per_seq=seq_len // ROW_TILE),
        grid=(T // ROW_TILE,),
        in_specs=[pl.BlockSpec((SUB, D_MODEL), lambda i: (jnp.maximum(i * per - 1, 0), 0)),
                  pl.BlockSpec((ROW_TILE, D_MODEL), row),
                  pl.BlockSpec((SUB, D_MODEL), lambda i: (jnp.minimum((i + 1) * per, last_sub), 0)),
                  pl.BlockSpec((1, D_MODEL), const),
                  pl.BlockSpec((D_MODEL, CONV_COLS), const),
                  pl.BlockSpec((D_MODEL, REST_COLS), const),
                  pl.BlockSpec((SHORT_CONV, CONV_COLS), const),
                  pl.BlockSpec((1, HY_IN), const),
                  pl.BlockSpec((3, LANES), const)],
        out_specs=[o512, o512, o512, o512, o512, o512, pl.BlockSpec((ROW_TILE, LANES), row)],
        out_shape=[s512, s512, s512, s512, s512, s512, jax.ShapeDtypeStruct((T, LANES), F32)],
        scratch_shapes=[pltpu.VMEM((ROW_TILE + 2 * SUB, CONV_COLS), F32)],
        compiler_params=pltpu.CompilerParams(dimension_semantics=("parallel",),
                                             vmem_limit_bytes=VMEM_LIMIT),
        name="in_proj",
    )(x2d, x2d, x2d, g.reshape(1, D_MODEL), w_conv, w_rest, conv_w, conv_b, gate_tab)


def _in_proj_params(w_in, hy_conv_w, hy_conv_b, dn_conv_w, dn_a_log, dn_dt_bias):
    H = DN_HEADS
    w16 = w_in.astype(BF16)
    gc = w16[:, CONV_COLS + DN_W:]
    gc = jnp.concatenate([gc[:, 0:H], gc[:, 2 * H:3 * H], gc[:, H:2 * H], gc[:, 3 * H:]], axis=1)
    w_rest = jnp.concatenate([w16[:, CONV_COLS:CONV_COLS + DN_W],
                              jnp.pad(gc, ((0, 0), (0, LANES - N_GATE)))], axis=1)
    zero, one, pad = jnp.zeros((H,), F32), jnp.ones((H,), F32), jnp.zeros((LANES - N_GATE,), F32)
    gate_tab = jnp.stack([jnp.concatenate([jnp.exp(dn_a_log[0]), zero, jnp.exp(dn_a_log[1]), zero, pad]),
                          jnp.concatenate([dn_dt_bias[0], zero, dn_dt_bias[1], zero, pad]),
                          jnp.concatenate([one, zero, one, zero, pad])])
    return (w16[:, :CONV_COLS], w_rest, jnp.concatenate([hy_conv_w, dn_conv_w], axis=1),
            hy_conv_b.reshape(1, HY_IN), gate_tab)


def _out_proj_kernel(x_ref, yh_ref, of_ref, ob_ref, z_ref, nw_ref, wh_ref, wd_ref, g_ref, wrh_ref,
                     wrl_ref, br_ref, x1_ref, h_ref, lg_ref):
    o = of_ref[...] + ob_ref[...]
    heads = []
    for hd in range(DN_HEADS):
        oh = o[:, hd * DN_DV:(hd + 1) * DN_DV]
        heads.append(oh * lax.rsqrt(jnp.mean(oh * oh, axis=-1, keepdims=True) + EPS))
    z = z_ref[...]
    y_dn = jnp.concatenate(heads, axis=1) * nw_ref[...] * (z * jax.nn.sigmoid(z))
    x1 = (x_ref[...]
          + jnp.dot(yh_ref[...].astype(BF16), wh_ref[...], preferred_element_type=F32)
          + jnp.dot(y_dn.astype(BF16), wd_ref[...], preferred_element_type=F32))
    x1_ref[...] = x1
    h = _rms(x1, g_ref[...])
    h_hi = h.astype(BF16)
    h_lo = (h - h_hi.astype(F32)).astype(BF16)
    _store_pieces(h_ref, h)
    lg_ref[...] = (jnp.dot(h_hi, wrh_ref[...], preferred_element_type=F32)
                   + jnp.dot(h_lo, wrh_ref[...], preferred_element_type=F32)
                   + jnp.dot(h_hi, wrl_ref[...], preferred_element_type=F32)
                   + br_ref[...])


def _out_proj(x2d, y_hy, o_f, o_b, z, dn_norm_w, w_oh, w_od, g, wr_hi, wr_lo, br):
    T = x2d.shape[0]
    const = lambda i: (0, 0)
    row = lambda i: (i, 0)
    return pl.pallas_call(
        _out_proj_kernel,
        grid=(T // ROW_TILE,),
        in_specs=[pl.BlockSpec((ROW_TILE, D_MODEL), row),
                  pl.BlockSpec((ROW_TILE, HY_CH), row),
                  pl.BlockSpec((ROW_TILE, DN_W), row),
                  pl.BlockSpec((ROW_TILE, DN_W), row),
                  pl.BlockSpec((ROW_TILE, DN_W), row),
                  pl.BlockSpec((1, DN_W), const),
                  pl.BlockSpec((HY_CH, D_MODEL), const),
                  pl.BlockSpec((DN_W, D_MODEL), const),
                  pl.BlockSpec((1, D_MODEL), const),
                  pl.BlockSpec((D_MODEL, LANES), const),
                  pl.BlockSpec((D_MODEL, LANES), const),
                  pl.BlockSpec((1, LANES), const)],
        out_specs=[pl.BlockSpec((ROW_TILE, D_MODEL), row),
                   pl.BlockSpec((ROW_TILE * SC_PIECES, SC_ROW), row),
                   pl.BlockSpec((ROW_TILE, LANES), row)],
        out_shape=[jax.ShapeDtypeStruct((T, D_MODEL), F32),
                   jax.ShapeDtypeStruct((T * SC_PIECES, SC_ROW), F32),
                   jax.ShapeDtypeStruct((T, LANES), F32)],
        compiler_params=pltpu.CompilerParams(dimension_semantics=("parallel",),
                                             vmem_limit_bytes=VMEM_LIMIT),
        name="out_proj_router",
    )(x2d, y_hy, o_f, o_b, z, jnp.tile(dn_norm_w, DN_HEADS).reshape(1, DN_W), w_oh, w_od,
      g.reshape(1, D_MODEL), wr_hi, wr_lo, br)


def _deint_kernel(w_ref, p_ref, og_ref, ol_ref):
    half = DEINT_COLS // 2
    sel = jnp.dot(w_ref[0].astype(BF16), p_ref[...], preferred_element_type=F32)
    og_ref[0] = sel[:, :half].astype(BF16)
    ol_ref[0] = sel[:, half:].astype(BF16)


def _deinterleave(w_gate_up):
    half = DEINT_COLS // 2
    r = np.arange(DEINT_COLS)[:, None]
    c = np.arange(DEINT_COLS)[None, :]
    perm = jnp.asarray(np.where(c < half, r == 2 * c, r == 2 * (c - half) + 1), BF16)
    out = jax.ShapeDtypeStruct((N_EXPERTS, D_MODEL, D_FF), BF16)
    return pl.pallas_call(
        _deint_kernel,
        grid=(N_EXPERTS, 2 * D_FF // DEINT_COLS),
        in_specs=[pl.BlockSpec((1, D_MODEL, DEINT_COLS), lambda e, j: (e, 0, j)),
                  pl.BlockSpec((DEINT_COLS, DEINT_COLS), lambda e, j: (0, 0))],
        out_specs=[pl.BlockSpec((1, D_MODEL, half), lambda e, j: (e, 0, j)),
                   pl.BlockSpec((1, D_MODEL, half), lambda e, j: (e, 0, j))],
        out_shape=[out, out],
        compiler_params=pltpu.CompilerParams(dimension_semantics=("parallel", "parallel")),
        name="deinterleave_gate_up",
    )(w_gate_up, perm)


def _store_pieces(ref, x):
    rows = x.shape[0]
    for c in range(SC_PIECES):
        ref[pl.ds(c, rows, stride=SC_PIECES), :] = x[:, c * SC_ROW:(c + 1) * SC_ROW]


def _load_pieces(ref, rows):
    return jnp.concatenate([ref[pl.ds(c, rows, stride=SC_PIECES), :] for c in range(SC_PIECES)], axis=1)


def _sc_scatter(x, indices, n_out):
    n = indices.shape[0]
    rows, width = x.shape
    nsrc = rows // SC_WINDOW
    mesh = plsc.VectorSubcoreMesh(core_axis_name="core", subcore_axis_name="subcore")

    @functools.partial(pl.kernel, out_type=jax.ShapeDtypeStruct((n_out, width), x.dtype), mesh=mesh)
    def scatter(x_hbm, i_hbm, o_hbm):
        def body(x_vmem, i_vmem):
            pltpu.sync_copy(x_vmem, o_hbm.at[i_vmem.at[0]])

        pltpu.emit_pipeline(
            body,
            grid=(n // SC_WINDOW,),
            in_specs=[pl.BlockSpec((SC_WINDOW, width), index_map=lambda i: (i % nsrc, 0)),
                      pl.BlockSpec((1, SC_WINDOW), index_map=lambda i: (0, i))],
            out_specs=[],
            core_axis_name=("core", "subcore"),
            dimension_semantics=(pltpu.PARALLEL,),
        )(x_hbm, i_hbm)

    return scatter(x, indices.reshape(1, n))


def _expert_kernel(be_ref, nb_ref, nv_ref, xb_ref, wg_ref, wl_ref, bg_ref, bl_ref, wd_ref, bd_ref, y_ref):
    i = pl.program_id(0)

    @pl.when(i < nb_ref[0])
    def _():
        row_id = lax.broadcasted_iota(jnp.int32, (MOE_BLOCK, 1), 0)
        xb = jnp.where(row_id < nv_ref[i], _load_pieces(xb_ref, MOE_BLOCK), 0.0).astype(BF16)
        hg = jnp.dot(xb, wg_ref[0], preferred_element_type=F32) + bg_ref[0]
        hl = jnp.dot(xb, wl_ref[0], preferred_element_type=F32) + bl_ref[0]
        x_glu = jnp.minimum(hg, SWIGLU_LIMIT)
        x_lin = jnp.clip(hl, -SWIGLU_LIMIT, SWIGLU_LIMIT)
        act = x_glu * jax.nn.sigmoid(SWIGLU_ALPHA * x_glu) * (x_lin + 1.0)
        y = jnp.dot(act.astype(BF16), wd_ref[0], preferred_element_type=F32) + bd_ref[0]
        _store_pieces(y_ref, y)

    @pl.when(i >= nb_ref[0])
    def _():
        y_ref[...] = jnp.zeros_like(y_ref)


def _expert_mlp(xb, block_e, n_used, n_valid, wg, wl, bg, bl, wd, bd):
    n_blocks = xb.shape[0] // (MOE_BLOCK * SC_PIECES)
    n_rows = n_blocks * MOE_BLOCK
    rowm = lambda i, be, nb, nv: (i, 0)
    exp3 = lambda i, be, nb, nv: (be[i], 0, 0)
    grid_spec = pltpu.PrefetchScalarGridSpec(
        num_scalar_prefetch=3,
        grid=(n_blocks,),
        in_specs=[pl.BlockSpec((MOE_BLOCK * SC_PIECES, SC_ROW), rowm),
                  pl.BlockSpec((1, D_MODEL, D_FF), exp3),
                  pl.BlockSpec((1, D_MODEL, D_FF), exp3),
                  pl.BlockSpec((1, 1, D_FF), exp3),
                  pl.BlockSpec((1, 1, D_FF), exp3),
                  pl.BlockSpec((1, D_FF, D_MODEL), exp3),
                  pl.BlockSpec((1, 1, D_MODEL), exp3)],
        out_specs=pl.BlockSpec((MOE_BLOCK * SC_PIECES, SC_ROW), rowm),
    )
    return pl.pallas_call(
        _expert_kernel,
        grid_spec=grid_spec,
        out_shape=jax.ShapeDtypeStruct((n_rows * SC_PIECES, SC_ROW), F32),
        compiler_params=pltpu.CompilerParams(dimension_semantics=("arbitrary",),
                                             vmem_limit_bytes=VMEM_LIMIT),
        name="expert_mlp",
    )(block_e, n_used, n_valid, xb, wg, wl, bg, bl, wd, bd)


def _moe(h, logits, wg, wl, bg, bl, wd, bd):
    T = logits.shape[0]
    TK = T * TOP_K
    top_vals, top_idx = lax.top_k(logits, TOP_K)
    gates = jax.nn.softmax(top_vals, axis=-1)
    onehot = jax.nn.one_hot(top_idx, N_EXPERTS, dtype=jnp.int32)
    sel = jnp.sum(onehot, axis=1)
    before = jnp.cumsum(sel, axis=0) - sel
    counts = jnp.sum(sel, axis=0)
    padded = (counts + MOE_BLOCK - 1) // MOE_BLOCK * MOE_BLOCK
    pad_end = jnp.cumsum(padded)
    pad_start = pad_end - padded
    dest = jnp.sum(onehot * (before + pad_start[None, :])[:, None, :], axis=-1).astype(jnp.int32)
    n_blocks = (TK + MOE_BLOCK - 1) // MOE_BLOCK + N_EXPERTS
    n_rows = n_blocks * MOE_BLOCK
    block_start = jnp.arange(n_blocks, dtype=jnp.int32) * MOE_BLOCK
    block_e = jnp.minimum(jnp.sum((block_start[:, None] >= pad_end[None, :]).astype(jnp.int32), axis=1),
                          N_EXPERTS - 1)
    n_used = (pad_end[-1] // MOE_BLOCK).astype(jnp.int32).reshape(1)
    be_onehot = jax.nn.one_hot(block_e, N_EXPERTS, dtype=jnp.int32)
    n_valid = jnp.clip(jnp.sum(be_onehot * (counts + pad_start)[None, :], axis=1) - block_start,
                       0, MOE_BLOCK).astype(jnp.int32)
    tok = jnp.broadcast_to(jnp.tile(jnp.arange(T, dtype=jnp.int32), TOP_K)[:, None], (TK, SC_ROW))
    row_tok = jnp.clip(_sc_scatter(tok, dest.T.reshape(-1), n_rows)[:, 0], 0, T - 1)
    idx = (row_tok[:, None] * SC_PIECES + jnp.arange(SC_PIECES, dtype=jnp.int32)[None, :]).reshape(-1)
    xb = _sc_gather(h, idx)
    yb = _expert_mlp(xb, block_e, n_used, n_valid, wg, wl, bg, bl, wd, bd)
    return yb, dest, gates


def _sc_gather(x, indices):
    n = indices.shape[0]
    width = x.shape[1]
    mesh = plsc.VectorSubcoreMesh(core_axis_name="core", subcore_axis_name="subcore")

    @functools.partial(pl.kernel, out_type=jax.ShapeDtypeStruct((n, width), x.dtype), mesh=mesh)
    def gather(x_hbm, i_hbm, o_hbm):
        def body(i_vmem, o_vmem):
            pltpu.sync_copy(x_hbm.at[i_vmem.at[0]], o_vmem)

        pltpu.emit_pipeline(
            body,
            grid=(n // SC_WINDOW,),
            in_specs=[pl.BlockSpec((1, SC_WINDOW), index_map=lambda i: (0, i))],
            out_specs=[pl.BlockSpec((SC_WINDOW, width), index_map=lambda i: (i, 0))],
            core_axis_name=("core", "subcore"),
            dimension_semantics=(pltpu.PARALLEL,),
        )(i_hbm, o_hbm)

    return gather(x, indices.reshape(1, n))


def _combine_kernel(gate_ref, x1_ref, g_ref, y4_ref, o_ref):
    tm = x1_ref.shape[0]
    pieces = SC_PIECES
    cols = []
    for c in range(pieces):
        acc = x1_ref[:, c * SC_ROW:(c + 1) * SC_ROW]
        for k in range(TOP_K):
            acc = acc + gate_ref[:, k:k + 1] * y4_ref[pl.ds((k * pieces + c) * tm, tm), :]
        cols.append(acc)
    o_ref[...] = _rms(jnp.concatenate(cols, axis=1), g_ref[...])


def _combine_final(yb, dest, gates, x1, g):
    T = x1.shape[0]
    tm = COMBINE_TILE
    pieces = SC_PIECES
    d = dest.reshape(T // tm, tm, TOP_K).transpose(0, 2, 1)
    idx = (d[:, :, None, :] * pieces + jnp.arange(pieces, dtype=jnp.int32)[None, None, :, None]).reshape(-1)
    y4 = _sc_gather(yb, idx)
    row = lambda i: (i, 0)
    return pl.pallas_call(
        _combine_kernel,
        grid=(T // tm,),
        in_specs=[pl.BlockSpec((tm, TOP_K), row),
                  pl.BlockSpec((tm, D_MODEL), row),
                  pl.BlockSpec((1, D_MODEL), lambda i: (0, 0)),
                  pl.BlockSpec((tm * TOP_K * pieces, SC_ROW), row)],
        out_specs=pl.BlockSpec((tm, D_MODEL), row),
        out_shape=jax.ShapeDtypeStruct((T, D_MODEL), F32),
        compiler_params=pltpu.CompilerParams(dimension_semantics=("parallel",),
                                             vmem_limit_bytes=VMEM_LIMIT),
        name="moe_combine_final",
    )(gates, x1, g.reshape(1, D_MODEL), y4)


def _fft_tables(nc, nf):
    n = nc * nf
    kc = np.arange(nc, dtype=np.float64)
    a1 = 2.0 * np.pi * np.outer(kc, np.arange(nc // 2)) / nc
    c1, s1 = np.cos(a1), np.sin(a1)
    l1 = np.block([[c1, s1], [-s1, c1]])
    a2 = 2.0 * np.pi * np.outer(np.arange(nf), np.arange(nf)) / nf
    c2, s2 = np.cos(a2), np.sin(a2)
    m2 = np.block([[c2, s2], [-s2, c2]])
    m2i = np.block([[c2, -s2], [s2, c2]])
    a3 = 2.0 * np.pi * np.outer(np.arange(nc // 2), kc) / nc
    c3, s3 = np.cos(a3), np.sin(a3)
    l3 = np.block([[c3, -s3], [s3, c3]]) / n
    return tuple(jnp.asarray(m, F32).astype(BF16) for m in (l1, m2, m2i, l3))


def _twiddle(nc, nf):
    n = nc * nf
    ph = (jnp.arange(nf, dtype=jnp.int32)[:, None] * jnp.arange(nc, dtype=jnp.int32)[None, :]) % n
    ang = ph.astype(F32) * (2.0 * math.pi / n)
    tw = jnp.stack([jnp.cos(ang), jnp.sin(ang)], axis=1)
    return jnp.broadcast_to(tw[..., None], (nf, 2, nc, LANES))


def _lane_tile(t, width):
    return t if width == LANES else jnp.concatenate([t] * (width // LANES), axis=1)


def _fft1_kernel(z_ref, l1_ref, tw_ref, o_ref, *, nc):
    cb = o_ref.shape[-1]
    for j in range(SUB):
        rhs = jnp.concatenate([z_ref[0, :, j, :], z_ref[1, :, j, :]], axis=0).astype(BF16)
        a = jnp.dot(l1_ref[...], rhs, preferred_element_type=F32)
        ar, ai = a[:nc], a[nc:]
        twr = _lane_tile(tw_ref[j, 0], cb)
        twi = _lane_tile(tw_ref[j, 1], cb)
        o_ref[j, 0] = ar * twr + ai * twi
        o_ref[j, 1] = ai * twr - ar * twi


def _fft1(z, l1, tw, nc, nf):
    P, C = z.shape[0], z.shape[-1]
    cb = FFT_CB
    z_spec = pl.BlockSpec((None, 2, nc // 2, SUB, cb), lambda p, j, c: (p, 0, 0, j, c))
    return pl.pallas_call(
        functools.partial(_fft1_kernel, nc=nc),
        grid=(P, nf // SUB, C // cb),
        in_specs=[z_spec,
                  pl.BlockSpec(l1.shape, lambda p, j, c: (0, 0)),
                  pl.BlockSpec((SUB, 2, nc, LANES), lambda p, j, c: (j, 0, 0, 0))],
        out_specs=pl.BlockSpec((None, SUB, 2, nc, cb), lambda p, j, c: (p, j, 0, 0, c)),
        out_shape=jax.ShapeDtypeStruct((P, nf, 2, nc, C), F32),
        compiler_params=pltpu.CompilerParams(dimension_semantics=("parallel", "parallel", "parallel"),
                                             vmem_limit_bytes=VMEM_LIMIT),
        name="hyena_fft_stage1",
    )(z, l1, tw)


def _fft2_filter_kernel(a_ref, m2_ref, k_ref, *, nf):
    for j in range(SUB):
        rhs = jnp.concatenate([a_ref[:, 0, j, :], a_ref[:, 1, j, :]], axis=0).astype(BF16)
        x = jnp.dot(m2_ref[...], rhs, preferred_element_type=F32)
        k_ref[j, 0] = x[:nf]
        k_ref[j, 1] = x[nf:]


def _fft2_filter(a, m2, nc, nf):
    C = a.shape[-1]
    cb = FFT_CB
    return pl.pallas_call(
        functools.partial(_fft2_filter_kernel, nf=nf),
        grid=(nc // SUB, C // cb),
        in_specs=[pl.BlockSpec((None, nf, 2, SUB, cb), lambda k, c: (0, 0, 0, k, c)),
                  pl.BlockSpec(m2.shape, lambda k, c: (0, 0))],
        out_specs=pl.BlockSpec((SUB, 2, nf, cb), lambda k, c: (k, 0, 0, c)),
        out_shape=jax.ShapeDtypeStruct((nc, 2, nf, C), F32),
        compiler_params=pltpu.CompilerParams(dimension_semantics=("parallel", "parallel"),
                                             vmem_limit_bytes=VMEM_LIMIT),
        name="hyena_filter_spectrum",
    )(a, m2)


def _fft2_kernel(a_ref, k_ref, m2_ref, m2i_ref, o_ref, *, nf):
    for j in range(SUB):
        rhs = jnp.concatenate([a_ref[:, 0, j, :], a_ref[:, 1, j, :]], axis=0).astype(BF16)
        x = jnp.dot(m2_ref[...], rhs, preferred_element_type=F32)
        xr, xi = x[:nf], x[nf:]
        kr, ki = k_ref[j, 0], k_ref[j, 1]
        y = jnp.concatenate([xr * kr - xi * ki, xr * ki + xi * kr], axis=0).astype(BF16)
        b = jnp.dot(m2i_ref[...], y, preferred_element_type=F32)
        o_ref[:, 0, j, :] = b[:nf]
        o_ref[:, 1, j, :] = b[nf:]


def _fft2(a, kspec, m2, m2i, nc, nf):
    P, C = a.shape[0], a.shape[-1]
    cb = FFT_CB
    blk = pl.BlockSpec((None, nf, 2, SUB, cb), lambda p, k, c: (p, 0, 0, k, c))
    return pl.pallas_call(
        functools.partial(_fft2_kernel, nf=nf),
        grid=(P, nc // SUB, C // cb),
        in_specs=[blk,
                  pl.BlockSpec((SUB, 2, nf, cb), lambda p, k, c: (k, 0, 0, c)),
                  pl.BlockSpec(m2.shape, lambda p, k, c: (0, 0)),
                  pl.BlockSpec(m2i.shape, lambda p, k, c: (0, 0))],
        out_specs=blk,
        out_shape=jax.ShapeDtypeStruct(a.shape, F32),
        compiler_params=pltpu.CompilerParams(dimension_semantics=("parallel", "parallel", "parallel"),
                                             vmem_limit_bytes=VMEM_LIMIT),
        name="hyena_fft_stage2",
    )(a, kspec, m2, m2i)


def _fft3_kernel(b_ref, tw_ref, l3_ref, s_ref, x0_ref, bias_ref, o_ref, *, nc):
    cb = o_ref.shape[-1]
    half = nc // 2
    for j in range(SUB):
        br, bi = b_ref[j, 0], b_ref[j, 1]
        twr = _lane_tile(tw_ref[j, 0], cb)
        twi = _lane_tile(tw_ref[j, 1], cb)
        rhs = jnp.concatenate([br * twr - bi * twi, br * twi + bi * twr], axis=0).astype(BF16)
        y = jnp.dot(l3_ref[...], rhs, preferred_element_type=F32)
        for q in range(2):
            s = s_ref[q, :, j, :]
            o_ref[q, :, j, :] = x0_ref[q, :, j, :] * (y[q * half:(q + 1) * half] + s * bias_ref[...])


def _fft3(b, tw, l3, s5, x05, bias, nc, nf):
    P, C = b.shape[0], b.shape[-1]
    cb = FFT_CB
    seq = pl.BlockSpec((None, 2, nc // 2, SUB, cb), lambda p, j, c: (p, 0, 0, j, c))
    return pl.pallas_call(
        functools.partial(_fft3_kernel, nc=nc),
        grid=(P, nf // SUB, C // cb),
        in_specs=[pl.BlockSpec((None, SUB, 2, nc, cb), lambda p, j, c: (p, j, 0, 0, c)),
                  pl.BlockSpec((SUB, 2, nc, LANES), lambda p, j, c: (j, 0, 0, 0)),
                  pl.BlockSpec(l3.shape, lambda p, j, c: (0, 0)),
                  seq, seq,
                  pl.BlockSpec((1, cb), lambda p, j, c: (0, c))],
        out_specs=seq,
        out_shape=jax.ShapeDtypeStruct(s5.shape, F32),
        compiler_params=pltpu.CompilerParams(dimension_semantics=("parallel", "parallel", "parallel"),
                                             vmem_limit_bytes=VMEM_LIMIT),
        name="hyena_fft_stage3",
    )(b, tw, l3, s5, x05, bias)


def _fft_split(n):
    nf = 1 << (int(math.log2(n)) // 2)
    return n // nf, nf


def _filter_tables(nc):
    half = nc // 2
    kc = np.arange(nc, dtype=np.float64)[:, None]
    r = np.arange(half, dtype=np.float64)
    mats = []
    for rows_b, drop0 in ((half + (half - 1 - r), False), (nc - r, True)):
        a = 2.0 * np.pi * kc * np.concatenate([r, rows_b])[None, :] / nc
        m = np.concatenate([np.cos(a), -np.sin(a)], axis=0)
        if drop0:
            m[:, half] = 0.0
        mats.append(m)
    return tuple(jnp.asarray(m, F32).astype(BF16) for m in mats)


def _fft1_filter_kernel(hf_ref, hba_ref, hbb_ref, mb_ref, mb0_ref, tw_ref, o_ref, *, nc):
    cb = o_ref.shape[-1]
    m_first = jnp.where(pl.program_id(0) == 0, mb0_ref[...], mb_ref[...])
    for j in range(SUB):
        src = hbb_ref[:, 0, :] if j == 0 else hba_ref[:, SUB - j, :]
        rhs = jnp.concatenate([hf_ref[:, j, :], src], axis=0).astype(BF16)
        a = jnp.dot(m_first if j == 0 else mb_ref[...], rhs, preferred_element_type=F32)
        ar, ai = a[:nc], a[nc:]
        twr = _lane_tile(tw_ref[j, 0], cb)
        twi = _lane_tile(tw_ref[j, 1], cb)
        o_ref[j, 0] = ar * twr + ai * twi
        o_ref[j, 1] = ai * twr - ar * twi


def _fft1_filter(h_f, h_b, mb, mb0, tw, nc, nf):
    C = h_f.shape[-1]
    cb = FFT_CB
    nblk = nf // SUB
    hf3 = h_f.reshape(nc // 2, nf, C)
    hb3 = h_b.reshape(nc // 2, nf, C)
    blk = lambda f: pl.BlockSpec((nc // 2, SUB, cb), f)
    return pl.pallas_call(
        functools.partial(_fft1_filter_kernel, nc=nc),
        grid=(nblk, C // cb),
        in_specs=[blk(lambda j, c: (0, j, c)),
                  blk(lambda j, c: (0, nblk - 1 - j, c)),
                  blk(lambda j, c: (0, (nblk - j) % nblk, c)),
                  pl.BlockSpec(mb.shape, lambda j, c: (0, 0)),
                  pl.BlockSpec(mb0.shape, lambda j, c: (0, 0)),
                  pl.BlockSpec((SUB, 2, nc, LANES), lambda j, c: (j, 0, 0, 0))],
        out_specs=pl.BlockSpec((None, SUB, 2, nc, cb), lambda j, c: (0, j, 0, 0, c)),
        out_shape=jax.ShapeDtypeStruct((1, nf, 2, nc, C), F32),
        compiler_params=pltpu.CompilerParams(dimension_semantics=("parallel", "parallel"),
                                             vmem_limit_bytes=VMEM_LIMIT),
        name="hyena_filter_stage1",
    )(hf3, hb3, hb3, mb, mb0, tw)


def _hyena_conv(s, x0, h_f, h_b, bias):
    B, L, C = s.shape
    nc, nf = _fft_split(2 * L)
    tw = _twiddle(nc, nf)
    l1, m2, m2i, l3 = _fft_tables(nc, nf)
    mb, mb0 = _filter_tables(nc)
    kspec = _fft2_filter(_fft1_filter(h_f, h_b, mb, mb0, tw, nc, nf), m2, nc, nf)
    s5 = s.reshape(B // 2, 2, nc // 2, nf, C)
    x05 = x0.reshape(B // 2, 2, nc // 2, nf, C)
    a = _fft1(s5, l1, tw, nc, nf)
    b = _fft2(a, kspec, m2, m2i, nc, nf)
    return _fft3(b, tw, l3, s5, x05, bias.reshape(1, C), nc, nf).reshape(B, L, C)


def _hyena_filter(L, w1, b1, w2, b2, w3, freq):
    pos = jnp.arange(L, dtype=F32)
    t = jnp.linspace(0.0, 1.0, L, dtype=F32)[:, None]
    bands = jnp.linspace(1e-4, HY_BANDS - 1, HY_BANDS, dtype=F32)
    ang = (2.0 * math.pi / L) * pos[:, None] * bands[None, :]
    z = jnp.concatenate([t, jnp.cos(ang), -jnp.sin(ang)], axis=-1)
    h = jnp.sin(freq * (z @ w1 + b1))
    h = jnp.sin(freq * (h @ w2 + b2))
    h = h @ w3
    deltas = jnp.abs(jnp.linspace(math.log(HY_DECAY_TARGET) / HY_SLOW_PCT,
                                  math.log(HY_DECAY_TARGET) / HY_FAST_PCT, HY_CH, dtype=F32))
    window = jnp.exp(-t * deltas[None, :])
    h_f = h[:, :HY_CH] * window
    h_b = h[:, HY_CH:] * window
    l1 = jnp.sum(jnp.abs(h_f), axis=0) + jnp.sum(jnp.abs(h_b[1:]), axis=0)
    return h_f / l1, h_b / l1


def _split3(x):
    x1 = x.astype(BF16)
    r = x - x1.astype(F32)
    x2 = r.astype(BF16)
    x3 = (r - x2.astype(F32)).astype(BF16)
    return x1, x2, x3


def _mm(a, b):
    return jnp.dot(a.astype(BF16), b.astype(BF16), preferred_element_type=F32)


def _delta_kernel(qf_ref, kf_ref, vf_ref, gf_ref, qb_ref, kb_ref, vb_ref, gb_ref, of_ref, ob_ref, s_ref,
                  *, n_sub):
    @pl.when(pl.program_id(1) == 0)
    def _():
        s_ref[...] = jnp.zeros_like(s_ref)

    row = lax.broadcasted_iota(jnp.int32, (CHUNK, CHUNK), 0)
    col = lax.broadcasted_iota(jnp.int32, (CHUNK, CHUNK), 1)
    eye_f = jnp.where(row == col, 1.0, 0.0).astype(F32)
    dirs = ((qf_ref, kf_ref, vf_ref, gf_ref, of_ref, row >= col, row > col),
            (qb_ref, kb_ref, vb_ref, gb_ref, ob_ref, row <= col, row < col))

    ch = []
    for d, (q_ref, k_ref, v_ref, g_ref, _, incl, strict) in enumerate(dirs):
        tri = jnp.where(incl, 1.0, 0.0).astype(BF16)
        for c in range(n_sub):
            rows = slice(c * CHUNK, (c + 1) * CHUNK)
            gates = g_ref[rows, :]
            gl = 2 * DN_HEADS * d
            g1, g2, g3 = _split3(gates)
            gc_all = (jnp.dot(tri, g1, preferred_element_type=F32)
                      + jnp.dot(tri, g2, preferred_element_type=F32)
                      + jnp.dot(tri, g3, preferred_element_type=F32))
            gsum_all = jnp.sum(gates, axis=0, keepdims=True)
            for h in range(DN_HEADS):
                lanes = slice(h * DN_DK, (h + 1) * DN_DK)
                kh = k_ref[rows, lanes]
                beta = gates[:, gl + DN_HEADS + h:gl + DN_HEADS + h + 1]
                gc = gc_all[:, gl + h:gl + h + 1]
                g_last = gsum_all[:, gl + h:gl + h + 1]
                gc_b = jnp.broadcast_to(gc, (CHUNK, CHUNK))
                gc_row = jnp.sum(gc_b * eye_f, axis=0, keepdims=True)
                decay = jnp.where(incl, jnp.exp(jnp.minimum(gc_b - gc_row, 0.0)), 0.0)
                e_gc = jnp.exp(gc)
                kb = kh * beta
                ch.append(dict(d=d, c=c, h=h, lanes=lanes, rows=rows, strict=strict, decay=decay,
                               kh16=kh.astype(BF16), kb16=kb.astype(BF16),
                               q16=q_ref[rows, lanes].astype(BF16),
                               rhs=jnp.concatenate([v_ref[rows, lanes] * beta, kb * e_gc],
                                                   axis=1).astype(BF16),
                               qe=q_ref[rows, lanes] * e_gc,
                               kdT=(kh * jnp.exp(g_last - gc)).T.astype(BF16),
                               e_last=jnp.exp(g_last)))
    for x in ch:
        a = lax.dot_general(x["kb16"], x["kh16"], _NT, preferred_element_type=F32) * x["decay"]
        x["a"] = jnp.where(x["strict"], a, 0.0)
        x["qk"] = (lax.dot_general(x["q16"], x["kh16"], _NT, preferred_element_type=F32)
                   * x["decay"]).astype(BF16)
    for x in ch:
        x["t"] = eye_f - x["a"]
        x["p"] = _mm(x["a"], x["a"])
    for level in range(5):
        for x in ch:
            p16 = x["p"].astype(BF16)
            x["t"] = x["t"] + jnp.dot(x["t"].astype(BF16), p16, preferred_element_type=F32)
            if level < 4:
                x["p"] = jnp.dot(p16, p16, preferred_element_type=F32)
    for x in ch:
        uw = jnp.dot(x["t"].astype(BF16), x["rhs"], preferred_element_type=F32)
        x["u"] = uw[:, :DN_DV]
        x["wq"] = jnp.concatenate([uw[:, DN_DV:], x["qe"]], axis=0).astype(BF16)

    for step in range(n_sub):
        cur = [x for x in ch if x["c"] == (step if x["d"] == 0 else n_sub - 1 - step)]
        for x in cur:
            x["s"] = s_ref[x["d"], x["h"]]
            x["ws"] = jnp.dot(x["wq"], x["s"].astype(BF16), preferred_element_type=F32)
        for x in cur:
            x["vn"] = (x["u"] - x["ws"][:CHUNK]).astype(BF16)
        for x in cur:
            o = x["ws"][CHUNK:] + jnp.dot(x["qk"], x["vn"], preferred_element_type=F32)
            dirs[x["d"]][4][x["rows"], x["lanes"]] = o
            s_ref[x["d"], x["h"]] = (x["s"] * x["e_last"]
                                     + jnp.dot(x["kdT"], x["vn"], preferred_element_type=F32))


def _delta_scan(q, k, v, gates):
    B, L, _ = q.shape
    n_sub = DN_STEP_CHUNKS
    rows = n_sub * CHUNK
    nblk = L // rows
    fwd = pl.BlockSpec((None, rows, DN_W), lambda b, j: (b, j, 0))
    bwd = pl.BlockSpec((None, rows, DN_W), lambda b, j: (b, nblk - 1 - j, 0))
    gfwd = pl.BlockSpec((None, rows, LANES), lambda b, j: (b, j, 0))
    gbwd = pl.BlockSpec((None, rows, LANES), lambda b, j: (b, nblk - 1 - j, 0))
    out = jax.ShapeDtypeStruct((B, L, DN_W), F32)
    return pl.pallas_call(
        functools.partial(_delta_kernel, n_sub=n_sub),
        grid=(B, nblk),
        in_specs=[fwd, fwd, fwd, gfwd, bwd, bwd, bwd, gbwd],
        out_specs=[fwd, bwd],
        out_shape=[out, out],
        scratch_shapes=[pltpu.VMEM((2, DN_HEADS, DN_DK, DN_DV), F32)],
        compiler_params=pltpu.CompilerParams(dimension_semantics=("parallel", "arbitrary"),
                                             vmem_limit_bytes=VMEM_LIMIT),
        name="delta_scan",
    )(q, k, v, gates, q, k, v, gates)


def _trunk(x, w, hy_pos_w1, hy_pos_b1, hy_pos_w2, hy_pos_b2, hy_pos_w3, hy_sin_freq, hy_bias, dn_norm_w):
    B, L, D = x.shape
    T = B * L
    x2d = x.reshape(T, D)
    x0, s, q, k, v, z, dn_gates = _in_proj(x2d, L, w["norm_mix"], *w["in_proj"])
    seq = lambda a: a.reshape(B, L, a.shape[-1])
    h_f, h_b = _hyena_filter(L, hy_pos_w1, hy_pos_b1, hy_pos_w2, hy_pos_b2, hy_pos_w3, hy_sin_freq)
    y_hy = _hyena_conv(seq(s), seq(x0), h_f, h_b, hy_bias)
    o_f, o_b = _delta_scan(seq(q), seq(k), seq(v), seq(dn_gates))
    x1, h_ffn, logits = _out_proj(x2d, y_hy.reshape(T, HY_CH), o_f.reshape(T, DN_W), o_b.reshape(T, DN_W),
                                 z, dn_norm_w, w["out_hy"], w["out_dn"], w["norm_ffn"],
                                 w["router_hi"], w["router_lo"], w["router_b"])
    yb, dest, gates = _moe(h_ffn, logits[:, :N_EXPERTS], w["wg"], w["wl"], w["bg"], w["bl"], w["wd"],
                           w["bd"])
    return _combine_final(yb, dest, gates, x1, w["norm_final"]).reshape(B, L, D)


def kernel(x_prompt, x_sample, norm_mix_w, w_in, hy_conv_w, hy_conv_b, hy_pos_w1, hy_pos_b1, hy_pos_w2, hy_pos_b2, hy_pos_w3, hy_sin_freq, hy_bias, dn_conv_w, dn_a_log, dn_dt_bias, dn_norm_w, w_out, norm_ffn_w, w_router, b_router, w_gate_up, b_gate_up, w_down, b_down, norm_final_w):
    w_out16 = w_out[0].astype(BF16)
    wr = jnp.pad(w_router[0], ((0, 0), (0, LANES - N_EXPERTS)))
    wr_hi = wr.astype(BF16)
    wg, wl = _deinterleave(w_gate_up[0])
    w = {
        "norm_mix": norm_mix_w[0],
        "in_proj": _in_proj_params(w_in[0], hy_conv_w[0], hy_conv_b[0], dn_conv_w[0], dn_a_log[0],
                                   dn_dt_bias[0]),
        "out_hy": w_out16[:HY_CH],
        "out_dn": w_out16[HY_CH:],
        "norm_ffn": norm_ffn_w[0],
        "router_hi": wr_hi,
        "router_lo": (wr - wr_hi.astype(F32)).astype(BF16),
        "router_b": jnp.pad(b_router[0], (0, LANES - N_EXPERTS)).reshape(1, LANES),
        "wg": wg,
        "wl": wl,
        "bg": b_gate_up[0][:, 0::2].reshape(N_EXPERTS, 1, D_FF),
        "bl": b_gate_up[0][:, 1::2].reshape(N_EXPERTS, 1, D_FF),
        "wd": w_down[0].astype(BF16),
        "bd": b_down[0].reshape(N_EXPERTS, 1, D_MODEL),
        "norm_final": norm_final_w,
    }
    mix = (hy_pos_w1[0], hy_pos_b1[0], hy_pos_w2[0], hy_pos_b2[0], hy_pos_w3[0], hy_sin_freq[0],
           hy_bias[0], dn_norm_w[0])
    return (_trunk(x_prompt, w, *mix), _trunk(x_sample, w, *mix))
```

```python
import functools
import math

import jax
import jax.numpy as jnp
import numpy as np
from jax import lax
from jax.experimental import pallas as pl
from jax.experimental.pallas import tpu as pltpu
from jax.experimental.pallas import tpu_sc as plsc

D_MODEL = 1024
HY_CH = 512
DN_HEADS = 4
DN_DK = 128
DN_DV = 128
DN_QK = DN_HEADS * DN_DK
DN_W = DN_HEADS * DN_DV
HY_IN = 3 * HY_CH
DN_CONV = 2 * DN_QK + DN_W
N_GATE = 4 * DN_HEADS
SHORT_CONV = 3
CONV_COLS = HY_IN + DN_CONV
REST_COLS = DN_W + 128
HY_EMB = 33
HY_BANDS = (HY_EMB - 1) // 2
HY_DECAY_TARGET = 1e-2
HY_FAST_PCT = 0.3
HY_SLOW_PCT = 1.5
CHUNK = 64
N_EXPERTS = 32
TOP_K = 4
D_FF = D_MODEL
SWIGLU_ALPHA = 1.702
SWIGLU_LIMIT = 7.0
MOE_BLOCK = 512
EPS = 1e-6

LANES = 128
SUB = 8
ROW_TILE = 512
FFT_CB = 256
DN_STEP_CHUNKS = 4
DEINT_COLS = 512
COMBINE_TILE = 512
SC_WINDOW = 128
SC_ROW = LANES
SC_PIECES = D_MODEL // SC_ROW
VMEM_LIMIT = 56 * 1024 * 1024

F32 = jnp.float32
BF16 = jnp.bfloat16

_NT = (((1,), (1,)), ((), ()))


def _rms(x, g):
    return x * lax.rsqrt(jnp.mean(x * x, axis=-1, keepdims=True) + EPS) * g


def _silu(x):
    return x * jax.nn.sigmoid(x)


def _head_l2norm(x):
    parts = []
    for hd in range(DN_HEADS):
        xh = x[:, hd * DN_DK:(hd + 1) * DN_DK]
        parts.append(xh * lax.rsqrt(jnp.sum(xh * xh, axis=-1, keepdims=True) + EPS))
    return jnp.concatenate(parts, axis=1)


def _in_proj_kernel(xp_ref, x_ref, xn_ref, g_ref, wc_ref, wr_ref, cw_ref, cb_ref, gt_ref,
                    x0_ref, s_ref, q_ref, k_ref, v_ref, z_ref, gate_ref, p_scr, *, tiles_per_seq):
    i = pl.program_id(0)
    first = (i % tiles_per_seq) == 0
    last = (i % tiles_per_seq) == tiles_per_seq - 1
    g = g_ref[...]
    hp = jnp.where(first, 0.0, _rms(xp_ref[...], g))
    hn = jnp.where(last, 0.0, _rms(xn_ref[...], g))
    h_all = jnp.concatenate([hp, _rms(x_ref[...], g), hn], axis=0).astype(BF16)
    h = h_all[SUB:SUB + ROW_TILE]

    def project(c0):
        cols = slice(c0, c0 + HY_CH)
        p_scr[:, cols] = jnp.dot(h_all, wc_ref[:, cols], preferred_element_type=F32)

    def conv(c0):
        cols = slice(c0, c0 + HY_CH)
        return (p_scr[pl.ds(SUB - 1, ROW_TILE), cols] * cw_ref[0:1, cols]
                + p_scr[pl.ds(SUB, ROW_TILE), cols] * cw_ref[1:2, cols]
                + p_scr[pl.ds(SUB + 1, ROW_TILE), cols] * cw_ref[2:3, cols])

    project(0)
    project(HY_CH)
    x0_ref[...] = conv(0) + cb_ref[:, 0:HY_CH]
    project(2 * HY_CH)
    project(HY_IN)
    s_ref[...] = (conv(HY_CH) + cb_ref[:, HY_CH:2 * HY_CH]) * (conv(2 * HY_CH) + cb_ref[:, 2 * HY_CH:])
    project(HY_IN + DN_QK)
    q_ref[...] = _head_l2norm(_silu(conv(HY_IN))) * (DN_DK ** -0.5)
    project(HY_IN + 2 * DN_QK)
    k_ref[...] = _head_l2norm(_silu(conv(HY_IN + DN_QK)))
    rest = jnp.dot(h, wr_ref[...], preferred_element_type=F32)
    v_ref[...] = _silu(conv(HY_IN + 2 * DN_QK))
    z_ref[...] = rest[:, :DN_W]
    a = rest[:, DN_W:] + gt_ref[1:2]
    softplus = jnp.maximum(a, 0.0) + jnp.log(1.0 + jnp.exp(-jnp.abs(a)))
    gate_ref[...] = jnp.where(gt_ref[2:3] > 0.5, -gt_ref[0:1] * softplus, jax.nn.sigmoid(rest[:, DN_W:]))


def _in_proj(x2d, seq_len, g, w_conv, w_rest, conv_w, conv_b, gate_tab):
    T = x2d.shape[0]
    per = ROW_TILE // SUB
    last_sub = T // SUB - 1
    const = lambda i: (0, 0)
    row = lambda i: (i, 0)
    o512 = pl.BlockSpec((ROW_TILE, HY_CH), row)
    s512 = jax.ShapeDtypeStruct((T, HY_CH), F32)
    return pl.pallas_call(
        functools.partial(_in_proj_kernel, tiles_per_seq=seq_len // ROW_TILE),
        grid=(T // ROW_TILE,),
        in_specs=[pl.BlockSpec((SUB, D_MODEL), lambda i: (jnp.maximum(i * per - 1, 0), 0)),
                  pl.BlockSpec((ROW_TILE, D_MODEL), row),
                  pl.BlockSpec((SUB, D_MODEL), lambda i: (jnp.minimum((i + 1) * per, last_sub), 0)),
                  pl.BlockSpec((1, D_MODEL), const),
                  pl.BlockSpec((D_MODEL, CONV_COLS), const),
                  pl.BlockSpec((D_MODEL, REST_COLS), const),
                  pl.BlockSpec((SHORT_CONV, CONV_COLS), const),
                  pl.BlockSpec((1, HY_IN), const),
                  pl.BlockSpec((3, LANES), const)],
        out_specs=[o512, o512, o512, o512, o512, o512, pl.BlockSpec((ROW_TILE, LANES), row)],
        out_shape=[s512, s512, s512, s512, s512, s512, jax.ShapeDtypeStruct((T, LANES), F32)],
        scratch_shapes=[pltpu.VMEM((ROW_TILE + 2 * SUB, CONV_COLS), F32)],
        compiler_params=pltpu.CompilerParams(dimension_semantics=("parallel",),
                                             vmem_limit_bytes=VMEM_LIMIT),
        name="in_proj",
    )(x2d, x2d, x2d, g.reshape(1, D_MODEL), w_conv, w_rest, conv_w, conv_b, gate_tab)


def _in_proj_params(w_in, hy_conv_w, hy_conv_b, dn_conv_w, dn_a_log, dn_dt_bias):
    H = DN_HEADS
    w16 = w_in.astype(BF16)
    gc = w16[:, CONV_COLS + DN_W:]
    gc = jnp.concatenate([gc[:, 0:H], gc[:, 2 * H:3 * H], gc[:, H:2 * H], gc[:, 3 * H:]], axis=1)
    w_rest = jnp.concatenate([w16[:, CONV_COLS:CONV_COLS + DN_W],
                              jnp.pad(gc, ((0, 0), (0, LANES - N_GATE)))], axis=1)
    zero, one, pad = jnp.zeros((H,), F32), jnp.ones((H,), F32), jnp.zeros((LANES - N_GATE,), F32)
    gate_tab = jnp.stack([jnp.concatenate([jnp.exp(dn_a_log[0]), zero, jnp.exp(dn_a_log[1]), zero, pad]),
                          jnp.concatenate([dn_dt_bias[0], zero, dn_dt_bias[1], zero, pad]),
                          jnp.concatenate([one, zero, one, zero, pad])])
    return (w16[:, :CONV_COLS], w_rest, jnp.concatenate([hy_conv_w, dn_conv_w], axis=1),
            hy_conv_b.reshape(1, HY_IN), gate_tab)


def _out_proj_kernel(x_ref, yh_ref, of_ref, ob_ref, z_ref, nw_ref, wh_ref, wd_ref, g_ref, wrh_ref,
                     wrl_ref, br_ref, x1_ref, h_ref, lg_ref):
    o = of_ref[...] + ob_ref[...]
    heads = []
    for hd in range(DN_HEADS):
        oh = o[:, hd * DN_DV:(hd + 1) * DN_DV]
        heads.append(oh * lax.rsqrt(jnp.mean(oh * oh, axis=-1, keepdims=True) + EPS))
    z = z_ref[...]
    y_dn = jnp.concatenate(heads, axis=1) * nw_ref[...] * (z * jax.nn.sigmoid(z))
    x1 = (x_ref[...]
          + jnp.dot(yh_ref[...].astype(BF16), wh_ref[...], preferred_element_type=F32)
          + jnp.dot(y_dn.astype(BF16), wd_ref[...], preferred_element_type=F32))
    x1_ref[...] = x1
    h = _rms(x1, g_ref[...])
    h_hi = h.astype(BF16)
    h_lo = (h - h_hi.astype(F32)).astype(BF16)
    _store_pieces(h_ref, h)
    lg_ref[...] = (jnp.dot(h_hi, wrh_ref[...], preferred_element_type=F32)
                   + jnp.dot(h_lo, wrh_ref[...], preferred_element_type=F32)
                   + jnp.dot(h_hi, wrl_ref[...], preferred_element_type=F32)
                   + br_ref[...])


def _out_proj(x2d, y_hy, o_f, o_b, z, dn_norm_w, w_oh, w_od, g, wr_hi, wr_lo, br):
    T = x2d.shape[0]
    const = lambda i: (0, 0)
    row = lambda i: (i, 0)
    return pl.pallas_call(
        _out_proj_kernel,
        grid=(T // ROW_TILE,),
        in_specs=[pl.BlockSpec((ROW_TILE, D_MODEL), row),
                  pl.BlockSpec((ROW_TILE, HY_CH), row),
                  pl.BlockSpec((ROW_TILE, DN_W), row),
                  pl.BlockSpec((ROW_TILE, DN_W), row),
                  pl.BlockSpec((ROW_TILE, DN_W), row),
                  pl.BlockSpec((1, DN_W), const),
                  pl.BlockSpec((HY_CH, D_MODEL), const),
                  pl.BlockSpec((DN_W, D_MODEL), const),
                  pl.BlockSpec((1, D_MODEL), const),
                  pl.BlockSpec((D_MODEL, LANES), const),
                  pl.BlockSpec((D_MODEL, LANES), const),
                  pl.BlockSpec((1, LANES), const)],
        out_specs=[pl.BlockSpec((ROW_TILE, D_MODEL), row),
                   pl.BlockSpec((ROW_TILE * SC_PIECES, SC_ROW), row),
                   pl.BlockSpec((ROW_TILE, LANES), row)],
        out_shape=[jax.ShapeDtypeStruct((T, D_MODEL), F32),
                   jax.ShapeDtypeStruct((T * SC_PIECES, SC_ROW), F32),
                   jax.ShapeDtypeStruct((T, LANES), F32)],
        compiler_params=pltpu.CompilerParams(dimension_semantics=("parallel",),
                                             vmem_limit_bytes=VMEM_LIMIT),
        name="out_proj_router",
    )(x2d, y_hy, o_f, o_b, z, jnp.tile(dn_norm_w, DN_HEADS).reshape(1, DN_W), w_oh, w_od,
      g.reshape(1, D_MODEL), wr_hi, wr_lo, br)


def _deint_kernel(w_ref, p_ref, og_ref, ol_ref):
    half = DEINT_COLS // 2
    sel = jnp.dot(w_ref[0].astype(BF16), p_ref[...], preferred_element_type=F32)
    og_ref[0] = sel[:, :half].astype(BF16)
    ol_ref[0] = sel[:, half:].astype(BF16)


def _deinterleave(w_gate_up):
    half = DEINT_COLS // 2
    r = np.arange(DEINT_COLS)[:, None]
    c = np.arange(DEINT_COLS)[None, :]
    perm = jnp.asarray(np.where(c < half, r == 2 * c, r == 2 * (c - half) + 1), BF16)
    out = jax.ShapeDtypeStruct((N_EXPERTS, D_MODEL, D_FF), BF16)
    return pl.pallas_call(
        _deint_kernel,
        grid=(N_EXPERTS, 2 * D_FF // DEINT_COLS),
        in_specs=[pl.BlockSpec((1, D_MODEL, DEINT_COLS), lambda e, j: (e, 0, j)),
                  pl.BlockSpec((DEINT_COLS, DEINT_COLS), lambda e, j: (0, 0))],
        out_specs=[pl.BlockSpec((1, D_MODEL, half), lambda e, j: (e, 0, j)),
                   pl.BlockSpec((1, D_MODEL, half), lambda e, j: (e, 0, j))],
        out_shape=[out, out],
        compiler_params=pltpu.CompilerParams(dimension_semantics=("parallel", "parallel")),
        name="deinterleave_gate_up",
    )(w_gate_up, perm)


def _store_pieces(ref, x):
    rows = x.shape[0]
    for c in range(SC_PIECES):
        ref[pl.ds(c, rows, stride=SC_PIECES), :] = x[:, c * SC_ROW:(c + 1) * SC_ROW]


def _load_pieces(ref, rows):
    return jnp.concatenate([ref[pl.ds(c, rows, stride=SC_PIECES), :] for c in range(SC_PIECES)], axis=1)


def _sc_scatter(x, indices, n_out):
    n = indices.shape[0]
    rows, width = x.shape
    nsrc = rows // SC_WINDOW
    mesh = plsc.VectorSubcoreMesh(core_axis_name="core", subcore_axis_name="subcore")

    @functools.partial(pl.kernel, out_type=jax.ShapeDtypeStruct((n_out, width), x.dtype), mesh=mesh)
    def scatter(x_hbm, i_hbm, o_hbm):
        def body(x_vmem, i_vmem):
            pltpu.sync_copy(x_vmem, o_hbm.at[i_vmem.at[0]])

        pltpu.emit_pipeline(
            body,
            grid=(n // SC_WINDOW,),
            in_specs=[pl.BlockSpec((SC_WINDOW, width), index_map=lambda i: (i % nsrc, 0)),
                      pl.BlockSpec((1, SC_WINDOW), index_map=lambda i: (0, i))],
            out_specs=[],
            core_axis_name=("core", "subcore"),
            dimension_semantics=(pltpu.PARALLEL,),
        )(x_hbm, i_hbm)

    return scatter(x, indices.reshape(1, n))


def _expert_kernel(be_ref, nb_ref, nv_ref, xb_ref, wg_ref, wl_ref, bg_ref, bl_ref, wd_ref, bd_ref, y_ref):
    i = pl.program_id(0)

    @pl.when(i < nb_ref[0])
    def _():
        row_id = lax.broadcasted_iota(jnp.int32, (MOE_BLOCK, 1), 0)
        xb = jnp.where(row_id < nv_ref[i], _load_pieces(xb_ref, MOE_BLOCK), 0.0).astype(BF16)
        hg = jnp.dot(xb, wg_ref[0], preferred_element_type=F32) + bg_ref[0]
        hl = jnp.dot(xb, wl_ref[0], preferred_element_type=F32) + bl_ref[0]
        x_glu = jnp.minimum(hg, SWIGLU_LIMIT)
        x_lin = jnp.clip(hl, -SWIGLU_LIMIT, SWIGLU_LIMIT)
        act = x_glu * jax.nn.sigmoid(SWIGLU_ALPHA * x_glu) * (x_lin + 1.0)
        y = jnp.dot(act.astype(BF16), wd_ref[0], preferred_element_type=F32) + bd_ref[0]
        _store_pieces(y_ref, y)

    @pl.when(i >= nb_ref[0])
    def _():
        y_ref[...] = jnp.zeros_like(y_ref)


def _expert_mlp(xb, block_e, n_used, n_valid, wg, wl, bg, bl, wd, bd):
    n_blocks = xb.shape[0] // (MOE_BLOCK * SC_PIECES)
    n_rows = n_blocks * MOE_BLOCK
    rowm = lambda i, be, nb, nv: (i, 0)
    exp3 = lambda i, be, nb, nv: (be[i], 0, 0)
    grid_spec = pltpu.PrefetchScalarGridSpec(
        num_scalar_prefetch=3,
        grid=(n_blocks,),
        in_specs=[pl.BlockSpec((MOE_BLOCK * SC_PIECES, SC_ROW), rowm),
                  pl.BlockSpec((1, D_MODEL, D_FF), exp3),
                  pl.BlockSpec((1, D_MODEL, D_FF), exp3),
                  pl.BlockSpec((1, 1, D_FF), exp3),
                  pl.BlockSpec((1, 1, D_FF), exp3),
                  pl.BlockSpec((1, D_FF, D_MODEL), exp3),
                  pl.BlockSpec((1, 1, D_MODEL), exp3)],
        out_specs=pl.BlockSpec((MOE_BLOCK * SC_PIECES, SC_ROW), rowm),
    )
    return pl.pallas_call(
        _expert_kernel,
        grid_spec=grid_spec,
        out_shape=jax.ShapeDtypeStruct((n_rows * SC_PIECES, SC_ROW), F32),
        compiler_params=pltpu.CompilerParams(dimension_semantics=("arbitrary",),
                                             vmem_limit_bytes=VMEM_LIMIT),
        name="expert_mlp",
    )(block_e, n_used, n_valid, xb, wg, wl, bg, bl, wd, bd)


def _moe(h, logits, wg, wl, bg, bl, wd, bd):
    T = logits.shape[0]
    TK = T * TOP_K
    top_vals, top_idx = lax.top_k(logits, TOP_K)
    gates = jax.nn.softmax(top_vals, axis=-1)
    onehot = jax.nn.one_hot(top_idx, N_EXPERTS, dtype=jnp.int32)
    sel = jnp.sum(onehot, axis=1)
    before = jnp.cumsum(sel, axis=0) - sel
    counts = jnp.sum(sel, axis=0)
    padded = (counts + MOE_BLOCK - 1) // MOE_BLOCK * MOE_BLOCK
    pad_end = jnp.cumsum(padded)
    pad_start = pad_end - padded
    dest = jnp.sum(onehot * (before + pad_start[None, :])[:, None, :], axis=-1).astype(jnp.int32)
    n_blocks = (TK + MOE_BLOCK - 1) // MOE_BLOCK + N_EXPERTS
    n_rows = n_blocks * MOE_BLOCK
    block_start = jnp.arange(n_blocks, dtype=jnp.int32) * MOE_BLOCK
    block_e = jnp.minimum(jnp.sum((block_start[:, None] >= pad_end[None, :]).astype(jnp.int32), axis=1),
                          N_EXPERTS - 1)
    n_used = (pad_end[-1] // MOE_BLOCK).astype(jnp.int32).reshape(1)
    be_onehot = jax.nn.one_hot(block_e, N_EXPERTS, dtype=jnp.int32)
    n_valid = jnp.clip(jnp.sum(be_onehot * (counts + pad_start)[None, :], axis=1) - block_start,
                       0, MOE_BLOCK).astype(jnp.int32)
    idx = (dest.T[:, :, None] * SC_PIECES + jnp.arange(SC_PIECES, dtype=jnp.int32)[None, None, :]).reshape(-1)
    xb = _sc_scatter(h, idx, n_rows * SC_PIECES)
    yb = _expert_mlp(xb, block_e, n_used, n_valid, wg, wl, bg, bl, wd, bd)
    return yb, dest, gates


def _sc_gather(x, indices):
    n = indices.shape[0]
    width = x.shape[1]
    mesh = plsc.VectorSubcoreMesh(core_axis_name="core", subcore_axis_name="subcore")

    @functools.partial(pl.kernel, out_type=jax.ShapeDtypeStruct((n, width), x.dtype), mesh=mesh)
    def gather(x_hbm, i_hbm, o_hbm):
        def body(i_vmem, o_vmem):
            pltpu.sync_copy(x_hbm.at[i_vmem.at[0]], o_vmem)

        pltpu.emit_pipeline(
            body,
            grid=(n // SC_WINDOW,),
            in_specs=[pl.BlockSpec((1, SC_WINDOW), index_map=lambda i: (0, i))],
            out_specs=[pl.BlockSpec((SC_WINDOW, width), index_map=lambda i: (i, 0))],
            core_axis_name=("core", "subcore"),
            dimension_semantics=(pltpu.PARALLEL,),
        )(i_hbm, o_hbm)

    return gather(x, indices.reshape(1, n))


def _combine_kernel(gate_ref, x1_ref, g_ref, y4_ref, o_ref):
    tm = x1_ref.shape[0]
    pieces = SC_PIECES
    cols = []
    for c in range(pieces):
        acc = x1_ref[:, c * SC_ROW:(c + 1) * SC_ROW]
        for k in range(TOP_K):
            acc = acc + gate_ref[:, k:k + 1] * y4_ref[pl.ds((k * pieces + c) * tm, tm), :]
        cols.append(acc)
    o_ref[...] = _rms(jnp.concatenate(cols, axis=1), g_ref[...])


def _combine_final(yb, dest, gates, x1, g):
    T = x1.shape[0]
    tm = COMBINE_TILE
    pieces = SC_PIECES
    d = dest.reshape(T // tm, tm, TOP_K).transpose(0, 2, 1)
    idx = (d[:, :, None, :] * pieces + jnp.arange(pieces, dtype=jnp.int32)[None, None, :, None]).reshape(-1)
    y4 = _sc_gather(yb, idx)
    row = lambda i: (i, 0)
    return pl.pallas_call(
        _combine_kernel,
        grid=(T // tm,),
        in_specs=[pl.BlockSpec((tm, TOP_K), row),
                  pl.BlockSpec((tm, D_MODEL), row),
                  pl.BlockSpec((1, D_MODEL), lambda i: (0, 0)),
                  pl.BlockSpec((tm * TOP_K * pieces, SC_ROW), row)],
        out_specs=pl.BlockSpec((tm, D_MODEL), row),
        out_shape=jax.ShapeDtypeStruct((T, D_MODEL), F32),
        compiler_params=pltpu.CompilerParams(dimension_semantics=("parallel",),
                                             vmem_limit_bytes=VMEM_LIMIT),
        name="moe_combine_final",
    )(gates, x1, g.reshape(1, D_MODEL), y4)


def _fft_tables(nc, nf):
    n = nc * nf
    kc = np.arange(nc, dtype=np.float64)
    a1 = 2.0 * np.pi * np.outer(kc, np.arange(nc // 2)) / nc
    c1, s1 = np.cos(a1), np.sin(a1)
    l1 = np.block([[c1, s1], [-s1, c1]])
    a2 = 2.0 * np.pi * np.outer(np.arange(nf), np.arange(nf)) / nf
    c2, s2 = np.cos(a2), np.sin(a2)
    m2 = np.block([[c2, s2], [-s2, c2]])
    m2i = np.block([[c2, -s2], [s2, c2]])
    a3 = 2.0 * np.pi * np.outer(np.arange(nc // 2), kc) / nc
    c3, s3 = np.cos(a3), np.sin(a3)
    l3 = np.block([[c3, -s3], [s3, c3]]) / n
    return tuple(jnp.asarray(m, F32).astype(BF16) for m in (l1, m2, m2i, l3))


def _twiddle(nc, nf):
    n = nc * nf
    ph = (jnp.arange(nf, dtype=jnp.int32)[:, None] * jnp.arange(nc, dtype=jnp.int32)[None, :]) % n
    ang = ph.astype(F32) * (2.0 * math.pi / n)
    tw = jnp.stack([jnp.cos(ang), jnp.sin(ang)], axis=1)
    return jnp.broadcast_to(tw[..., None], (nf, 2, nc, LANES))


def _lane_tile(t, width):
    return t if width == LANES else jnp.concatenate([t] * (width // LANES), axis=1)


def _fft1_kernel(z_ref, l1_ref, tw_ref, o_ref, *, nc):
    cb = o_ref.shape[-1]
    for j in range(SUB):
        rhs = jnp.concatenate([z_ref[0, :, j, :], z_ref[1, :, j, :]], axis=0).astype(BF16)
        a = jnp.dot(l1_ref[...], rhs, preferred_element_type=F32)
        ar, ai = a[:nc], a[nc:]
        twr = _lane_tile(tw_ref[j, 0], cb)
        twi = _lane_tile(tw_ref[j, 1], cb)
        o_ref[j, 0] = ar * twr + ai * twi
        o_ref[j, 1] = ai * twr - ar * twi


def _fft1(z, l1, tw, nc, nf):
    P, C = z.shape[0], z.shape[-1]
    cb = FFT_CB
    z_spec = pl.BlockSpec((None, 2, nc // 2, SUB, cb), lambda p, j, c: (p, 0, 0, j, c))
    return pl.pallas_call(
        functools.partial(_fft1_kernel, nc=nc),
        grid=(P, nf // SUB, C // cb),
        in_specs=[z_spec,
                  pl.BlockSpec(l1.shape, lambda p, j, c: (0, 0)),
                  pl.BlockSpec((SUB, 2, nc, LANES), lambda p, j, c: (j, 0, 0, 0))],
        out_specs=pl.BlockSpec((None, SUB, 2, nc, cb), lambda p, j, c: (p, j, 0, 0, c)),
        out_shape=jax.ShapeDtypeStruct((P, nf, 2, nc, C), F32),
        compiler_params=pltpu.CompilerParams(dimension_semantics=("parallel", "parallel", "parallel"),
                                             vmem_limit_bytes=VMEM_LIMIT),
        name="hyena_fft_stage1",
    )(z, l1, tw)


def _fft2_filter_kernel(a_ref, m2_ref, k_ref, *, nf):
    for j in range(SUB):
        rhs = jnp.concatenate([a_ref[:, 0, j, :], a_ref[:, 1, j, :]], axis=0).astype(BF16)
        x = jnp.dot(m2_ref[...], rhs, preferred_element_type=F32)
        k_ref[j, 0] = x[:nf]
        k_ref[j, 1] = x[nf:]


def _fft2_filter(a, m2, nc, nf):
    C = a.shape[-1]
    cb = FFT_CB
    return pl.pallas_call(
        functools.partial(_fft2_filter_kernel, nf=nf),
        grid=(nc // SUB, C // cb),
        in_specs=[pl.BlockSpec((None, nf, 2, SUB, cb), lambda k, c: (0, 0, 0, k, c)),
                  pl.BlockSpec(m2.shape, lambda k, c: (0, 0))],
        out_specs=pl.BlockSpec((SUB, 2, nf, cb), lambda k, c: (k, 0, 0, c)),
        out_shape=jax.ShapeDtypeStruct((nc, 2, nf, C), F32),
        compiler_params=pltpu.CompilerParams(dimension_semantics=("parallel", "parallel"),
                                             vmem_limit_bytes=VMEM_LIMIT),
        name="hyena_filter_spectrum",
    )(a, m2)


def _fft2_kernel(a_ref, k_ref, m2_ref, m2i_ref, o_ref, *, nf):
    for j in range(SUB):
        rhs = jnp.concatenate([a_ref[:, 0, j, :], a_ref[:, 1, j, :]], axis=0).astype(BF16)
        x = jnp.dot(m2_ref[...], rhs, preferred_element_type=F32)
        xr, xi = x[:nf], x[nf:]
        kr, ki = k_ref[j, 0], k_ref[j, 1]
        y = jnp.concatenate([xr * kr - xi * ki, xr * ki + xi * kr], axis=0).astype(BF16)
        b = jnp.dot(m2i_ref[...], y, preferred_element_type=F32)
        o_ref[:, 0, j, :] = b[:nf]
        o_ref[:, 1, j, :] = b[nf:]


def _fft2(a, kspec, m2, m2i, nc, nf):
    P, C = a.shape[0], a.shape[-1]
    cb = FFT_CB
    blk = pl.BlockSpec((None, nf, 2, SUB, cb), lambda p, k, c: (p, 0, 0, k, c))
    return pl.pallas_call(
        functools.partial(_fft2_kernel, nf=nf),
        grid=(P, nc // SUB, C // cb),
        in_specs=[blk,
                  pl.BlockSpec((SUB, 2, nf, cb), lambda p, k, c: (k, 0, 0, c)),
                  pl.BlockSpec(m2.shape, lambda p, k, c: (0, 0)),
                  pl.BlockSpec(m2i.shape, lambda p, k, c: (0, 0))],
        out_specs=blk,
        out_shape=jax.ShapeDtypeStruct(a.shape, F32),
        compiler_params=pltpu.CompilerParams(dimension_semantics=("parallel", "parallel", "parallel"),
                                             vmem_limit_bytes=VMEM_LIMIT),
        name="hyena_fft_stage2",
    )(a, kspec, m2, m2i)


def _fft3_kernel(b_ref, tw_ref, l3_ref, s_ref, x0_ref, bias_ref, o_ref, *, nc):
    cb = o_ref.shape[-1]
    half = nc // 2
    for j in range(SUB):
        br, bi = b_ref[j, 0], b_ref[j, 1]
        twr = _lane_tile(tw_ref[j, 0], cb)
        twi = _lane_tile(tw_ref[j, 1], cb)
        rhs = jnp.concatenate([br * twr - bi * twi, br * twi + bi * twr], axis=0).astype(BF16)
        y = jnp.dot(l3_ref[...], rhs, preferred_element_type=F32)
        for q in range(2):
            s = s_ref[q, :, j, :]
            o_ref[q, :, j, :] = x0_ref[q, :, j, :] * (y[q * half:(q + 1) * half] + s * bias_ref[...])


def _fft3(b, tw, l3, s5, x05, bias, nc, nf):
    P, C = b.shape[0], b.shape[-1]
    cb = FFT_CB
    seq = pl.BlockSpec((None, 2, nc // 2, SUB, cb), lambda p, j, c: (p, 0, 0, j, c))
    return pl.pallas_call(
        functools.partial(_fft3_kernel, nc=nc),
        grid=(P, nf // SUB, C // cb),
        in_specs=[pl.BlockSpec((None, SUB, 2, nc, cb), lambda p, j, c: (p, j, 0, 0, c)),
                  pl.BlockSpec((SUB, 2, nc, LANES), lambda p, j, c: (j, 0, 0, 0)),
                  pl.BlockSpec(l3.shape, lambda p, j, c: (0, 0)),
                  seq, seq,
                  pl.BlockSpec((1, cb), lambda p, j, c: (0, c))],
        out_specs=seq,
        out_shape=jax.ShapeDtypeStruct(s5.shape, F32),
        compiler_params=pltpu.CompilerParams(dimension_semantics=("parallel", "parallel", "parallel"),
                                             vmem_limit_bytes=VMEM_LIMIT),
        name="hyena_fft_stage3",
    )(b, tw, l3, s5, x05, bias)


def _fft_split(n):
    nf = 1 << (int(math.log2(n)) // 2)
    return n // nf, nf


def _filter_tables(nc):
    half = nc // 2
    kc = np.arange(nc, dtype=np.float64)[:, None]
    r = np.arange(half, dtype=np.float64)
    mats = []
    for rows_b, drop0 in ((half + (half - 1 - r), False), (nc - r, True)):
        a = 2.0 * np.pi * kc * np.concatenate([r, rows_b])[None, :] / nc
        m = np.concatenate([np.cos(a), -np.sin(a)], axis=0)
        if drop0:
            m[:, half] = 0.0
        mats.append(m)
    return tuple(jnp.asarray(m, F32).astype(BF16) for m in mats)


def _fft1_filter_kernel(hf_ref, hba_ref, hbb_ref, mb_ref, mb0_ref, tw_ref, o_ref, *, nc):
    cb = o_ref.shape[-1]
    m_first = jnp.where(pl.program_id(0) == 0, mb0_ref[...], mb_ref[...])
    for j in range(SUB):
        src = hbb_ref[:, 0, :] if j == 0 else hba_ref[:, SUB - j, :]
        rhs = jnp.concatenate([hf_ref[:, j, :], src], axis=0).astype(BF16)
        a = jnp.dot(m_first if j == 0 else mb_ref[...], rhs, preferred_element_type=F32)
        ar, ai = a[:nc], a[nc:]
        twr = _lane_tile(tw_ref[j, 0], cb)
        twi = _lane_tile(tw_ref[j, 1], cb)
        o_ref[j, 0] = ar * twr + ai * twi
        o_ref[j, 1] = ai * twr - ar * twi


def _fft1_filter(h_f, h_b, mb, mb0, tw, nc, nf):
    C = h_f.shape[-1]
    cb = FFT_CB
    nblk = nf // SUB
    hf3 = h_f.reshape(nc // 2, nf, C)
    hb3 = h_b.reshape(nc // 2, nf, C)
    blk = lambda f: pl.BlockSpec((nc // 2, SUB, cb), f)
    return pl.pallas_call(
        functools.partial(_fft1_filter_kernel, nc=nc),
        grid=(nblk, C // cb),
        in_specs=[blk(lambda j, c: (0, j, c)),
                  blk(lambda j, c: (0, nblk - 1 - j, c)),
                  blk(lambda j, c: (0, (nblk - j) % nblk, c)),
                  pl.BlockSpec(mb.shape, lambda j, c: (0, 0)),
                  pl.BlockSpec(mb0.shape, lambda j, c: (0, 0)),
                  pl.BlockSpec((SUB, 2, nc, LANES), lambda j, c: (j, 0, 0, 0))],
        out_specs=pl.BlockSpec((None, SUB, 2, nc, cb), lambda j, c: (0, j, 0, 0, c)),
        out_shape=jax.ShapeDtypeStruct((1, nf, 2, nc, C), F32),
        compiler_params=pltpu.CompilerParams(dimension_semantics=("parallel", "parallel"),
                                             vmem_limit_bytes=VMEM_LIMIT),
        name="hyena_filter_stage1",
    )(hf3, hb3, hb3, mb, mb0, tw)


def _hyena_conv(s, x0, h_f, h_b, bias):
    B, L, C = s.shape
    nc, nf = _fft_split(2 * L)
    tw = _twiddle(nc, nf)
    l1, m2, m2i, l3 = _fft_tables(nc, nf)
    mb, mb0 = _filter_tables(nc)
    kspec = _fft2_filter(_fft1_filter(h_f, h_b, mb, mb0, tw, nc, nf), m2, nc, nf)
    s5 = s.reshape(B // 2, 2, nc // 2, nf, C)
    x05 = x0.reshape(B // 2, 2, nc // 2, nf, C)
    a = _fft1(s5, l1, tw, nc, nf)
    b = _fft2(a, kspec, m2, m2i, nc, nf)
    return _fft3(b, tw, l3, s5, x05, bias.reshape(1, C), nc, nf).reshape(B, L, C)


def _hyena_filter(L, w1, b1, w2, b2, w3, freq):
    pos = jnp.arange(L, dtype=F32)
    t = jnp.linspace(0.0, 1.0, L, dtype=F32)[:, None]
    bands = jnp.linspace(1e-4, HY_BANDS - 1, HY_BANDS, dtype=F32)
    ang = (2.0 * math.pi / L) * pos[:, None] * bands[None, :]
    z = jnp.concatenate([t, jnp.cos(ang), -jnp.sin(ang)], axis=-1)
    h = jnp.sin(freq * (z @ w1 + b1))
    h = jnp.sin(freq * (h @ w2 + b2))
    h = h @ w3
    deltas = jnp.abs(jnp.linspace(math.log(HY_DECAY_TARGET) / HY_SLOW_PCT,
                                  math.log(HY_DECAY_TARGET) / HY_FAST_PCT, HY_CH, dtype=F32))
    window = jnp.exp(-t * deltas[None, :])
    h_f = h[:, :HY_CH] * window
    h_b = h[:, HY_CH:] * window
    l1 = jnp.sum(jnp.abs(h_f), axis=0) + jnp.sum(jnp.abs(h_b[1:]), axis=0)
    return h_f / l1, h_b / l1


def _split3(x):
    x1 = x.astype(BF16)
    r = x - x1.astype(F32)
    x2 = r.astype(BF16)
    x3 = (r - x2.astype(F32)).astype(BF16)
    return x1, x2, x3


def _mm(a, b):
    return jnp.dot(a.astype(BF16), b.astype(BF16), preferred_element_type=F32)


def _delta_kernel(qf_ref, kf_ref, vf_ref, gf_ref, qb_ref, kb_ref, vb_ref, gb_ref, of_ref, ob_ref, s_ref,
                  *, n_sub):
    @pl.when(pl.program_id(1) == 0)
    def _():
        s_ref[...] = jnp.zeros_like(s_ref)

    row = lax.broadcasted_iota(jnp.int32, (CHUNK, CHUNK), 0)
    col = lax.broadcasted_iota(jnp.int32, (CHUNK, CHUNK), 1)
    eye_f = jnp.where(row == col, 1.0, 0.0).astype(F32)
    dirs = ((qf_ref, kf_ref, vf_ref, gf_ref, of_ref, row >= col, row > col),
            (qb_ref, kb_ref, vb_ref, gb_ref, ob_ref, row <= col, row < col))

    ch = []
    for d, (q_ref, k_ref, v_ref, g_ref, _, incl, strict) in enumerate(dirs):
        tri = jnp.where(incl, 1.0, 0.0).astype(BF16)
        for c in range(n_sub):
            rows = slice(c * CHUNK, (c + 1) * CHUNK)
            gates = g_ref[rows, :]
            gl = 2 * DN_HEADS * d
            g1, g2, g3 = _split3(gates)
            gc_all = (jnp.dot(tri, g1, preferred_element_type=F32)
                      + jnp.dot(tri, g2, preferred_element_type=F32)
                      + jnp.dot(tri, g3, preferred_element_type=F32))
            gsum_all = jnp.sum(gates, axis=0, keepdims=True)
            for h in range(DN_HEADS):
                lanes = slice(h * DN_DK, (h + 1) * DN_DK)
                kh = k_ref[rows, lanes]
                beta = gates[:, gl + DN_HEADS + h:gl + DN_HEADS + h + 1]
                gc = gc_all[:, gl + h:gl + h + 1]
                g_last = gsum_all[:, gl + h:gl + h + 1]
                gc_b = jnp.broadcast_to(gc, (CHUNK, CHUNK))
                gc_row = jnp.sum(gc_b * eye_f, axis=0, keepdims=True)
                decay = jnp.where(incl, jnp.exp(jnp.minimum(gc_b - gc_row, 0.0)), 0.0)
                e_gc = jnp.exp(gc)
                kb = kh * beta
                ch.append(dict(d=d, c=c, h=h, lanes=lanes, rows=rows, strict=strict, decay=decay,
                               kh16=kh.astype(BF16), kb16=kb.astype(BF16),
                               q16=q_ref[rows, lanes].astype(BF16),
                               rhs=jnp.concatenate([v_ref[rows, lanes] * beta, kb * e_gc],
                                                   axis=1).astype(BF16),
                               qe=q_ref[rows, lanes] * e_gc,
                               kdT=(kh * jnp.exp(g_last - gc)).T.astype(BF16),
                               e_last=jnp.exp(g_last)))
    for x in ch:
        a = lax.dot_general(x["kb16"], x["kh16"], _NT, preferred_element_type=F32) * x["decay"]
        x["a"] = jnp.where(x["strict"], a, 0.0)
        x["qk"] = (lax.dot_general(x["q16"], x["kh16"], _NT, preferred_element_type=F32)
                   * x["decay"]).astype(BF16)
    for x in ch:
        x["t"] = eye_f - x["a"]
        x["p"] = _mm(x["a"], x["a"])
    for level in range(5):
        for x in ch:
            p16 = x["p"].astype(BF16)
            x["t"] = x["t"] + jnp.dot(x["t"].astype(BF16), p16, preferred_element_type=F32)
            if level < 4:
                x["p"] = jnp.dot(p16, p16, preferred_element_type=F32)
    for x in ch:
        uw = jnp.dot(x["t"].astype(BF16), x["rhs"], preferred_element_type=F32)
        x["u"] = uw[:, :DN_DV]
        x["wq"] = jnp.concatenate([uw[:, DN_DV:], x["qe"]], axis=0).astype(BF16)

    for step in range(n_sub):
        cur = [x for x in ch if x["c"] == (step if x["d"] == 0 else n_sub - 1 - step)]
        for x in cur:
            x["s"] = s_ref[x["d"], x["h"]]
            x["ws"] = jnp.dot(x["wq"], x["s"].astype(BF16), preferred_element_type=F32)
        for x in cur:
            x["vn"] = (x["u"] - x["ws"][:CHUNK]).astype(BF16)
        for x in cur:
            o = x["ws"][CHUNK:] + jnp.dot(x["qk"], x["vn"], preferred_element_type=F32)
            dirs[x["d"]][4][x["rows"], x["lanes"]] = o
            s_ref[x["d"], x["h"]] = (x["s"] * x["e_last"]
                                     + jnp.dot(x["kdT"], x["vn"], preferred_element_type=F32))


def _delta_scan(q, k, v, gates):
    B, L, _ = q.shape
    n_sub = DN_STEP_CHUNKS
    rows = n_sub * CHUNK
    nblk = L // rows
    fwd = pl.BlockSpec((None, rows, DN_W), lambda b, j: (b, j, 0))
    bwd = pl.BlockSpec((None, rows, DN_W), lambda b, j: (b, nblk - 1 - j, 0))
    gfwd = pl.BlockSpec((None, rows, LANES), lambda b, j: (b, j, 0))
    gbwd = pl.BlockSpec((None, rows, LANES), lambda b, j: (b, nblk - 1 - j, 0))
    out = jax.ShapeDtypeStruct((B, L, DN_W), F32)
    return pl.pallas_call(
        functools.partial(_delta_kernel, n_sub=n_sub),
        grid=(B, nblk),
        in_specs=[fwd, fwd, fwd, gfwd, bwd, bwd, bwd, gbwd],
        out_specs=[fwd, bwd],
        out_shape=[out, out],
        scratch_shapes=[pltpu.VMEM((2, DN_HEADS, DN_DK, DN_DV), F32)],
        compiler_params=pltpu.CompilerParams(dimension_semantics=("parallel", "arbitrary"),
                                             vmem_limit_bytes=VMEM_LIMIT),
        name="delta_scan",
    )(q, k, v, gates, q, k, v, gates)


def _trunk(x, w, hy_pos_w1, hy_pos_b1, hy_pos_w2, hy_pos_b2, hy_pos_w3, hy_sin_freq, hy_bias, dn_norm_w):
    B, L, D = x.shape
    T = B * L
    x2d = x.reshape(T, D)
    x0, s, q, k, v, z, dn_gates = _in_proj(x2d, L, w["norm_mix"], *w["in_proj"])
    seq = lambda a: a.reshape(B, L, a.shape[-1])
    h_f, h_b = _hyena_filter(L, hy_pos_w1, hy_pos_b1, hy_pos_w2, hy_pos_b2, hy_pos_w3, hy_sin_freq)
    y_hy = _hyena_conv(seq(s), seq(x0), h_f, h_b, hy_bias)
    o_f, o_b = _delta_scan(seq(q), seq(k), seq(v), seq(dn_gates))
    x1, h_ffn, logits = _out_proj(x2d, y_hy.reshape(T, HY_CH), o_f.reshape(T, DN_W), o_b.reshape(T, DN_W),
                                 z, dn_norm_w, w["out_hy"], w["out_dn"], w["norm_ffn"],
                                 w["router_hi"], w["router_lo"], w["router_b"])
    yb, dest, gates = _moe(h_ffn, logits[:, :N_EXPERTS], w["wg"], w["wl"], w["bg"], w["bl"], w["wd"],
                           w["bd"])
    return _combine_final(yb, dest, gates, x1, w["norm_final"]).reshape(B, L, D)


def kernel(x_prompt, x_sample, norm_mix_w, w_in, hy_conv_w, hy_conv_b, hy_pos_w1, hy_pos_b1, hy_pos_w2, hy_pos_b2, hy_pos_w3, hy_sin_freq, hy_bias, dn_conv_w, dn_a_log, dn_dt_bias, dn_norm_w, w_out, norm_ffn_w, w_router, b_router, w_gate_up, b_gate_up, w_down, b_down, norm_final_w):
    w_out16 = w_out[0].astype(BF16)
    wr = jnp.pad(w_router[0], ((0, 0), (0, LANES - N_EXPERTS)))
    wr_hi = wr.astype(BF16)
    wg, wl = _deinterleave(w_gate_up[0])
    w = {
        "norm_mix": norm_mix_w[0],
        "in_proj": _in_proj_params(w_in[0], hy_conv_w[0], hy_conv_b[0], dn_conv_w[0], dn_a_log[0],
                                   dn_dt_bias[0]),
        "out_hy": w_out16[:HY_CH],
        "out_dn": w_out16[HY_CH:],
        "norm_ffn": norm_ffn_w[0],
        "router_hi": wr_hi,
        "router_lo": (wr - wr_hi.astype(F32)).astype(BF16),
        "router_b": jnp.pad(b_router[0], (0, LANES - N_EXPERTS)).reshape(1, LANES),
        "wg": wg,
        "wl": wl,
        "bg": b_gate_up[0][:, 0::2].reshape(N_EXPERTS, 1, D_FF),
        "bl": b_gate_up[0][:, 1::2].reshape(N_EXPERTS, 1, D_FF),
        "wd": w_down[0].astype(BF16),
        "bd": b_down[0].reshape(N_EXPERTS, 1, D_MODEL),
        "norm_final": norm_final_w,
    }
    mix = (hy_pos_w1[0], hy_pos_b1[0], hy_pos_w2[0], hy_pos_b2[0], hy_pos_w3[0], hy_sin_freq[0],
           hy_bias[0], dn_norm_w[0])
    y_sample = _trunk(x_sample, w, *mix)
    y_prompt = _trunk(x_prompt, w, *mix)
    return (y_prompt, y_sample)
```

```python
import functools
import math

import jax
import jax.numpy as jnp
import numpy as np
from jax import lax
from jax.experimental import pallas as pl
from jax.experimental.pallas import tpu as pltpu
from jax.experimental.pallas import tpu_sc as plsc

D_MODEL = 1024
HY_CH = 512
DN_HEADS = 4
DN_DK = 128
DN_DV = 128
DN_QK = DN_HEADS * DN_DK
DN_W = DN_HEADS * DN_DV
HY_IN = 3 * HY_CH
DN_CONV = 2 * DN_QK + DN_W
N_GATE = 4 * DN_HEADS
SHORT_CONV = 3
CONV_COLS = HY_IN + DN_CONV
REST_COLS = DN_W + 128
HY_EMB = 33
HY_BANDS = (HY_EMB - 1) // 2
HY_DECAY_TARGET = 1e-2
HY_FAST_PCT = 0.3
HY_SLOW_PCT = 1.5
CHUNK = 64
N_EXPERTS = 32
TOP_K = 4
D_FF = D_MODEL
SWIGLU_ALPHA = 1.702
SWIGLU_LIMIT = 7.0
MOE_BLOCK = 512
EPS = 1e-6

LANES = 128
SUB = 8
ROW_TILE = 512
FFT_CB = 256
DN_STEP_CHUNKS = 4
DEINT_COLS = 512
COMBINE_TILE = 512
SC_WINDOW = 128
SC_ROW = LANES
SC_PIECES = D_MODEL // SC_ROW
VMEM_LIMIT = 56 * 1024 * 1024

F32 = jnp.float32
BF16 = jnp.bfloat16

_NT = (((1,), (1,)), ((), ()))


def _rms(x, g):
    return x * lax.rsqrt(jnp.mean(x * x, axis=-1, keepdims=True) + EPS) * g


def _silu(x):
    return x * jax.nn.sigmoid(x)


def _head_l2norm(x):
    parts = []
    for hd in range(DN_HEADS):
        xh = x[:, hd * DN_DK:(hd + 1) * DN_DK]
        parts.append(xh * lax.rsqrt(jnp.sum(xh * xh, axis=-1, keepdims=True) + EPS))
    return jnp.concatenate(parts, axis=1)


def _in_proj_kernel(xp_ref, x_ref, xn_ref, g_ref, wc_ref, wr_ref, cw_ref, cb_ref, gt_ref,
                    x0_ref, s_ref, q_ref, k_ref, v_ref, z_ref, gate_ref, p_scr, *, tiles_per_seq):
    i = pl.program_id(0)
    first = (i % tiles_per_seq) == 0
    last = (i % tiles_per_seq) == tiles_per_seq - 1
    g = g_ref[...]
    hp = jnp.where(first, 0.0, _rms(xp_ref[...], g))
    hn = jnp.where(last, 0.0, _rms(xn_ref[...], g))
    h_all = jnp.concatenate([hp, _rms(x_ref[...], g), hn], axis=0).astype(BF16)
    h = h_all[SUB:SUB + ROW_TILE]

    def project(c0):
        cols = slice(c0, c0 + HY_CH)
        p_scr[:, cols] = jnp.dot(h_all, wc_ref[:, cols], preferred_element_type=F32)

    def conv(c0):
        cols = slice(c0, c0 + HY_CH)
        return (p_scr[pl.ds(SUB - 1, ROW_TILE), cols] * cw_ref[0:1, cols]
                + p_scr[pl.ds(SUB, ROW_TILE), cols] * cw_ref[1:2, cols]
                + p_scr[pl.ds(SUB + 1, ROW_TILE), cols] * cw_ref[2:3, cols])

    project(0)
    project(HY_CH)
    x0_ref[...] = conv(0) + cb_ref[:, 0:HY_CH]
    project(2 * HY_CH)
    project(HY_IN)
    s_ref[...] = (conv(HY_CH) + cb_ref[:, HY_CH:2 * HY_CH]) * (conv(2 * HY_CH) + cb_ref[:, 2 * HY_CH:])
    project(HY_IN + DN_QK)
    q_ref[...] = _head_l2norm(_silu(conv(HY_IN))) * (DN_DK ** -0.5)
    project(HY_IN + 2 * DN_QK)
    k_ref[...] = _head_l2norm(_silu(conv(HY_IN + DN_QK)))
    rest = jnp.dot(h, wr_ref[...], preferred_element_type=F32)
    v_ref[...] = _silu(conv(HY_IN + 2 * DN_QK))
    z_ref[...] = rest[:, :DN_W]
    a = rest[:, DN_W:] + gt_ref[1:2]
    softplus = jnp.maximum(a, 0.0) + jnp.log(1.0 + jnp.exp(-jnp.abs(a)))
    gate_ref[...] = jnp.where(gt_ref[2:3] > 0.5, -gt_ref[0:1] * softplus, jax.nn.sigmoid(rest[:, DN_W:]))


def _in_proj(x2d, seq_len, g, w_conv, w_rest, conv_w, conv_b, gate_tab):
    T = x2d.shape[0]
    per = ROW_TILE // SUB
    last_sub = T // SUB - 1
    const = lambda i: (0, 0)
    row = lambda i: (i, 0)
    o512 = pl.BlockSpec((ROW_TILE, HY_CH), row)
    s512 = jax.ShapeDtypeStruct((T, HY_CH), F32)
    return pl.pallas_call(
        functools.partial(_in_proj_kernel, tiles_per_seq=seq_len // ROW_TILE),
        grid=(T // ROW_TILE,),
        in_specs=[pl.BlockSpec((SUB, D_MODEL), lambda i: (jnp.maximum(i * per - 1, 0), 0)),
                  pl.BlockSpec((ROW_TILE, D_MODEL), row),
                  pl.BlockSpec((SUB, D_MODEL), lambda i: (jnp.minimum((i + 1) * per, last_sub), 0)),
                  pl.BlockSpec((1, D_MODEL), const),
                  pl.BlockSpec((D_MODEL, CONV_COLS), const),
                  pl.BlockSpec((D_MODEL, REST_COLS), const),
                  pl.BlockSpec((SHORT_CONV, CONV_COLS), const),
                  pl.BlockSpec((1, HY_IN), const),
                  pl.BlockSpec((3, LANES), const)],
        out_specs=[o512, o512, o512, o512, o512, o512, pl.BlockSpec((ROW_TILE, LANES), row)],
        out_shape=[s512, s512, s512, s512, s512, s512, jax.ShapeDtypeStruct((T, LANES), F32)],
        scratch_shapes=[pltpu.VMEM((ROW_TILE + 2 * SUB, CONV_COLS), F32)],
        compiler_params=pltpu.CompilerParams(dimension_semantics=("parallel",),
                                             vmem_limit_bytes=VMEM_LIMIT),
        name="in_proj",
    )(x2d, x2d, x2d, g.reshape(1, D_MODEL), w_conv, w_rest, conv_w, conv_b, gate_tab)


def _in_proj_params(w_in, hy_conv_w, hy_conv_b, dn_conv_w, dn_a_log, dn_dt_bias):
    H = DN_HEADS
    w16 = w_in.astype(BF16)
    gc = w16[:, CONV_COLS + DN_W:]
    gc = jnp.concatenate([gc[:, 0:H], gc[:, 2 * H:3 * H], gc[:, H:2 * H], gc[:, 3 * H:]], axis=1)
    w_rest = jnp.concatenate([w16[:, CONV_COLS:CONV_COLS + DN_W],
                              jnp.pad(gc, ((0, 0), (0, LANES - N_GATE)))], axis=1)
    zero, one, pad = jnp.zeros((H,), F32), jnp.ones((H,), F32), jnp.zeros((LANES - N_GATE,), F32)
    gate_tab = jnp.stack([jnp.concatenate([jnp.exp(dn_a_log[0]), zero, jnp.exp(dn_a_log[1]), zero, pad]),
                          jnp.concatenate([dn_dt_bias[0], zero, dn_dt_bias[1], zero, pad]),
                          jnp.concatenate([one, zero, one, zero, pad])])
    return (w16[:, :CONV_COLS], w_rest, jnp.concatenate([hy_conv_w, dn_conv_w], axis=1),
            hy_conv_b.reshape(1, HY_IN), gate_tab)


def _out_proj_kernel(x_ref, yh_ref, of_ref, ob_ref, z_ref, nw_ref, wh_ref, wd_ref, g_ref, wrh_ref,
                     wrl_ref, br_ref, x1_ref, h_ref, lg_ref):
    o = of_ref[...] + ob_ref[...]
    heads = []
    for hd in range(DN_HEADS):
        oh = o[:, hd * DN_DV:(hd + 1) * DN_DV]
        heads.append(oh * lax.rsqrt(jnp.mean(oh * oh, axis=-1, keepdims=True) + EPS))
    z = z_ref[...]
    y_dn = jnp.concatenate(heads, axis=1) * nw_ref[...] * (z * jax.nn.sigmoid(z))
    x1 = (x_ref[...]
          + jnp.dot(yh_ref[...].astype(BF16), wh_ref[...], preferred_element_type=F32)
          + jnp.dot(y_dn.astype(BF16), wd_ref[...], preferred_element_type=F32))
    x1_ref[...] = x1
    h = _rms(x1, g_ref[...])
    h_hi = h.astype(BF16)
    h_lo = (h - h_hi.astype(F32)).astype(BF16)
    _store_pieces(h_ref, h)
    lg_ref[...] = (jnp.dot(h_hi, wrh_ref[...], preferred_element_type=F32)
                   + jnp.dot(h_lo, wrh_ref[...], preferred_element_type=F32)
                   + jnp.dot(h_hi, wrl_ref[...], preferred_element_type=F32)
                   + br_ref[...])


def _out_proj(x2d, y_hy, o_f, o_b, z, dn_norm_w, w_oh, w_od, g, wr_hi, wr_lo, br):
    T = x2d.shape[0]
    const = lambda i: (0, 0)
    row = lambda i: (i, 0)
    return pl.pallas_call(
        _out_proj_kernel,
        grid=(T // ROW_TILE,),
        in_specs=[pl.BlockSpec((ROW_TILE, D_MODEL), row),
                  pl.BlockSpec((ROW_TILE, HY_CH), row),
                  pl.BlockSpec((ROW_TILE, DN_W), row),
                  pl.BlockSpec((ROW_TILE, DN_W), row),
                  pl.BlockSpec((ROW_TILE, DN_W), row),
                  pl.BlockSpec((1, DN_W), const),
                  pl.BlockSpec((HY_CH, D_MODEL), const),
                  pl.BlockSpec((DN_W, D_MODEL), const),
                  pl.BlockSpec((1, D_MODEL), const),
                  pl.BlockSpec((D_MODEL, LANES), const),
                  pl.BlockSpec((D_MODEL, LANES), const),
                  pl.BlockSpec((1, LANES), const)],
        out_specs=[pl.BlockSpec((ROW_TILE, D_MODEL), row),
                   pl.BlockSpec((ROW_TILE * SC_PIECES, SC_ROW), row),
                   pl.BlockSpec((ROW_TILE, LANES), row)],
        out_shape=[jax.ShapeDtypeStruct((T, D_MODEL), F32),
                   jax.ShapeDtypeStruct((T * SC_PIECES, SC_ROW), F32),
                   jax.ShapeDtypeStruct((T, LANES), F32)],
        compiler_params=pltpu.CompilerParams(dimension_semantics=("parallel",),
                                             vmem_limit_bytes=VMEM_LIMIT),
        name="out_proj_router",
    )(x2d, y_hy, o_f, o_b, z, jnp.tile(dn_norm_w, DN_HEADS).reshape(1, DN_W), w_oh, w_od,
      g.reshape(1, D_MODEL), wr_hi, wr_lo, br)


def _deint_kernel(w_ref, p_ref, og_ref, ol_ref):
    half = DEINT_COLS // 2
    sel = jnp.dot(w_ref[0].astype(BF16), p_ref[...], preferred_element_type=F32)
    og_ref[0] = sel[:, :half].astype(BF16)
    ol_ref[0] = sel[:, half:].astype(BF16)


def _deinterleave(w_gate_up):
    half = DEINT_COLS // 2
    r = np.arange(DEINT_COLS)[:, None]
    c = np.arange(DEINT_COLS)[None, :]
    perm = jnp.asarray(np.where(c < half, r == 2 * c, r == 2 * (c - half) + 1), BF16)
    out = jax.ShapeDtypeStruct((N_EXPERTS, D_MODEL, D_FF), BF16)
    return pl.pallas_call(
        _deint_kernel,
        grid=(N_EXPERTS, 2 * D_FF // DEINT_COLS),
        in_specs=[pl.BlockSpec((1, D_MODEL, DEINT_COLS), lambda e, j: (e, 0, j)),
                  pl.BlockSpec((DEINT_COLS, DEINT_COLS), lambda e, j: (0, 0))],
        out_specs=[pl.BlockSpec((1, D_MODEL, half), lambda e, j: (e, 0, j)),
                   pl.BlockSpec((1, D_MODEL, half), lambda e, j: (e, 0, j))],
        out_shape=[out, out],
        compiler_params=pltpu.CompilerParams(dimension_semantics=("parallel", "parallel")),
        name="deinterleave_gate_up",
    )(w_gate_up, perm)


def _store_pieces(ref, x):
    rows = x.shape[0]
    for c in range(SC_PIECES):
        ref[pl.ds(c, rows, stride=SC_PIECES), :] = x[:, c * SC_ROW:(c + 1) * SC_ROW]


def _load_pieces(ref, rows):
    return jnp.concatenate([ref[pl.ds(c, rows, stride=SC_PIECES), :] for c in range(SC_PIECES)], axis=1)


def _sc_scatter(x, indices, n_out):
    n = indices.shape[0]
    rows, width = x.shape
    nsrc = rows // SC_WINDOW
    mesh = plsc.VectorSubcoreMesh(core_axis_name="core", subcore_axis_name="subcore")

    @functools.partial(pl.kernel, out_type=jax.ShapeDtypeStruct((n_out, width), x.dtype), mesh=mesh)
    def scatter(x_hbm, i_hbm, o_hbm):
        def body(x_vmem, i_vmem):
            pltpu.sync_copy(x_vmem, o_hbm.at[i_vmem.at[0]])

        pltpu.emit_pipeline(
            body,
            grid=(n // SC_WINDOW,),
            in_specs=[pl.BlockSpec((SC_WINDOW, width), index_map=lambda i: (i % nsrc, 0)),
                      pl.BlockSpec((1, SC_WINDOW), index_map=lambda i: (0, i))],
            out_specs=[],
            core_axis_name=("core", "subcore"),
            dimension_semantics=(pltpu.PARALLEL,),
        )(x_hbm, i_hbm)

    return scatter(x, indices.reshape(1, n))


def _expert_kernel(be_ref, nb_ref, nv_ref, xb_ref, wg_ref, wl_ref, bg_ref, bl_ref, wd_ref, bd_ref, y_ref):
    i = pl.program_id(0)

    @pl.when(i < nb_ref[0])
    def _():
        row_id = lax.broadcasted_iota(jnp.int32, (MOE_BLOCK, 1), 0)
        xb = jnp.where(row_id < nv_ref[i], _load_pieces(xb_ref, MOE_BLOCK), 0.0).astype(BF16)
        hg = jnp.dot(xb, wg_ref[0], preferred_element_type=F32) + bg_ref[0]
        hl = jnp.dot(xb, wl_ref[0], preferred_element_type=F32) + bl_ref[0]
        x_glu = jnp.minimum(hg, SWIGLU_LIMIT)
        x_lin = jnp.clip(hl, -SWIGLU_LIMIT, SWIGLU_LIMIT)
        act = x_glu * jax.nn.sigmoid(SWIGLU_ALPHA * x_glu) * (x_lin + 1.0)
        y = jnp.dot(act.astype(BF16), wd_ref[0].astype(BF16), preferred_element_type=F32) + bd_ref[0]
        _store_pieces(y_ref, y)

    @pl.when(i >= nb_ref[0])
    def _():
        y_ref[...] = jnp.zeros_like(y_ref)


def _expert_mlp(xb, block_e, n_used, n_valid, wg, wl, bg, bl, wd, bd):
    n_blocks = xb.shape[0] // (MOE_BLOCK * SC_PIECES)
    n_rows = n_blocks * MOE_BLOCK
    rowm = lambda i, be, nb, nv: (i, 0)
    exp3 = lambda i, be, nb, nv: (be[i], 0, 0)
    grid_spec = pltpu.PrefetchScalarGridSpec(
        num_scalar_prefetch=3,
        grid=(n_blocks,),
        in_specs=[pl.BlockSpec((MOE_BLOCK * SC_PIECES, SC_ROW), rowm),
                  pl.BlockSpec((1, D_MODEL, D_FF), exp3),
                  pl.BlockSpec((1, D_MODEL, D_FF), exp3),
                  pl.BlockSpec((1, 1, D_FF), exp3),
                  pl.BlockSpec((1, 1, D_FF), exp3),
                  pl.BlockSpec((1, D_FF, D_MODEL), exp3),
                  pl.BlockSpec((1, 1, D_MODEL), exp3)],
        out_specs=pl.BlockSpec((MOE_BLOCK * SC_PIECES, SC_ROW), rowm),
    )
    return pl.pallas_call(
        _expert_kernel,
        grid_spec=grid_spec,
        out_shape=jax.ShapeDtypeStruct((n_rows * SC_PIECES, SC_ROW), F32),
        compiler_params=pltpu.CompilerParams(dimension_semantics=("arbitrary",),
                                             vmem_limit_bytes=VMEM_LIMIT),
        name="expert_mlp",
    )(block_e, n_used, n_valid, xb, wg, wl, bg, bl, wd, bd)


def _moe(h, logits, wg, wl, bg, bl, wd, bd):
    T = logits.shape[0]
    TK = T * TOP_K
    top_vals, top_idx = lax.top_k(logits, TOP_K)
    gates = jax.nn.softmax(top_vals, axis=-1)
    onehot = jax.nn.one_hot(top_idx, N_EXPERTS, dtype=jnp.int32)
    sel = jnp.sum(onehot, axis=1)
    before = jnp.cumsum(sel, axis=0) - sel
    counts = jnp.sum(sel, axis=0)
    padded = (counts + MOE_BLOCK - 1) // MOE_BLOCK * MOE_BLOCK
    pad_end = jnp.cumsum(padded)
    pad_start = pad_end - padded
    dest = jnp.sum(onehot * (before + pad_start[None, :])[:, None, :], axis=-1).astype(jnp.int32)
    n_blocks = (TK + MOE_BLOCK - 1) // MOE_BLOCK + N_EXPERTS
    n_rows = n_blocks * MOE_BLOCK
    block_start = jnp.arange(n_blocks, dtype=jnp.int32) * MOE_BLOCK
    block_e = jnp.minimum(jnp.sum((block_start[:, None] >= pad_end[None, :]).astype(jnp.int32), axis=1),
                          N_EXPERTS - 1)
    n_used = (pad_end[-1] // MOE_BLOCK).astype(jnp.int32).reshape(1)
    be_onehot = jax.nn.one_hot(block_e, N_EXPERTS, dtype=jnp.int32)
    n_valid = jnp.clip(jnp.sum(be_onehot * (counts + pad_start)[None, :], axis=1) - block_start,
                       0, MOE_BLOCK).astype(jnp.int32)
    idx = (dest.T[:, :, None] * SC_PIECES + jnp.arange(SC_PIECES, dtype=jnp.int32)[None, None, :]).reshape(-1)
    xb = _sc_scatter(h, idx, n_rows * SC_PIECES)
    yb = _expert_mlp(xb, block_e, n_used, n_valid, wg, wl, bg, bl, wd, bd)
    return yb, dest, gates


def _sc_gather(x, indices):
    n = indices.shape[0]
    width = x.shape[1]
    mesh = plsc.VectorSubcoreMesh(core_axis_name="core", subcore_axis_name="subcore")

    @functools.partial(pl.kernel, out_type=jax.ShapeDtypeStruct((n, width), x.dtype), mesh=mesh)
    def gather(x_hbm, i_hbm, o_hbm):
        def body(i_vmem, o_vmem):
            pltpu.sync_copy(x_hbm.at[i_vmem.at[0]], o_vmem)

        pltpu.emit_pipeline(
            body,
            grid=(n // SC_WINDOW,),
            in_specs=[pl.BlockSpec((1, SC_WINDOW), index_map=lambda i: (0, i))],
            out_specs=[pl.BlockSpec((SC_WINDOW, width), index_map=lambda i: (i, 0))],
            core_axis_name=("core", "subcore"),
            dimension_semantics=(pltpu.PARALLEL,),
        )(i_hbm, o_hbm)

    return gather(x, indices.reshape(1, n))


def _combine_kernel(gate_ref, x1_ref, g_ref, y4_ref, o_ref):
    tm = x1_ref.shape[0]
    pieces = SC_PIECES
    cols = []
    for c in range(pieces):
        acc = x1_ref[:, c * SC_ROW:(c + 1) * SC_ROW]
        for k in range(TOP_K):
            acc = acc + gate_ref[:, k:k + 1] * y4_ref[pl.ds((k * pieces + c) * tm, tm), :]
        cols.append(acc)
    o_ref[...] = _rms(jnp.concatenate(cols, axis=1), g_ref[...])


def _combine_final(yb, dest, gates, x1, g):
    T = x1.shape[0]
    tm = COMBINE_TILE
    pieces = SC_PIECES
    d = dest.reshape(T // tm, tm, TOP_K).transpose(0, 2, 1)
    idx = (d[:, :, None, :] * pieces + jnp.arange(pieces, dtype=jnp.int32)[None, None, :, None]).reshape(-1)
    y4 = _sc_gather(yb, idx)
    row = lambda i: (i, 0)
    return pl.pallas_call(
        _combine_kernel,
        grid=(T // tm,),
        in_specs=[pl.BlockSpec((tm, TOP_K), row),
                  pl.BlockSpec((tm, D_MODEL), row),
                  pl.BlockSpec((1, D_MODEL), lambda i: (0, 0)),
                  pl.BlockSpec((tm * TOP_K * pieces, SC_ROW), row)],
        out_specs=pl.BlockSpec((tm, D_MODEL), row),
        out_shape=jax.ShapeDtypeStruct((T, D_MODEL), F32),
        compiler_params=pltpu.CompilerParams(dimension_semantics=("parallel",),
                                             vmem_limit_bytes=VMEM_LIMIT),
        name="moe_combine_final",
    )(gates, x1, g.reshape(1, D_MODEL), y4)


def _fft_tables(nc, nf):
    n = nc * nf
    kc = np.arange(nc, dtype=np.float64)
    a1 = 2.0 * np.pi * np.outer(kc, np.arange(nc // 2)) / nc
    c1, s1 = np.cos(a1), np.sin(a1)
    l1 = np.block([[c1, s1], [-s1, c1]])
    a2 = 2.0 * np.pi * np.outer(np.arange(nf), np.arange(nf)) / nf
    c2, s2 = np.cos(a2), np.sin(a2)
    m2 = np.block([[c2, s2], [-s2, c2]])
    m2i = np.block([[c2, -s2], [s2, c2]])
    a3 = 2.0 * np.pi * np.outer(np.arange(nc // 2), kc) / nc
    c3, s3 = np.cos(a3), np.sin(a3)
    l3 = np.block([[c3, -s3], [s3, c3]]) / n
    return tuple(jnp.asarray(m, F32).astype(BF16) for m in (l1, m2, m2i, l3))


def _twiddle(nc, nf):
    n = nc * nf
    ph = (jnp.arange(nf, dtype=jnp.int32)[:, None] * jnp.arange(nc, dtype=jnp.int32)[None, :]) % n
    ang = ph.astype(F32) * (2.0 * math.pi / n)
    tw = jnp.stack([jnp.cos(ang), jnp.sin(ang)], axis=1)
    return jnp.broadcast_to(tw[..., None], (nf, 2, nc, LANES))


def _lane_tile(t, width):
    return t if width == LANES else jnp.concatenate([t] * (width // LANES), axis=1)


def _fft1_kernel(z_ref, l1_ref, tw_ref, o_ref, *, nc):
    cb = o_ref.shape[-1]
    for j in range(SUB):
        rhs = jnp.concatenate([z_ref[0, :, j, :], z_ref[1, :, j, :]], axis=0).astype(BF16)
        a = jnp.dot(l1_ref[...], rhs, preferred_element_type=F32)
        ar, ai = a[:nc], a[nc:]
        twr = _lane_tile(tw_ref[j, 0], cb)
        twi = _lane_tile(tw_ref[j, 1], cb)
        o_ref[j, 0] = ar * twr + ai * twi
        o_ref[j, 1] = ai * twr - ar * twi


def _fft1(z, l1, tw, nc, nf):
    P, C = z.shape[0], z.shape[-1]
    cb = FFT_CB
    z_spec = pl.BlockSpec((None, 2, nc // 2, SUB, cb), lambda p, j, c: (p, 0, 0, j, c))
    return pl.pallas_call(
        functools.partial(_fft1_kernel, nc=nc),
        grid=(P, nf // SUB, C // cb),
        in_specs=[z_spec,
                  pl.BlockSpec(l1.shape, lambda p, j, c: (0, 0)),
                  pl.BlockSpec((SUB, 2, nc, LANES), lambda p, j, c: (j, 0, 0, 0))],
        out_specs=pl.BlockSpec((None, SUB, 2, nc, cb), lambda p, j, c: (p, j, 0, 0, c)),
        out_shape=jax.ShapeDtypeStruct((P, nf, 2, nc, C), F32),
        compiler_params=pltpu.CompilerParams(dimension_semantics=("parallel", "parallel", "parallel"),
                                             vmem_limit_bytes=VMEM_LIMIT),
        name="hyena_fft_stage1",
    )(z, l1, tw)


def _fft2_filter_kernel(a_ref, m2_ref, k_ref, *, nf):
    for j in range(SUB):
        rhs = jnp.concatenate([a_ref[:, 0, j, :], a_ref[:, 1, j, :]], axis=0).astype(BF16)
        x = jnp.dot(m2_ref[...], rhs, preferred_element_type=F32)
        k_ref[j, 0] = x[:nf]
        k_ref[j, 1] = x[nf:]


def _fft2_filter(a, m2, nc, nf):
    C = a.shape[-1]
    cb = FFT_CB
    return pl.pallas_call(
        functools.partial(_fft2_filter_kernel, nf=nf),
        grid=(nc // SUB, C // cb),
        in_specs=[pl.BlockSpec((None, nf, 2, SUB, cb), lambda k, c: (0, 0, 0, k, c)),
                  pl.BlockSpec(m2.shape, lambda k, c: (0, 0))],
        out_specs=pl.BlockSpec((SUB, 2, nf, cb), lambda k, c: (k, 0, 0, c)),
        out_shape=jax.ShapeDtypeStruct((nc, 2, nf, C), F32),
        compiler_params=pltpu.CompilerParams(dimension_semantics=("parallel", "parallel"),
                                             vmem_limit_bytes=VMEM_LIMIT),
        name="hyena_filter_spectrum",
    )(a, m2)


def _fft2_kernel(a_ref, k_ref, m2_ref, m2i_ref, o_ref, *, nf):
    for j in range(SUB):
        rhs = jnp.concatenate([a_ref[:, 0, j, :], a_ref[:, 1, j, :]], axis=0).astype(BF16)
        x = jnp.dot(m2_ref[...], rhs, preferred_element_type=F32)
        xr, xi = x[:nf], x[nf:]
        kr, ki = k_ref[j, 0], k_ref[j, 1]
        y = jnp.concatenate([xr * kr - xi * ki, xr * ki + xi * kr], axis=0).astype(BF16)
        b = jnp.dot(m2i_ref[...], y, preferred_element_type=F32)
        o_ref[:, 0, j, :] = b[:nf]
        o_ref[:, 1, j, :] = b[nf:]


def _fft2(a, kspec, m2, m2i, nc, nf):
    P, C = a.shape[0], a.shape[-1]
    cb = FFT_CB
    blk = pl.BlockSpec((None, nf, 2, SUB, cb), lambda p, k, c: (p, 0, 0, k, c))
    return pl.pallas_call(
        functools.partial(_fft2_kernel, nf=nf),
        grid=(P, nc // SUB, C // cb),
        in_specs=[blk,
                  pl.BlockSpec((SUB, 2, nf, cb), lambda p, k, c: (k, 0, 0, c)),
                  pl.BlockSpec(m2.shape, lambda p, k, c: (0, 0)),
                  pl.BlockSpec(m2i.shape, lambda p, k, c: (0, 0))],
        out_specs=blk,
        out_shape=jax.ShapeDtypeStruct(a.shape, F32),
        compiler_params=pltpu.CompilerParams(dimension_semantics=("parallel", "parallel", "parallel"),
                                             vmem_limit_bytes=VMEM_LIMIT),
        name="hyena_fft_stage2",
    )(a, kspec, m2, m2i)


def _fft3_kernel(b_ref, tw_ref, l3_ref, s_ref, x0_ref, bias_ref, o_ref, *, nc):
    cb = o_ref.shape[-1]
    half = nc // 2
    for j in range(SUB):
        br, bi = b_ref[j, 0], b_ref[j, 1]
        twr = _lane_tile(tw_ref[j, 0], cb)
        twi = _lane_tile(tw_ref[j, 1], cb)
        rhs = jnp.concatenate([br * twr - bi * twi, br * twi + bi * twr], axis=0).astype(BF16)
        y = jnp.dot(l3_ref[...], rhs, preferred_element_type=F32)
        for q in range(2):
            s = s_ref[q, :, j, :]
            o_ref[q, :, j, :] = x0_ref[q, :, j, :] * (y[q * half:(q + 1) * half] + s * bias_ref[...])


def _fft3(b, tw, l3, s5, x05, bias, nc, nf):
    P, C = b.shape[0], b.shape[-1]
    cb = FFT_CB
    seq = pl.BlockSpec((None, 2, nc // 2, SUB, cb), lambda p, j, c: (p, 0, 0, j, c))
    return pl.pallas_call(
        functools.partial(_fft3_kernel, nc=nc),
        grid=(P, nf // SUB, C // cb),
        in_specs=[pl.BlockSpec((None, SUB, 2, nc, cb), lambda p, j, c: (p, j, 0, 0, c)),
                  pl.BlockSpec((SUB, 2, nc, LANES), lambda p, j, c: (j, 0, 0, 0)),
                  pl.BlockSpec(l3.shape, lambda p, j, c: (0, 0)),
                  seq, seq,
                  pl.BlockSpec((1, cb), lambda p, j, c: (0, c))],
        out_specs=seq,
        out_shape=jax.ShapeDtypeStruct(s5.shape, F32),
        compiler_params=pltpu.CompilerParams(dimension_semantics=("parallel", "parallel", "parallel"),
                                             vmem_limit_bytes=VMEM_LIMIT),
        name="hyena_fft_stage3",
    )(b, tw, l3, s5, x05, bias)


def _fft_split(n):
    nf = 1 << (int(math.log2(n)) // 2)
    return n // nf, nf


def _filter_tables(nc):
    half = nc // 2
    kc = np.arange(nc, dtype=np.float64)[:, None]
    r = np.arange(half, dtype=np.float64)
    mats = []
    for rows_b, drop0 in ((half + (half - 1 - r), False), (nc - r, True)):
        a = 2.0 * np.pi * kc * np.concatenate([r, rows_b])[None, :] / nc
        m = np.concatenate([np.cos(a), -np.sin(a)], axis=0)
        if drop0:
            m[:, half] = 0.0
        mats.append(m)
    return tuple(jnp.asarray(m, F32).astype(BF16) for m in mats)


def _fft1_filter_kernel(hf_ref, hba_ref, hbb_ref, mb_ref, mb0_ref, tw_ref, o_ref, *, nc):
    cb = o_ref.shape[-1]
    m_first = jnp.where(pl.program_id(0) == 0, mb0_ref[...], mb_ref[...])
    for j in range(SUB):
        src = hbb_ref[:, 0, :] if j == 0 else hba_ref[:, SUB - j, :]
        rhs = jnp.concatenate([hf_ref[:, j, :], src], axis=0).astype(BF16)
        a = jnp.dot(m_first if j == 0 else mb_ref[...], rhs, preferred_element_type=F32)
        ar, ai = a[:nc], a[nc:]
        twr = _lane_tile(tw_ref[j, 0], cb)
        twi = _lane_tile(tw_ref[j, 1], cb)
        o_ref[j, 0] = ar * twr + ai * twi
        o_ref[j, 1] = ai * twr - ar * twi


def _fft1_filter(h_f, h_b, mb, mb0, tw, nc, nf):
    C = h_f.shape[-1]
    cb = FFT_CB
    nblk = nf // SUB
    hf3 = h_f.reshape(nc // 2, nf, C)
    hb3 = h_b.reshape(nc // 2, nf, C)
    blk = lambda f: pl.BlockSpec((nc // 2, SUB, cb), f)
    return pl.pallas_call(
        functools.partial(_fft1_filter_kernel, nc=nc),
        grid=(nblk, C // cb),
        in_specs=[blk(lambda j, c: (0, j, c)),
                  blk(lambda j, c: (0, nblk - 1 - j, c)),
                  blk(lambda j, c: (0, (nblk - j) % nblk, c)),
                  pl.BlockSpec(mb.shape, lambda j, c: (0, 0)),
                  pl.BlockSpec(mb0.shape, lambda j, c: (0, 0)),
                  pl.BlockSpec((SUB, 2, nc, LANES), lambda j, c: (j, 0, 0, 0))],
        out_specs=pl.BlockSpec((None, SUB, 2, nc, cb), lambda j, c: (0, j, 0, 0, c)),
        out_shape=jax.ShapeDtypeStruct((1, nf, 2, nc, C), F32),
        compiler_params=pltpu.CompilerParams(dimension_semantics=("parallel", "parallel"),
                                             vmem_limit_bytes=VMEM_LIMIT),
        name="hyena_filter_stage1",
    )(hf3, hb3, hb3, mb, mb0, tw)


def _hyena_conv(s, x0, h_f, h_b, bias):
    B, L, C = s.shape
    nc, nf = _fft_split(2 * L)
    tw = _twiddle(nc, nf)
    l1, m2, m2i, l3 = _fft_tables(nc, nf)
    mb, mb0 = _filter_tables(nc)
    kspec = _fft2_filter(_fft1_filter(h_f, h_b, mb, mb0, tw, nc, nf), m2, nc, nf)
    s5 = s.reshape(B // 2, 2, nc // 2, nf, C)
    x05 = x0.reshape(B // 2, 2, nc // 2, nf, C)
    a = _fft1(s5, l1, tw, nc, nf)
    b = _fft2(a, kspec, m2, m2i, nc, nf)
    return _fft3(b, tw, l3, s5, x05, bias.reshape(1, C), nc, nf).reshape(B, L, C)


def _hyena_filter(L, w1, b1, w2, b2, w3, freq):
    pos = jnp.arange(L, dtype=F32)
    t = jnp.linspace(0.0, 1.0, L, dtype=F32)[:, None]
    bands = jnp.linspace(1e-4, HY_BANDS - 1, HY_BANDS, dtype=F32)
    ang = (2.0 * math.pi / L) * pos[:, None] * bands[None, :]
    z = jnp.concatenate([t, jnp.cos(ang), -jnp.sin(ang)], axis=-1)
    h = jnp.sin(freq * (z @ w1 + b1))
    h = jnp.sin(freq * (h @ w2 + b2))
    h = h @ w3
    deltas = jnp.abs(jnp.linspace(math.log(HY_DECAY_TARGET) / HY_SLOW_PCT,
                                  math.log(HY_DECAY_TARGET) / HY_FAST_PCT, HY_CH, dtype=F32))
    window = jnp.exp(-t * deltas[None, :])
    h_f = h[:, :HY_CH] * window
    h_b = h[:, HY_CH:] * window
    l1 = jnp.sum(jnp.abs(h_f), axis=0) + jnp.sum(jnp.abs(h_b[1:]), axis=0)
    return h_f / l1, h_b / l1


def _split3(x):
    x1 = x.astype(BF16)
    r = x - x1.astype(F32)
    x2 = r.astype(BF16)
    x3 = (r - x2.astype(F32)).astype(BF16)
    return x1, x2, x3


def _mm(a, b):
    return jnp.dot(a.astype(BF16), b.astype(BF16), preferred_element_type=F32)


def _delta_kernel(qf_ref, kf_ref, vf_ref, gf_ref, qb_ref, kb_ref, vb_ref, gb_ref, of_ref, ob_ref, s_ref,
                  *, n_sub):
    @pl.when(pl.program_id(1) == 0)
    def _():
        s_ref[...] = jnp.zeros_like(s_ref)

    row = lax.broadcasted_iota(jnp.int32, (CHUNK, CHUNK), 0)
    col = lax.broadcasted_iota(jnp.int32, (CHUNK, CHUNK), 1)
    eye_f = jnp.where(row == col, 1.0, 0.0).astype(F32)
    dirs = ((qf_ref, kf_ref, vf_ref, gf_ref, of_ref, row >= col, row > col),
            (qb_ref, kb_ref, vb_ref, gb_ref, ob_ref, row <= col, row < col))

    ch = []
    for d, (q_ref, k_ref, v_ref, g_ref, _, incl, strict) in enumerate(dirs):
        tri = jnp.where(incl, 1.0, 0.0).astype(BF16)
        for c in range(n_sub):
            rows = slice(c * CHUNK, (c + 1) * CHUNK)
            gates = g_ref[rows, :]
            gl = 2 * DN_HEADS * d
            g1, g2, g3 = _split3(gates)
            gc_all = (jnp.dot(tri, g1, preferred_element_type=F32)
                      + jnp.dot(tri, g2, preferred_element_type=F32)
                      + jnp.dot(tri, g3, preferred_element_type=F32))
            gsum_all = jnp.sum(gates, axis=0, keepdims=True)
            for h in range(DN_HEADS):
                lanes = slice(h * DN_DK, (h + 1) * DN_DK)
                kh = k_ref[rows, lanes]
                beta = gates[:, gl + DN_HEADS + h:gl + DN_HEADS + h + 1]
                gc = gc_all[:, gl + h:gl + h + 1]
                g_last = gsum_all[:, gl + h:gl + h + 1]
                gc_b = jnp.broadcast_to(gc, (CHUNK, CHUNK))
                gc_row = jnp.sum(gc_b * eye_f, axis=0, keepdims=True)
                decay = jnp.where(incl, jnp.exp(jnp.minimum(gc_b - gc_row, 0.0)), 0.0)
                e_gc = jnp.exp(gc)
                kb = kh * beta
                ch.append(dict(d=d, c=c, h=h, lanes=lanes, rows=rows, strict=strict, decay=decay,
                               kh16=kh.astype(BF16), kb16=kb.astype(BF16),
                               q16=q_ref[rows, lanes].astype(BF16),
                               rhs=jnp.concatenate([v_ref[rows, lanes] * beta, kb * e_gc],
                                                   axis=1).astype(BF16),
                               qe=q_ref[rows, lanes] * e_gc,
                               kdT=(kh * jnp.exp(g_last - gc)).T.astype(BF16),
                               e_last=jnp.exp(g_last)))
    for x in ch:
        a = lax.dot_general(x["kb16"], x["kh16"], _NT, preferred_element_type=F32) * x["decay"]
        x["a"] = jnp.where(x["strict"], a, 0.0)
        x["qk"] = (lax.dot_general(x["q16"], x["kh16"], _NT, preferred_element_type=F32)
                   * x["decay"]).astype(BF16)
    for x in ch:
        x["t"] = eye_f - x["a"]
        x["p"] = _mm(x["a"], x["a"])
    for level in range(5):
        for x in ch:
            p16 = x["p"].astype(BF16)
            x["t"] = x["t"] + jnp.dot(x["t"].astype(BF16), p16, preferred_element_type=F32)
            if level < 4:
                x["p"] = jnp.dot(p16, p16, preferred_element_type=F32)
    for x in ch:
        uw = jnp.dot(x["t"].astype(BF16), x["rhs"], preferred_element_type=F32)
        x["u"] = uw[:, :DN_DV]
        x["wq"] = jnp.concatenate([uw[:, DN_DV:], x["qe"]], axis=0).astype(BF16)

    for step in range(n_sub):
        cur = [x for x in ch if x["c"] == (step if x["d"] == 0 else n_sub - 1 - step)]
        for x in cur:
            x["s"] = s_ref[x["d"], x["h"]]
            x["ws"] = jnp.dot(x["wq"], x["s"].astype(BF16), preferred_element_type=F32)
        for x in cur:
            x["vn"] = (x["u"] - x["ws"][:CHUNK]).astype(BF16)
        for x in cur:
            o = x["ws"][CHUNK:] + jnp.dot(x["qk"], x["vn"], preferred_element_type=F32)
            dirs[x["d"]][4][x["rows"], x["lanes"]] = o
            s_ref[x["d"], x["h"]] = (x["s"] * x["e_last"]
                                     + jnp.dot(x["kdT"], x["vn"], preferred_element_type=F32))


def _delta_scan(q, k, v, gates):
    B, L, _ = q.shape
    n_sub = DN_STEP_CHUNKS
    rows = n_sub * CHUNK
    nblk = L // rows
    fwd = pl.BlockSpec((None, rows, DN_W), lambda b, j: (b, j, 0))
    bwd = pl.BlockSpec((None, rows, DN_W), lambda b, j: (b, nblk - 1 - j, 0))
    gfwd = pl.BlockSpec((None, rows, LANES), lambda b, j: (b, j, 0))
    gbwd = pl.BlockSpec((None, rows, LANES), lambda b, j: (b, nblk - 1 - j, 0))
    out = jax.ShapeDtypeStruct((B, L, DN_W), F32)
    return pl.pallas_call(
        functools.partial(_delta_kernel, n_sub=n_sub),
        grid=(B, nblk),
        in_specs=[fwd, fwd, fwd, gfwd, bwd, bwd, bwd, gbwd],
        out_specs=[fwd, bwd],
        out_shape=[out, out],
        scratch_shapes=[pltpu.VMEM((2, DN_HEADS, DN_DK, DN_DV), F32)],
        compiler_params=pltpu.CompilerParams(dimension_semantics=("parallel", "arbitrary"),
                                             vmem_limit_bytes=VMEM_LIMIT),
        name="delta_scan",
    )(q, k, v, gates, q, k, v, gates)


def _trunk(x, w, hy_pos_w1, hy_pos_b1, hy_pos_w2, hy_pos_b2, hy_pos_w3, hy_sin_freq, hy_bias, dn_norm_w):
    B, L, D = x.shape
    T = B * L
    x2d = x.reshape(T, D)
    x0, s, q, k, v, z, dn_gates = _in_proj(x2d, L, w["norm_mix"], *w["in_proj"])
    seq = lambda a: a.reshape(B, L, a.shape[-1])
    h_f, h_b = _hyena_filter(L, hy_pos_w1, hy_pos_b1, hy_pos_w2, hy_pos_b2, hy_pos_w3, hy_sin_freq)
    y_hy = _hyena_conv(seq(s), seq(x0), h_f, h_b, hy_bias)
    o_f, o_b = _delta_scan(seq(q), seq(k), seq(v), seq(dn_gates))
    x1, h_ffn, logits = _out_proj(x2d, y_hy.reshape(T, HY_CH), o_f.reshape(T, DN_W), o_b.reshape(T, DN_W),
                                 z, dn_norm_w, w["out_hy"], w["out_dn"], w["norm_ffn"],
                                 w["router_hi"], w["router_lo"], w["router_b"])
    yb, dest, gates = _moe(h_ffn, logits[:, :N_EXPERTS], w["wg"], w["wl"], w["bg"], w["bl"], w["wd"],
                           w["bd"])
    return _combine_final(yb, dest, gates, x1, w["norm_final"]).reshape(B, L, D)


def kernel(x_prompt, x_sample, norm_mix_w, w_in, hy_conv_w, hy_conv_b, hy_pos_w1, hy_pos_b1, hy_pos_w2, hy_pos_b2, hy_pos_w3, hy_sin_freq, hy_bias, dn_conv_w, dn_a_log, dn_dt_bias, dn_norm_w, w_out, norm_ffn_w, w_router, b_router, w_gate_up, b_gate_up, w_down, b_down, norm_final_w):
    w_out16 = w_out[0].astype(BF16)
    wr = jnp.pad(w_router[0], ((0, 0), (0, LANES - N_EXPERTS)))
    wr_hi = wr.astype(BF16)
    wg, wl = _deinterleave(w_gate_up[0])
    w = {
        "norm_mix": norm_mix_w[0],
        "in_proj": _in_proj_params(w_in[0], hy_conv_w[0], hy_conv_b[0], dn_conv_w[0], dn_a_log[0],
                                   dn_dt_bias[0]),
        "out_hy": w_out16[:HY_CH],
        "out_dn": w_out16[HY_CH:],
        "norm_ffn": norm_ffn_w[0],
        "router_hi": wr_hi,
        "router_lo": (wr - wr_hi.astype(F32)).astype(BF16),
        "router_b": jnp.pad(b_router[0], (0, LANES - N_EXPERTS)).reshape(1, LANES),
        "wg": wg,
        "wl": wl,
        "bg": b_gate_up[0][:, 0::2].reshape(N_EXPERTS, 1, D_FF),
        "bl": b_gate_up[0][:, 1::2].reshape(N_EXPERTS, 1, D_FF),
        "wd": w_down[0],
        "bd": b_down[0].reshape(N_EXPERTS, 1, D_MODEL),
        "norm_final": norm_final_w,
    }
    mix = (hy_pos_w1[0], hy_pos_b1[0], hy_pos_w2[0], hy_pos_b2[0], hy_pos_w3[0], hy_sin_freq[0],
           hy_bias[0], dn_norm_w[0])
    y_sample = _trunk(x_sample, w, *mix)
    y_prompt = _trunk(x_prompt, w, *mix)
    return (y_prompt, y_sample)
```
